```python
import jax, jax.numpy as jnp
from jax import lax
import numpy as np

D_MODEL = 1024
BATCH = 8
SEQ = 2048
DEPTH = 1
DEC_BATCH = 128
DEC_SEQ = 8
PAST_LEN = 16384
PAGE_SIZE = 128

D_MIX = D_MODEL
D_RG = D_MIX // 2
D_SG = D_MIX - D_RG
RG_HEADS = 8
RG_HEAD_DIM = D_RG // RG_HEADS
SG_HEADS = 8
SG_HEAD_DIM = D_SG // SG_HEADS
CHUNK = 128
CONV_W = 4
LRU_C = 8.0
D_FF = 4 * D_MODEL
D_IN = 2 * D_RG + 2 * D_SG
EPS = 1e-6

kernel_name = 'hymba_rglru_gmlp_decoder_step'


def rms_norm(x, g):
    xf = x.astype(jnp.float32)
    y = xf * lax.rsqrt(jnp.mean(xf * xf, axis=-1, keepdims=True) + EPS)
    return (y * g.astype(jnp.float32)).astype(x.dtype)


def layer_norm(x, g, b):
    xf = x.astype(jnp.float32)
    mu = jnp.mean(xf, axis=-1, keepdims=True)
    var = jnp.mean(jnp.square(xf - mu), axis=-1, keepdims=True)
    y = (xf - mu) * lax.rsqrt(var + EPS)
    return (y * g.astype(jnp.float32) + b.astype(jnp.float32)).astype(x.dtype)


def causal_conv(xb, buf, w, b):
    t_len = xb.shape[1]
    xp = jnp.concatenate([buf.astype(xb.dtype), xb], axis=1)
    out = b
    for k in range(CONV_W):
        out = out + xp[:, k:k + t_len] * w[k]
    return out, xp[:, -(CONV_W - 1):]


def block_diag(x, w, b):
    bsz, t_len, _ = x.shape
    xh = x.reshape(bsz, t_len, RG_HEADS, RG_HEAD_DIM)
    y = jnp.einsum('bthi,hij->bthj', xh, w) + b
    return y.reshape(bsz, t_len, D_RG)


def rg_lru(x, h0, w_ra, b_ra, w_ri, b_ri, lru_l, pos0):
    t_len = x.shape[1]
    r = jax.nn.sigmoid(block_diag(x, w_ra, b_ra).astype(jnp.float32))
    i = jax.nn.sigmoid(block_diag(x, w_ri, b_ri).astype(jnp.float32))
    log_a = LRU_C * r * jax.nn.log_sigmoid(lru_l.astype(jnp.float32))
    a = jnp.exp(log_a)
    mult = jnp.sqrt(-jnp.expm1(2.0 * log_a))
    reset = (pos0 + jnp.arange(t_len)) == 0
    mult = jnp.where(reset[None, :, None], 1.0, mult)
    bterm = mult * i * x.astype(jnp.float32)
    bterm = bterm.at[:, 0].add(a[:, 0] * h0.astype(jnp.float32))

    def combine(lhs, rhs):
        a1, b1 = lhs
        a2, b2 = rhs
        return a1 * a2, a2 * b1 + b2

    _, h = lax.associative_scan(combine, (a, bterm), axis=1)
    return h.astype(x.dtype), h[:, -1].astype(x.dtype)


def spatial_gate(u, v, w_s, b_s):
    bsz, t_len, _ = v.shape
    n_chunk = -(-t_len // CHUNK)
    pad = n_chunk * CHUNK - t_len
    vp = jnp.pad(v, ((0, 0), (0, pad), (0, 0))).reshape(bsz, n_chunk, CHUNK, SG_HEADS, SG_HEAD_DIM)
    mask = jnp.tril(jnp.ones((CHUNK, CHUNK), dtype=bool))
    ws = jnp.where(mask[None], w_s, 0)
    mixed = jnp.einsum('hts,bcshd->bcthd', ws, vp) + b_s.T[None, None, :, :, None]
    mixed = mixed.reshape(bsz, n_chunk * CHUNK, D_SG)[:, :t_len]
    return u * mixed


def trunk(x, c, conv_state, lru_state, pos0,
          w_ada, b_ada, g_mix, g_ffn, w_in, conv_w, conv_b, w_ra, b_ra, w_ri, b_ri, lru_l,
          ln_v_g, ln_v_b, w_s, b_s, g_rg, g_sg, w_out, w_ff1, w_ff2, g_final):
    t_len = x.shape[1]
    v_start = ((t_len - 1) // CHUNK) * CHUNK
    convs, lrus, vrows = [], [], []
    for l in range(DEPTH):
        ada = jax.nn.silu(c) @ w_ada[l] + b_ada[l]
        sh_m, sc_m, gt_m, sh_f, sc_f, gt_f = jnp.split(ada[:, None, :], 6, axis=-1)
        hm = rms_norm(x, g_mix[l]) * (1.0 + sc_m) + sh_m
        proj = hm @ w_in[l]
        xr, yg, u, v = jnp.split(proj, [D_RG, 2 * D_RG, 2 * D_RG + D_SG], axis=-1)
        xc, new_buf = causal_conv(xr, conv_state[l], conv_w[l], conv_b[l])
        hr, h_last = rg_lru(xc, lru_state[l], w_ra[l], b_ra[l], w_ri[l], b_ri[l], lru_l[l], pos0)
        rg_out = hr * jax.nn.gelu(yg)
        sg_out = spatial_gate(u, layer_norm(v, ln_v_g[l], ln_v_b[l]), w_s[l], b_s[l])
        merged = jnp.concatenate([rms_norm(rg_out, g_rg[l]), rms_norm(sg_out, g_sg[l])], axis=-1)
        x = x + gt_m * (merged @ w_out[l])
        hf = rms_norm(x, g_ffn[l]) * (1.0 + sc_f) + sh_f
        x = x + gt_f * (jnp.square(jax.nn.relu(hf @ w_ff1[l])) @ w_ff2[l])
        convs.append(new_buf)
        lrus.append(h_last)
        vrows.append(v[:, v_start:])
    y = rms_norm(x, g_final)
    return y, jnp.stack(convs), jnp.stack(lrus), jnp.stack(vrows)


def setup_inputs(seed: int = 0) -> dict:
    key = jax.random.key(seed)
    ks = jax.random.split(key, 32)
    f32 = jnp.float32

    def nrm(k, shape, scale):
        return jax.random.normal(k, shape, f32) * scale

    p_lru = jax.random.uniform(ks[12], (DEPTH, D_RG), f32, 0.9, 0.999)
    return {
        'x_prompt': nrm(ks[0], (BATCH, SEQ, D_MODEL), 1.0),
        'x_sample': nrm(ks[1], (DEC_BATCH, DEC_SEQ, D_MODEL), 1.0),
        'c_prompt': nrm(ks[2], (BATCH, D_MODEL), 1.0),
        'c_sample': nrm(ks[3], (DEC_BATCH, D_MODEL), 1.0),
        'state_conv': nrm(ks[4], (DEPTH, DEC_BATCH, CONV_W - 1, D_RG), 1.0),
        'state_rglru': nrm(ks[5], (DEPTH, DEC_BATCH, D_RG), 0.5),
        'w_ada': nrm(ks[6], (DEPTH, D_MODEL, 6 * D_MODEL), 0.5 * D_MODEL ** -0.5),
        'b_ada': nrm(ks[7], (DEPTH, 6 * D_MODEL), 0.1),
        'g_mix': 1.0 + nrm(ks[8], (DEPTH, D_MODEL), 0.1),
        'g_ffn': 1.0 + nrm(ks[9], (DEPTH, D_MODEL), 0.1),
        'w_in': nrm(ks[10], (DEPTH, D_MODEL, D_IN), D_MODEL ** -0.5),
        'conv_w': nrm(ks[11], (DEPTH, CONV_W, D_RG), CONV_W ** -0.5),
        'conv_b': nrm(ks[13], (DEPTH, D_RG), 0.02),
        'w_ra': nrm(ks[14], (DEPTH, RG_HEADS, RG_HEAD_DIM, RG_HEAD_DIM), RG_HEAD_DIM ** -0.5),
        'b_ra': nrm(ks[15], (DEPTH, RG_HEADS, RG_HEAD_DIM), 0.02),
        'w_ri': nrm(ks[16], (DEPTH, RG_HEADS, RG_HEAD_DIM, RG_HEAD_DIM), RG_HEAD_DIM ** -0.5),
        'b_ri': nrm(ks[17], (DEPTH, RG_HEADS, RG_HEAD_DIM), 0.02),
        'lru_l': jnp.log(p_lru) - jnp.log1p(-p_lru),
        'ln_v_g': 1.0 + nrm(ks[18], (DEPTH, D_SG), 0.1),
        'ln_v_b': nrm(ks[19], (DEPTH, D_SG), 0.02),
        'w_s': nrm(ks[20], (DEPTH, SG_HEADS, CHUNK, CHUNK), 0.5 * CHUNK ** -0.5),
        'b_s': 1.0 + nrm(ks[21], (DEPTH, SG_HEADS, CHUNK), 0.1),
        'g_rg': 1.0 + nrm(ks[22], (DEPTH, D_RG), 0.1),
        'g_sg': 1.0 + nrm(ks[23], (DEPTH, D_SG), 0.1),
        'w_out': nrm(ks[24], (DEPTH, D_MIX, D_MODEL), D_MIX ** -0.5),
        'w_ff1': nrm(ks[25], (DEPTH, D_MODEL, D_FF), D_MODEL ** -0.5),
        'w_ff2': nrm(ks[26], (DEPTH, D_FF, D_MODEL), D_FF ** -0.5),
        'g_final': 1.0 + nrm(ks[27], (D_MODEL,), 0.1),
    }


def reference(x_prompt, x_sample, c_prompt, c_sample, state_conv, state_rglru,
              w_ada, b_ada, g_mix, g_ffn, w_in, conv_w, conv_b, w_ra, b_ra, w_ri, b_ri, lru_l,
              ln_v_g, ln_v_b, w_s, b_s, g_rg, g_sg, w_out, w_ff1, w_ff2, g_final):
    conv0 = jnp.zeros((DEPTH, x_prompt.shape[0], CONV_W - 1, D_RG), x_prompt.dtype)
    lru0 = jnp.zeros((DEPTH, x_prompt.shape[0], D_RG), x_prompt.dtype)
    y_prompt, conv_p, lru_p, vrows_p = trunk(
        x_prompt, c_prompt, conv0, lru0, 0,
        w_ada, b_ada, g_mix, g_ffn, w_in, conv_w, conv_b, w_ra, b_ra, w_ri, b_ri, lru_l,
        ln_v_g, ln_v_b, w_s, b_s, g_rg, g_sg, w_out, w_ff1, w_ff2, g_final)
    y_sample, conv_s, lru_s, vrows_s = trunk(
        x_sample, c_sample, state_conv, state_rglru, PAST_LEN,
        w_ada, b_ada, g_mix, g_ffn, w_in, conv_w, conv_b, w_ra, b_ra, w_ri, b_ri, lru_l,
        ln_v_g, ln_v_b, w_s, b_s, g_rg, g_sg, w_out, w_ff1, w_ff2, g_final)
    return (y_prompt, y_sample, conv_p, lru_p, vrows_p, conv_s, lru_s, vrows_s)
```

```python
import functools

import jax
import jax.numpy as jnp
from jax import lax
from jax.experimental import pallas as pl
from jax.experimental.pallas import tpu as pltpu

D_MODEL = 1024
D_RG = 512
D_SG = 512
D_IN = 2 * D_RG + 2 * D_SG
D_FF = 4 * D_MODEL
RG_HEADS = 8
RG_HEAD_DIM = D_RG // RG_HEADS
SG_HEADS = 8
SG_HEAD_DIM = D_SG // SG_HEADS
CHUNK = 128
CONV_W = 4
LRU_C = 8.0
EPS = 1e-6

SUBLANES = 8
GATE_GROUPS = 2
GATE_W = D_RG // GATE_GROUPS
PROMPT_TILE = 512
SAMPLE_NB = 64
FF_CHUNK = 1024
ADA_TN = 1024
VMEM_LIMIT_BYTES = 60000 * 1024

F32 = jnp.float32
BF16 = jnp.bfloat16

_V_CONV_W, _V_CONV_B, _V_B_RA, _V_B_RI, _V_LRU_L, _V_LN_G, _V_LN_B, _V_G_RG, _V_G_SG = 0, 4, 5, 6, 7, 8, 9, 10, 11


def _dot(a, b):
    return jnp.dot(a, b, preferred_element_type=F32)


def _rms(x, g):
    ms = jnp.mean(x * x, axis=-1, keepdims=True)
    return (x * lax.rsqrt(ms + EPS)) * g


def _layer_norm(x, g, b):
    mu = jnp.mean(x, axis=-1, keepdims=True)
    xc = x - mu
    var = jnp.mean(xc * xc, axis=-1, keepdims=True)
    return (xc * lax.rsqrt(var + EPS)) * g + b


def _cat(parts, axis=0):
    return parts[0] if len(parts) == 1 else jnp.concatenate(parts, axis=axis)


def _split_mods(ada):
    return [ada[:, k * D_MODEL:(k + 1) * D_MODEL] for k in range(6)]


def _modulated_norm(x_blocks, g, sc, sh):
    return _cat([(_rms(xb, g) * (1.0 + sc) + sh).astype(BF16) for xb in x_blocks])


def _gates(xc, v512, wg_ref):
    xcb = xc.astype(BF16)
    r_parts, i_parts = [], []
    for g in range(GATE_GROUPS):
        lo, hi = g * GATE_W, (g + 1) * GATE_W
        pre = _dot(xcb[:, lo:hi], wg_ref[g])
        r_parts.append(jax.nn.sigmoid(pre[:, :GATE_W] + v512[_V_B_RA:_V_B_RA + 1, lo:hi]))
        i_parts.append(jax.nn.sigmoid(pre[:, GATE_W:] + v512[_V_B_RI:_V_B_RI + 1, lo:hi]))
    return _cat(r_parts, 1), _cat(i_parts, 1)


def _lru_coeffs(r, v512):
    lam = v512[_V_LRU_L:_V_LRU_L + 1, :]
    log_sig = jnp.minimum(lam, 0.0) - jnp.log1p(jnp.exp(-jnp.abs(lam)))
    log_a = (LRU_C * r) * log_sig
    a = jnp.exp(log_a)
    mult = jnp.sqrt(-jnp.tanh(log_a) * (a * a + 1.0))
    return a, mult


def _tail(x_blocks, merged, mods, v1024, w_out_ref, w_ff1_ref, w_ff2_ref, store_y):
    _, _, gt_m, sh_f, sc_f, gt_f = mods
    rows = x_blocks[0].shape[0]
    mm = _dot(merged, w_out_ref[...])
    x1 = [xb + gt_m * mm[i * rows:(i + 1) * rows] for i, xb in enumerate(x_blocks)]
    hf = _modulated_norm(x1, v1024[1:2, :], sc_f, sh_f)
    acc = None
    for c in range(D_FF // FF_CHUNK):
        lo, hi = c * FF_CHUNK, (c + 1) * FF_CHUNK
        h1 = jnp.maximum(_dot(hf, w_ff1_ref[:, lo:hi]), 0.0)
        part = _dot((h1 * h1).astype(BF16), w_ff2_ref[lo:hi, :])
        acc = part if acc is None else acc + part
    for i, xb in enumerate(x1):
        x2 = xb + gt_f * acc[i * rows:(i + 1) * rows]
        store_y(i, _rms(x2, v1024[2:3, :]))


def _group_scan(a, b, tmod):
    s = 1
    while s < SUBLANES:
        a_sh = pltpu.roll(a, s, 0)
        b_sh = pltpu.roll(b, s, 0)
        keep = tmod >= s
        b = jnp.where(keep, a * b_sh + b, b)
        a = jnp.where(keep, a * a_sh, a)
        s *= 2
    return a, b


def _prompt_kernel(x_ref, ada_ref, v1024_ref, v512_ref, w_in_ref, wg_ref, wsp_ref, bias_s_ref,
                   w_out_ref, w_ff1_ref, w_ff2_ref,
                   y_ref, conv_ref, lru_ref, vrows_ref,
                   ext_ref, h_ref):
    tt = x_ref.shape[0]
    j = pl.program_id(1)

    @pl.when(j == 0)
    def _():
        ext_ref[0:SUBLANES, :] = jnp.zeros((SUBLANES, D_RG), F32)
        h_ref[...] = jnp.zeros((SUBLANES, D_RG), F32)

    x = x_ref[...]
    mods = _split_mods(ada_ref[...])
    sh_m, sc_m = mods[0], mods[1]
    v1024 = v1024_ref[...]
    v512 = v512_ref[...]

    hm = _modulated_norm([x], v1024[0:1, :], sc_m, sh_m)
    proj = _dot(hm, w_in_ref[...])
    xr = proj[:, 0:D_RG]
    yg = proj[:, D_RG:2 * D_RG]
    u = proj[:, 2 * D_RG:2 * D_RG + D_SG]
    v = proj[:, 2 * D_RG + D_SG:]

    ext_ref[SUBLANES:, :] = xr
    xc = v512[_V_CONV_B:_V_CONV_B + 1, :]
    for k in range(CONV_W):
        off = SUBLANES - (CONV_W - 1) + k
        xc = xc + ext_ref[off:off + tt, :] * v512[_V_CONV_W + k:_V_CONV_W + k + 1, :]
    tail = xr[tt - SUBLANES:, :]
    ext_ref[0:SUBLANES, :] = tail
    conv_ref[...] = tail

    r, i_gate = _gates(xc, v512, wg_ref)
    a, mult = _lru_coeffs(r, v512)
    row = lax.broadcasted_iota(jnp.int32, (tt, D_RG), 0)
    mult = jnp.where(row + j * tt == 0, 1.0, mult)
    bterm = mult * i_gate * xc

    a_pref, b_pref = _group_scan(a, bterm, row & (SUBLANES - 1))
    carry = h_ref[SUBLANES - 1:SUBLANES, :]
    h_groups = []
    for g in range(tt // SUBLANES):
        lo = g * SUBLANES
        hg = a_pref[lo:lo + SUBLANES, :] * carry + b_pref[lo:lo + SUBLANES, :]
        carry = hg[SUBLANES - 1:SUBLANES, :]
        h_groups.append(hg)
    h_ref[...] = h_groups[-1]
    lru_ref[...] = carry
    rg_out = _cat(h_groups) * jax.nn.gelu(yg)

    vrows_ref[...] = v[tt - CHUNK:, :]
    vn = _layer_norm(v, v512[_V_LN_G:_V_LN_G + 1, :], v512[_V_LN_B:_V_LN_B + 1, :])
    lane = lax.broadcasted_iota(jnp.int32, (CHUNK, 2 * SG_HEAD_DIM), 1)
    t_idx = lax.broadcasted_iota(jnp.int32, (CHUNK, 2 * CHUNK), 0)
    s_idx = lax.broadcasted_iota(jnp.int32, (CHUNK, 2 * CHUNK), 1) & (CHUNK - 1)
    causal = s_idx <= t_idx
    ws_pairs = [jnp.where(causal, wsp_ref[p], jnp.zeros((), BF16)) for p in range(SG_HEADS // 2)]
    mixed_chunks = []
    for c in range(tt // CHUNK):
        vc = vn[c * CHUNK:(c + 1) * CHUNK, :]
        outs = []
        for p in range(SG_HEADS // 2):
            vp = vc[:, p * 2 * SG_HEAD_DIM:(p + 1) * 2 * SG_HEAD_DIM]
            rhs = jnp.concatenate([jnp.where(lane < SG_HEAD_DIM, vp, 0.0),
                                   jnp.where(lane >= SG_HEAD_DIM, vp, 0.0)], axis=0).astype(BF16)
            outs.append(_dot(ws_pairs[p], rhs))
        mixed_chunks.append(_cat(outs, 1) + bias_s_ref[...])
    sg_out = u * _cat(mixed_chunks)

    merged = jnp.concatenate(
        [_rms(rg_out, v512[_V_G_RG:_V_G_RG + 1, :]).astype(BF16),
         _rms(sg_out, v512[_V_G_SG:_V_G_SG + 1, :]).astype(BF16)], axis=1)

    def store_y(_, val):
        y_ref[...] = val

    _tail([x], merged, mods, v1024, w_out_ref, w_ff1_ref, w_ff2_ref, store_y)


def _sample_kernel(x_ref, ada_ref, cs_ref, h0_ref, v1024_ref, v512_ref, w_in_ref, wg_ref, w8_ref, b8_ref,
                   w_out_ref, w_ff1_ref, w_ff2_ref,
                   y_ref, conv_ref, lru_ref, vrows_ref):
    n_t, nb = x_ref.shape[0], x_ref.shape[1]
    mods = _split_mods(ada_ref[...])
    sh_m, sc_m = mods[0], mods[1]
    v1024 = v1024_ref[...]
    v512 = v512_ref[...]

    def blk(arr, t):
        return arr[t * nb:(t + 1) * nb, :]

    x_blocks = [x_ref[t] for t in range(n_t)]
    hm = _modulated_norm(x_blocks, v1024[0:1, :], sc_m, sh_m)
    proj = _dot(hm, w_in_ref[...])
    xr = proj[:, 0:D_RG]
    yg = proj[:, D_RG:2 * D_RG]
    u = proj[:, 2 * D_RG:2 * D_RG + D_SG]
    v = proj[:, 2 * D_RG + D_SG:]

    xp = [cs_ref[k] for k in range(CONV_W - 1)] + [blk(xr, t) for t in range(n_t)]
    xc_blocks = []
    for t in range(n_t):
        acc = v512[_V_CONV_B:_V_CONV_B + 1, :]
        for k in range(CONV_W):
            acc = acc + xp[t + k] * v512[_V_CONV_W + k:_V_CONV_W + k + 1, :]
        xc_blocks.append(acc)
    for k in range(CONV_W - 1):
        conv_ref[k] = xp[n_t + k]
    xc = _cat(xc_blocks)

    r, i_gate = _gates(xc, v512, wg_ref)
    a, mult = _lru_coeffs(r, v512)
    bterm = mult * i_gate * xc
    h = h0_ref[...]
    h_blocks = []
    for t in range(n_t):
        h = blk(a, t) * h + blk(bterm, t)
        h_blocks.append(h)
    lru_ref[...] = h
    rg_out = _cat(h_blocks) * jax.nn.gelu(yg)

    vn = _layer_norm(v, v512[_V_LN_G:_V_LN_G + 1, :], v512[_V_LN_B:_V_LN_B + 1, :])
    sg_blocks = []
    for t in range(n_t):
        vrows_ref[t] = blk(v, t)
        mixed = b8_ref[t:t + 1, :]
        for s in range(t + 1):
            mixed = mixed + w8_ref[t, s:s + 1, :] * blk(vn, s)
        sg_blocks.append(blk(u, t) * mixed)
    sg_out = _cat(sg_blocks)

    merged = jnp.concatenate(
        [_rms(rg_out, v512[_V_G_RG:_V_G_RG + 1, :]).astype(BF16),
         _rms(sg_out, v512[_V_G_SG:_V_G_SG + 1, :]).astype(BF16)], axis=1)

    def store_y(t, val):
        y_ref[t] = val

    _tail(x_blocks, merged, mods, v1024, w_out_ref, w_ff1_ref, w_ff2_ref, store_y)


def _ada_kernel(c_ref, w_ref, b_ref, o_ref):
    c = c_ref[...]
    s = (c * jax.nn.sigmoid(c)).astype(BF16)
    o_ref[...] = _dot(s, w_ref[...].astype(BF16)) + b_ref[...]


def _resident(shape):
    nd = len(shape)
    return pl.BlockSpec(shape, lambda *_: (0,) * nd, pipeline_mode=pl.Buffered(1))


def _block_diag_groups(w):
    per = RG_HEADS // GATE_GROUPS
    w4 = w.reshape(GATE_GROUPS, per, RG_HEAD_DIM, RG_HEAD_DIM)
    bd = jnp.einsum('ghij,hk->ghikj', w4, jnp.eye(per, dtype=w.dtype))
    return bd.reshape(GATE_GROUPS, GATE_W, GATE_W)


def kernel(x_prompt, x_sample, c_prompt, c_sample, state_conv, state_rglru, w_ada, b_ada, g_mix, g_ffn, w_in, conv_w, conv_b, w_ra, b_ra, w_ri, b_ri, lru_l, ln_v_g, ln_v_b, w_s, b_s, g_rg, g_sg, w_out, w_ff1, w_ff2, g_final):
    batch, seq, _ = x_prompt.shape
    dec_batch, dec_seq, _ = x_sample.shape
    assert w_ada.shape[0] == 1, "single-layer step"
    assert seq % PROMPT_TILE == 0 and PROMPT_TILE % CHUNK == 0 and dec_batch % SAMPLE_NB == 0
    assert dec_seq <= CHUNK

    c_all = jnp.concatenate([c_prompt, c_sample], axis=0)
    n_c = c_all.shape[0]
    ada = pl.pallas_call(
        _ada_kernel,
        out_shape=jax.ShapeDtypeStruct((n_c, 6 * D_MODEL), F32),
        grid=(6 * D_MODEL // ADA_TN,),
        in_specs=[pl.BlockSpec((n_c, D_MODEL), lambda n: (0, 0)),
                  pl.BlockSpec((D_MODEL, ADA_TN), lambda n: (0, n)),
                  pl.BlockSpec((1, ADA_TN), lambda n: (0, n))],
        out_specs=pl.BlockSpec((n_c, ADA_TN), lambda n: (0, n)),
        compiler_params=pltpu.CompilerParams(dimension_semantics=("arbitrary",),
                                             vmem_limit_bytes=VMEM_LIMIT_BYTES),
        name="ada_proj",
    )(c_all, w_ada[0], b_ada)
    ada_p = ada[:batch].reshape(batch, 1, 6 * D_MODEL)
    ada_s = ada[batch:]

    v1024 = jnp.stack([g_mix[0], g_ffn[0], g_final], axis=0)
    v512 = jnp.concatenate(
        [conv_w[0], conv_b, b_ra[0].reshape(1, D_RG), b_ri[0].reshape(1, D_RG), lru_l,
         ln_v_g, ln_v_b, g_rg, g_sg], axis=0)
    w_in_b = w_in[0].astype(BF16)
    w_out_b = w_out[0].astype(BF16)
    w_ff1_b = w_ff1[0].astype(BF16)
    w_ff2_b = w_ff2[0].astype(BF16)
    wg = jnp.concatenate([_block_diag_groups(w_ra[0]), _block_diag_groups(w_ri[0])], axis=-1).astype(BF16)
    wsp = (w_s[0].reshape(SG_HEADS // 2, 2, CHUNK, CHUNK).transpose(0, 2, 1, 3)
           .reshape(SG_HEADS // 2, CHUNK, 2 * CHUNK).astype(BF16))
    bias_s = jnp.repeat(b_s[0].T, SG_HEAD_DIM, axis=1)
    w8 = jnp.repeat(w_s[0, :, :dec_seq, :dec_seq].transpose(1, 2, 0), SG_HEAD_DIM, axis=2)
    b8 = bias_s[:dec_seq]

    weights = (w_in_b, wg)
    weight_specs = [_resident(w_in_b.shape), _resident(wg.shape)]
    tail_weights = (w_out_b, w_ff1_b, w_ff2_b)
    tail_specs = [_resident(w.shape) for w in tail_weights]
    cparams = functools.partial(pltpu.CompilerParams, vmem_limit_bytes=VMEM_LIMIT_BYTES)

    tt = PROMPT_TILE
    y_p, conv8_p, lru_p, vrows_p = pl.pallas_call(
        _prompt_kernel,
        out_shape=(jax.ShapeDtypeStruct((batch, seq, D_MODEL), F32),
                   jax.ShapeDtypeStruct((batch, SUBLANES, D_RG), F32),
                   jax.ShapeDtypeStruct((batch, 1, D_RG), F32),
                   jax.ShapeDtypeStruct((batch, CHUNK, D_SG), F32)),
        grid=(batch, seq // tt),
        in_specs=[pl.BlockSpec((None, tt, D_MODEL), lambda b, j: (b, j, 0)),
                  pl.BlockSpec((None, 1, 6 * D_MODEL), lambda b, j: (b, 0, 0)),
                  _resident(v1024.shape), _resident(v512.shape)] + weight_specs +
                 [_resident(wsp.shape), _resident(bias_s.shape)] + tail_specs,
        out_specs=(pl.BlockSpec((None, tt, D_MODEL), lambda b, j: (b, j, 0)),
                   pl.BlockSpec((None, SUBLANES, D_RG), lambda b, j: (b, 0, 0)),
                   pl.BlockSpec((None, 1, D_RG), lambda b, j: (b, 0, 0)),
                   pl.BlockSpec((None, CHUNK, D_SG), lambda b, j: (b, 0, 0))),
        scratch_shapes=[pltpu.VMEM((tt + SUBLANES, D_RG), F32), pltpu.VMEM((SUBLANES, D_RG), F32)],
        compiler_params=cparams(dimension_semantics=("arbitrary", "arbitrary")),
        name="prompt_trunk",
    )(x_prompt, ada_p, v1024, v512, *weights, wsp, bias_s, *tail_weights)
    conv_p = conv8_p[:, SUBLANES - (CONV_W - 1):, :][None]
    lru_p = lru_p.reshape(1, batch, D_RG)
    vrows_p = vrows_p[None]

    nb = SAMPLE_NB
    x_s = x_sample.transpose(1, 0, 2)
    cs = state_conv[0].transpose(1, 0, 2)
    y_s, conv_s, lru_s, vrows_s = pl.pallas_call(
        _sample_kernel,
        out_shape=(jax.ShapeDtypeStruct((dec_seq, dec_batch, D_MODEL), F32),
                   jax.ShapeDtypeStruct((CONV_W - 1, dec_batch, D_RG), F32),
                   jax.ShapeDtypeStruct((dec_batch, D_RG), F32),
                   jax.ShapeDtypeStruct((dec_seq, dec_batch, D_SG), F32)),
        grid=(dec_batch // nb,),
        in_specs=[pl.BlockSpec((dec_seq, nb, D_MODEL), lambda i: (0, i, 0)),
                  pl.BlockSpec((nb, 6 * D_MODEL), lambda i: (i, 0)),
                  pl.BlockSpec((CONV_W - 1, nb, D_RG), lambda i: (0, i, 0)),
                  pl.BlockSpec((nb, D_RG), lambda i: (i, 0)),
                  _resident(v1024.shape), _resident(v512.shape)] + weight_specs +
                 [_resident(w8.shape), _resident(b8.shape)] + tail_specs,
        out_specs=(pl.BlockSpec((dec_seq, nb, D_MODEL), lambda i: (0, i, 0)),
                   pl.BlockSpec((CONV_W - 1, nb, D_RG), lambda i: (0, i, 0)),
                   pl.BlockSpec((nb, D_RG), lambda i: (i, 0)),
                   pl.BlockSpec((dec_seq, nb, D_SG), lambda i: (0, i, 0))),
        compiler_params=cparams(dimension_semantics=("arbitrary",)),
        name="sample_trunk",
    )(x_s, ada_s, cs, state_rglru[0], v1024, v512, *weights, w8, b8, *tail_weights)

    return (y_p, y_s.transpose(1, 0, 2), conv_p, lru_p, vrows_p,
            conv_s.transpose(1, 0, 2)[None], lru_s[None], vrows_s.transpose(1, 0, 2)[None])
```

```python
import functools

import jax
import jax.numpy as jnp
from jax import lax
from jax.experimental import pallas as pl
from jax.experimental.pallas import tpu as pltpu

D_MODEL = 1024
D_RG = 512
D_SG = 512
D_IN = 2 * D_RG + 2 * D_SG
D_FF = 4 * D_MODEL
RG_HEADS = 8
RG_HEAD_DIM = D_RG // RG_HEADS
SG_HEADS = 8
SG_HEAD_DIM = D_SG // SG_HEADS
CHUNK = 128
CONV_W = 4
LRU_C = 8.0
EPS = 1e-6

SUBLANES = 8
GATE_GROUPS = 2
GATE_W = D_RG // GATE_GROUPS
PROMPT_TILE = 512
SAMPLE_NB = 64
FF_CHUNK = 1024
ADA_TN = 1024
VMEM_LIMIT_BYTES = 60000 * 1024

F32 = jnp.float32
BF16 = jnp.bfloat16

_V_CONV_W, _V_CONV_B, _V_B_RA, _V_B_RI, _V_LRU_L, _V_LN_G, _V_LN_B, _V_G_RG, _V_G_SG = 0, 4, 5, 6, 7, 8, 9, 10, 11


def _dot(a, b):
    return jnp.dot(a, b, preferred_element_type=F32)


def _rms(x, g):
    ms = jnp.mean(x * x, axis=-1, keepdims=True)
    return (x * lax.rsqrt(ms + EPS)) * g


def _gelu_tanh(x):
    k = 2.0 * 0.7978845608028654
    w = x * (k + (k * 0.044715) * (x * x))
    return x / (1.0 + jnp.exp(-w))


def _layer_norm(x, g, b):
    mu = jnp.mean(x, axis=-1, keepdims=True)
    xc = x - mu
    var = jnp.mean(xc * xc, axis=-1, keepdims=True)
    return (xc * lax.rsqrt(var + EPS)) * g + b


def _cat(parts, axis=0):
    return parts[0] if len(parts) == 1 else jnp.concatenate(parts, axis=axis)


def _split_mods(ada):
    return [ada[:, k * D_MODEL:(k + 1) * D_MODEL] for k in range(6)]


def _modulated_norm(x_blocks, g, sc, sh):
    gain = g * (1.0 + sc)
    return _cat([(_rms(xb, gain) + sh).astype(BF16) for xb in x_blocks])


def _gates(xc, v512, wg_ref):
    xcb = xc.astype(BF16)
    r_parts, i_parts = [], []
    for g in range(GATE_GROUPS):
        lo, hi = g * GATE_W, (g + 1) * GATE_W
        pre = _dot(xcb[:, lo:hi], wg_ref[g])
        r_parts.append(jax.nn.sigmoid(pre[:, :GATE_W] + v512[_V_B_RA:_V_B_RA + 1, lo:hi]))
        i_parts.append(jax.nn.sigmoid(pre[:, GATE_W:] + v512[_V_B_RI:_V_B_RI + 1, lo:hi]))
    return _cat(r_parts, 1), _cat(i_parts, 1)


def _lru_coeffs(r, v512):
    lam = v512[_V_LRU_L:_V_LRU_L + 1, :]
    log_sig = jnp.minimum(lam, 0.0) - jnp.log1p(jnp.exp(-jnp.abs(lam)))
    log_a = (LRU_C * r) * log_sig
    a = jnp.exp(log_a)
    z = 1.0 - a * a
    mult = jnp.where(z > 0.0, z * lax.rsqrt(z), 0.0)
    return a, mult


def _tail(x_blocks, merged, mods, v1024, w_out_ref, w_ff1_ref, w_ff2_ref, store_y, fillers=()):
    _, _, gt_m, sh_f, sc_f, gt_f = mods
    rows = x_blocks[0].shape[0]
    fillers = list(fillers)
    n_chunks = D_FF // FF_CHUNK

    def run_fillers(n):
        for _ in range(min(n, len(fillers))):
            fillers.pop(0)()

    def ff1(c):
        return _dot(hf, w_ff1_ref[:, c * FF_CHUNK:(c + 1) * FF_CHUNK])

    mm = _dot(merged, w_out_ref[...])
    run_fillers(1)
    x1 = [xb + gt_m * mm[i * rows:(i + 1) * rows] for i, xb in enumerate(x_blocks)]
    hf = _modulated_norm(x1, v1024[1:2, :], sc_f, sh_f)
    run_fillers(1)
    per_chunk = -(-len(fillers) // n_chunks)
    acc = None
    pre = ff1(0)
    for c in range(n_chunks):
        h1 = jnp.maximum(pre.astype(BF16), 0.0)
        h1 = h1 * h1
        if c + 1 < n_chunks:
            pre = ff1(c + 1)
        run_fillers(per_chunk)
        part = _dot(h1, w_ff2_ref[c * FF_CHUNK:(c + 1) * FF_CHUNK, :])
        acc = part if acc is None else acc + part
    run_fillers(len(fillers))
    for i, xb in enumerate(x1):
        x2 = xb + gt_f * acc[i * rows:(i + 1) * rows]
        store_y(i, _rms(x2, v1024[2:3, :]))


def _group_scan(a, b, tmod):
    rows, lanes = a.shape
    shape3 = (rows // SUBLANES, SUBLANES, lanes)
    a, b, tmod = a.reshape(shape3), b.reshape(shape3), tmod.reshape(shape3)
    s = 1
    while s < SUBLANES:
        a_sh = pltpu.roll(a, s, 1)
        b_sh = pltpu.roll(b, s, 1)
        keep = tmod >= s
        b = jnp.where(keep, a * b_sh + b, b)
        a = jnp.where(keep, a * a_sh, a)
        s *= 2
    return a.reshape(rows, lanes), b.reshape(rows, lanes)


def _prompt_mix_thunks(x_ref, ada_ref, v1024, v512, w_in_ref, wg_ref, wsp_ref, bias_s_ref,
                       conv_ref, lru_ref, vrows_ref, ext_ref, h_ref, merged_ref, j):
    tt = x_ref.shape[0]
    st = {}

    def norm():
        mods = _split_mods(ada_ref[...])
        st['hm'] = _modulated_norm([x_ref[...]], v1024[0:1, :], mods[1], mods[0])

    def project():
        st['proj'] = proj = _dot(st['hm'], w_in_ref[...])
        ext_ref[SUBLANES:, :] = proj[:, 0:D_RG]
        st['carry'] = h_ref[SUBLANES - 1:SUBLANES, :]

    def recurrent(c):
        r0 = c * CHUNK
        row = lax.broadcasted_iota(jnp.int32, (CHUNK, D_RG), 0)
        yg = st['proj'][r0:r0 + CHUNK, D_RG:2 * D_RG]
        xc = v512[_V_CONV_B:_V_CONV_B + 1, :]
        for k in range(CONV_W):
            off = r0 + SUBLANES - (CONV_W - 1) + k
            xc = xc + ext_ref[off:off + CHUNK, :] * v512[_V_CONV_W + k:_V_CONV_W + k + 1, :]
        r, i_gate = _gates(xc, v512, wg_ref)
        a, mult = _lru_coeffs(r, v512)
        mult = jnp.where(row + (j * tt + r0) == 0, 1.0, mult)
        bterm = mult * i_gate * xc
        a_pref, b_pref = _group_scan(a, bterm, row & (SUBLANES - 1))
        carry = st['carry']
        h_groups = []
        for g in range(CHUNK // SUBLANES):
            lo = g * SUBLANES
            hg = a_pref[lo:lo + SUBLANES, :] * carry + b_pref[lo:lo + SUBLANES, :]
            carry = hg[SUBLANES - 1:SUBLANES, :]
            h_groups.append(hg)
        st['carry'] = carry
        rg_out = _cat(h_groups) * _gelu_tanh(yg)
        merged_ref[r0:r0 + CHUNK, 0:D_RG] = _rms(rg_out, v512[_V_G_RG:_V_G_RG + 1, :]).astype(BF16)
        if r0 + CHUNK == tt:
            h_ref[...] = h_groups[-1]
            lru_ref[...] = carry
            tail = ext_ref[tt:tt + SUBLANES, :]
            ext_ref[0:SUBLANES, :] = tail
            conv_ref[...] = tail

    def spatial(c):
        r0 = c * CHUNK
        lane = lax.broadcasted_iota(jnp.int32, (CHUNK, 2 * SG_HEAD_DIM), 1)
        t_idx = lax.broadcasted_iota(jnp.int32, (CHUNK, 2 * CHUNK), 0)
        s_idx = lax.broadcasted_iota(jnp.int32, (CHUNK, 2 * CHUNK), 1) & (CHUNK - 1)
        u = st['proj'][r0:r0 + CHUNK, 2 * D_RG:2 * D_RG + D_SG]
        v = st['proj'][r0:r0 + CHUNK, 2 * D_RG + D_SG:]
        vn = _layer_norm(v, v512[_V_LN_G:_V_LN_G + 1, :], v512[_V_LN_B:_V_LN_B + 1, :])
        outs = []
        for p in range(SG_HEADS // 2):
            vp = vn[:, p * 2 * SG_HEAD_DIM:(p + 1) * 2 * SG_HEAD_DIM]
            rhs = jnp.concatenate([jnp.where(lane < SG_HEAD_DIM, vp, 0.0),
                                   jnp.where(lane >= SG_HEAD_DIM, vp, 0.0)], axis=0).astype(BF16)
            ws_pair = jnp.where(s_idx <= t_idx, wsp_ref[p], jnp.zeros((), BF16))
            outs.append(_dot(ws_pair, rhs))
        sg_out = u * (_cat(outs, 1) + bias_s_ref[...])
        merged_ref[r0:r0 + CHUNK, D_RG:] = _rms(sg_out, v512[_V_G_SG:_V_G_SG + 1, :]).astype(BF16)
        if r0 + CHUNK == tt:
            vrows_ref[...] = v

    thunks = [norm, project]
    for c in range(tt // CHUNK):
        thunks += [functools.partial(recurrent, c), functools.partial(spatial, c)]
    return thunks


def _prompt_kernel(x_ref, xprev_ref, ada_ref, adaprev_ref, v1024_ref, v512_ref, w_in_ref, wg_ref, wsp_ref,
                   bias_s_ref, w_out_ref, w_ff1_ref, w_ff2_ref,
                   y_ref, conv_ref, lru_ref, vrows_ref,
                   ext_ref, h_ref, merged_ref, *, tiles_per_seq):
    s = pl.program_id(0)
    j = lax.rem(s, tiles_per_seq)

    @pl.when(s == 0)
    def _():
        merged_ref[...] = jnp.zeros(merged_ref.shape, BF16)

    @pl.when(j == 0)
    def _():
        ext_ref[0:SUBLANES, :] = jnp.zeros((SUBLANES, D_RG), F32)
        h_ref[...] = jnp.zeros((SUBLANES, D_RG), F32)

    v1024 = v1024_ref[...]
    v512 = v512_ref[...]

    def store_y(_, val):
        y_ref[...] = val

    stage_a = _prompt_mix_thunks(x_ref, ada_ref, v1024, v512, w_in_ref, wg_ref, wsp_ref, bias_s_ref,
                                 conv_ref, lru_ref, vrows_ref, ext_ref, h_ref, merged_ref, j)
    stage_a.pop(0)()
    _tail([xprev_ref[...]], merged_ref[...], _split_mods(adaprev_ref[...]), v1024,
          w_out_ref, w_ff1_ref, w_ff2_ref, store_y, fillers=stage_a)


def _sample_kernel(x_ref, ada_ref, cs_ref, h0_ref, v1024_ref, v512_ref, w_in_ref, wg_ref, w8_ref, b8_ref,
                   w_out_ref, w_ff1_ref, w_ff2_ref,
                   y_ref, conv_ref, lru_ref, vrows_ref):
    n_t, nb = x_ref.shape[0], x_ref.shape[1]
    mods = _split_mods(ada_ref[...])
    sh_m, sc_m = mods[0], mods[1]
    v1024 = v1024_ref[...]
    v512 = v512_ref[...]

    def blk(arr, t):
        return arr[t * nb:(t + 1) * nb, :]

    x_blocks = [x_ref[t] for t in range(n_t)]
    hm = _modulated_norm(x_blocks, v1024[0:1, :], sc_m, sh_m)
    proj = _dot(hm, w_in_ref[...])
    xr = proj[:, 0:D_RG]
    yg = proj[:, D_RG:2 * D_RG]
    u = proj[:, 2 * D_RG:2 * D_RG + D_SG]
    v = proj[:, 2 * D_RG + D_SG:]

    xp = [cs_ref[k] for k in range(CONV_W - 1)] + [blk(xr, t) for t in range(n_t)]
    xc_blocks = []
    for t in range(n_t):
        acc = v512[_V_CONV_B:_V_CONV_B + 1, :]
        for k in range(CONV_W):
            acc = acc + xp[t + k] * v512[_V_CONV_W + k:_V_CONV_W + k + 1, :]
        xc_blocks.append(acc)
    for k in range(CONV_W - 1):
        conv_ref[k] = xp[n_t + k]
    xc = _cat(xc_blocks)

    r, i_gate = _gates(xc, v512, wg_ref)
    a, mult = _lru_coeffs(r, v512)
    bterm = mult * i_gate * xc
    h = h0_ref[...]
    h_blocks = []
    for t in range(n_t):
        h = blk(a, t) * h + blk(bterm, t)
        h_blocks.append(h)
    lru_ref[...] = h
    rg_out = _cat(h_blocks) * _gelu_tanh(yg)

    vn = _layer_norm(v, v512[_V_LN_G:_V_LN_G + 1, :], v512[_V_LN_B:_V_LN_B + 1, :])
    sg_blocks = []
    for t in range(n_t):
        vrows_ref[t] = blk(v, t)
        mixed = b8_ref[t:t + 1, :]
        for s in range(t + 1):
            mixed = mixed + w8_ref[t, s:s + 1, :] * blk(vn, s)
        sg_blocks.append(blk(u, t) * mixed)
    sg_out = _cat(sg_blocks)

    merged = jnp.concatenate(
        [_rms(rg_out, v512[_V_G_RG:_V_G_RG + 1, :]).astype(BF16),
         _rms(sg_out, v512[_V_G_SG:_V_G_SG + 1, :]).astype(BF16)], axis=1)

    def store_y(t, val):
        y_ref[t] = val

    _tail(x_blocks, merged, mods, v1024, w_out_ref, w_ff1_ref, w_ff2_ref, store_y)


def _ada_kernel(c_ref, w_ref, b_ref, o_ref):
    c = c_ref[...]
    s = (c * jax.nn.sigmoid(c)).astype(BF16)
    o_ref[...] = _dot(s, w_ref[...].astype(BF16)) + b_ref[...]


def _resident(shape):
    nd = len(shape)
    return pl.BlockSpec(shape, lambda *_: (0,) * nd, pipeline_mode=pl.Buffered(1))


def _block_diag_groups(w):
    per = RG_HEADS // GATE_GROUPS
    w4 = w.reshape(GATE_GROUPS, per, RG_HEAD_DIM, RG_HEAD_DIM)
    bd = jnp.einsum('ghij,hk->ghikj', w4, jnp.eye(per, dtype=w.dtype))
    return bd.reshape(GATE_GROUPS, GATE_W, GATE_W)


def kernel(x_prompt, x_sample, c_prompt, c_sample, state_conv, state_rglru, w_ada, b_ada, g_mix, g_ffn, w_in, conv_w, conv_b, w_ra, b_ra, w_ri, b_ri, lru_l, ln_v_g, ln_v_b, w_s, b_s, g_rg, g_sg, w_out, w_ff1, w_ff2, g_final):
    batch, seq, _ = x_prompt.shape
    dec_batch, dec_seq, _ = x_sample.shape
    assert w_ada.shape[0] == 1, "single-layer step"
    assert seq % PROMPT_TILE == 0 and PROMPT_TILE % CHUNK == 0 and dec_batch % SAMPLE_NB == 0
    assert dec_seq <= CHUNK

    c_all = jnp.concatenate([c_prompt, c_sample], axis=0)
    n_c = c_all.shape[0]
    ada = pl.pallas_call(
        _ada_kernel,
        out_shape=jax.ShapeDtypeStruct((n_c, 6 * D_MODEL), F32),
        grid=(6 * D_MODEL // ADA_TN,),
        in_specs=[pl.BlockSpec((n_c, D_MODEL), lambda n: (0, 0)),
                  pl.BlockSpec((D_MODEL, ADA_TN), lambda n: (0, n)),
                  pl.BlockSpec((1, ADA_TN), lambda n: (0, n))],
        out_specs=pl.BlockSpec((n_c, ADA_TN), lambda n: (0, n)),
        compiler_params=pltpu.CompilerParams(dimension_semantics=("arbitrary",),
                                             vmem_limit_bytes=VMEM_LIMIT_BYTES),
        name="ada_proj",
    )(c_all, w_ada[0], b_ada)
    ada_p = ada[:batch].reshape(batch, 1, 6 * D_MODEL)
    ada_s = ada[batch:]

    v1024 = jnp.stack([g_mix[0], g_ffn[0], g_final], axis=0)
    v512 = jnp.concatenate(
        [conv_w[0], conv_b, b_ra[0].reshape(1, D_RG), b_ri[0].reshape(1, D_RG), lru_l,
         ln_v_g, ln_v_b, g_rg, g_sg], axis=0)
    w_in_b = w_in[0].astype(BF16)
    w_out_b = w_out[0].astype(BF16)
    w_ff1_b = w_ff1[0].astype(BF16)
    w_ff2_b = w_ff2[0].astype(BF16)
    wg = jnp.concatenate([_block_diag_groups(w_ra[0]), _block_diag_groups(w_ri[0])], axis=-1).astype(BF16)
    wsp = (w_s[0].reshape(SG_HEADS // 2, 2, CHUNK, CHUNK).transpose(0, 2, 1, 3)
           .reshape(SG_HEADS // 2, CHUNK, 2 * CHUNK).astype(BF16))
    bias_s = jnp.repeat(b_s[0].T, SG_HEAD_DIM, axis=1)
    w8 = jnp.repeat(w_s[0, :, :dec_seq, :dec_seq].transpose(1, 2, 0), SG_HEAD_DIM, axis=2)
    b8 = bias_s[:dec_seq]

    weights = (w_in_b, wg)
    weight_specs = [_resident(w_in_b.shape), _resident(wg.shape)]
    tail_weights = (w_out_b, w_ff1_b, w_ff2_b)
    tail_specs = [_resident(w.shape) for w in tail_weights]
    cparams = functools.partial(pltpu.CompilerParams, vmem_limit_bytes=VMEM_LIMIT_BYTES)

    tt = PROMPT_TILE
    nt = seq // tt
    n_tiles = batch * nt

    def cur(s):
        return jnp.minimum(s, n_tiles - 1)

    def prev(s):
        return jnp.maximum(s - 1, 0)

    y_p, conv8_p, lru_p, vrows_p = pl.pallas_call(
        functools.partial(_prompt_kernel, tiles_per_seq=nt),
        out_shape=(jax.ShapeDtypeStruct((batch, seq, D_MODEL), F32),
                   jax.ShapeDtypeStruct((n_tiles + 1, SUBLANES, D_RG), F32),
                   jax.ShapeDtypeStruct((n_tiles + 1, 1, D_RG), F32),
                   jax.ShapeDtypeStruct((n_tiles + 1, CHUNK, D_SG), F32)),
        grid=(n_tiles + 1,),
        in_specs=[pl.BlockSpec((None, tt, D_MODEL), lambda s: (cur(s) // nt, cur(s) % nt, 0)),
                  pl.BlockSpec((None, tt, D_MODEL), lambda s: (prev(s) // nt, prev(s) % nt, 0)),
                  pl.BlockSpec((None, 1, 6 * D_MODEL), lambda s: (cur(s) // nt, 0, 0)),
                  pl.BlockSpec((None, 1, 6 * D_MODEL), lambda s: (prev(s) // nt, 0, 0)),
                  _resident(v1024.shape), _resident(v512.shape)] + weight_specs +
                 [_resident(wsp.shape), _resident(bias_s.shape)] + tail_specs,
        out_specs=(pl.BlockSpec((None, tt, D_MODEL), lambda s: (prev(s) // nt, prev(s) % nt, 0)),
                   pl.BlockSpec((None, SUBLANES, D_RG), lambda s: (s, 0, 0)),
                   pl.BlockSpec((None, 1, D_RG), lambda s: (s, 0, 0)),
                   pl.BlockSpec((None, CHUNK, D_SG), lambda s: (s, 0, 0))),
        scratch_shapes=[pltpu.VMEM((tt + SUBLANES, D_RG), F32), pltpu.VMEM((SUBLANES, D_RG), F32),
                        pltpu.VMEM((tt, D_MODEL), BF16)],
        compiler_params=cparams(dimension_semantics=("arbitrary",)),
        name="prompt_trunk",
    )(x_prompt, x_prompt, ada_p, ada_p, v1024, v512, *weights, wsp, bias_s, *tail_weights)
    conv_p = conv8_p[nt - 1:n_tiles:nt, SUBLANES - (CONV_W - 1):, :][None]
    lru_p = lru_p[nt - 1:n_tiles:nt].reshape(1, batch, D_RG)
    vrows_p = vrows_p[nt - 1:n_tiles:nt][None]

    nb = SAMPLE_NB
    x_s = x_sample.transpose(1, 0, 2)
    cs = state_conv[0].transpose(1, 0, 2)
    y_s, conv_s, lru_s, vrows_s = pl.pallas_call(
        _sample_kernel,
        out_shape=(jax.ShapeDtypeStruct((dec_seq, dec_batch, D_MODEL), F32),
                   jax.ShapeDtypeStruct((CONV_W - 1, dec_batch, D_RG), F32),
                   jax.ShapeDtypeStruct((dec_batch, D_RG), F32),
                   jax.ShapeDtypeStruct((dec_seq, dec_batch, D_SG), F32)),
        grid=(dec_batch // nb,),
        in_specs=[pl.BlockSpec((dec_seq, nb, D_MODEL), lambda i: (0, i, 0)),
                  pl.BlockSpec((nb, 6 * D_MODEL), lambda i: (i, 0)),
                  pl.BlockSpec((CONV_W - 1, nb, D_RG), lambda i: (0, i, 0)),
                  pl.BlockSpec((nb, D_RG), lambda i: (i, 0)),
                  _resident(v1024.shape), _resident(v512.shape)] + weight_specs +
                 [_resident(w8.shape), _resident(b8.shape)] + tail_specs,
        out_specs=(pl.BlockSpec((dec_seq, nb, D_MODEL), lambda i: (0, i, 0)),
                   pl.BlockSpec((CONV_W - 1, nb, D_RG), lambda i: (0, i, 0)),
                   pl.BlockSpec((nb, D_RG), lambda i: (i, 0)),
                   pl.BlockSpec((dec_seq, nb, D_SG), lambda i: (0, i, 0))),
        compiler_params=cparams(dimension_semantics=("arbitrary",)),
        name="sample_trunk",
    )(x_s, ada_s, cs, state_rglru[0], v1024, v512, *weights, w8, b8, *tail_weights)

    return (y_p, y_s.transpose(1, 0, 2), conv_p, lru_p, vrows_p,
            conv_s.transpose(1, 0, 2)[None], lru_s[None], vrows_s.transpose(1, 0, 2)[None])
```

```python
import functools

import jax
import jax.numpy as jnp
from jax import lax
from jax.experimental import pallas as pl
from jax.experimental.pallas import tpu as pltpu

D_MODEL = 1024
D_RG = 512
D_SG = 512
D_IN = 2 * D_RG + 2 * D_SG
D_FF = 4 * D_MODEL
RG_HEADS = 8
RG_HEAD_DIM = D_RG // RG_HEADS
SG_HEADS = 8
SG_HEAD_DIM = D_SG // SG_HEADS
CHUNK = 128
CONV_W = 4
LRU_C = 8.0
EPS = 1e-6

SUBLANES = 8
GATE_GROUPS = 2
GATE_W = D_RG // GATE_GROUPS
PROMPT_TILE = 512
SAMPLE_NB = 64
FF_CHUNK = 1024
ADA_TN = 1024
VMEM_LIMIT_BYTES = 60000 * 1024

F32 = jnp.float32
BF16 = jnp.bfloat16

_V_CONV_W, _V_CONV_B, _V_B_RA, _V_B_RI, _V_LRU_L, _V_LN_G, _V_LN_B, _V_G_RG, _V_G_SG = 0, 4, 5, 6, 7, 8, 9, 10, 11


def _dot(a, b):
    return jnp.dot(a, b, preferred_element_type=F32)


def _rms(x, g):
    ms = jnp.mean(x * x, axis=-1, keepdims=True)
    return (x * lax.rsqrt(ms + EPS)) * g


def _gelu_tanh(x):
    k = 2.0 * 0.7978845608028654
    w = x * (k + (k * 0.044715) * (x * x))
    return x / (1.0 + jnp.exp(-w))


def _layer_norm(x, g, b):
    mu = jnp.mean(x, axis=-1, keepdims=True)
    xc = x - mu
    var = jnp.mean(xc * xc, axis=-1, keepdims=True)
    return (xc * lax.rsqrt(var + EPS)) * g + b


def _cat(parts, axis=0):
    return parts[0] if len(parts) == 1 else jnp.concatenate(parts, axis=axis)


def _split_mods(ada):
    return [ada[:, k * D_MODEL:(k + 1) * D_MODEL] for k in range(6)]


def _modulated_norm(x_blocks, g, sc, sh):
    gain = g * (1.0 + sc)
    return _cat([(_rms(xb, gain) + sh).astype(BF16) for xb in x_blocks])


def _gates(xc, v512, wg_ref):
    xcb = xc.astype(BF16)
    r_parts, i_parts = [], []
    for g in range(GATE_GROUPS):
        lo, hi = g * GATE_W, (g + 1) * GATE_W
        pre = _dot(xcb[:, lo:hi], wg_ref[g])
        r_parts.append(jax.nn.sigmoid(pre[:, :GATE_W] + v512[_V_B_RA:_V_B_RA + 1, lo:hi]))
        i_parts.append(jax.nn.sigmoid(pre[:, GATE_W:] + v512[_V_B_RI:_V_B_RI + 1, lo:hi]))
    return _cat(r_parts, 1), _cat(i_parts, 1)


def _lru_coeffs(r, v512):
    lam = v512[_V_LRU_L:_V_LRU_L + 1, :]
    log_sig = jnp.minimum(lam, 0.0) - jnp.log1p(jnp.exp(-jnp.abs(lam)))
    log_a = (LRU_C * r) * log_sig
    a = jnp.exp(log_a)
    z = 1.0 - a * a
    mult = jnp.where(z > 0.0, z * lax.rsqrt(z), 0.0)
    return a, mult


def _tail(x_blocks, merged, mods, v1024, w_out_ref, w_ff1_ref, w_ff2_ref, store_y, fillers=()):
    _, _, gt_m, sh_f, sc_f, gt_f = mods
    rows = x_blocks[0].shape[0]
    fillers = list(fillers)
    n_chunks = D_FF // FF_CHUNK

    def run_fillers(n):
        for _ in range(min(n, len(fillers))):
            fillers.pop(0)()

    def ff1(c):
        return _dot(hf, w_ff1_ref[:, c * FF_CHUNK:(c + 1) * FF_CHUNK])

    mm = _dot(merged, w_out_ref[...])
    run_fillers(1)
    x1 = [xb + gt_m * mm[i * rows:(i + 1) * rows] for i, xb in enumerate(x_blocks)]
    hf = _modulated_norm(x1, v1024[1:2, :], sc_f, sh_f)
    run_fillers(1)
    per_chunk = -(-len(fillers) // n_chunks)
    acc = None
    pre = ff1(0)
    for c in range(n_chunks):
        h1 = jnp.maximum(pre.astype(BF16), 0.0)
        h1 = h1 * h1
        if c + 1 < n_chunks:
            pre = ff1(c + 1)
        run_fillers(per_chunk)
        part = _dot(h1, w_ff2_ref[c * FF_CHUNK:(c + 1) * FF_CHUNK, :])
        acc = part if acc is None else acc + part
    run_fillers(len(fillers))
    for i, xb in enumerate(x1):
        x2 = xb + gt_f * acc[i * rows:(i + 1) * rows]
        store_y(i, _rms(x2, v1024[2:3, :]))


def _group_scan(a, b, tmod):
    rows, lanes = a.shape
    shape3 = (rows // SUBLANES, SUBLANES, lanes)
    a, b, tmod = a.reshape(shape3), b.reshape(shape3), tmod.reshape(shape3)
    s = 1
    while s < SUBLANES:
        a_sh = pltpu.roll(a, s, 1)
        b_sh = pltpu.roll(b, s, 1)
        keep = tmod >= s
        b = jnp.where(keep, a * b_sh + b, b)
        a = jnp.where(keep, a * a_sh, a)
        s *= 2
    return a.reshape(rows, lanes), b.reshape(rows, lanes)


def _prompt_mix_thunks(x_ref, ada, v1024, v512, w_in_ref, wg_ref, wsp_ref, bias_s_ref,
                       conv_ref, lru_ref, vrows_ref, ext_ref, h_ref, merged_ref, j, slot):
    tt = x_ref.shape[0]
    st = {}

    def norm():
        mods = _split_mods(ada)
        st['hm'] = _modulated_norm([x_ref[...]], v1024[0:1, :], mods[1], mods[0])

    def project():
        st['proj'] = proj = _dot(st['hm'], w_in_ref[...])
        ext_ref[SUBLANES:, :] = proj[:, 0:D_RG]
        st['carry'] = h_ref[SUBLANES - 1:SUBLANES, :]

    def recurrent(c):
        r0 = c * CHUNK
        row = lax.broadcasted_iota(jnp.int32, (CHUNK, D_RG), 0)
        yg = st['proj'][r0:r0 + CHUNK, D_RG:2 * D_RG]
        xc = v512[_V_CONV_B:_V_CONV_B + 1, :]
        for k in range(CONV_W):
            off = r0 + SUBLANES - (CONV_W - 1) + k
            xc = xc + ext_ref[off:off + CHUNK, :] * v512[_V_CONV_W + k:_V_CONV_W + k + 1, :]
        r, i_gate = _gates(xc, v512, wg_ref)
        a, mult = _lru_coeffs(r, v512)
        mult = jnp.where(row + (j * tt + r0) == 0, 1.0, mult)
        bterm = mult * i_gate * xc
        a_pref, b_pref = _group_scan(a, bterm, row & (SUBLANES - 1))
        carry = st['carry']
        h_groups = []
        for g in range(CHUNK // SUBLANES):
            lo = g * SUBLANES
            hg = a_pref[lo:lo + SUBLANES, :] * carry + b_pref[lo:lo + SUBLANES, :]
            carry = hg[SUBLANES - 1:SUBLANES, :]
            h_groups.append(hg)
        st['carry'] = carry
        rg_out = _cat(h_groups) * _gelu_tanh(yg)
        merged_ref[r0:r0 + CHUNK, 0:D_RG] = _rms(rg_out, v512[_V_G_RG:_V_G_RG + 1, :]).astype(BF16)
        if r0 + CHUNK == tt:
            h_ref[...] = h_groups[-1]
            lru_ref[slot] = carry
            tail = ext_ref[tt:tt + SUBLANES, :]
            ext_ref[0:SUBLANES, :] = tail
            conv_ref[slot] = tail

    def spatial(c):
        r0 = c * CHUNK
        lane = lax.broadcasted_iota(jnp.int32, (CHUNK, 2 * SG_HEAD_DIM), 1)
        t_idx = lax.broadcasted_iota(jnp.int32, (CHUNK, 2 * CHUNK), 0)
        s_idx = lax.broadcasted_iota(jnp.int32, (CHUNK, 2 * CHUNK), 1) & (CHUNK - 1)
        u = st['proj'][r0:r0 + CHUNK, 2 * D_RG:2 * D_RG + D_SG]
        v = st['proj'][r0:r0 + CHUNK, 2 * D_RG + D_SG:]
        vn = _layer_norm(v, v512[_V_LN_G:_V_LN_G + 1, :], v512[_V_LN_B:_V_LN_B + 1, :])
        outs = []
        for p in range(SG_HEADS // 2):
            vp = vn[:, p * 2 * SG_HEAD_DIM:(p + 1) * 2 * SG_HEAD_DIM]
            rhs = jnp.concatenate([jnp.where(lane < SG_HEAD_DIM, vp, 0.0),
                                   jnp.where(lane >= SG_HEAD_DIM, vp, 0.0)], axis=0).astype(BF16)
            ws_pair = jnp.where(s_idx <= t_idx, wsp_ref[p], jnp.zeros((), BF16))
            outs.append(_dot(ws_pair, rhs))
        sg_out = u * (_cat(outs, 1) + bias_s_ref[...])
        merged_ref[r0:r0 + CHUNK, D_RG:] = _rms(sg_out, v512[_V_G_SG:_V_G_SG + 1, :]).astype(BF16)
        if r0 + CHUNK == tt:
            vrows_ref[slot] = v

    thunks = [norm, project]
    for c in range(tt // CHUNK):
        thunks += [functools.partial(recurrent, c), functools.partial(spatial, c)]
    return thunks


def _prompt_kernel(x_ref, xprev_ref, ada_ref, v1024_ref, v512_ref, w_in_ref, wg_ref, wsp_ref,
                   bias_s_ref, w_out_ref, w_ff1_ref, w_ff2_ref,
                   y_ref, conv_ref, lru_ref, vrows_ref,
                   ext_ref, h_ref, merged_ref, *, tiles_per_seq):
    s = pl.program_id(0)
    n_tiles = pl.num_programs(0) - 1
    n_seq = conv_ref.shape[0] - 1
    j = lax.rem(s, tiles_per_seq)
    seq_cur = jnp.minimum(s, n_tiles - 1) // tiles_per_seq
    seq_prev = jnp.maximum(s - 1, 0) // tiles_per_seq
    slot = jnp.where(s == n_tiles, n_seq, seq_cur)

    @pl.when(s == 0)
    def _():
        merged_ref[...] = jnp.zeros(merged_ref.shape, BF16)

    @pl.when(j == 0)
    def _():
        ext_ref[0:SUBLANES, :] = jnp.zeros((SUBLANES, D_RG), F32)
        h_ref[...] = jnp.zeros((SUBLANES, D_RG), F32)

    v1024 = v1024_ref[...]
    v512 = v512_ref[...]

    def store_y(_, val):
        y_ref[...] = val

    stage_a = _prompt_mix_thunks(x_ref, ada_ref[seq_cur], v1024, v512, w_in_ref, wg_ref, wsp_ref, bias_s_ref,
                                 conv_ref, lru_ref, vrows_ref, ext_ref, h_ref, merged_ref, j, slot)
    stage_a.pop(0)()
    _tail([xprev_ref[...]], merged_ref[...], _split_mods(ada_ref[seq_prev]), v1024,
          w_out_ref, w_ff1_ref, w_ff2_ref, store_y, fillers=stage_a)


def _sample_kernel(x_ref, ada_ref, cs_ref, h0_ref, v1024_ref, v512_ref, w_in_ref, wg_ref, w8_ref, b8_ref,
                   w_out_ref, w_ff1_ref, w_ff2_ref,
                   y_ref, conv_ref, lru_ref, vrows_ref):
    n_t, nb = x_ref.shape[0], x_ref.shape[1]
    mods = _split_mods(ada_ref[...])
    sh_m, sc_m = mods[0], mods[1]
    v1024 = v1024_ref[...]
    v512 = v512_ref[...]

    def blk(arr, t):
        return arr[t * nb:(t + 1) * nb, :]

    x_blocks = [x_ref[t] for t in range(n_t)]
    hm = _modulated_norm(x_blocks, v1024[0:1, :], sc_m, sh_m)
    proj = _dot(hm, w_in_ref[...])
    xr = proj[:, 0:D_RG]
    yg = proj[:, D_RG:2 * D_RG]
    u = proj[:, 2 * D_RG:2 * D_RG + D_SG]
    v = proj[:, 2 * D_RG + D_SG:]

    xp = [cs_ref[k] for k in range(CONV_W - 1)] + [blk(xr, t) for t in range(n_t)]
    xc_blocks = []
    for t in range(n_t):
        acc = v512[_V_CONV_B:_V_CONV_B + 1, :]
        for k in range(CONV_W):
            acc = acc + xp[t + k] * v512[_V_CONV_W + k:_V_CONV_W + k + 1, :]
        xc_blocks.append(acc)
    for k in range(CONV_W - 1):
        conv_ref[k] = xp[n_t + k]
    xc = _cat(xc_blocks)

    r, i_gate = _gates(xc, v512, wg_ref)
    a, mult = _lru_coeffs(r, v512)
    bterm = mult * i_gate * xc
    h = h0_ref[...]
    h_blocks = []
    for t in range(n_t):
        h = blk(a, t) * h + blk(bterm, t)
        h_blocks.append(h)
    lru_ref[...] = h
    rg_out = _cat(h_blocks) * _gelu_tanh(yg)

    vn = _layer_norm(v, v512[_V_LN_G:_V_LN_G + 1, :], v512[_V_LN_B:_V_LN_B + 1, :])
    sg_blocks = []
    for t in range(n_t):
        vrows_ref[t] = blk(v, t)
        mixed = b8_ref[t:t + 1, :]
        for s in range(t + 1):
            mixed = mixed + w8_ref[t, s:s + 1, :] * blk(vn, s)
        sg_blocks.append(blk(u, t) * mixed)
    sg_out = _cat(sg_blocks)

    merged = jnp.concatenate(
        [_rms(rg_out, v512[_V_G_RG:_V_G_RG + 1, :]).astype(BF16),
         _rms(sg_out, v512[_V_G_SG:_V_G_SG + 1, :]).astype(BF16)], axis=1)

    def store_y(t, val):
        y_ref[t] = val

    _tail(x_blocks, merged, mods, v1024, w_out_ref, w_ff1_ref, w_ff2_ref, store_y)


def _ada_kernel(c_ref, w_ref, b_ref, o_ref):
    c = c_ref[...]
    s = (c * jax.nn.sigmoid(c)).astype(BF16)
    o_ref[...] = _dot(s, w_ref[...].astype(BF16)) + b_ref[...]


def _resident(shape):
    nd = len(shape)
    return pl.BlockSpec(shape, lambda *_: (0,) * nd, pipeline_mode=pl.Buffered(1))


def _block_diag_groups(w):
    per = RG_HEADS // GATE_GROUPS
    w4 = w.reshape(GATE_GROUPS, per, RG_HEAD_DIM, RG_HEAD_DIM)
    bd = jnp.einsum('ghij,hk->ghikj', w4, jnp.eye(per, dtype=w.dtype))
    return bd.reshape(GATE_GROUPS, GATE_W, GATE_W)


def kernel(x_prompt, x_sample, c_prompt, c_sample, state_conv, state_rglru, w_ada, b_ada, g_mix, g_ffn, w_in, conv_w, conv_b, w_ra, b_ra, w_ri, b_ri, lru_l, ln_v_g, ln_v_b, w_s, b_s, g_rg, g_sg, w_out, w_ff1, w_ff2, g_final):
    batch, seq, _ = x_prompt.shape
    dec_batch, dec_seq, _ = x_sample.shape
    assert w_ada.shape[0] == 1, "single-layer step"
    assert seq % PROMPT_TILE == 0 and PROMPT_TILE % CHUNK == 0 and dec_batch % SAMPLE_NB == 0
    assert dec_seq <= CHUNK

    c_all = jnp.concatenate([c_prompt, c_sample], axis=0)
    n_c = c_all.shape[0]
    ada = pl.pallas_call(
        _ada_kernel,
        out_shape=jax.ShapeDtypeStruct((n_c, 6 * D_MODEL), F32),
        grid=(6 * D_MODEL // ADA_TN,),
        in_specs=[pl.BlockSpec((n_c, D_MODEL), lambda n: (0, 0)),
                  pl.BlockSpec((D_MODEL, ADA_TN), lambda n: (0, n)),
                  pl.BlockSpec((1, ADA_TN), lambda n: (0, n))],
        out_specs=pl.BlockSpec((n_c, ADA_TN), lambda n: (0, n)),
        compiler_params=pltpu.CompilerParams(dimension_semantics=("arbitrary",),
                                             vmem_limit_bytes=VMEM_LIMIT_BYTES),
        name="ada_proj",
    )(c_all, w_ada[0], b_ada)
    ada_p = ada[:batch].reshape(batch, 1, 6 * D_MODEL)
    ada_s = ada[batch:]

    v1024 = jnp.stack([g_mix[0], g_ffn[0], g_final], axis=0)
    v512 = jnp.concatenate(
        [conv_w[0], conv_b, b_ra[0].reshape(1, D_RG), b_ri[0].reshape(1, D_RG), lru_l,
         ln_v_g, ln_v_b, g_rg, g_sg], axis=0)
    w_in_b = w_in[0].astype(BF16)
    w_out_b = w_out[0].astype(BF16)
    w_ff1_b = w_ff1[0].astype(BF16)
    w_ff2_b = w_ff2[0].astype(BF16)
    wg = jnp.concatenate([_block_diag_groups(w_ra[0]), _block_diag_groups(w_ri[0])], axis=-1).astype(BF16)
    wsp = (w_s[0].reshape(SG_HEADS // 2, 2, CHUNK, CHUNK).transpose(0, 2, 1, 3)
           .reshape(SG_HEADS // 2, CHUNK, 2 * CHUNK).astype(BF16))
    bias_s = jnp.repeat(b_s[0].T, SG_HEAD_DIM, axis=1)
    w8 = jnp.repeat(w_s[0, :, :dec_seq, :dec_seq].transpose(1, 2, 0), SG_HEAD_DIM, axis=2)
    b8 = bias_s[:dec_seq]

    weights = (w_in_b, wg)
    weight_specs = [_resident(w_in_b.shape), _resident(wg.shape)]
    tail_weights = (w_out_b, w_ff1_b, w_ff2_b)
    tail_specs = [_resident(w.shape) for w in tail_weights]
    cparams = functools.partial(pltpu.CompilerParams, vmem_limit_bytes=VMEM_LIMIT_BYTES)

    tt = PROMPT_TILE
    nt = seq // tt
    n_tiles = batch * nt

    def cur(s):
        return jnp.minimum(s, n_tiles - 1)

    def prev(s):
        return jnp.maximum(s - 1, 0)

    y_p, conv8_p, lru_p, vrows_p = pl.pallas_call(
        functools.partial(_prompt_kernel, tiles_per_seq=nt),
        out_shape=(jax.ShapeDtypeStruct((batch, seq, D_MODEL), F32),
                   jax.ShapeDtypeStruct((batch + 1, SUBLANES, D_RG), F32),
                   jax.ShapeDtypeStruct((batch + 1, 1, D_RG), F32),
                   jax.ShapeDtypeStruct((batch + 1, CHUNK, D_SG), F32)),
        grid=(n_tiles + 1,),
        in_specs=[pl.BlockSpec((None, tt, D_MODEL), lambda s: (cur(s) // nt, cur(s) % nt, 0)),
                  pl.BlockSpec((None, tt, D_MODEL), lambda s: (prev(s) // nt, prev(s) % nt, 0)),
                  _resident(ada_p.shape), _resident(v1024.shape), _resident(v512.shape)] + weight_specs +
                 [_resident(wsp.shape), _resident(bias_s.shape)] + tail_specs,
        out_specs=(pl.BlockSpec((None, tt, D_MODEL), lambda s: (prev(s) // nt, prev(s) % nt, 0)),
                   pl.BlockSpec((batch + 1, SUBLANES, D_RG), lambda s: (0, 0, 0)),
                   pl.BlockSpec((batch + 1, 1, D_RG), lambda s: (0, 0, 0)),
                   pl.BlockSpec((batch + 1, CHUNK, D_SG), lambda s: (0, 0, 0))),
        scratch_shapes=[pltpu.VMEM((tt + SUBLANES, D_RG), F32), pltpu.VMEM((SUBLANES, D_RG), F32),
                        pltpu.VMEM((tt, D_MODEL), BF16)],
        compiler_params=cparams(dimension_semantics=("arbitrary",)),
        name="prompt_trunk",
    )(x_prompt, x_prompt, ada_p, v1024, v512, *weights, wsp, bias_s, *tail_weights)
    conv_p = conv8_p[:batch, SUBLANES - (CONV_W - 1):, :][None]
    lru_p = lru_p[:batch].reshape(1, batch, D_RG)
    vrows_p = vrows_p[:batch][None]

    nb = SAMPLE_NB
    x_s = x_sample.transpose(1, 0, 2)
    cs = state_conv[0].transpose(1, 0, 2)
    y_s, conv_s, lru_s, vrows_s = pl.pallas_call(
        _sample_kernel,
        out_shape=(jax.ShapeDtypeStruct((dec_seq, dec_batch, D_MODEL), F32),
                   jax.ShapeDtypeStruct((CONV_W - 1, dec_batch, D_RG), F32),
                   jax.ShapeDtypeStruct((dec_batch, D_RG), F32),
                   jax.ShapeDtypeStruct((dec_seq, dec_batch, D_SG), F32)),
        grid=(dec_batch // nb,),
        in_specs=[pl.BlockSpec((dec_seq, nb, D_MODEL), lambda i: (0, i, 0)),
                  pl.BlockSpec((nb, 6 * D_MODEL), lambda i: (i, 0)),
                  pl.BlockSpec((CONV_W - 1, nb, D_RG), lambda i: (0, i, 0)),
                  pl.BlockSpec((nb, D_RG), lambda i: (i, 0)),
                  _resident(v1024.shape), _resident(v512.shape)] + weight_specs +
                 [_resident(w8.shape), _resident(b8.shape)] + tail_specs,
        out_specs=(pl.BlockSpec((dec_seq, nb, D_MODEL), lambda i: (0, i, 0)),
                   pl.BlockSpec((CONV_W - 1, nb, D_RG), lambda i: (0, i, 0)),
                   pl.BlockSpec((nb, D_RG), lambda i: (i, 0)),
                   pl.BlockSpec((dec_seq, nb, D_SG), lambda i: (0, i, 0))),
        compiler_params=cparams(dimension_semantics=("arbitrary",)),
        name="sample_trunk",
    )(x_s, ada_s, cs, state_rglru[0], v1024, v512, *weights, w8, b8, *tail_weights)

    return (y_p, y_s.transpose(1, 0, 2), conv_p, lru_p, vrows_p,
            conv_s.transpose(1, 0, 2)[None], lru_s[None], vrows_s.transpose(1, 0, 2)[None])
```

```python
import functools

import jax
import jax.numpy as jnp
from jax import lax
from jax.experimental import pallas as pl
from jax.experimental.pallas import tpu as pltpu

D_MODEL = 1024
D_RG = 512
D_SG = 512
D_IN = 2 * D_RG + 2 * D_SG
D_FF = 4 * D_MODEL
RG_HEADS = 8
RG_HEAD_DIM = D_RG // RG_HEADS
SG_HEADS = 8
SG_HEAD_DIM = D_SG // SG_HEADS
CHUNK = 128
CONV_W = 4
LRU_C = 8.0
EPS = 1e-6

SUBLANES = 8
GATE_GROUPS = 2
GATE_W = D_RG // GATE_GROUPS
PROMPT_TILE = 512
SAMPLE_NB = 64
FF_CHUNK = 1024
ADA_TN = 1024
VMEM_LIMIT_BYTES = 60000 * 1024

F32 = jnp.float32
BF16 = jnp.bfloat16

_V_CONV_W, _V_CONV_B, _V_B_RA, _V_B_RI, _V_LRU_L, _V_LN_G, _V_LN_B, _V_G_RG, _V_G_SG = 0, 4, 5, 6, 7, 8, 9, 10, 11


def _dot(a, b):
    return jnp.dot(a, b, preferred_element_type=F32)


def _rms(x, g):
    ms = jnp.mean(x * x, axis=-1, keepdims=True)
    return (x * lax.rsqrt(ms + EPS)) * g


def _gelu_tanh(x):
    k = 2.0 * 0.7978845608028654
    w = x * (k + (k * 0.044715) * (x * x))
    return x / (1.0 + jnp.exp(-w))


def _layer_norm(x, g, b):
    mu = jnp.mean(x, axis=-1, keepdims=True)
    xc = x - mu
    var = jnp.mean(xc * xc, axis=-1, keepdims=True)
    return (xc * lax.rsqrt(var + EPS)) * g + b


def _cat(parts, axis=0):
    return parts[0] if len(parts) == 1 else jnp.concatenate(parts, axis=axis)


def _split_mods(ada):
    return [ada[:, k * D_MODEL:(k + 1) * D_MODEL] for k in range(6)]


def _modulated_norm(x_blocks, g, sc, sh):
    gain = g * (1.0 + sc)
    return _cat([(_rms(xb, gain) + sh).astype(BF16) for xb in x_blocks])


def _gates(xc, v512, wg_ref):
    xcb = xc.astype(BF16)
    r_parts, i_parts = [], []
    for g in range(GATE_GROUPS):
        lo, hi = g * GATE_W, (g + 1) * GATE_W
        pre = _dot(xcb[:, lo:hi], wg_ref[g])
        r_parts.append(jax.nn.sigmoid(pre[:, :GATE_W] + v512[_V_B_RA:_V_B_RA + 1, lo:hi]))
        i_parts.append(jax.nn.sigmoid(pre[:, GATE_W:] + v512[_V_B_RI:_V_B_RI + 1, lo:hi]))
    return _cat(r_parts, 1), _cat(i_parts, 1)


def _lru_coeffs(r, v512):
    lam = v512[_V_LRU_L:_V_LRU_L + 1, :]
    log_sig = jnp.minimum(lam, 0.0) - jnp.log1p(jnp.exp(-jnp.abs(lam)))
    log_a = (LRU_C * r) * log_sig
    a = jnp.exp(log_a)
    z = 1.0 - a * a
    mult = jnp.where(z > 0.0, z * lax.rsqrt(z), 0.0)
    return a, mult


def _tail(x_blocks, merged, mods, v1024, w_out_ref, w_ff1_ref, w_ff2_ref, store_y, fillers=()):
    _, _, gt_m, sh_f, sc_f, gt_f = mods
    rows = x_blocks[0].shape[0]
    fillers = list(fillers)
    n_chunks = D_FF // FF_CHUNK

    def run_fillers(n):
        for _ in range(min(n, len(fillers))):
            fillers.pop(0)()

    def ff1(c):
        return _dot(hf, w_ff1_ref[:, c * FF_CHUNK:(c + 1) * FF_CHUNK])

    mm = _dot(merged, w_out_ref[...])
    run_fillers(1)
    x1 = [xb + gt_m * mm[i * rows:(i + 1) * rows] for i, xb in enumerate(x_blocks)]
    hf = _modulated_norm(x1, v1024[1:2, :], sc_f, sh_f)
    run_fillers(1)
    per_chunk = -(-len(fillers) // n_chunks)
    acc = None
    pre = ff1(0)
    for c in range(n_chunks):
        h1 = jnp.maximum(pre.astype(BF16), 0.0)
        h1 = h1 * h1
        if c + 1 < n_chunks:
            pre = ff1(c + 1)
        run_fillers(per_chunk)
        part = _dot(h1, w_ff2_ref[c * FF_CHUNK:(c + 1) * FF_CHUNK, :])
        acc = part if acc is None else acc + part
    run_fillers(len(fillers))
    for i, xb in enumerate(x1):
        x2 = xb + gt_f * acc[i * rows:(i + 1) * rows]
        store_y(i, _rms(x2, v1024[2:3, :]))


def _group_scan(a, b, tmod):
    rows, lanes = a.shape
    shape3 = (rows // SUBLANES, SUBLANES, lanes)
    a, b, tmod = a.reshape(shape3), b.reshape(shape3), tmod.reshape(shape3)
    s = 1
    while s < SUBLANES:
        a_sh = pltpu.roll(a, s, 1)
        b_sh = pltpu.roll(b, s, 1)
        keep = tmod >= s
        b = jnp.where(keep, a * b_sh + b, b)
        a = jnp.where(keep, a * a_sh, a)
        s *= 2
    return a.reshape(rows, lanes), b.reshape(rows, lanes)


def _prompt_mix_thunks(x_ref, ada, v1024, v512, w_in_ref, wg_ref, wsp_ref, bias_s_ref,
                       conv_ref, lru_ref, vrows_ref, ext_ref, h_ref, merged_ref, j, slot):
    tt = x_ref.shape[0]
    st = {}

    def norm():
        mods = _split_mods(ada)
        st['hm'] = _modulated_norm([x_ref[...]], v1024[0:1, :], mods[1], mods[0])

    def project():
        st['proj'] = proj = _dot(st['hm'], w_in_ref[...])
        ext_ref[SUBLANES:, :] = proj[:, 0:D_RG]
        st['carry'] = h_ref[SUBLANES - 1:SUBLANES, :]

    def recurrent(c):
        r0 = c * CHUNK
        row = lax.broadcasted_iota(jnp.int32, (CHUNK, D_RG), 0)
        yg = st['proj'][r0:r0 + CHUNK, D_RG:2 * D_RG]
        xc = v512[_V_CONV_B:_V_CONV_B + 1, :]
        for k in range(CONV_W):
            off = r0 + SUBLANES - (CONV_W - 1) + k
            xc = xc + ext_ref[off:off + CHUNK, :] * v512[_V_CONV_W + k:_V_CONV_W + k + 1, :]
        r, i_gate = _gates(xc, v512, wg_ref)
        a, mult = _lru_coeffs(r, v512)
        mult = jnp.where(row + (j * tt + r0) == 0, 1.0, mult)
        bterm = mult * i_gate * xc
        a_pref, b_pref = _group_scan(a, bterm, row & (SUBLANES - 1))
        carry = st['carry']
        h_groups = []
        for g in range(CHUNK // SUBLANES):
            lo = g * SUBLANES
            hg = a_pref[lo:lo + SUBLANES, :] * carry + b_pref[lo:lo + SUBLANES, :]
            carry = hg[SUBLANES - 1:SUBLANES, :]
            h_groups.append(hg)
        st['carry'] = carry
        rg_out = _cat(h_groups) * _gelu_tanh(yg)
        merged_ref[r0:r0 + CHUNK, 0:D_RG] = _rms(rg_out, v512[_V_G_RG:_V_G_RG + 1, :]).astype(BF16)
        if r0 + CHUNK == tt:
            h_ref[...] = h_groups[-1]
            lru_ref[slot] = carry
            tail = ext_ref[tt:tt + SUBLANES, :]
            ext_ref[0:SUBLANES, :] = tail
            conv_ref[slot] = tail

    def spatial(c):
        r0 = c * CHUNK
        lane = lax.broadcasted_iota(jnp.int32, (CHUNK, 2 * SG_HEAD_DIM), 1)
        t_idx = lax.broadcasted_iota(jnp.int32, (CHUNK, 2 * CHUNK), 0)
        s_idx = lax.broadcasted_iota(jnp.int32, (CHUNK, 2 * CHUNK), 1) & (CHUNK - 1)
        u = st['proj'][r0:r0 + CHUNK, 2 * D_RG:2 * D_RG + D_SG]
        v = st['proj'][r0:r0 + CHUNK, 2 * D_RG + D_SG:]
        vn = _layer_norm(v, v512[_V_LN_G:_V_LN_G + 1, :], v512[_V_LN_B:_V_LN_B + 1, :])
        outs = []
        for p in range(SG_HEADS // 2):
            vp = vn[:, p * 2 * SG_HEAD_DIM:(p + 1) * 2 * SG_HEAD_DIM]
            rhs = jnp.concatenate([jnp.where(lane < SG_HEAD_DIM, vp, 0.0),
                                   jnp.where(lane >= SG_HEAD_DIM, vp, 0.0)], axis=0).astype(BF16)
            ws_pair = jnp.where(s_idx <= t_idx, wsp_ref[p], jnp.zeros((), BF16))
            outs.append(_dot(ws_pair, rhs))
        sg_out = u * (_cat(outs, 1) + bias_s_ref[...])
        merged_ref[r0:r0 + CHUNK, D_RG:] = _rms(sg_out, v512[_V_G_SG:_V_G_SG + 1, :]).astype(BF16)
        if r0 + CHUNK == tt:
            vrows_ref[slot] = v

    thunks = [norm, project]
    for c in range(tt // CHUNK):
        thunks += [functools.partial(recurrent, c), functools.partial(spatial, c)]
    return thunks


def _prompt_kernel(x_ref, xprev_ref, ada_ref, v1024_ref, v512_ref, w_in_ref, wg_ref, wsp_ref,
                   bias_s_ref, w_out_ref, w_ff1_ref, w_ff2_ref,
                   y_ref, conv_ref, lru_ref, vrows_ref,
                   ext_ref, h_ref, merged_ref, *, tiles_per_seq):
    s = pl.program_id(0)
    n_tiles = pl.num_programs(0) - 1
    j = lax.rem(s, tiles_per_seq)
    seq_cur = jnp.minimum(s, n_tiles - 1) // tiles_per_seq
    seq_prev = jnp.maximum(s - 1, 0) // tiles_per_seq

    @pl.when(j == 0)
    def _():
        ext_ref[0:SUBLANES, :] = jnp.zeros((SUBLANES, D_RG), F32)
        h_ref[...] = jnp.zeros((SUBLANES, D_RG), F32)

    v1024 = v1024_ref[...]
    v512 = v512_ref[...]

    def store_y(_, val):
        y_ref[...] = val

    def stage_a():
        return _prompt_mix_thunks(x_ref, ada_ref[seq_cur], v1024, v512, w_in_ref, wg_ref, wsp_ref, bias_s_ref,
                                  conv_ref, lru_ref, vrows_ref, ext_ref, h_ref, merged_ref, j, seq_cur)

    def stage_b(fillers):
        _tail([xprev_ref[...]], merged_ref[...], _split_mods(ada_ref[seq_prev]), v1024,
              w_out_ref, w_ff1_ref, w_ff2_ref, store_y, fillers=fillers)

    @pl.when(s == 0)
    def _():
        for thunk in stage_a():
            thunk()

    @pl.when(jnp.logical_and(s > 0, s < n_tiles))
    def _():
        thunks = stage_a()
        thunks.pop(0)()
        stage_b(thunks)

    @pl.when(s == n_tiles)
    def _():
        stage_b(())


def _sample_kernel(x_ref, ada_ref, cs_ref, h0_ref, v1024_ref, v512_ref, w_in_ref, wg_ref, w8_ref, b8_ref,
                   w_out_ref, w_ff1_ref, w_ff2_ref,
                   y_ref, conv_ref, lru_ref, vrows_ref):
    n_t, nb = x_ref.shape[0], x_ref.shape[1]
    mods = _split_mods(ada_ref[...])
    sh_m, sc_m = mods[0], mods[1]
    v1024 = v1024_ref[...]
    v512 = v512_ref[...]

    def blk(arr, t):
        return arr[t * nb:(t + 1) * nb, :]

    x_blocks = [x_ref[t] for t in range(n_t)]
    hm = _modulated_norm(x_blocks, v1024[0:1, :], sc_m, sh_m)
    proj = _dot(hm, w_in_ref[...])
    xr = proj[:, 0:D_RG]
    yg = proj[:, D_RG:2 * D_RG]
    u = proj[:, 2 * D_RG:2 * D_RG + D_SG]
    v = proj[:, 2 * D_RG + D_SG:]

    xp = [cs_ref[k] for k in range(CONV_W - 1)] + [blk(xr, t) for t in range(n_t)]
    xc_blocks = []
    for t in range(n_t):
        acc = v512[_V_CONV_B:_V_CONV_B + 1, :]
        for k in range(CONV_W):
            acc = acc + xp[t + k] * v512[_V_CONV_W + k:_V_CONV_W + k + 1, :]
        xc_blocks.append(acc)
    for k in range(CONV_W - 1):
        conv_ref[k] = xp[n_t + k]
    xc = _cat(xc_blocks)

    r, i_gate = _gates(xc, v512, wg_ref)
    a, mult = _lru_coeffs(r, v512)
    bterm = mult * i_gate * xc
    h = h0_ref[...]
    h_blocks = []
    for t in range(n_t):
        h = blk(a, t) * h + blk(bterm, t)
        h_blocks.append(h)
    lru_ref[...] = h
    rg_out = _cat(h_blocks) * _gelu_tanh(yg)

    vn = _layer_norm(v, v512[_V_LN_G:_V_LN_G + 1, :], v512[_V_LN_B:_V_LN_B + 1, :])
    sg_blocks = []
    for t in range(n_t):
        vrows_ref[t] = blk(v, t)
        mixed = b8_ref[t:t + 1, :]
        for s in range(t + 1):
            mixed = mixed + w8_ref[t, s:s + 1, :] * blk(vn, s)
        sg_blocks.append(blk(u, t) * mixed)
    sg_out = _cat(sg_blocks)

    merged = jnp.concatenate(
        [_rms(rg_out, v512[_V_G_RG:_V_G_RG + 1, :]).astype(BF16),
         _rms(sg_out, v512[_V_G_SG:_V_G_SG + 1, :]).astype(BF16)], axis=1)

    def store_y(t, val):
        y_ref[t] = val

    _tail(x_blocks, merged, mods, v1024, w_out_ref, w_ff1_ref, w_ff2_ref, store_y)


def _ada_kernel(c_ref, w_ref, b_ref, o_ref):
    c = c_ref[...]
    s = (c * jax.nn.sigmoid(c)).astype(BF16)
    o_ref[...] = _dot(s, w_ref[...].astype(BF16)) + b_ref[...]


def _resident(shape):
    nd = len(shape)
    return pl.BlockSpec(shape, lambda *_: (0,) * nd, pipeline_mode=pl.Buffered(1))


def _block_diag_groups(w):
    per = RG_HEADS // GATE_GROUPS
    w4 = w.reshape(GATE_GROUPS, per, RG_HEAD_DIM, RG_HEAD_DIM)
    bd = jnp.einsum('ghij,hk->ghikj', w4, jnp.eye(per, dtype=w.dtype))
    return bd.reshape(GATE_GROUPS, GATE_W, GATE_W)


def kernel(x_prompt, x_sample, c_prompt, c_sample, state_conv, state_rglru, w_ada, b_ada, g_mix, g_ffn, w_in, conv_w, conv_b, w_ra, b_ra, w_ri, b_ri, lru_l, ln_v_g, ln_v_b, w_s, b_s, g_rg, g_sg, w_out, w_ff1, w_ff2, g_final):
    batch, seq, _ = x_prompt.shape
    dec_batch, dec_seq, _ = x_sample.shape
    assert w_ada.shape[0] == 1, "single-layer step"
    assert seq % PROMPT_TILE == 0 and PROMPT_TILE % CHUNK == 0 and dec_batch % SAMPLE_NB == 0
    assert dec_seq <= CHUNK

    c_all = jnp.concatenate([c_prompt, c_sample], axis=0)
    n_c = c_all.shape[0]
    ada = pl.pallas_call(
        _ada_kernel,
        out_shape=jax.ShapeDtypeStruct((n_c, 6 * D_MODEL), F32),
        grid=(6 * D_MODEL // ADA_TN,),
        in_specs=[pl.BlockSpec((n_c, D_MODEL), lambda n: (0, 0)),
                  pl.BlockSpec((D_MODEL, ADA_TN), lambda n: (0, n)),
                  pl.BlockSpec((1, ADA_TN), lambda n: (0, n))],
        out_specs=pl.BlockSpec((n_c, ADA_TN), lambda n: (0, n)),
        compiler_params=pltpu.CompilerParams(dimension_semantics=("arbitrary",),
                                             vmem_limit_bytes=VMEM_LIMIT_BYTES),
        name="ada_proj",
    )(c_all, w_ada[0], b_ada)
    ada_p = ada[:batch].reshape(batch, 1, 6 * D_MODEL)
    ada_s = ada[batch:]

    v1024 = jnp.stack([g_mix[0], g_ffn[0], g_final], axis=0)
    v512 = jnp.concatenate(
        [conv_w[0], conv_b, b_ra[0].reshape(1, D_RG), b_ri[0].reshape(1, D_RG), lru_l,
         ln_v_g, ln_v_b, g_rg, g_sg], axis=0)
    w_in_b = w_in[0].astype(BF16)
    w_out_b = w_out[0].astype(BF16)
    w_ff1_b = w_ff1[0].astype(BF16)
    w_ff2_b = w_ff2[0].astype(BF16)
    wg = jnp.concatenate([_block_diag_groups(w_ra[0]), _block_diag_groups(w_ri[0])], axis=-1).astype(BF16)
    wsp = (w_s[0].reshape(SG_HEADS // 2, 2, CHUNK, CHUNK).transpose(0, 2, 1, 3)
           .reshape(SG_HEADS // 2, CHUNK, 2 * CHUNK).astype(BF16))
    bias_s = jnp.repeat(b_s[0].T, SG_HEAD_DIM, axis=1)
    w8 = jnp.repeat(w_s[0, :, :dec_seq, :dec_seq].transpose(1, 2, 0), SG_HEAD_DIM, axis=2)
    b8 = bias_s[:dec_seq]

    weights = (w_in_b, wg)
    weight_specs = [_resident(w_in_b.shape), _resident(wg.shape)]
    tail_weights = (w_out_b, w_ff1_b, w_ff2_b)
    tail_specs = [_resident(w.shape) for w in tail_weights]
    cparams = functools.partial(pltpu.CompilerParams, vmem_limit_bytes=VMEM_LIMIT_BYTES)

    tt = PROMPT_TILE
    nt = seq // tt
    n_tiles = batch * nt

    def cur(s):
        return jnp.minimum(s, n_tiles - 1)

    def prev(s):
        return jnp.maximum(s - 1, 0)

    y_p, conv8_p, lru_p, vrows_p = pl.pallas_call(
        functools.partial(_prompt_kernel, tiles_per_seq=nt),
        out_shape=(jax.ShapeDtypeStruct((batch, seq, D_MODEL), F32),
                   jax.ShapeDtypeStruct((batch, SUBLANES, D_RG), F32),
                   jax.ShapeDtypeStruct((batch, 1, D_RG), F32),
                   jax.ShapeDtypeStruct((batch, CHUNK, D_SG), F32)),
        grid=(n_tiles + 1,),
        in_specs=[pl.BlockSpec((None, tt, D_MODEL), lambda s: (cur(s) // nt, cur(s) % nt, 0)),
                  pl.BlockSpec((None, tt, D_MODEL), lambda s: (prev(s) // nt, prev(s) % nt, 0)),
                  _resident(ada_p.shape), _resident(v1024.shape), _resident(v512.shape)] + weight_specs +
                 [_resident(wsp.shape), _resident(bias_s.shape)] + tail_specs,
        out_specs=(pl.BlockSpec((None, tt, D_MODEL), lambda s: (prev(s) // nt, prev(s) % nt, 0)),
                   pl.BlockSpec((batch, SUBLANES, D_RG), lambda s: (0, 0, 0)),
                   pl.BlockSpec((batch, 1, D_RG), lambda s: (0, 0, 0)),
                   pl.BlockSpec((batch, CHUNK, D_SG), lambda s: (0, 0, 0))),
        scratch_shapes=[pltpu.VMEM((tt + SUBLANES, D_RG), F32), pltpu.VMEM((SUBLANES, D_RG), F32),
                        pltpu.VMEM((tt, D_MODEL), BF16)],
        compiler_params=cparams(dimension_semantics=("arbitrary",)),
        name="prompt_trunk",
    )(x_prompt, x_prompt, ada_p, v1024, v512, *weights, wsp, bias_s, *tail_weights)
    conv_p = conv8_p[:, SUBLANES - (CONV_W - 1):, :][None]
    lru_p = lru_p.reshape(1, batch, D_RG)
    vrows_p = vrows_p[None]

    nb = SAMPLE_NB
    x_s = x_sample.transpose(1, 0, 2)
    cs = state_conv[0].transpose(1, 0, 2)
    y_s, conv_s, lru_s, vrows_s = pl.pallas_call(
        _sample_kernel,
        out_shape=(jax.ShapeDtypeStruct((dec_seq, dec_batch, D_MODEL), F32),
                   jax.ShapeDtypeStruct((CONV_W - 1, dec_batch, D_RG), F32),
                   jax.ShapeDtypeStruct((dec_batch, D_RG), F32),
                   jax.ShapeDtypeStruct((dec_seq, dec_batch, D_SG), F32)),
        grid=(dec_batch // nb,),
        in_specs=[pl.BlockSpec((dec_seq, nb, D_MODEL), lambda i: (0, i, 0)),
                  pl.BlockSpec((nb, 6 * D_MODEL), lambda i: (i, 0)),
                  pl.BlockSpec((CONV_W - 1, nb, D_RG), lambda i: (0, i, 0)),
                  pl.BlockSpec((nb, D_RG), lambda i: (i, 0)),
                  _resident(v1024.shape), _resident(v512.shape)] + weight_specs +
                 [_resident(w8.shape), _resident(b8.shape)] + tail_specs,
        out_specs=(pl.BlockSpec((dec_seq, nb, D_MODEL), lambda i: (0, i, 0)),
                   pl.BlockSpec((CONV_W - 1, nb, D_RG), lambda i: (0, i, 0)),
                   pl.BlockSpec((nb, D_RG), lambda i: (i, 0)),
                   pl.BlockSpec((dec_seq, nb, D_SG), lambda i: (0, i, 0))),
        compiler_params=cparams(dimension_semantics=("arbitrary",)),
        name="sample_trunk",
    )(x_s, ada_s, cs, state_rglru[0], v1024, v512, *weights, w8, b8, *tail_weights)

    return (y_p, y_s.transpose(1, 0, 2), conv_p, lru_p, vrows_p,
            conv_s.transpose(1, 0, 2)[None], lru_s[None], vrows_s.transpose(1, 0, 2)[None])
```

```python
import functools

import jax
import jax.numpy as jnp
from jax import lax
from jax.experimental import pallas as pl
from jax.experimental.pallas import tpu as pltpu

D_MODEL = 1024
D_RG = 512
D_SG = 512
D_IN = 2 * D_RG + 2 * D_SG
D_FF = 4 * D_MODEL
RG_HEADS = 8
RG_HEAD_DIM = D_RG // RG_HEADS
SG_HEADS = 8
SG_HEAD_DIM = D_SG // SG_HEADS
CHUNK = 128
CONV_W = 4
LRU_C = 8.0
EPS = 1e-6

SUBLANES = 8
GATE_GROUPS = 2
GATE_W = D_RG // GATE_GROUPS
PROMPT_TILE = 512
SAMPLE_NB = 64
FF_CHUNK = 1024
PREP_STEPS = 8
VMEM_LIMIT_BYTES = 60000 * 1024

F32 = jnp.float32
BF16 = jnp.bfloat16

_V_CONV_W, _V_CONV_B, _V_B_RA, _V_B_RI, _V_LRU_L, _V_LN_G, _V_LN_B, _V_G_RG, _V_G_SG = 0, 4, 5, 6, 7, 8, 9, 10, 11


def _dot(a, b):
    return jnp.dot(a, b, preferred_element_type=F32)


def _rms(x, g):
    ms = jnp.mean(x * x, axis=-1, keepdims=True)
    return (x * lax.rsqrt(ms + EPS)) * g


def _gelu_tanh(x):
    k = 2.0 * 0.7978845608028654
    w = x * (k + (k * 0.044715) * (x * x))
    return x / (1.0 + jnp.exp(-w))


def _layer_norm(x, g, b):
    mu = jnp.mean(x, axis=-1, keepdims=True)
    xc = x - mu
    var = jnp.mean(xc * xc, axis=-1, keepdims=True)
    return (xc * lax.rsqrt(var + EPS)) * g + b


def _cat(parts, axis=0):
    return parts[0] if len(parts) == 1 else jnp.concatenate(parts, axis=axis)


def _split_mods(ada):
    return [ada[:, k * D_MODEL:(k + 1) * D_MODEL] for k in range(6)]


def _modulated_norm(x_blocks, g, sc, sh):
    gain = g * (1.0 + sc)
    return _cat([(_rms(xb, gain) + sh).astype(BF16) for xb in x_blocks])


def _gates(xc, v512, wg_ref):
    xcb = xc.astype(BF16)
    r_parts, i_parts = [], []
    for g in range(GATE_GROUPS):
        lo, hi = g * GATE_W, (g + 1) * GATE_W
        pre = _dot(xcb[:, lo:hi], wg_ref[g])
        r_parts.append(jax.nn.sigmoid(pre[:, :GATE_W] + v512[_V_B_RA:_V_B_RA + 1, lo:hi]))
        i_parts.append(jax.nn.sigmoid(pre[:, GATE_W:] + v512[_V_B_RI:_V_B_RI + 1, lo:hi]))
    return _cat(r_parts, 1), _cat(i_parts, 1)


def _lru_coeffs(r, v512):
    lam = v512[_V_LRU_L:_V_LRU_L + 1, :]
    log_sig = jnp.minimum(lam, 0.0) - jnp.log1p(jnp.exp(-jnp.abs(lam)))
    log_a = (LRU_C * r) * log_sig
    a = jnp.exp(log_a)
    z = 1.0 - a * a
    mult = jnp.where(z > 0.0, z * lax.rsqrt(z), 0.0)
    return a, mult


def _tail(x_blocks, merged, mods, v1024, w_out_ref, w_ff1_ref, w_ff2_ref, store_y, fillers=()):
    _, _, gt_m, sh_f, sc_f, gt_f = mods
    rows = x_blocks[0].shape[0]
    fillers = list(fillers)
    n_chunks = D_FF // FF_CHUNK

    def run_fillers(n):
        for _ in range(min(n, len(fillers))):
            fillers.pop(0)()

    def ff1(c):
        return _dot(hf, w_ff1_ref[:, c * FF_CHUNK:(c + 1) * FF_CHUNK])

    mm = _dot(merged, w_out_ref[...])
    run_fillers(1)
    x1 = [xb + gt_m * mm[i * rows:(i + 1) * rows] for i, xb in enumerate(x_blocks)]
    hf = _modulated_norm(x1, v1024[1:2, :], sc_f, sh_f)
    run_fillers(1)
    per_chunk = -(-len(fillers) // n_chunks)
    acc = None
    pre = ff1(0)
    for c in range(n_chunks):
        h1 = jnp.maximum(pre.astype(BF16), 0.0)
        h1 = h1 * h1
        if c + 1 < n_chunks:
            pre = ff1(c + 1)
        run_fillers(per_chunk)
        part = _dot(h1, w_ff2_ref[c * FF_CHUNK:(c + 1) * FF_CHUNK, :])
        acc = part if acc is None else acc + part
    run_fillers(len(fillers))
    for i, xb in enumerate(x1):
        x2 = xb + gt_f * acc[i * rows:(i + 1) * rows]
        store_y(i, _rms(x2, v1024[2:3, :]))


def _group_scan(a, b, tmod):
    rows, lanes = a.shape
    shape3 = (rows // SUBLANES, SUBLANES, lanes)
    a, b, tmod = a.reshape(shape3), b.reshape(shape3), tmod.reshape(shape3)
    s = 1
    while s < SUBLANES:
        a_sh = pltpu.roll(a, s, 1)
        b_sh = pltpu.roll(b, s, 1)
        keep = tmod >= s
        b = jnp.where(keep, a * b_sh + b, b)
        a = jnp.where(keep, a * a_sh, a)
        s *= 2
    return a.reshape(rows, lanes), b.reshape(rows, lanes)


def _prompt_mix_thunks(x_ref, ada, v1024, v512, w_in_ref, wg_ref, wsp_ref, bias_s_ref,
                       conv_ref, lru_ref, vrows_ref, ext_ref, h_ref, merged_ref, j, slot):
    tt = x_ref.shape[0]
    st = {}

    def norm():
        mods = _split_mods(ada)
        st['hm'] = _modulated_norm([x_ref[...]], v1024[0:1, :], mods[1], mods[0])

    def project():
        st['proj'] = proj = _dot(st['hm'], w_in_ref[...])
        ext_ref[SUBLANES:, :] = proj[:, 0:D_RG]
        st['carry'] = h_ref[SUBLANES - 1:SUBLANES, :]

    def recurrent(c):
        r0 = c * CHUNK
        row = lax.broadcasted_iota(jnp.int32, (CHUNK, D_RG), 0)
        yg = st['proj'][r0:r0 + CHUNK, D_RG:2 * D_RG]
        xc = v512[_V_CONV_B:_V_CONV_B + 1, :]
        for k in range(CONV_W):
            off = r0 + SUBLANES - (CONV_W - 1) + k
            xc = xc + ext_ref[off:off + CHUNK, :] * v512[_V_CONV_W + k:_V_CONV_W + k + 1, :]
        r, i_gate = _gates(xc, v512, wg_ref)
        a, mult = _lru_coeffs(r, v512)
        mult = jnp.where(row + (j * tt + r0) == 0, 1.0, mult)
        bterm = mult * i_gate * xc
        a_pref, b_pref = _group_scan(a, bterm, row & (SUBLANES - 1))
        carry = st['carry']
        h_groups = []
        for g in range(CHUNK // SUBLANES):
            lo = g * SUBLANES
            hg = a_pref[lo:lo + SUBLANES, :] * carry + b_pref[lo:lo + SUBLANES, :]
            carry = hg[SUBLANES - 1:SUBLANES, :]
            h_groups.append(hg)
        st['carry'] = carry
        rg_out = _cat(h_groups) * _gelu_tanh(yg)
        merged_ref[r0:r0 + CHUNK, 0:D_RG] = _rms(rg_out, v512[_V_G_RG:_V_G_RG + 1, :]).astype(BF16)
        if r0 + CHUNK == tt:
            h_ref[...] = h_groups[-1]
            lru_ref[slot] = carry
            tail = ext_ref[tt:tt + SUBLANES, :]
            ext_ref[0:SUBLANES, :] = tail
            conv_ref[slot] = tail

    def spatial(c):
        r0 = c * CHUNK
        lane = lax.broadcasted_iota(jnp.int32, (CHUNK, 2 * SG_HEAD_DIM), 1)
        t_idx = lax.broadcasted_iota(jnp.int32, (CHUNK, 2 * CHUNK), 0)
        s_idx = lax.broadcasted_iota(jnp.int32, (CHUNK, 2 * CHUNK), 1) & (CHUNK - 1)
        u = st['proj'][r0:r0 + CHUNK, 2 * D_RG:2 * D_RG + D_SG]
        v = st['proj'][r0:r0 + CHUNK, 2 * D_RG + D_SG:]
        vn = _layer_norm(v, v512[_V_LN_G:_V_LN_G + 1, :], v512[_V_LN_B:_V_LN_B + 1, :])
        outs = []
        for p in range(SG_HEADS // 2):
            vp = vn[:, p * 2 * SG_HEAD_DIM:(p + 1) * 2 * SG_HEAD_DIM]
            rhs = jnp.concatenate([jnp.where(lane < SG_HEAD_DIM, vp, 0.0),
                                   jnp.where(lane >= SG_HEAD_DIM, vp, 0.0)], axis=0).astype(BF16)
            ws_pair = jnp.where(s_idx <= t_idx, wsp_ref[p], jnp.zeros((), BF16))
            outs.append(_dot(ws_pair, rhs))
        sg_out = u * (_cat(outs, 1) + bias_s_ref[...])
        merged_ref[r0:r0 + CHUNK, D_RG:] = _rms(sg_out, v512[_V_G_SG:_V_G_SG + 1, :]).astype(BF16)
        if r0 + CHUNK == tt:
            vrows_ref[slot] = v

    thunks = [norm, project]
    for c in range(tt // CHUNK):
        thunks += [functools.partial(recurrent, c), functools.partial(spatial, c)]
    return thunks


def _prompt_kernel(x_ref, xprev_ref, ada_ref, v1024_ref, v512_ref, w_in_ref, wg_ref, wsp_ref,
                   bias_s_ref, w_out_ref, w_ff1_ref, w_ff2_ref,
                   y_ref, conv_ref, lru_ref, vrows_ref,
                   ext_ref, h_ref, merged_ref, *, tiles_per_seq):
    s = pl.program_id(0)
    n_tiles = pl.num_programs(0) - 1
    j = lax.rem(s, tiles_per_seq)
    seq_cur = jnp.minimum(s, n_tiles - 1) // tiles_per_seq
    seq_prev = jnp.maximum(s - 1, 0) // tiles_per_seq

    @pl.when(j == 0)
    def _():
        ext_ref[0:SUBLANES, :] = jnp.zeros((SUBLANES, D_RG), F32)
        h_ref[...] = jnp.zeros((SUBLANES, D_RG), F32)

    v1024 = v1024_ref[...]
    v512 = v512_ref[...]

    def store_y(_, val):
        y_ref[...] = val

    def stage_a():
        return _prompt_mix_thunks(x_ref, ada_ref[pl.ds(seq_cur, 1), :], v1024, v512, w_in_ref, wg_ref, wsp_ref, bias_s_ref,
                                  conv_ref, lru_ref, vrows_ref, ext_ref, h_ref, merged_ref, j, seq_cur)

    def stage_b(fillers):
        _tail([xprev_ref[...]], merged_ref[...], _split_mods(ada_ref[pl.ds(seq_prev, 1), :]), v1024,
              w_out_ref, w_ff1_ref, w_ff2_ref, store_y, fillers=fillers)

    @pl.when(s == 0)
    def _():
        for thunk in stage_a():
            thunk()

    @pl.when(jnp.logical_and(s > 0, s < n_tiles))
    def _():
        thunks = stage_a()
        thunks.pop(0)()
        stage_b(thunks)

    @pl.when(s == n_tiles)
    def _():
        stage_b(())


def _sample_kernel(x_ref, ada_ref, cs_ref, h0_ref, v1024_ref, v512_ref, w_in_ref, wg_ref, w8_ref, b8_ref,
                   w_out_ref, w_ff1_ref, w_ff2_ref,
                   y_ref, conv_ref, lru_ref, vrows_ref):
    n_t, nb = x_ref.shape[0], x_ref.shape[1]
    mods = _split_mods(ada_ref[...])
    sh_m, sc_m = mods[0], mods[1]
    v1024 = v1024_ref[...]
    v512 = v512_ref[...]

    def blk(arr, t):
        return arr[t * nb:(t + 1) * nb, :]

    x_blocks = [x_ref[t] for t in range(n_t)]
    hm = _modulated_norm(x_blocks, v1024[0:1, :], sc_m, sh_m)
    proj = _dot(hm, w_in_ref[...])
    xr = proj[:, 0:D_RG]
    yg = proj[:, D_RG:2 * D_RG]
    u = proj[:, 2 * D_RG:2 * D_RG + D_SG]
    v = proj[:, 2 * D_RG + D_SG:]

    xp = [cs_ref[k] for k in range(CONV_W - 1)] + [blk(xr, t) for t in range(n_t)]
    xc_blocks = []
    for t in range(n_t):
        acc = v512[_V_CONV_B:_V_CONV_B + 1, :]
        for k in range(CONV_W):
            acc = acc + xp[t + k] * v512[_V_CONV_W + k:_V_CONV_W + k + 1, :]
        xc_blocks.append(acc)
    for k in range(CONV_W - 1):
        conv_ref[k] = xp[n_t + k]
    xc = _cat(xc_blocks)

    r, i_gate = _gates(xc, v512, wg_ref)
    a, mult = _lru_coeffs(r, v512)
    bterm = mult * i_gate * xc
    h = h0_ref[...]
    h_blocks = []
    for t in range(n_t):
        h = blk(a, t) * h + blk(bterm, t)
        h_blocks.append(h)
    lru_ref[...] = h
    rg_out = _cat(h_blocks) * _gelu_tanh(yg)

    vn = _layer_norm(v, v512[_V_LN_G:_V_LN_G + 1, :], v512[_V_LN_B:_V_LN_B + 1, :])
    sg_blocks = []
    for t in range(n_t):
        vrows_ref[t] = blk(v, t)
        mixed = b8_ref[t:t + 1, :]
        for s in range(t + 1):
            mixed = mixed + w8_ref[t, s:s + 1, :] * blk(vn, s)
        sg_blocks.append(blk(u, t) * mixed)
    sg_out = _cat(sg_blocks)

    merged = jnp.concatenate(
        [_rms(rg_out, v512[_V_G_RG:_V_G_RG + 1, :]).astype(BF16),
         _rms(sg_out, v512[_V_G_SG:_V_G_SG + 1, :]).astype(BF16)], axis=1)

    def store_y(t, val):
        y_ref[t] = val

    _tail(x_blocks, merged, mods, v1024, w_out_ref, w_ff1_ref, w_ff2_ref, store_y)


def _prep_kernel(cs_ref, cp_ref, w_ada_ref, b_ada_ref, w_in_ref, w_out_ref, w_ff1_ref, w_ff2_ref,
                 ada_ref, w_in_o, w_out_o, w_ff1_o, w_ff2_o):
    c = jnp.concatenate([cs_ref[...], cp_ref[...]], axis=0)
    s = (c * jax.nn.sigmoid(c)).astype(BF16)
    ada_ref[...] = _dot(s, w_ada_ref[...].astype(BF16)) + b_ada_ref[...]
    w_in_o[...] = w_in_ref[...].astype(BF16)
    w_out_o[...] = w_out_ref[...].astype(BF16)
    w_ff1_o[...] = w_ff1_ref[...].astype(BF16)
    w_ff2_o[...] = w_ff2_ref[...].astype(BF16)


def _resident(shape):
    nd = len(shape)
    return pl.BlockSpec(shape, lambda *_: (0,) * nd, pipeline_mode=pl.Buffered(1))


def _block_diag_groups(w):
    per = RG_HEADS // GATE_GROUPS
    w4 = w.reshape(GATE_GROUPS, per, RG_HEAD_DIM, RG_HEAD_DIM)
    bd = jnp.einsum('ghij,hk->ghikj', w4, jnp.eye(per, dtype=w.dtype))
    return bd.reshape(GATE_GROUPS, GATE_W, GATE_W)


def kernel(x_prompt, x_sample, c_prompt, c_sample, state_conv, state_rglru, w_ada, b_ada, g_mix, g_ffn, w_in, conv_w, conv_b, w_ra, b_ra, w_ri, b_ri, lru_l, ln_v_g, ln_v_b, w_s, b_s, g_rg, g_sg, w_out, w_ff1, w_ff2, g_final):
    batch, seq, _ = x_prompt.shape
    dec_batch, dec_seq, _ = x_sample.shape
    assert w_ada.shape[0] == 1, "single-layer step"
    assert seq % PROMPT_TILE == 0 and PROMPT_TILE % CHUNK == 0 and dec_batch % SAMPLE_NB == 0
    assert dec_seq <= CHUNK

    n_c = dec_batch + batch
    assert dec_batch % SUBLANES == 0 and dec_batch % batch == 0
    steps = PREP_STEPS

    def row_block(w):
        return pl.BlockSpec((w.shape[0] // steps, w.shape[1]), lambda n: (n, 0))

    mats = (w_in[0], w_out[0], w_ff1[0], w_ff2[0])
    ada, w_in_b, w_out_b, w_ff1_b, w_ff2_b = pl.pallas_call(
        _prep_kernel,
        out_shape=(jax.ShapeDtypeStruct((n_c, 6 * D_MODEL), F32),) +
                  tuple(jax.ShapeDtypeStruct(w.shape, BF16) for w in mats),
        grid=(steps,),
        in_specs=[pl.BlockSpec((dec_batch, D_MODEL), lambda n: (0, 0)),
                  pl.BlockSpec((batch, D_MODEL), lambda n: (0, 0)),
                  pl.BlockSpec((D_MODEL, 6 * D_MODEL // steps), lambda n: (0, n)),
                  pl.BlockSpec((1, 6 * D_MODEL // steps), lambda n: (0, n))] + [row_block(w) for w in mats],
        out_specs=(pl.BlockSpec((n_c, 6 * D_MODEL // steps), lambda n: (0, n)),) +
                  tuple(row_block(w) for w in mats),
        compiler_params=pltpu.CompilerParams(dimension_semantics=("arbitrary",),
                                             vmem_limit_bytes=VMEM_LIMIT_BYTES),
        name="prep",
    )(c_sample, c_prompt, w_ada[0], b_ada, *mats)

    v1024 = jnp.stack([g_mix[0], g_ffn[0], g_final], axis=0)
    v512 = jnp.concatenate(
        [conv_w[0], conv_b, b_ra[0].reshape(1, D_RG), b_ri[0].reshape(1, D_RG), lru_l,
         ln_v_g, ln_v_b, g_rg, g_sg], axis=0)
    wg = jnp.concatenate([_block_diag_groups(w_ra[0]), _block_diag_groups(w_ri[0])], axis=-1).astype(BF16)
    wsp = (w_s[0].reshape(SG_HEADS // 2, 2, CHUNK, CHUNK).transpose(0, 2, 1, 3)
           .reshape(SG_HEADS // 2, CHUNK, 2 * CHUNK).astype(BF16))
    bias_s = jnp.repeat(b_s[0].T, SG_HEAD_DIM, axis=1)
    w8 = jnp.repeat(w_s[0, :, :dec_seq, :dec_seq].transpose(1, 2, 0), SG_HEAD_DIM, axis=2)
    b8 = bias_s[:dec_seq]

    weights = (w_in_b, wg)
    weight_specs = [_resident(w_in_b.shape), _resident(wg.shape)]
    tail_weights = (w_out_b, w_ff1_b, w_ff2_b)
    tail_specs = [_resident(w.shape) for w in tail_weights]
    cparams = functools.partial(pltpu.CompilerParams, vmem_limit_bytes=VMEM_LIMIT_BYTES)

    tt = PROMPT_TILE
    nt = seq // tt
    n_tiles = batch * nt

    def cur(s):
        return jnp.minimum(s, n_tiles - 1)

    def prev(s):
        return jnp.maximum(s - 1, 0)

    y_p, conv8_p, lru_p, vrows_p = pl.pallas_call(
        functools.partial(_prompt_kernel, tiles_per_seq=nt),
        out_shape=(jax.ShapeDtypeStruct((batch, seq, D_MODEL), F32),
                   jax.ShapeDtypeStruct((batch, SUBLANES, D_RG), F32),
                   jax.ShapeDtypeStruct((batch, 1, D_RG), F32),
                   jax.ShapeDtypeStruct((batch, CHUNK, D_SG), F32)),
        grid=(n_tiles + 1,),
        in_specs=[pl.BlockSpec((None, tt, D_MODEL), lambda s: (cur(s) // nt, cur(s) % nt, 0)),
                  pl.BlockSpec((None, tt, D_MODEL), lambda s: (prev(s) // nt, prev(s) % nt, 0)),
                  pl.BlockSpec((batch, 6 * D_MODEL), lambda s: (dec_batch // batch, 0),
                               pipeline_mode=pl.Buffered(1)),
                  _resident(v1024.shape), _resident(v512.shape)] + weight_specs +
                 [_resident(wsp.shape), _resident(bias_s.shape)] + tail_specs,
        out_specs=(pl.BlockSpec((None, tt, D_MODEL), lambda s: (prev(s) // nt, prev(s) % nt, 0)),
                   pl.BlockSpec((batch, SUBLANES, D_RG), lambda s: (0, 0, 0)),
                   pl.BlockSpec((batch, 1, D_RG), lambda s: (0, 0, 0)),
                   pl.BlockSpec((batch, CHUNK, D_SG), lambda s: (0, 0, 0))),
        scratch_shapes=[pltpu.VMEM((tt + SUBLANES, D_RG), F32), pltpu.VMEM((SUBLANES, D_RG), F32),
                        pltpu.VMEM((tt, D_MODEL), BF16)],
        compiler_params=cparams(dimension_semantics=("arbitrary",)),
        name="prompt_trunk",
    )(x_prompt, x_prompt, ada, v1024, v512, *weights, wsp, bias_s, *tail_weights)
    conv_p = conv8_p[:, SUBLANES - (CONV_W - 1):, :][None]
    lru_p = lru_p.reshape(1, batch, D_RG)
    vrows_p = vrows_p[None]

    nb = SAMPLE_NB
    x_s = x_sample.transpose(1, 0, 2)
    cs = state_conv[0].transpose(1, 0, 2)
    y_s, conv_s, lru_s, vrows_s = pl.pallas_call(
        _sample_kernel,
        out_shape=(jax.ShapeDtypeStruct((dec_seq, dec_batch, D_MODEL), F32),
                   jax.ShapeDtypeStruct((CONV_W - 1, dec_batch, D_RG), F32),
                   jax.ShapeDtypeStruct((dec_batch, D_RG), F32),
                   jax.ShapeDtypeStruct((dec_seq, dec_batch, D_SG), F32)),
        grid=(dec_batch // nb,),
        in_specs=[pl.BlockSpec((dec_seq, nb, D_MODEL), lambda i: (0, i, 0)),
                  pl.BlockSpec((nb, 6 * D_MODEL), lambda i: (i, 0)),
                  pl.BlockSpec((CONV_W - 1, nb, D_RG), lambda i: (0, i, 0)),
                  pl.BlockSpec((nb, D_RG), lambda i: (i, 0)),
                  _resident(v1024.shape), _resident(v512.shape)] + weight_specs +
                 [_resident(w8.shape), _resident(b8.shape)] + tail_specs,
        out_specs=(pl.BlockSpec((dec_seq, nb, D_MODEL), lambda i: (0, i, 0)),
                   pl.BlockSpec((CONV_W - 1, nb, D_RG), lambda i: (0, i, 0)),
                   pl.BlockSpec((nb, D_RG), lambda i: (i, 0)),
                   pl.BlockSpec((dec_seq, nb, D_SG), lambda i: (0, i, 0))),
        compiler_params=cparams(dimension_semantics=("arbitrary",)),
        name="sample_trunk",
    )(x_s, ada, cs, state_rglru[0], v1024, v512, *weights, w8, b8, *tail_weights)

    return (y_p, y_s.transpose(1, 0, 2), conv_p, lru_p, vrows_p,
            conv_s.transpose(1, 0, 2)[None], lru_s[None], vrows_s.transpose(1, 0, 2)[None])
```

```python
import functools

import jax
import jax.numpy as jnp
from jax import lax
from jax.experimental import pallas as pl
from jax.experimental.pallas import tpu as pltpu

D_MODEL = 1024
D_RG = 512
D_SG = 512
D_IN = 2 * D_RG + 2 * D_SG
D_FF = 4 * D_MODEL
RG_HEADS = 8
RG_HEAD_DIM = D_RG // RG_HEADS
SG_HEADS = 8
SG_HEAD_DIM = D_SG // SG_HEADS
CHUNK = 128
CONV_W = 4
LRU_C = 8.0
EPS = 1e-6

SUBLANES = 8
GATE_GROUPS = 2
GATE_W = D_RG // GATE_GROUPS
PROMPT_TILE = 512
SAMPLE_NB = 64
FF_CHUNK = 1024
PREP_STEPS = 8
VMEM_LIMIT_BYTES = 60000 * 1024

F32 = jnp.float32
BF16 = jnp.bfloat16

_V_CONV_W, _V_CONV_B, _V_B_RA, _V_B_RI, _V_LRU_L, _V_LN_G, _V_LN_B, _V_G_RG, _V_G_SG = 0, 4, 5, 6, 7, 8, 9, 10, 11


def _dot(a, b):
    return jnp.dot(a, b, preferred_element_type=F32)


def _rms(x, g):
    ms = jnp.mean(x * x, axis=-1, keepdims=True)
    return (x * lax.rsqrt(ms + EPS)) * g


def _gelu_tanh(x):
    k = 2.0 * 0.7978845608028654
    w = x * (k + (k * 0.044715) * (x * x))
    return x / (1.0 + jnp.exp(-w))


def _layer_norm(x, g, b):
    mu = jnp.mean(x, axis=-1, keepdims=True)
    xc = x - mu
    var = jnp.mean(xc * xc, axis=-1, keepdims=True)
    return (xc * lax.rsqrt(var + EPS)) * g + b


def _cat(parts, axis=0):
    return parts[0] if len(parts) == 1 else jnp.concatenate(parts, axis=axis)


def _split_mods(ada):
    return [ada[:, k * D_MODEL:(k + 1) * D_MODEL] for k in range(6)]


def _modulated_norm(x_blocks, g, sc, sh):
    gain = g * (1.0 + sc)
    return _cat([(_rms(xb, gain) + sh).astype(BF16) for xb in x_blocks])


def _gates(xc, v512, wg_ref):
    xcb = xc.astype(BF16)
    r_parts, i_parts = [], []
    for g in range(GATE_GROUPS):
        lo, hi = g * GATE_W, (g + 1) * GATE_W
        pre = _dot(xcb[:, lo:hi], wg_ref[g])
        r_parts.append(jax.nn.sigmoid(pre[:, :GATE_W] + v512[_V_B_RA:_V_B_RA + 1, lo:hi]))
        i_parts.append(jax.nn.sigmoid(pre[:, GATE_W:] + v512[_V_B_RI:_V_B_RI + 1, lo:hi]))
    return _cat(r_parts, 1), _cat(i_parts, 1)


def _lru_coeffs(r, v512):
    lam = v512[_V_LRU_L:_V_LRU_L + 1, :]
    log_sig = jnp.minimum(lam, 0.0) - jnp.log1p(jnp.exp(-jnp.abs(lam)))
    log_a = (LRU_C * r) * log_sig
    a = jnp.exp(log_a)
    z = 1.0 - a * a
    mult = jnp.where(z > 0.0, z * lax.rsqrt(z), 0.0)
    return a, mult


def _tail(x_blocks, merged, mods, v1024, w_out_ref, w_ff1_ref, w_ff2_ref, store_y, fillers=()):
    _, _, gt_m, sh_f, sc_f, gt_f = mods
    rows = x_blocks[0].shape[0]
    fillers = list(fillers)
    n_chunks = D_FF // FF_CHUNK

    def run_fillers(n):
        for _ in range(min(n, len(fillers))):
            fillers.pop(0)()

    def ff1(c):
        return _dot(hf, w_ff1_ref[:, c * FF_CHUNK:(c + 1) * FF_CHUNK])

    mm = _dot(merged, w_out_ref[...])
    run_fillers(1)
    x1 = [xb + gt_m * mm[i * rows:(i + 1) * rows] for i, xb in enumerate(x_blocks)]
    hf = _modulated_norm(x1, v1024[1:2, :], sc_f, sh_f)
    run_fillers(1)
    per_chunk = -(-len(fillers) // n_chunks)
    acc = None
    pre = ff1(0)
    for c in range(n_chunks):
        h1 = jnp.maximum(pre.astype(BF16), 0.0)
        h1 = h1 * h1
        if c + 1 < n_chunks:
            pre = ff1(c + 1)
        run_fillers(per_chunk)
        part = _dot(h1, w_ff2_ref[c * FF_CHUNK:(c + 1) * FF_CHUNK, :])
        acc = part if acc is None else acc + part
    run_fillers(len(fillers))
    for i, xb in enumerate(x1):
        x2 = xb + gt_f * acc[i * rows:(i + 1) * rows]
        store_y(i, _rms(x2, v1024[2:3, :]))


def _group_scan(a, b, tmod):
    rows, lanes = a.shape
    shape3 = (rows // SUBLANES, SUBLANES, lanes)
    a, b, tmod = a.reshape(shape3), b.reshape(shape3), tmod.reshape(shape3)
    s = 1
    while s < SUBLANES:
        a_sh = pltpu.roll(a, s, 1)
        b_sh = pltpu.roll(b, s, 1)
        keep = tmod >= s
        b = jnp.where(keep, a * b_sh + b, b)
        a = jnp.where(keep, a * a_sh, a)
        s *= 2
    return a.reshape(rows, lanes), b.reshape(rows, lanes)


def _prompt_mix_thunks(x_ref, ada, v1024, v512, w_in_ref, wg_ref, wsp_ref, bias_s_ref,
                       conv_ref, lru_ref, vrows_ref, ext_ref, h_ref, merged_ref, j, slot):
    tt = x_ref.shape[0]
    st = {}

    def norm():
        mods = _split_mods(ada)
        st['hm'] = _modulated_norm([x_ref[...]], v1024[0:1, :], mods[1], mods[0])

    def project():
        st['proj'] = proj = _dot(st['hm'], w_in_ref[...])
        ext_ref[SUBLANES:, :] = proj[:, 0:D_RG]
        st['carry'] = h_ref[SUBLANES - 1:SUBLANES, :]

    def recurrent(c):
        r0 = c * CHUNK
        row = lax.broadcasted_iota(jnp.int32, (CHUNK, D_RG), 0)
        yg = st['proj'][r0:r0 + CHUNK, D_RG:2 * D_RG]
        xc = v512[_V_CONV_B:_V_CONV_B + 1, :]
        for k in range(CONV_W):
            off = r0 + SUBLANES - (CONV_W - 1) + k
            xc = xc + ext_ref[off:off + CHUNK, :] * v512[_V_CONV_W + k:_V_CONV_W + k + 1, :]
        r, i_gate = _gates(xc, v512, wg_ref)
        a, mult = _lru_coeffs(r, v512)
        mult = jnp.where(row + (j * tt + r0) == 0, 1.0, mult)
        bterm = mult * i_gate * xc
        a_pref, b_pref = _group_scan(a, bterm, row & (SUBLANES - 1))
        carry = st['carry']
        h_groups = []
        for g in range(CHUNK // SUBLANES):
            lo = g * SUBLANES
            hg = a_pref[lo:lo + SUBLANES, :] * carry + b_pref[lo:lo + SUBLANES, :]
            carry = hg[SUBLANES - 1:SUBLANES, :]
            h_groups.append(hg)
        st['carry'] = carry
        rg_out = _cat(h_groups) * _gelu_tanh(yg)
        merged_ref[r0:r0 + CHUNK, 0:D_RG] = _rms(rg_out, v512[_V_G_RG:_V_G_RG + 1, :]).astype(BF16)
        if r0 + CHUNK == tt:
            h_ref[...] = h_groups[-1]
            lru_ref[slot] = carry
            tail = ext_ref[tt:tt + SUBLANES, :]
            ext_ref[0:SUBLANES, :] = tail
            conv_ref[slot] = tail

    def spatial(c):
        r0 = c * CHUNK
        lane = lax.broadcasted_iota(jnp.int32, (CHUNK, 2 * SG_HEAD_DIM), 1)
        t_idx = lax.broadcasted_iota(jnp.int32, (CHUNK, 2 * CHUNK), 0)
        s_idx = lax.broadcasted_iota(jnp.int32, (CHUNK, 2 * CHUNK), 1) & (CHUNK - 1)
        u = st['proj'][r0:r0 + CHUNK, 2 * D_RG:2 * D_RG + D_SG]
        v = st['proj'][r0:r0 + CHUNK, 2 * D_RG + D_SG:]
        vn = _layer_norm(v, v512[_V_LN_G:_V_LN_G + 1, :], v512[_V_LN_B:_V_LN_B + 1, :])
        outs = []
        for p in range(SG_HEADS // 2):
            vp = vn[:, p * 2 * SG_HEAD_DIM:(p + 1) * 2 * SG_HEAD_DIM]
            rhs = jnp.concatenate([jnp.where(lane < SG_HEAD_DIM, vp, 0.0),
                                   jnp.where(lane >= SG_HEAD_DIM, vp, 0.0)], axis=0).astype(BF16)
            ws_pair = jnp.where(s_idx <= t_idx, wsp_ref[p], jnp.zeros((), BF16))
            outs.append(_dot(ws_pair, rhs))
        sg_out = u * (_cat(outs, 1) + bias_s_ref[...])
        merged_ref[r0:r0 + CHUNK, D_RG:] = _rms(sg_out, v512[_V_G_SG:_V_G_SG + 1, :]).astype(BF16)
        if r0 + CHUNK == tt:
            vrows_ref[slot] = v

    thunks = [norm, project]
    for c in range(tt // CHUNK):
        thunks += [functools.partial(recurrent, c), functools.partial(spatial, c)]
    return thunks


def _prompt_kernel(x_ref, xprev_ref, ada_ref, v1024_ref, v512_ref, w_in_ref, wg_ref, wsp_ref,
                   bias_s_ref, w_out_ref, w_ff1_ref, w_ff2_ref,
                   y_ref, conv_ref, lru_ref, vrows_ref,
                   ext_ref, h_ref, merged_ref, *, tiles_per_seq):
    s = pl.program_id(0)
    n_tiles = pl.num_programs(0) - 1
    j = lax.rem(s, tiles_per_seq)
    seq_cur = jnp.minimum(s, n_tiles - 1) // tiles_per_seq
    seq_prev = jnp.maximum(s - 1, 0) // tiles_per_seq

    @pl.when(j == 0)
    def _():
        ext_ref[0:SUBLANES, :] = jnp.zeros((SUBLANES, D_RG), F32)
        h_ref[...] = jnp.zeros((SUBLANES, D_RG), F32)

    v1024 = v1024_ref[...]
    v512 = v512_ref[...]

    def store_y(_, val):
        y_ref[...] = val

    def stage_a():
        return _prompt_mix_thunks(x_ref, ada_ref[pl.ds(seq_cur, 1), :], v1024, v512, w_in_ref, wg_ref, wsp_ref, bias_s_ref,
                                  conv_ref, lru_ref, vrows_ref, ext_ref, h_ref, merged_ref, j, seq_cur)

    def stage_b(fillers):
        _tail([xprev_ref[...]], merged_ref[...], _split_mods(ada_ref[pl.ds(seq_prev, 1), :]), v1024,
              w_out_ref, w_ff1_ref, w_ff2_ref, store_y, fillers=fillers)

    @pl.when(s == 0)
    def _():
        for thunk in stage_a():
            thunk()

    @pl.when(jnp.logical_and(s > 0, s < n_tiles))
    def _():
        thunks = stage_a()
        thunks.pop(0)()
        stage_b(thunks)

    @pl.when(s == n_tiles)
    def _():
        stage_b(())


def _sample_kernel(x_ref, ada_ref, cs_ref, h0_ref, v1024_ref, v512_ref, w_in_ref, wg_ref, w8_ref, b8_ref,
                   w_out_ref, w_ff1_ref, w_ff2_ref,
                   y_ref, conv_ref, lru_ref, vrows_ref):
    nb, n_t = x_ref.shape[0], x_ref.shape[1] // D_MODEL
    mods = _split_mods(ada_ref[...])
    sh_m, sc_m = mods[0], mods[1]
    v1024 = v1024_ref[...]
    v512 = v512_ref[...]

    def blk(arr, t):
        return arr[t * nb:(t + 1) * nb, :]

    x_blocks = [x_ref[:, t * D_MODEL:(t + 1) * D_MODEL] for t in range(n_t)]
    hm = _modulated_norm(x_blocks, v1024[0:1, :], sc_m, sh_m)
    proj = _dot(hm, w_in_ref[...])
    xr = proj[:, 0:D_RG]
    yg = proj[:, D_RG:2 * D_RG]
    u = proj[:, 2 * D_RG:2 * D_RG + D_SG]
    v = proj[:, 2 * D_RG + D_SG:]

    xp = [cs_ref[:, k * D_RG:(k + 1) * D_RG] for k in range(CONV_W - 1)] + [blk(xr, t) for t in range(n_t)]
    xc_blocks = []
    for t in range(n_t):
        acc = v512[_V_CONV_B:_V_CONV_B + 1, :]
        for k in range(CONV_W):
            acc = acc + xp[t + k] * v512[_V_CONV_W + k:_V_CONV_W + k + 1, :]
        xc_blocks.append(acc)
    for k in range(CONV_W - 1):
        conv_ref[:, k * D_RG:(k + 1) * D_RG] = xp[n_t + k]
    xc = _cat(xc_blocks)

    r, i_gate = _gates(xc, v512, wg_ref)
    a, mult = _lru_coeffs(r, v512)
    bterm = mult * i_gate * xc
    h = h0_ref[...]
    h_blocks = []
    for t in range(n_t):
        h = blk(a, t) * h + blk(bterm, t)
        h_blocks.append(h)
    lru_ref[...] = h
    rg_out = _cat(h_blocks) * _gelu_tanh(yg)

    vn = _layer_norm(v, v512[_V_LN_G:_V_LN_G + 1, :], v512[_V_LN_B:_V_LN_B + 1, :])
    sg_blocks = []
    for t in range(n_t):
        vrows_ref[:, t * D_SG:(t + 1) * D_SG] = blk(v, t)
        mixed = b8_ref[t:t + 1, :]
        for s in range(t + 1):
            mixed = mixed + w8_ref[t, s:s + 1, :] * blk(vn, s)
        sg_blocks.append(blk(u, t) * mixed)
    sg_out = _cat(sg_blocks)

    merged = jnp.concatenate(
        [_rms(rg_out, v512[_V_G_RG:_V_G_RG + 1, :]).astype(BF16),
         _rms(sg_out, v512[_V_G_SG:_V_G_SG + 1, :]).astype(BF16)], axis=1)

    def store_y(t, val):
        y_ref[:, t * D_MODEL:(t + 1) * D_MODEL] = val

    _tail(x_blocks, merged, mods, v1024, w_out_ref, w_ff1_ref, w_ff2_ref, store_y)


def _prep_kernel(cs_ref, cp_ref, w_ada_ref, b_ada_ref, w_in_ref, w_out_ref, w_ff1_ref, w_ff2_ref,
                 ada_ref, w_in_o, w_out_o, w_ff1_o, w_ff2_o):
    c = jnp.concatenate([cs_ref[...], cp_ref[...]], axis=0)
    s = (c * jax.nn.sigmoid(c)).astype(BF16)
    ada_ref[...] = _dot(s, w_ada_ref[...].astype(BF16)) + b_ada_ref[...]
    w_in_o[...] = w_in_ref[...].astype(BF16)
    w_out_o[...] = w_out_ref[...].astype(BF16)
    w_ff1_o[...] = w_ff1_ref[...].astype(BF16)
    w_ff2_o[...] = w_ff2_ref[...].astype(BF16)


def _resident(shape):
    nd = len(shape)
    return pl.BlockSpec(shape, lambda *_: (0,) * nd, pipeline_mode=pl.Buffered(1))


def _block_diag_groups(w):
    per = RG_HEADS // GATE_GROUPS
    w4 = w.reshape(GATE_GROUPS, per, RG_HEAD_DIM, RG_HEAD_DIM)
    bd = jnp.einsum('ghij,hk->ghikj', w4, jnp.eye(per, dtype=w.dtype))
    return bd.reshape(GATE_GROUPS, GATE_W, GATE_W)


def kernel(x_prompt, x_sample, c_prompt, c_sample, state_conv, state_rglru, w_ada, b_ada, g_mix, g_ffn, w_in, conv_w, conv_b, w_ra, b_ra, w_ri, b_ri, lru_l, ln_v_g, ln_v_b, w_s, b_s, g_rg, g_sg, w_out, w_ff1, w_ff2, g_final):
    batch, seq, _ = x_prompt.shape
    dec_batch, dec_seq, _ = x_sample.shape
    assert w_ada.shape[0] == 1, "single-layer step"
    assert seq % PROMPT_TILE == 0 and PROMPT_TILE % CHUNK == 0 and dec_batch % SAMPLE_NB == 0
    assert dec_seq <= CHUNK

    n_c = dec_batch + batch
    assert dec_batch % SUBLANES == 0 and dec_batch % batch == 0
    steps = PREP_STEPS

    def row_block(w):
        return pl.BlockSpec((w.shape[0] // steps, w.shape[1]), lambda n: (n, 0))

    mats = (w_in[0], w_out[0], w_ff1[0], w_ff2[0])
    ada, w_in_b, w_out_b, w_ff1_b, w_ff2_b = pl.pallas_call(
        _prep_kernel,
        out_shape=(jax.ShapeDtypeStruct((n_c, 6 * D_MODEL), F32),) +
                  tuple(jax.ShapeDtypeStruct(w.shape, BF16) for w in mats),
        grid=(steps,),
        in_specs=[pl.BlockSpec((dec_batch, D_MODEL), lambda n: (0, 0)),
                  pl.BlockSpec((batch, D_MODEL), lambda n: (0, 0)),
                  pl.BlockSpec((D_MODEL, 6 * D_MODEL // steps), lambda n: (0, n)),
                  pl.BlockSpec((1, 6 * D_MODEL // steps), lambda n: (0, n))] + [row_block(w) for w in mats],
        out_specs=(pl.BlockSpec((n_c, 6 * D_MODEL // steps), lambda n: (0, n)),) +
                  tuple(row_block(w) for w in mats),
        compiler_params=pltpu.CompilerParams(dimension_semantics=("arbitrary",),
                                             vmem_limit_bytes=VMEM_LIMIT_BYTES),
        name="prep",
    )(c_sample, c_prompt, w_ada[0], b_ada, *mats)

    v1024 = jnp.stack([g_mix[0], g_ffn[0], g_final], axis=0)
    v512 = jnp.concatenate(
        [conv_w[0], conv_b, b_ra[0].reshape(1, D_RG), b_ri[0].reshape(1, D_RG), lru_l,
         ln_v_g, ln_v_b, g_rg, g_sg], axis=0)
    wg = jnp.concatenate([_block_diag_groups(w_ra[0]), _block_diag_groups(w_ri[0])], axis=-1).astype(BF16)
    wsp = (w_s[0].reshape(SG_HEADS // 2, 2, CHUNK, CHUNK).transpose(0, 2, 1, 3)
           .reshape(SG_HEADS // 2, CHUNK, 2 * CHUNK).astype(BF16))
    bias_s = jnp.repeat(b_s[0].T, SG_HEAD_DIM, axis=1)
    w8 = jnp.repeat(w_s[0, :, :dec_seq, :dec_seq].transpose(1, 2, 0), SG_HEAD_DIM, axis=2)
    b8 = bias_s[:dec_seq]

    weights = (w_in_b, wg)
    weight_specs = [_resident(w_in_b.shape), _resident(wg.shape)]
    tail_weights = (w_out_b, w_ff1_b, w_ff2_b)
    tail_specs = [_resident(w.shape) for w in tail_weights]
    cparams = functools.partial(pltpu.CompilerParams, vmem_limit_bytes=VMEM_LIMIT_BYTES)

    tt = PROMPT_TILE
    nt = seq // tt
    n_tiles = batch * nt

    def cur(s):
        return jnp.minimum(s, n_tiles - 1)

    def prev(s):
        return jnp.maximum(s - 1, 0)

    y_p, conv8_p, lru_p, vrows_p = pl.pallas_call(
        functools.partial(_prompt_kernel, tiles_per_seq=nt),
        out_shape=(jax.ShapeDtypeStruct((batch, seq, D_MODEL), F32),
                   jax.ShapeDtypeStruct((batch, SUBLANES, D_RG), F32),
                   jax.ShapeDtypeStruct((batch, 1, D_RG), F32),
                   jax.ShapeDtypeStruct((batch, CHUNK, D_SG), F32)),
        grid=(n_tiles + 1,),
        in_specs=[pl.BlockSpec((None, tt, D_MODEL), lambda s: (cur(s) // nt, cur(s) % nt, 0)),
                  pl.BlockSpec((None, tt, D_MODEL), lambda s: (prev(s) // nt, prev(s) % nt, 0)),
                  pl.BlockSpec((batch, 6 * D_MODEL), lambda s: (dec_batch // batch, 0),
                               pipeline_mode=pl.Buffered(1)),
                  _resident(v1024.shape), _resident(v512.shape)] + weight_specs +
                 [_resident(wsp.shape), _resident(bias_s.shape)] + tail_specs,
        out_specs=(pl.BlockSpec((None, tt, D_MODEL), lambda s: (prev(s) // nt, prev(s) % nt, 0)),
                   pl.BlockSpec((batch, SUBLANES, D_RG), lambda s: (0, 0, 0)),
                   pl.BlockSpec((batch, 1, D_RG), lambda s: (0, 0, 0)),
                   pl.BlockSpec((batch, CHUNK, D_SG), lambda s: (0, 0, 0))),
        scratch_shapes=[pltpu.VMEM((tt + SUBLANES, D_RG), F32), pltpu.VMEM((SUBLANES, D_RG), F32),
                        pltpu.VMEM((tt, D_MODEL), BF16)],
        compiler_params=cparams(dimension_semantics=("arbitrary",)),
        name="prompt_trunk",
    )(x_prompt, x_prompt, ada, v1024, v512, *weights, wsp, bias_s, *tail_weights)
    conv_p = conv8_p[:, SUBLANES - (CONV_W - 1):, :][None]
    lru_p = lru_p.reshape(1, batch, D_RG)
    vrows_p = vrows_p[None]

    nb = SAMPLE_NB
    n_cs = CONV_W - 1

    def seq_rows(width):
        return pl.BlockSpec((nb, width), lambda i: (i, 0))

    y_s, conv_s, lru_s, vrows_s = pl.pallas_call(
        _sample_kernel,
        out_shape=(jax.ShapeDtypeStruct((dec_batch, dec_seq * D_MODEL), F32),
                   jax.ShapeDtypeStruct((dec_batch, n_cs * D_RG), F32),
                   jax.ShapeDtypeStruct((dec_batch, D_RG), F32),
                   jax.ShapeDtypeStruct((dec_batch, dec_seq * D_SG), F32)),
        grid=(dec_batch // nb,),
        in_specs=[seq_rows(dec_seq * D_MODEL), seq_rows(6 * D_MODEL), seq_rows(n_cs * D_RG), seq_rows(D_RG),
                  _resident(v1024.shape), _resident(v512.shape)] + weight_specs +
                 [_resident(w8.shape), _resident(b8.shape)] + tail_specs,
        out_specs=(seq_rows(dec_seq * D_MODEL), seq_rows(n_cs * D_RG), seq_rows(D_RG), seq_rows(dec_seq * D_SG)),
        compiler_params=cparams(dimension_semantics=("arbitrary",)),
        name="sample_trunk",
    )(x_sample.reshape(dec_batch, dec_seq * D_MODEL), ada, state_conv[0].reshape(dec_batch, n_cs * D_RG),
      state_rglru[0], v1024, v512, *weights, w8, b8, *tail_weights)

    return (y_p, y_s.reshape(dec_batch, dec_seq, D_MODEL), conv_p, lru_p, vrows_p,
            conv_s.reshape(1, dec_batch, n_cs, D_RG), lru_s[None], vrows_s.reshape(1, dec_batch, dec_seq, D_SG))
```

```python
import functools

import jax
import jax.numpy as jnp
from jax import lax
from jax.experimental import pallas as pl
from jax.experimental.pallas import tpu as pltpu

D_MODEL = 1024
D_RG = 512
D_SG = 512
D_IN = 2 * D_RG + 2 * D_SG
D_FF = 4 * D_MODEL
RG_HEADS = 8
RG_HEAD_DIM = D_RG // RG_HEADS
SG_HEADS = 8
SG_HEAD_DIM = D_SG // SG_HEADS
CHUNK = 128
CONV_W = 4
LRU_C = 8.0
EPS = 1e-6

SUBLANES = 8
LANES = 128
GATE_GROUPS = 2
GATE_W = D_RG // GATE_GROUPS
PROMPT_TILE = 512
SAMPLE_NB = 64
FF_CHUNK = 1024
N_FF_CHUNKS = D_FF // FF_CHUNK
PREP_STEPS = 8
VMEM_LIMIT_BYTES = 60000 * 1024

F32 = jnp.float32
BF16 = jnp.bfloat16

_V_CONV_W, _V_CONV_B, _V_B_RA, _V_B_RI, _V_LRU_L, _V_LN_G, _V_LN_B, _V_G_RG, _V_G_SG = 0, 4, 5, 6, 7, 8, 9, 10, 11


def _dot(a, b):
    return jnp.dot(a, b, preferred_element_type=F32)


def _rms(x, g):
    ms = jnp.mean(x * x, axis=-1, keepdims=True)
    return (x * lax.rsqrt(ms + EPS)) * g


def _gelu_tanh(x):
    k = 2.0 * 0.7978845608028654
    w = x * (k + (k * 0.044715) * (x * x))
    return x / (1.0 + jnp.exp(-w))


def _layer_norm(x, g, b):
    mu = jnp.mean(x, axis=-1, keepdims=True)
    xc = x - mu
    var = jnp.mean(xc * xc, axis=-1, keepdims=True)
    return (xc * lax.rsqrt(var + EPS)) * g + b


def _cat(parts, axis=0):
    return parts[0] if len(parts) == 1 else jnp.concatenate(parts, axis=axis)


def _split_mods(ada):
    return [ada[:, k * D_MODEL:(k + 1) * D_MODEL] for k in range(6)]


def _modulated_norm(x_blocks, g, sc, sh):
    gain = g * (1.0 + sc)
    return _cat([(_rms(xb, gain) + sh).astype(BF16) for xb in x_blocks])


def _gate_preacts(xc, wg_ref):
    xcb = xc.astype(BF16)
    return _cat([_dot(xcb[:, g * GATE_W:(g + 1) * GATE_W], wg_ref[g]) for g in range(GATE_GROUPS)], 1)


def _gates(pre, v512):
    r_parts, i_parts = [], []
    for g in range(GATE_GROUPS):
        lo, hi = g * GATE_W, (g + 1) * GATE_W
        ra = pre[:, 2 * lo:2 * lo + GATE_W]
        ri = pre[:, 2 * lo + GATE_W:2 * hi]
        r_parts.append(jax.nn.sigmoid(ra + v512[_V_B_RA:_V_B_RA + 1, lo:hi]))
        i_parts.append(jax.nn.sigmoid(ri + v512[_V_B_RI:_V_B_RI + 1, lo:hi]))
    return _cat(r_parts, 1), _cat(i_parts, 1)


def _lru_coeffs(r, v512):
    lam = v512[_V_LRU_L:_V_LRU_L + 1, :]
    log_sig = jnp.minimum(lam, 0.0) - jnp.log1p(jnp.exp(-jnp.abs(lam)))
    log_a = (LRU_C * r) * log_sig
    a = jnp.exp(log_a)
    z = 1.0 - a * a
    mult = jnp.where(z > 0.0, z * lax.rsqrt(z), 0.0)
    return a, mult


def _tail_thunks(x_blocks, merged_fn, mods, v1024, w_out_ref, w_ff1_ref, w_ff2_ref, store_y):
    _, _, gt_m, sh_f, sc_f, gt_f = mods
    rows = x_blocks[0].shape[0]
    st = {}

    def w_out():
        mm = _dot(merged_fn(), w_out_ref[...])
        st['x1'] = [xb + gt_m * mm[i * rows:(i + 1) * rows] for i, xb in enumerate(x_blocks)]

    def hf():
        st['hf'] = _modulated_norm(st['x1'], v1024[1:2, :], sc_f, sh_f)

    def ff1(c):
        pre = _dot(st['hf'], w_ff1_ref[:, c * FF_CHUNK:(c + 1) * FF_CHUNK])
        h1 = jnp.maximum(pre.astype(BF16), 0.0)
        st['h1_%d' % c] = h1 * h1

    def ff2(c):
        part = _dot(st.pop('h1_%d' % c), w_ff2_ref[c * FF_CHUNK:(c + 1) * FF_CHUNK, :])
        st['acc'] = part if c == 0 else st['acc'] + part

    def final():
        for i, xb in enumerate(st['x1']):
            x2 = xb + gt_f * st['acc'][i * rows:(i + 1) * rows]
            store_y(i, _rms(x2, v1024[2:3, :]))

    thunks = {'w_out': w_out, 'hf': hf, 'final': final}
    for c in range(N_FF_CHUNKS):
        thunks['ff1_%d' % c] = functools.partial(ff1, c)
        thunks['ff2_%d' % c] = functools.partial(ff2, c)
    return thunks


TAIL_ORDER = (['w_out', 'hf', 'ff1_0'] +
              [name for c in range(N_FF_CHUNKS)
               for name in (['ff1_%d' % (c + 1)] if c + 1 < N_FF_CHUNKS else []) + ['ff2_%d' % c]] +
              ['final'])


def _run(thunks, order):
    assert sorted(order) == sorted(thunks), (sorted(order), sorted(thunks))
    for name in order:
        thunks[name]()


def _prompt_mix_thunks(x_ref, ada, v1024, v512, w_in_ref, wg_ref, wsp_ref, bias_s_ref,
                       conv_ref, lru_ref, vrows_ref, xperm_ref, hperm_ref, tail_ref, h_ref, merged_ref, j, slot):
    tt = x_ref.shape[0]
    n_sub = SUBLANES
    sub_len = tt // n_sub
    n_slab = D_RG // LANES
    st = {}

    def norm():
        mods = _split_mods(ada)
        st['hm'] = _modulated_norm([x_ref[...]], v1024[0:1, :], mods[1], mods[0])

    def project():
        st['proj'] = proj = _dot(st['hm'], w_in_ref[...])
        xr = proj[:, 0:D_RG]
        st['xr_tail'] = xr[tt - SUBLANES:, :]
        for p in range(n_sub):
            for s in range(n_slab):
                xperm_ref[s, pl.ds(p, sub_len, stride=n_sub), :] = (
                    xr[p * sub_len:(p + 1) * sub_len, s * LANES:(s + 1) * LANES])

    def conv():
        xr_p = _cat([xperm_ref[s] for s in range(n_slab)], 1)
        sub = lax.broadcasted_iota(jnp.int32, (SUBLANES, D_RG), 0)
        prev_tail = tail_ref[...]
        head = []
        for k in range(CONV_W - 1, 0, -1):
            grp = pltpu.roll(xr_p[(sub_len - k) * SUBLANES:(sub_len - k + 1) * SUBLANES, :], 1, 0)
            head.append(jnp.where(sub == 0, prev_tail[SUBLANES - k:SUBLANES - k + 1, :], grp))
        ext = _cat(head + [xr_p])
        xc = v512[_V_CONV_B:_V_CONV_B + 1, :]
        for k in range(CONV_W):
            xc = xc + ext[k * SUBLANES:k * SUBLANES + tt, :] * v512[_V_CONV_W + k:_V_CONV_W + k + 1, :]
        st['xc'] = xc
        tail_ref[...] = st['xr_tail']
        conv_ref[slot] = st['xr_tail']

    def gate_mm():
        st['gate_pre'] = _gate_preacts(st['xc'], wg_ref)

    def coefficients():
        r, i_gate = _gates(st.pop('gate_pre'), v512)
        a, mult = _lru_coeffs(r, v512)
        row = lax.broadcasted_iota(jnp.int32, (tt, D_RG), 0)
        mult = jnp.where(row + j * tt == 0, 1.0, mult)
        st['a'] = a
        st['b'] = mult * i_gate * st.pop('xc')

    def recurrence():
        a, b = st['a'], st['b']
        sub = lax.broadcasted_iota(jnp.int32, (SUBLANES, D_RG), 0)
        hs, ps = [], []
        for q in range(sub_len):
            aq = a[q * SUBLANES:(q + 1) * SUBLANES, :]
            bq = b[q * SUBLANES:(q + 1) * SUBLANES, :]
            hs.append(bq if q == 0 else aq * hs[-1] + bq)
            ps.append(aq if q == 0 else aq * ps[-1])
        init = jnp.where(sub == 0, h_ref[SUBLANES - 1:SUBLANES, :], 0.0)
        for p in range(1, n_sub):
            end = hs[-1] + ps[-1] * init
            init = jnp.where(sub == p, pltpu.roll(end, 1, 0), init)
        end = hs[-1] + ps[-1] * init
        h_ref[...] = end
        lru_ref[slot] = end[SUBLANES - 1:SUBLANES, :]
        h_perm = _cat([hq + pq * init for hq, pq in zip(hs, ps)])
        for s in range(n_slab):
            hperm_ref[s] = h_perm[:, s * LANES:(s + 1) * LANES]

    def recurrent_out(c):
        r0 = c * CHUNK
        yg = st['proj'][r0:r0 + CHUNK, D_RG:2 * D_RG]
        subs = range(r0 // sub_len, (r0 + CHUNK) // sub_len)
        h = _cat([_cat([hperm_ref[s, pl.ds(p, sub_len, stride=n_sub), :] for p in subs]) for s in range(n_slab)], 1)
        rg_out = h * _gelu_tanh(yg)
        merged_ref[r0:r0 + CHUNK, 0:D_RG] = _rms(rg_out, v512[_V_G_RG:_V_G_RG + 1, :]).astype(BF16)

    def spatial_pre(c):
        r0 = c * CHUNK
        lane = lax.broadcasted_iota(jnp.int32, (CHUNK, 2 * SG_HEAD_DIM), 1)
        v = st['proj'][r0:r0 + CHUNK, 2 * D_RG + D_SG:]
        vn = _layer_norm(v, v512[_V_LN_G:_V_LN_G + 1, :], v512[_V_LN_B:_V_LN_B + 1, :])
        rhs = []
        for p in range(SG_HEADS // 2):
            vp = vn[:, p * 2 * SG_HEAD_DIM:(p + 1) * 2 * SG_HEAD_DIM]
            rhs.append(jnp.concatenate([jnp.where(lane < SG_HEAD_DIM, vp, 0.0),
                                        jnp.where(lane >= SG_HEAD_DIM, vp, 0.0)], axis=0).astype(BF16))
        st['rhs_%d' % c] = rhs
        if r0 + CHUNK == tt:
            vrows_ref[slot] = v

    def spatial_mm(c):
        t_idx = lax.broadcasted_iota(jnp.int32, (CHUNK, 2 * CHUNK), 0)
        s_idx = lax.broadcasted_iota(jnp.int32, (CHUNK, 2 * CHUNK), 1) & (CHUNK - 1)
        outs = []
        for p, rhs in enumerate(st.pop('rhs_%d' % c)):
            ws_pair = jnp.where(s_idx <= t_idx, wsp_ref[p], jnp.zeros((), BF16))
            outs.append(_dot(ws_pair, rhs))
        st['mixed_%d' % c] = _cat(outs, 1)

    def spatial_post(c):
        r0 = c * CHUNK
        u = st['proj'][r0:r0 + CHUNK, 2 * D_RG:2 * D_RG + D_SG]
        sg_out = u * (st.pop('mixed_%d' % c) + bias_s_ref[...])
        merged_ref[r0:r0 + CHUNK, D_RG:] = _rms(sg_out, v512[_V_G_SG:_V_G_SG + 1, :]).astype(BF16)

    thunks = {'norm': norm, 'w_in': project, 'conv': conv, 'gate_mm': gate_mm, 'coef': coefficients,
              'rec': recurrence}
    for c in range(tt // CHUNK):
        thunks['sp_pre_%d' % c] = functools.partial(spatial_pre, c)
        thunks['sp_mm_%d' % c] = functools.partial(spatial_mm, c)
        thunks['sp_post_%d' % c] = functools.partial(spatial_post, c)
        thunks['rg_out_%d' % c] = functools.partial(recurrent_out, c)
    return thunks


N_PROMPT_CHUNKS = PROMPT_TILE // CHUNK
MIX_ORDER = (['norm', 'w_in', 'conv', 'gate_mm', 'coef', 'rec'] +
             [f'{name}_{c}' for c in range(N_PROMPT_CHUNKS) for name in ('sp_pre', 'sp_mm', 'sp_post', 'rg_out')])

assert N_PROMPT_CHUNKS == 4 and N_FF_CHUNKS == 4
INTERLEAVED_ORDER = [
    ('a', 'norm'), ('b', 'w_out'), ('a', 'w_in'), ('b', 'hf'), ('b', 'ff1_0'), ('a', 'conv'),
    ('b', 'ff1_1'), ('a', 'sp_pre_0'), ('a', 'sp_pre_1'), ('b', 'ff2_0'), ('a', 'sp_mm_0'), ('a', 'sp_mm_1'),
    ('b', 'ff1_2'), ('a', 'gate_mm'), ('a', 'sp_post_0'), ('a', 'sp_post_1'), ('b', 'ff2_1'), ('a', 'coef'),
    ('a', 'sp_pre_2'), ('a', 'sp_pre_3'), ('b', 'ff1_3'), ('a', 'sp_mm_2'), ('a', 'sp_mm_3'), ('a', 'rec'),
    ('b', 'ff2_2'), ('a', 'sp_post_2'), ('a', 'sp_post_3'), ('a', 'rg_out_0'), ('a', 'rg_out_1'),
    ('b', 'ff2_3'), ('a', 'rg_out_2'), ('a', 'rg_out_3'), ('b', 'final'),
]


def _prompt_kernel(x_ref, xprev_ref, ada_ref, v1024_ref, v512_ref, w_in_ref, wg_ref, wsp_ref,
                   bias_s_ref, w_out_ref, w_ff1_ref, w_ff2_ref,
                   y_ref, conv_ref, lru_ref, vrows_ref,
                   xperm_ref, hperm_ref, tail_ref, h_ref, merged_ref, *, tiles_per_seq):
    s = pl.program_id(0)
    n_tiles = pl.num_programs(0) - 1
    j = lax.rem(s, tiles_per_seq)
    seq_cur = jnp.minimum(s, n_tiles - 1) // tiles_per_seq
    seq_prev = jnp.maximum(s - 1, 0) // tiles_per_seq

    @pl.when(j == 0)
    def _():
        tail_ref[...] = jnp.zeros((SUBLANES, D_RG), F32)
        h_ref[...] = jnp.zeros((SUBLANES, D_RG), F32)

    v1024 = v1024_ref[...]
    v512 = v512_ref[...]

    def store_y(_, val):
        y_ref[...] = val

    def stage_a():
        return _prompt_mix_thunks(x_ref, ada_ref[pl.ds(seq_cur, 1), :], v1024, v512, w_in_ref, wg_ref, wsp_ref,
                                  bias_s_ref, conv_ref, lru_ref, vrows_ref, xperm_ref, hperm_ref, tail_ref, h_ref,
                                  merged_ref, j, seq_cur)

    def stage_b():
        return _tail_thunks([xprev_ref[...]], lambda: merged_ref[...], _split_mods(ada_ref[pl.ds(seq_prev, 1), :]),
                            v1024, w_out_ref, w_ff1_ref, w_ff2_ref, store_y)

    @pl.when(s == 0)
    def _():
        _run(stage_a(), MIX_ORDER)

    @pl.when(jnp.logical_and(s > 0, s < n_tiles))
    def _():
        stages = {'a': stage_a(), 'b': stage_b()}
        assert sorted(n for k, n in INTERLEAVED_ORDER if k == 'a') == sorted(stages['a'])
        assert sorted(n for k, n in INTERLEAVED_ORDER if k == 'b') == sorted(stages['b'])
        for stage, name in INTERLEAVED_ORDER:
            stages[stage][name]()

    @pl.when(s == n_tiles)
    def _():
        _run(stage_b(), TAIL_ORDER)


def _sample_kernel(x_ref, ada_ref, cs_ref, h0_ref, v1024_ref, v512_ref, w_in_ref, wg_ref, w8_ref, b8_ref,
                   w_out_ref, w_ff1_ref, w_ff2_ref,
                   y_ref, conv_ref, lru_ref, vrows_ref):
    nb, n_t = x_ref.shape[0], x_ref.shape[1] // D_MODEL
    mods = _split_mods(ada_ref[...])
    sh_m, sc_m = mods[0], mods[1]
    v1024 = v1024_ref[...]
    v512 = v512_ref[...]

    def blk(arr, t):
        return arr[t * nb:(t + 1) * nb, :]

    x_blocks = [x_ref[:, t * D_MODEL:(t + 1) * D_MODEL] for t in range(n_t)]
    hm = _modulated_norm(x_blocks, v1024[0:1, :], sc_m, sh_m)
    proj = _dot(hm, w_in_ref[...])
    xr = proj[:, 0:D_RG]
    yg = proj[:, D_RG:2 * D_RG]
    u = proj[:, 2 * D_RG:2 * D_RG + D_SG]
    v = proj[:, 2 * D_RG + D_SG:]

    xp = [cs_ref[:, k * D_RG:(k + 1) * D_RG] for k in range(CONV_W - 1)] + [blk(xr, t) for t in range(n_t)]
    xc_blocks = []
    for t in range(n_t):
        acc = v512[_V_CONV_B:_V_CONV_B + 1, :]
        for k in range(CONV_W):
            acc = acc + xp[t + k] * v512[_V_CONV_W + k:_V_CONV_W + k + 1, :]
        xc_blocks.append(acc)
    for k in range(CONV_W - 1):
        conv_ref[:, k * D_RG:(k + 1) * D_RG] = xp[n_t + k]
    xc = _cat(xc_blocks)

    r, i_gate = _gates(_gate_preacts(xc, wg_ref), v512)
    a, mult = _lru_coeffs(r, v512)
    bterm = mult * i_gate * xc
    h = h0_ref[...]
    h_blocks = []
    for t in range(n_t):
        h = blk(a, t) * h + blk(bterm, t)
        h_blocks.append(h)
    lru_ref[...] = h
    rg_out = _cat(h_blocks) * _gelu_tanh(yg)

    vn = _layer_norm(v, v512[_V_LN_G:_V_LN_G + 1, :], v512[_V_LN_B:_V_LN_B + 1, :])
    sg_blocks = []
    for t in range(n_t):
        vrows_ref[:, t * D_SG:(t + 1) * D_SG] = blk(v, t)
        mixed = b8_ref[t:t + 1, :]
        for s in range(t + 1):
            mixed = mixed + w8_ref[t, s:s + 1, :] * blk(vn, s)
        sg_blocks.append(blk(u, t) * mixed)
    sg_out = _cat(sg_blocks)

    merged = jnp.concatenate(
        [_rms(rg_out, v512[_V_G_RG:_V_G_RG + 1, :]).astype(BF16),
         _rms(sg_out, v512[_V_G_SG:_V_G_SG + 1, :]).astype(BF16)], axis=1)

    def store_y(t, val):
        y_ref[:, t * D_MODEL:(t + 1) * D_MODEL] = val

    _run(_tail_thunks(x_blocks, lambda: merged, mods, v1024, w_out_ref, w_ff1_ref, w_ff2_ref, store_y), TAIL_ORDER)


def _prep_kernel(cs_ref, cp_ref, w_ada_ref, b_ada_ref, w_in_ref, w_out_ref, w_ff1_ref, w_ff2_ref,
                 ada_ref, w_in_o, w_out_o, w_ff1_o, w_ff2_o):
    c = jnp.concatenate([cs_ref[...], cp_ref[...]], axis=0)
    s = (c * jax.nn.sigmoid(c)).astype(BF16)
    ada_ref[...] = _dot(s, w_ada_ref[...].astype(BF16)) + b_ada_ref[...]
    w_in_o[...] = w_in_ref[...].astype(BF16)
    w_out_o[...] = w_out_ref[...].astype(BF16)
    w_ff1_o[...] = w_ff1_ref[...].astype(BF16)
    w_ff2_o[...] = w_ff2_ref[...].astype(BF16)


def _resident(shape):
    nd = len(shape)
    return pl.BlockSpec(shape, lambda *_: (0,) * nd, pipeline_mode=pl.Buffered(1))


def _block_diag_groups(w):
    per = RG_HEADS // GATE_GROUPS
    w4 = w.reshape(GATE_GROUPS, per, RG_HEAD_DIM, RG_HEAD_DIM)
    bd = jnp.einsum('ghij,hk->ghikj', w4, jnp.eye(per, dtype=w.dtype))
    return bd.reshape(GATE_GROUPS, GATE_W, GATE_W)


def kernel(x_prompt, x_sample, c_prompt, c_sample, state_conv, state_rglru, w_ada, b_ada, g_mix, g_ffn, w_in, conv_w, conv_b, w_ra, b_ra, w_ri, b_ri, lru_l, ln_v_g, ln_v_b, w_s, b_s, g_rg, g_sg, w_out, w_ff1, w_ff2, g_final):
    batch, seq, _ = x_prompt.shape
    dec_batch, dec_seq, _ = x_sample.shape
    assert w_ada.shape[0] == 1, "single-layer step"
    assert seq % PROMPT_TILE == 0 and PROMPT_TILE % CHUNK == 0 and dec_batch % SAMPLE_NB == 0
    assert dec_seq <= CHUNK

    n_c = dec_batch + batch
    assert dec_batch % SUBLANES == 0 and dec_batch % batch == 0
    steps = PREP_STEPS

    def row_block(w):
        return pl.BlockSpec((w.shape[0] // steps, w.shape[1]), lambda n: (n, 0))

    mats = (w_in[0], w_out[0], w_ff1[0], w_ff2[0])
    ada, w_in_b, w_out_b, w_ff1_b, w_ff2_b = pl.pallas_call(
        _prep_kernel,
        out_shape=(jax.ShapeDtypeStruct((n_c, 6 * D_MODEL), F32),) +
                  tuple(jax.ShapeDtypeStruct(w.shape, BF16) for w in mats),
        grid=(steps,),
        in_specs=[pl.BlockSpec((dec_batch, D_MODEL), lambda n: (0, 0)),
                  pl.BlockSpec((batch, D_MODEL), lambda n: (0, 0)),
                  pl.BlockSpec((D_MODEL, 6 * D_MODEL // steps), lambda n: (0, n)),
                  pl.BlockSpec((1, 6 * D_MODEL // steps), lambda n: (0, n))] + [row_block(w) for w in mats],
        out_specs=(pl.BlockSpec((n_c, 6 * D_MODEL // steps), lambda n: (0, n)),) +
                  tuple(row_block(w) for w in mats),
        compiler_params=pltpu.CompilerParams(dimension_semantics=("arbitrary",),
                                             vmem_limit_bytes=VMEM_LIMIT_BYTES),
        name="prep",
    )(c_sample, c_prompt, w_ada[0], b_ada, *mats)

    v1024 = jnp.stack([g_mix[0], g_ffn[0], g_final], axis=0)
    v512 = jnp.concatenate(
        [conv_w[0], conv_b, b_ra[0].reshape(1, D_RG), b_ri[0].reshape(1, D_RG), lru_l,
         ln_v_g, ln_v_b, g_rg, g_sg], axis=0)
    wg = jnp.concatenate([_block_diag_groups(w_ra[0]), _block_diag_groups(w_ri[0])], axis=-1).astype(BF16)
    wsp = (w_s[0].reshape(SG_HEADS // 2, 2, CHUNK, CHUNK).transpose(0, 2, 1, 3)
           .reshape(SG_HEADS // 2, CHUNK, 2 * CHUNK).astype(BF16))
    bias_s = jnp.repeat(b_s[0].T, SG_HEAD_DIM, axis=1)
    w8 = jnp.repeat(w_s[0, :, :dec_seq, :dec_seq].transpose(1, 2, 0), SG_HEAD_DIM, axis=2)
    b8 = bias_s[:dec_seq]

    weights = (w_in_b, wg)
    weight_specs = [_resident(w_in_b.shape), _resident(wg.shape)]
    tail_weights = (w_out_b, w_ff1_b, w_ff2_b)
    tail_specs = [_resident(w.shape) for w in tail_weights]
    cparams = functools.partial(pltpu.CompilerParams, vmem_limit_bytes=VMEM_LIMIT_BYTES)

    tt = PROMPT_TILE
    nt = seq // tt
    n_tiles = batch * nt

    def cur(s):
        return jnp.minimum(s, n_tiles - 1)

    def prev(s):
        return jnp.maximum(s - 1, 0)

    y_p, conv8_p, lru_p, vrows_p = pl.pallas_call(
        functools.partial(_prompt_kernel, tiles_per_seq=nt),
        out_shape=(jax.ShapeDtypeStruct((batch, seq, D_MODEL), F32),
                   jax.ShapeDtypeStruct((batch, SUBLANES, D_RG), F32),
                   jax.ShapeDtypeStruct((batch, 1, D_RG), F32),
                   jax.ShapeDtypeStruct((batch, CHUNK, D_SG), F32)),
        grid=(n_tiles + 1,),
        in_specs=[pl.BlockSpec((None, tt, D_MODEL), lambda s: (cur(s) // nt, cur(s) % nt, 0)),
                  pl.BlockSpec((None, tt, D_MODEL), lambda s: (prev(s) // nt, prev(s) % nt, 0)),
                  pl.BlockSpec((batch, 6 * D_MODEL), lambda s: (dec_batch // batch, 0),
                               pipeline_mode=pl.Buffered(1)),
                  _resident(v1024.shape), _resident(v512.shape)] + weight_specs +
                 [_resident(wsp.shape), _resident(bias_s.shape)] + tail_specs,
        out_specs=(pl.BlockSpec((None, tt, D_MODEL), lambda s: (prev(s) // nt, prev(s) % nt, 0)),
                   pl.BlockSpec((batch, SUBLANES, D_RG), lambda s: (0, 0, 0)),
                   pl.BlockSpec((batch, 1, D_RG), lambda s: (0, 0, 0)),
                   pl.BlockSpec((batch, CHUNK, D_SG), lambda s: (0, 0, 0))),
        scratch_shapes=[pltpu.VMEM((D_RG // LANES, tt, LANES), F32), pltpu.VMEM((D_RG // LANES, tt, LANES), F32),
                        pltpu.VMEM((SUBLANES, D_RG), F32), pltpu.VMEM((SUBLANES, D_RG), F32),
                        pltpu.VMEM((tt, D_MODEL), BF16)],
        compiler_params=cparams(dimension_semantics=("arbitrary",)),
        name="prompt_trunk",
    )(x_prompt, x_prompt, ada, v1024, v512, *weights, wsp, bias_s, *tail_weights)
    conv_p = conv8_p[:, SUBLANES - (CONV_W - 1):, :][None]
    lru_p = lru_p.reshape(1, batch, D_RG)
    vrows_p = vrows_p[None]

    nb = SAMPLE_NB
    n_cs = CONV_W - 1

    def seq_rows(width):
        return pl.BlockSpec((nb, width), lambda i: (i, 0))

    y_s, conv_s, lru_s, vrows_s = pl.pallas_call(
        _sample_kernel,
        out_shape=(jax.ShapeDtypeStruct((dec_batch, dec_seq * D_MODEL), F32),
                   jax.ShapeDtypeStruct((dec_batch, n_cs * D_RG), F32),
                   jax.ShapeDtypeStruct((dec_batch, D_RG), F32),
                   jax.ShapeDtypeStruct((dec_batch, dec_seq * D_SG), F32)),
        grid=(dec_batch // nb,),
        in_specs=[seq_rows(dec_seq * D_MODEL), seq_rows(6 * D_MODEL), seq_rows(n_cs * D_RG), seq_rows(D_RG),
                  _resident(v1024.shape), _resident(v512.shape)] + weight_specs +
                 [_resident(w8.shape), _resident(b8.shape)] + tail_specs,
        out_specs=(seq_rows(dec_seq * D_MODEL), seq_rows(n_cs * D_RG), seq_rows(D_RG), seq_rows(dec_seq * D_SG)),
        compiler_params=cparams(dimension_semantics=("arbitrary",)),
        name="sample_trunk",
    )(x_sample.reshape(dec_batch, dec_seq * D_MODEL), ada, state_conv[0].reshape(dec_batch, n_cs * D_RG),
      state_rglru[0], v1024, v512, *weights, w8, b8, *tail_weights)

    return (y_p, y_s.reshape(dec_batch, dec_seq, D_MODEL), conv_p, lru_p, vrows_p,
            conv_s.reshape(1, dec_batch, n_cs, D_RG), lru_s[None], vrows_s.reshape(1, dec_batch, dec_seq, D_SG))
```

```python
import functools

import jax
import jax.numpy as jnp
from jax import lax
from jax.experimental import pallas as pl
from jax.experimental.pallas import tpu as pltpu

D_MODEL = 1024
D_RG = 512
D_SG = 512
D_IN = 2 * D_RG + 2 * D_SG
D_FF = 4 * D_MODEL
RG_HEADS = 8
RG_HEAD_DIM = D_RG // RG_HEADS
SG_HEADS = 8
SG_HEAD_DIM = D_SG // SG_HEADS
CHUNK = 128
CONV_W = 4
LRU_C = 8.0
EPS = 1e-6

SUBLANES = 8
LANES = 128
GATE_GROUPS = 2
GATE_W = D_RG // GATE_GROUPS
PROMPT_TILE = 512
SAMPLE_NB = 64
FF_CHUNK = 1024
N_FF_CHUNKS = D_FF // FF_CHUNK
PREP_STEPS = 8
VMEM_LIMIT_BYTES = 60000 * 1024

F32 = jnp.float32
BF16 = jnp.bfloat16

_V_CONV_W, _V_CONV_B, _V_B_RA, _V_B_RI, _V_LRU_L, _V_LN_G, _V_LN_B, _V_G_RG, _V_G_SG = 0, 4, 5, 6, 7, 8, 9, 10, 11


def _dot(a, b):
    return jnp.dot(a, b, preferred_element_type=F32)


def _rms(x, g):
    ms = jnp.mean(x * x, axis=-1, keepdims=True)
    return (x * lax.rsqrt(ms + EPS)) * g


def _sigmoid(x):
    return 0.5 * jnp.tanh(0.5 * x) + 0.5


def _gelu_tanh(x):
    k = 0.7978845608028654
    u = x * (k + (k * 0.044715) * (x * x))
    return (0.5 * x) * (1.0 + jnp.tanh(u))


def _layer_norm(x, g, b):
    mu = jnp.mean(x, axis=-1, keepdims=True)
    xc = x - mu
    var = jnp.mean(xc * xc, axis=-1, keepdims=True)
    return (xc * lax.rsqrt(var + EPS)) * g + b


def _cat(parts, axis=0):
    return parts[0] if len(parts) == 1 else jnp.concatenate(parts, axis=axis)


def _split_mods(ada):
    return [ada[:, k * D_MODEL:(k + 1) * D_MODEL] for k in range(6)]


def _modulated_norm(x_blocks, g, sc, sh):
    gain = g * (1.0 + sc)
    return _cat([(_rms(xb, gain) + sh).astype(BF16) for xb in x_blocks])


def _gate_preacts(xc, wg_ref):
    xcb = xc.astype(BF16)
    return _cat([_dot(xcb[:, g * GATE_W:(g + 1) * GATE_W], wg_ref[g]) for g in range(GATE_GROUPS)], 1)


def _gates(pre, v512):
    r_parts, i_parts = [], []
    for g in range(GATE_GROUPS):
        lo, hi = g * GATE_W, (g + 1) * GATE_W
        ra = pre[:, 2 * lo:2 * lo + GATE_W]
        ri = pre[:, 2 * lo + GATE_W:2 * hi]
        r_parts.append(_sigmoid(ra + v512[_V_B_RA:_V_B_RA + 1, lo:hi]))
        i_parts.append(_sigmoid(ri + v512[_V_B_RI:_V_B_RI + 1, lo:hi]))
    return _cat(r_parts, 1), _cat(i_parts, 1)


def _lru_coeffs(r, v512):
    lam = v512[_V_LRU_L:_V_LRU_L + 1, :]
    log_sig = jnp.minimum(lam, 0.0) - jnp.log1p(jnp.exp(-jnp.abs(lam)))
    log_a = (LRU_C * r) * log_sig
    a = jnp.exp(log_a)
    z = 1.0 - a * a
    mult = jnp.where(z > 0.0, z * lax.rsqrt(z), 0.0)
    return a, mult


def _tail_thunks(x_blocks, merged_fn, mods, v1024, w_out_ref, w_ff1_ref, w_ff2_ref, store_y):
    _, _, gt_m, sh_f, sc_f, gt_f = mods
    rows = x_blocks[0].shape[0]
    st = {}

    def w_out():
        mm = _dot(merged_fn(), w_out_ref[...])
        st['x1'] = [xb + gt_m * mm[i * rows:(i + 1) * rows] for i, xb in enumerate(x_blocks)]

    def hf():
        st['hf'] = _modulated_norm(st['x1'], v1024[1:2, :], sc_f, sh_f)

    def ff1(c):
        pre = _dot(st['hf'], w_ff1_ref[:, c * FF_CHUNK:(c + 1) * FF_CHUNK])
        h1 = jnp.maximum(pre.astype(BF16), 0.0)
        st['h1_%d' % c] = h1 * h1

    def ff2(c):
        part = _dot(st.pop('h1_%d' % c), w_ff2_ref[c * FF_CHUNK:(c + 1) * FF_CHUNK, :])
        st['acc'] = part if c == 0 else st['acc'] + part

    def final():
        for i, xb in enumerate(st['x1']):
            x2 = xb + gt_f * st['acc'][i * rows:(i + 1) * rows]
            store_y(i, _rms(x2, v1024[2:3, :]))

    thunks = {'w_out': w_out, 'hf': hf, 'final': final}
    for c in range(N_FF_CHUNKS):
        thunks['ff1_%d' % c] = functools.partial(ff1, c)
        thunks['ff2_%d' % c] = functools.partial(ff2, c)
    return thunks


TAIL_ORDER = (['w_out', 'hf', 'ff1_0'] +
              [name for c in range(N_FF_CHUNKS)
               for name in (['ff1_%d' % (c + 1)] if c + 1 < N_FF_CHUNKS else []) + ['ff2_%d' % c]] +
              ['final'])


def _run(thunks, order):
    assert sorted(order) == sorted(thunks), (sorted(order), sorted(thunks))
    for name in order:
        thunks[name]()


def _prompt_mix_thunks(x_ref, ada, v1024, v512, w_in_ref, wg_ref, wsp_ref, bias_s_ref,
                       conv_ref, lru_ref, vrows_ref, xperm_ref, hperm_ref, tail_ref, h_ref, merged_ref, j, slot):
    tt = x_ref.shape[0]
    n_sub = SUBLANES
    sub_len = tt // n_sub
    n_slab = D_RG // LANES
    st = {}

    def norm():
        mods = _split_mods(ada)
        st['hm'] = _modulated_norm([x_ref[...]], v1024[0:1, :], mods[1], mods[0])

    def project():
        st['proj'] = proj = _dot(st['hm'], w_in_ref[...])
        xr = proj[:, 0:D_RG]
        st['xr_tail'] = xr[tt - SUBLANES:, :]
        for p in range(n_sub):
            for s in range(n_slab):
                xperm_ref[s, pl.ds(p, sub_len, stride=n_sub), :] = (
                    xr[p * sub_len:(p + 1) * sub_len, s * LANES:(s + 1) * LANES])

    def conv():
        xr_p = _cat([xperm_ref[s] for s in range(n_slab)], 1)
        sub = lax.broadcasted_iota(jnp.int32, (SUBLANES, D_RG), 0)
        prev_tail = tail_ref[...]
        head = []
        for k in range(CONV_W - 1, 0, -1):
            grp = pltpu.roll(xr_p[(sub_len - k) * SUBLANES:(sub_len - k + 1) * SUBLANES, :], 1, 0)
            head.append(jnp.where(sub == 0, prev_tail[SUBLANES - k:SUBLANES - k + 1, :], grp))
        ext = _cat(head + [xr_p])
        xc = v512[_V_CONV_B:_V_CONV_B + 1, :]
        for k in range(CONV_W):
            xc = xc + ext[k * SUBLANES:k * SUBLANES + tt, :] * v512[_V_CONV_W + k:_V_CONV_W + k + 1, :]
        st['xc'] = xc
        tail_ref[...] = st['xr_tail']
        conv_ref[slot] = st['xr_tail']

    def gate_mm():
        st['gate_pre'] = _gate_preacts(st['xc'], wg_ref)

    def coefficients():
        r, i_gate = _gates(st.pop('gate_pre'), v512)
        a, mult = _lru_coeffs(r, v512)
        row = lax.broadcasted_iota(jnp.int32, (tt, D_RG), 0)
        mult = jnp.where(row + j * tt == 0, 1.0, mult)
        st['a'] = a
        st['b'] = mult * i_gate * st.pop('xc')

    def recurrence():
        a, b = st['a'], st['b']
        sub = lax.broadcasted_iota(jnp.int32, (SUBLANES, D_RG), 0)
        hs, ps = [], []
        for q in range(sub_len):
            aq = a[q * SUBLANES:(q + 1) * SUBLANES, :]
            bq = b[q * SUBLANES:(q + 1) * SUBLANES, :]
            hs.append(bq if q == 0 else aq * hs[-1] + bq)
            ps.append(aq if q == 0 else aq * ps[-1])
        init = jnp.where(sub == 0, h_ref[SUBLANES - 1:SUBLANES, :], 0.0)
        for p in range(1, n_sub):
            end = hs[-1] + ps[-1] * init
            init = jnp.where(sub == p, pltpu.roll(end, 1, 0), init)
        end = hs[-1] + ps[-1] * init
        h_ref[...] = end
        lru_ref[slot] = end[SUBLANES - 1:SUBLANES, :]
        h_perm = _cat([hq + pq * init for hq, pq in zip(hs, ps)])
        for s in range(n_slab):
            hperm_ref[s] = h_perm[:, s * LANES:(s + 1) * LANES]

    def recurrent_out(c):
        r0 = c * CHUNK
        yg = st['proj'][r0:r0 + CHUNK, D_RG:2 * D_RG]
        subs = range(r0 // sub_len, (r0 + CHUNK) // sub_len)
        h = _cat([_cat([hperm_ref[s, pl.ds(p, sub_len, stride=n_sub), :] for p in subs]) for s in range(n_slab)], 1)
        rg_out = h * _gelu_tanh(yg)
        merged_ref[r0:r0 + CHUNK, 0:D_RG] = _rms(rg_out, v512[_V_G_RG:_V_G_RG + 1, :]).astype(BF16)

    def spatial_pre(c):
        r0 = c * CHUNK
        lane = lax.broadcasted_iota(jnp.int32, (CHUNK, 2 * SG_HEAD_DIM), 1)
        v = st['proj'][r0:r0 + CHUNK, 2 * D_RG + D_SG:]
        vn = _layer_norm(v, v512[_V_LN_G:_V_LN_G + 1, :], v512[_V_LN_B:_V_LN_B + 1, :])
        rhs = []
        for p in range(SG_HEADS // 2):
            vp = vn[:, p * 2 * SG_HEAD_DIM:(p + 1) * 2 * SG_HEAD_DIM]
            rhs.append(jnp.concatenate([jnp.where(lane < SG_HEAD_DIM, vp, 0.0),
                                        jnp.where(lane >= SG_HEAD_DIM, vp, 0.0)], axis=0).astype(BF16))
        st['rhs_%d' % c] = rhs
        if r0 + CHUNK == tt:
            vrows_ref[slot] = v

    def spatial_mm(c):
        t_idx = lax.broadcasted_iota(jnp.int32, (CHUNK, 2 * CHUNK), 0)
        s_idx = lax.broadcasted_iota(jnp.int32, (CHUNK, 2 * CHUNK), 1) & (CHUNK - 1)
        outs = []
        for p, rhs in enumerate(st.pop('rhs_%d' % c)):
            ws_pair = jnp.where(s_idx <= t_idx, wsp_ref[p], jnp.zeros((), BF16))
            outs.append(_dot(ws_pair, rhs))
        st['mixed_%d' % c] = _cat(outs, 1)

    def spatial_post(c):
        r0 = c * CHUNK
        u = st['proj'][r0:r0 + CHUNK, 2 * D_RG:2 * D_RG + D_SG]
        sg_out = u * (st.pop('mixed_%d' % c) + bias_s_ref[...])
        merged_ref[r0:r0 + CHUNK, D_RG:] = _rms(sg_out, v512[_V_G_SG:_V_G_SG + 1, :]).astype(BF16)

    thunks = {'norm': norm, 'w_in': project, 'conv': conv, 'gate_mm': gate_mm, 'coef': coefficients,
              'rec': recurrence}
    for c in range(tt // CHUNK):
        thunks['sp_pre_%d' % c] = functools.partial(spatial_pre, c)
        thunks['sp_mm_%d' % c] = functools.partial(spatial_mm, c)
        thunks['sp_post_%d' % c] = functools.partial(spatial_post, c)
        thunks['rg_out_%d' % c] = functools.partial(recurrent_out, c)
    return thunks


N_PROMPT_CHUNKS = PROMPT_TILE // CHUNK
MIX_ORDER = (['norm', 'w_in', 'conv', 'gate_mm', 'coef', 'rec'] +
             [f'{name}_{c}' for c in range(N_PROMPT_CHUNKS) for name in ('sp_pre', 'sp_mm', 'sp_post', 'rg_out')])

assert N_PROMPT_CHUNKS == 4 and N_FF_CHUNKS == 4
INTERLEAVED_ORDER = [
    ('a', 'norm'), ('b', 'w_out'), ('a', 'w_in'), ('b', 'hf'), ('b', 'ff1_0'), ('a', 'conv'),
    ('a', 'sp_pre_0'), ('a', 'sp_pre_1'), ('b', 'ff1_1'), ('a', 'gate_mm'), ('a', 'sp_pre_2'), ('a', 'sp_pre_3'),
    ('b', 'ff1_2'), ('a', 'coef'), ('a', 'sp_mm_0'), ('a', 'sp_mm_1'), ('b', 'ff1_3'), ('a', 'rec'),
    ('a', 'sp_mm_2'), ('a', 'sp_mm_3'), ('a', 'sp_post_0'), ('a', 'sp_post_1'), ('b', 'ff2_0'),
    ('a', 'rg_out_0'), ('a', 'rg_out_1'), ('a', 'sp_post_2'), ('a', 'sp_post_3'), ('b', 'ff2_1'),
    ('a', 'rg_out_2'), ('a', 'rg_out_3'), ('b', 'ff2_2'), ('b', 'ff2_3'), ('b', 'final'),
]


def _prompt_kernel(x_ref, xprev_ref, ada_ref, v1024_ref, v512_ref, w_in_ref, wg_ref, wsp_ref,
                   bias_s_ref, w_out_ref, w_ff1_ref, w_ff2_ref,
                   y_ref, conv_ref, lru_ref, vrows_ref,
                   xperm_ref, hperm_ref, tail_ref, h_ref, merged_ref, *, tiles_per_seq):
    s = pl.program_id(0)
    n_tiles = pl.num_programs(0) - 1
    j = lax.rem(s, tiles_per_seq)
    seq_cur = jnp.minimum(s, n_tiles - 1) // tiles_per_seq
    seq_prev = jnp.maximum(s - 1, 0) // tiles_per_seq

    @pl.when(j == 0)
    def _():
        tail_ref[...] = jnp.zeros((SUBLANES, D_RG), F32)
        h_ref[...] = jnp.zeros((SUBLANES, D_RG), F32)

    v1024 = v1024_ref[...]
    v512 = v512_ref[...]

    def store_y(_, val):
        y_ref[...] = val

    def stage_a():
        return _prompt_mix_thunks(x_ref, ada_ref[pl.ds(seq_cur, 1), :], v1024, v512, w_in_ref, wg_ref, wsp_ref,
                                  bias_s_ref, conv_ref, lru_ref, vrows_ref, xperm_ref, hperm_ref, tail_ref, h_ref,
                                  merged_ref, j, seq_cur)

    def stage_b():
        return _tail_thunks([xprev_ref[...]], lambda: merged_ref[...], _split_mods(ada_ref[pl.ds(seq_prev, 1), :]),
                            v1024, w_out_ref, w_ff1_ref, w_ff2_ref, store_y)

    @pl.when(s == 0)
    def _():
        _run(stage_a(), MIX_ORDER)

    @pl.when(jnp.logical_and(s > 0, s < n_tiles))
    def _():
        stages = {'a': stage_a(), 'b': stage_b()}
        assert sorted(n for k, n in INTERLEAVED_ORDER if k == 'a') == sorted(stages['a'])
        assert sorted(n for k, n in INTERLEAVED_ORDER if k == 'b') == sorted(stages['b'])
        for stage, name in INTERLEAVED_ORDER:
            stages[stage][name]()

    @pl.when(s == n_tiles)
    def _():
        _run(stage_b(), TAIL_ORDER)


def _sample_kernel(x_ref, ada_ref, cs_ref, h0_ref, v1024_ref, v512_ref, w_in_ref, wg_ref, w8_ref, b8_ref,
                   w_out_ref, w_ff1_ref, w_ff2_ref,
                   y_ref, conv_ref, lru_ref, vrows_ref):
    nb, n_t = x_ref.shape[0], x_ref.shape[1] // D_MODEL
    mods = _split_mods(ada_ref[...])
    sh_m, sc_m = mods[0], mods[1]
    v1024 = v1024_ref[...]
    v512 = v512_ref[...]

    def blk(arr, t):
        return arr[t * nb:(t + 1) * nb, :]

    x_blocks = [x_ref[:, t * D_MODEL:(t + 1) * D_MODEL] for t in range(n_t)]
    hm = _modulated_norm(x_blocks, v1024[0:1, :], sc_m, sh_m)
    proj = _dot(hm, w_in_ref[...])
    xr = proj[:, 0:D_RG]
    yg = proj[:, D_RG:2 * D_RG]
    u = proj[:, 2 * D_RG:2 * D_RG + D_SG]
    v = proj[:, 2 * D_RG + D_SG:]

    xp = [cs_ref[:, k * D_RG:(k + 1) * D_RG] for k in range(CONV_W - 1)] + [blk(xr, t) for t in range(n_t)]
    xc_blocks = []
    for t in range(n_t):
        acc = v512[_V_CONV_B:_V_CONV_B + 1, :]
        for k in range(CONV_W):
            acc = acc + xp[t + k] * v512[_V_CONV_W + k:_V_CONV_W + k + 1, :]
        xc_blocks.append(acc)
    for k in range(CONV_W - 1):
        conv_ref[:, k * D_RG:(k + 1) * D_RG] = xp[n_t + k]
    xc = _cat(xc_blocks)

    r, i_gate = _gates(_gate_preacts(xc, wg_ref), v512)
    a, mult = _lru_coeffs(r, v512)
    bterm = mult * i_gate * xc
    h = h0_ref[...]
    h_blocks = []
    for t in range(n_t):
        h = blk(a, t) * h + blk(bterm, t)
        h_blocks.append(h)
    lru_ref[...] = h
    rg_out = _cat(h_blocks) * _gelu_tanh(yg)

    vn = _layer_norm(v, v512[_V_LN_G:_V_LN_G + 1, :], v512[_V_LN_B:_V_LN_B + 1, :])
    sg_blocks = []
    for t in range(n_t):
        vrows_ref[:, t * D_SG:(t + 1) * D_SG] = blk(v, t)
        mixed = b8_ref[t:t + 1, :]
        for s in range(t + 1):
            mixed = mixed + w8_ref[t, s:s + 1, :] * blk(vn, s)
        sg_blocks.append(blk(u, t) * mixed)
    sg_out = _cat(sg_blocks)

    merged = jnp.concatenate(
        [_rms(rg_out, v512[_V_G_RG:_V_G_RG + 1, :]).astype(BF16),
         _rms(sg_out, v512[_V_G_SG:_V_G_SG + 1, :]).astype(BF16)], axis=1)

    def store_y(t, val):
        y_ref[:, t * D_MODEL:(t + 1) * D_MODEL] = val

    _run(_tail_thunks(x_blocks, lambda: merged, mods, v1024, w_out_ref, w_ff1_ref, w_ff2_ref, store_y), TAIL_ORDER)


def _prep_kernel(cs_ref, cp_ref, w_ada_ref, b_ada_ref, w_in_ref, w_out_ref, w_ff1_ref, w_ff2_ref,
                 ada_ref, w_in_o, w_out_o, w_ff1_o, w_ff2_o):
    c = jnp.concatenate([cs_ref[...], cp_ref[...]], axis=0)
    s = (c * jax.nn.sigmoid(c)).astype(BF16)
    ada_ref[...] = _dot(s, w_ada_ref[...].astype(BF16)) + b_ada_ref[...]
    w_in_o[...] = w_in_ref[...].astype(BF16)
    w_out_o[...] = w_out_ref[...].astype(BF16)
    w_ff1_o[...] = w_ff1_ref[...].astype(BF16)
    w_ff2_o[...] = w_ff2_ref[...].astype(BF16)


def _resident(shape):
    nd = len(shape)
    return pl.BlockSpec(shape, lambda *_: (0,) * nd, pipeline_mode=pl.Buffered(1))


def _block_diag_groups(w):
    per = RG_HEADS // GATE_GROUPS
    w4 = w.reshape(GATE_GROUPS, per, RG_HEAD_DIM, RG_HEAD_DIM)
    bd = jnp.einsum('ghij,hk->ghikj', w4, jnp.eye(per, dtype=w.dtype))
    return bd.reshape(GATE_GROUPS, GATE_W, GATE_W)


def kernel(x_prompt, x_sample, c_prompt, c_sample, state_conv, state_rglru, w_ada, b_ada, g_mix, g_ffn, w_in, conv_w, conv_b, w_ra, b_ra, w_ri, b_ri, lru_l, ln_v_g, ln_v_b, w_s, b_s, g_rg, g_sg, w_out, w_ff1, w_ff2, g_final):
    batch, seq, _ = x_prompt.shape
    dec_batch, dec_seq, _ = x_sample.shape
    assert w_ada.shape[0] == 1, "single-layer step"
    assert seq % PROMPT_TILE == 0 and PROMPT_TILE % CHUNK == 0 and dec_batch % SAMPLE_NB == 0
    assert dec_seq <= CHUNK

    n_c = dec_batch + batch
    assert dec_batch % SUBLANES == 0 and dec_batch % batch == 0
    steps = PREP_STEPS

    def row_block(w):
        return pl.BlockSpec((w.shape[0] // steps, w.shape[1]), lambda n: (n, 0))

    mats = (w_in[0], w_out[0], w_ff1[0], w_ff2[0])
    ada, w_in_b, w_out_b, w_ff1_b, w_ff2_b = pl.pallas_call(
        _prep_kernel,
        out_shape=(jax.ShapeDtypeStruct((n_c, 6 * D_MODEL), F32),) +
                  tuple(jax.ShapeDtypeStruct(w.shape, BF16) for w in mats),
        grid=(steps,),
        in_specs=[pl.BlockSpec((dec_batch, D_MODEL), lambda n: (0, 0)),
                  pl.BlockSpec((batch, D_MODEL), lambda n: (0, 0)),
                  pl.BlockSpec((D_MODEL, 6 * D_MODEL // steps), lambda n: (0, n)),
                  pl.BlockSpec((1, 6 * D_MODEL // steps), lambda n: (0, n))] + [row_block(w) for w in mats],
        out_specs=(pl.BlockSpec((n_c, 6 * D_MODEL // steps), lambda n: (0, n)),) +
                  tuple(row_block(w) for w in mats),
        compiler_params=pltpu.CompilerParams(dimension_semantics=("arbitrary",),
                                             vmem_limit_bytes=VMEM_LIMIT_BYTES),
        name="prep",
    )(c_sample, c_prompt, w_ada[0], b_ada, *mats)

    v1024 = jnp.stack([g_mix[0], g_ffn[0], g_final], axis=0)
    v512 = jnp.concatenate(
        [conv_w[0], conv_b, b_ra[0].reshape(1, D_RG), b_ri[0].reshape(1, D_RG), lru_l,
         ln_v_g, ln_v_b, g_rg, g_sg], axis=0)
    wg = jnp.concatenate([_block_diag_groups(w_ra[0]), _block_diag_groups(w_ri[0])], axis=-1).astype(BF16)
    wsp = (w_s[0].reshape(SG_HEADS // 2, 2, CHUNK, CHUNK).transpose(0, 2, 1, 3)
           .reshape(SG_HEADS // 2, CHUNK, 2 * CHUNK).astype(BF16))
    bias_s = jnp.repeat(b_s[0].T, SG_HEAD_DIM, axis=1)
    w8 = jnp.repeat(w_s[0, :, :dec_seq, :dec_seq].transpose(1, 2, 0), SG_HEAD_DIM, axis=2)
    b8 = bias_s[:dec_seq]

    weights = (w_in_b, wg)
    weight_specs = [_resident(w_in_b.shape), _resident(wg.shape)]
    tail_weights = (w_out_b, w_ff1_b, w_ff2_b)
    tail_specs = [_resident(w.shape) for w in tail_weights]
    cparams = functools.partial(pltpu.CompilerParams, vmem_limit_bytes=VMEM_LIMIT_BYTES)

    tt = PROMPT_TILE
    nt = seq // tt
    n_tiles = batch * nt

    def cur(s):
        return jnp.minimum(s, n_tiles - 1)

    def prev(s):
        return jnp.maximum(s - 1, 0)

    y_p, conv8_p, lru_p, vrows_p = pl.pallas_call(
        functools.partial(_prompt_kernel, tiles_per_seq=nt),
        out_shape=(jax.ShapeDtypeStruct((batch, seq, D_MODEL), F32),
                   jax.ShapeDtypeStruct((batch, SUBLANES, D_RG), F32),
                   jax.ShapeDtypeStruct((batch, 1, D_RG), F32),
                   jax.ShapeDtypeStruct((batch, CHUNK, D_SG), F32)),
        grid=(n_tiles + 1,),
        in_specs=[pl.BlockSpec((None, tt, D_MODEL), lambda s: (cur(s) // nt, cur(s) % nt, 0)),
                  pl.BlockSpec((None, tt, D_MODEL), lambda s: (prev(s) // nt, prev(s) % nt, 0)),
                  pl.BlockSpec((batch, 6 * D_MODEL), lambda s: (dec_batch // batch, 0),
                               pipeline_mode=pl.Buffered(1)),
                  _resident(v1024.shape), _resident(v512.shape)] + weight_specs +
                 [_resident(wsp.shape), _resident(bias_s.shape)] + tail_specs,
        out_specs=(pl.BlockSpec((None, tt, D_MODEL), lambda s: (prev(s) // nt, prev(s) % nt, 0)),
                   pl.BlockSpec((batch, SUBLANES, D_RG), lambda s: (0, 0, 0)),
                   pl.BlockSpec((batch, 1, D_RG), lambda s: (0, 0, 0)),
                   pl.BlockSpec((batch, CHUNK, D_SG), lambda s: (0, 0, 0))),
        scratch_shapes=[pltpu.VMEM((D_RG // LANES, tt, LANES), F32), pltpu.VMEM((D_RG // LANES, tt, LANES), F32),
                        pltpu.VMEM((SUBLANES, D_RG), F32), pltpu.VMEM((SUBLANES, D_RG), F32),
                        pltpu.VMEM((tt, D_MODEL), BF16)],
        compiler_params=cparams(dimension_semantics=("arbitrary",)),
        name="prompt_trunk",
    )(x_prompt, x_prompt, ada, v1024, v512, *weights, wsp, bias_s, *tail_weights)
    conv_p = conv8_p[:, SUBLANES - (CONV_W - 1):, :][None]
    lru_p = lru_p.reshape(1, batch, D_RG)
    vrows_p = vrows_p[None]

    nb = SAMPLE_NB
    n_cs = CONV_W - 1

    def seq_rows(width):
        return pl.BlockSpec((nb, width), lambda i: (i, 0))

    y_s, conv_s, lru_s, vrows_s = pl.pallas_call(
        _sample_kernel,
        out_shape=(jax.ShapeDtypeStruct((dec_batch, dec_seq * D_MODEL), F32),
                   jax.ShapeDtypeStruct((dec_batch, n_cs * D_RG), F32),
                   jax.ShapeDtypeStruct((dec_batch, D_RG), F32),
                   jax.ShapeDtypeStruct((dec_batch, dec_seq * D_SG), F32)),
        grid=(dec_batch // nb,),
        in_specs=[seq_rows(dec_seq * D_MODEL), seq_rows(6 * D_MODEL), seq_rows(n_cs * D_RG), seq_rows(D_RG),
                  _resident(v1024.shape), _resident(v512.shape)] + weight_specs +
                 [_resident(w8.shape), _resident(b8.shape)] + tail_specs,
        out_specs=(seq_rows(dec_seq * D_MODEL), seq_rows(n_cs * D_RG), seq_rows(D_RG), seq_rows(dec_seq * D_SG)),
        compiler_params=cparams(dimension_semantics=("arbitrary",)),
        name="sample_trunk",
    )(x_sample.reshape(dec_batch, dec_seq * D_MODEL), ada, state_conv[0].reshape(dec_batch, n_cs * D_RG),
      state_rglru[0], v1024, v512, *weights, w8, b8, *tail_weights)

    return (y_p, y_s.reshape(dec_batch, dec_seq, D_MODEL), conv_p, lru_p, vrows_p,
            conv_s.reshape(1, dec_batch, n_cs, D_RG), lru_s[None], vrows_s.reshape(1, dec_batch, dec_seq, D_SG))
```

```python
import functools

import jax
import jax.numpy as jnp
from jax import lax
from jax.experimental import pallas as pl
from jax.experimental.pallas import tpu as pltpu

D_MODEL = 1024
D_RG = 512
D_SG = 512
D_IN = 2 * D_RG + 2 * D_SG
D_FF = 4 * D_MODEL
RG_HEADS = 8
RG_HEAD_DIM = D_RG // RG_HEADS
SG_HEADS = 8
SG_HEAD_DIM = D_SG // SG_HEADS
CHUNK = 128
CONV_W = 4
LRU_C = 8.0
EPS = 1e-6

SUBLANES = 8
LANES = 128
GATE_GROUPS = 2
GATE_W = D_RG // GATE_GROUPS
PROMPT_TILE = 512
SAMPLE_NB = 64
FF_CHUNK = 1024
N_FF_CHUNKS = D_FF // FF_CHUNK
PREP_STEPS = 8
VMEM_LIMIT_BYTES = 60000 * 1024

F32 = jnp.float32
BF16 = jnp.bfloat16

_V_CONV_W, _V_CONV_B, _V_B_RA, _V_B_RI, _V_LRU_L, _V_LN_G, _V_LN_B, _V_G_RG, _V_G_SG = 0, 4, 5, 6, 7, 8, 9, 10, 11


def _dot(a, b):
    return jnp.dot(a, b, preferred_element_type=F32)


def _rms(x, g):
    ms = jnp.mean(x * x, axis=-1, keepdims=True)
    return (x * lax.rsqrt(ms + EPS)) * g


def _sigmoid(x):
    return 0.5 * jnp.tanh(0.5 * x) + 0.5


def _gelu_tanh(x):
    k = 0.7978845608028654
    u = x * (k + (k * 0.044715) * (x * x))
    return (0.5 * x) * (1.0 + jnp.tanh(u))


def _layer_norm(x, g, b):
    mu = jnp.mean(x, axis=-1, keepdims=True)
    xc = x - mu
    var = jnp.mean(xc * xc, axis=-1, keepdims=True)
    return (xc * lax.rsqrt(var + EPS)) * g + b


def _cat(parts, axis=0):
    return parts[0] if len(parts) == 1 else jnp.concatenate(parts, axis=axis)


def _split_mods(ada):
    return [ada[:, k * D_MODEL:(k + 1) * D_MODEL] for k in range(6)]


def _modulated_norm(x_blocks, g, sc, sh):
    gain = g * (1.0 + sc)
    return _cat([(_rms(xb, gain) + sh).astype(BF16) for xb in x_blocks])


def _gate_preacts(xc, wg_ref):
    xcb = xc.astype(BF16)
    return _cat([_dot(xcb[:, g * GATE_W:(g + 1) * GATE_W], wg_ref[g]) for g in range(GATE_GROUPS)], 1)


def _gates(pre, v512):
    r_parts, i_parts = [], []
    for g in range(GATE_GROUPS):
        lo, hi = g * GATE_W, (g + 1) * GATE_W
        ra = pre[:, 2 * lo:2 * lo + GATE_W]
        ri = pre[:, 2 * lo + GATE_W:2 * hi]
        r_parts.append(_sigmoid(ra + v512[_V_B_RA:_V_B_RA + 1, lo:hi]))
        i_parts.append(_sigmoid(ri + v512[_V_B_RI:_V_B_RI + 1, lo:hi]))
    return _cat(r_parts, 1), _cat(i_parts, 1)


def _lru_coeffs(r, v512):
    lam = v512[_V_LRU_L:_V_LRU_L + 1, :]
    log_sig = jnp.minimum(lam, 0.0) - jnp.log1p(jnp.exp(-jnp.abs(lam)))
    log_a = (LRU_C * r) * log_sig
    a = jnp.exp(log_a)
    z = 1.0 - a * a
    mult = jnp.where(z > 0.0, z * lax.rsqrt(z), 0.0)
    return a, mult


def _tail_thunks(x_blocks, merged_fn, mods, v1024, w_out_ref, w_ff1_ref, w_ff2_ref, store_y):
    _, _, gt_m, sh_f, sc_f, gt_f = mods
    rows = x_blocks[0].shape[0]
    st = {}

    def w_out():
        mm = _dot(merged_fn(), w_out_ref[...])
        st['x1'] = [xb + gt_m * mm[i * rows:(i + 1) * rows] for i, xb in enumerate(x_blocks)]

    def hf():
        st['hf'] = _modulated_norm(st['x1'], v1024[1:2, :], sc_f, sh_f)

    def ff1(c):
        pre = _dot(st['hf'], w_ff1_ref[:, c * FF_CHUNK:(c + 1) * FF_CHUNK])
        h1 = jnp.maximum(pre.astype(BF16), 0.0)
        st['h1_%d' % c] = h1 * h1

    def ff2(c):
        part = _dot(st.pop('h1_%d' % c), w_ff2_ref[c * FF_CHUNK:(c + 1) * FF_CHUNK, :])
        st['acc'] = part if c == 0 else st['acc'] + part

    def final():
        for i, xb in enumerate(st['x1']):
            x2 = xb + gt_f * st['acc'][i * rows:(i + 1) * rows]
            store_y(i, _rms(x2, v1024[2:3, :]))

    thunks = {'w_out': w_out, 'hf': hf, 'final': final}
    for c in range(N_FF_CHUNKS):
        thunks['ff1_%d' % c] = functools.partial(ff1, c)
        thunks['ff2_%d' % c] = functools.partial(ff2, c)
    return thunks


TAIL_ORDER = (['w_out', 'hf', 'ff1_0'] +
              [name for c in range(N_FF_CHUNKS)
               for name in (['ff1_%d' % (c + 1)] if c + 1 < N_FF_CHUNKS else []) + ['ff2_%d' % c]] +
              ['final'])


def _run(thunks, order):
    assert sorted(order) == sorted(thunks), (sorted(order), sorted(thunks))
    for name in order:
        thunks[name]()


def _prompt_mix_thunks(x_ref, ada, v1024, v512, w_in_ref, wg_ref, wsp_ref, bias_s_ref,
                       conv_ref, lru_ref, vrows_ref, xperm_ref, hperm_ref, tail_ref, h_ref, merged_ref, j, slot):
    tt = x_ref.shape[0]
    n_sub = SUBLANES
    sub_len = tt // n_sub
    n_slab = D_RG // LANES
    st = {}

    def norm():
        mods = _split_mods(ada)
        st['hm'] = _modulated_norm([x_ref[...]], v1024[0:1, :], mods[1], mods[0])

    def project():
        st['proj'] = proj = _dot(st['hm'], w_in_ref[...])
        xr = proj[:, 0:D_RG]
        st['xr_tail'] = xr[tt - SUBLANES:, :]
        for p in range(n_sub):
            for s in range(n_slab):
                xperm_ref[s, pl.ds(p, sub_len, stride=n_sub), :] = (
                    xr[p * sub_len:(p + 1) * sub_len, s * LANES:(s + 1) * LANES])

    def conv():
        xr_p = _cat([xperm_ref[s] for s in range(n_slab)], 1)
        sub = lax.broadcasted_iota(jnp.int32, (SUBLANES, D_RG), 0)
        prev_tail = tail_ref[...]
        head = []
        for k in range(CONV_W - 1, 0, -1):
            grp = pltpu.roll(xr_p[(sub_len - k) * SUBLANES:(sub_len - k + 1) * SUBLANES, :], 1, 0)
            head.append(jnp.where(sub == 0, prev_tail[SUBLANES - k:SUBLANES - k + 1, :], grp))
        ext = _cat(head + [xr_p])
        xc = v512[_V_CONV_B:_V_CONV_B + 1, :]
        for k in range(CONV_W):
            xc = xc + ext[k * SUBLANES:k * SUBLANES + tt, :] * v512[_V_CONV_W + k:_V_CONV_W + k + 1, :]
        st['xc'] = xc
        tail_ref[...] = st['xr_tail']
        conv_ref[slot] = st['xr_tail']

    def gate_mm():
        st['gate_pre'] = _gate_preacts(st['xc'], wg_ref)

    def coefficients():
        r, i_gate = _gates(st.pop('gate_pre'), v512)
        a, mult = _lru_coeffs(r, v512)
        row = lax.broadcasted_iota(jnp.int32, (tt, D_RG), 0)
        mult = jnp.where(row + j * tt == 0, 1.0, mult)
        st['a'] = a
        st['b'] = mult * i_gate * st.pop('xc')

    def recurrence():
        a, b = st['a'], st['b']
        sub = lax.broadcasted_iota(jnp.int32, (SUBLANES, D_RG), 0)
        hs, ps = [], []
        for q in range(sub_len):
            aq = a[q * SUBLANES:(q + 1) * SUBLANES, :]
            bq = b[q * SUBLANES:(q + 1) * SUBLANES, :]
            hs.append(bq if q == 0 else aq * hs[-1] + bq)
            ps.append(aq if q == 0 else aq * ps[-1])
        init = jnp.where(sub == 0, h_ref[SUBLANES - 1:SUBLANES, :], 0.0)
        for p in range(1, n_sub):
            end = hs[-1] + ps[-1] * init
            init = jnp.where(sub == p, pltpu.roll(end, 1, 0), init)
        end = hs[-1] + ps[-1] * init
        h_ref[...] = end
        lru_ref[slot] = end[SUBLANES - 1:SUBLANES, :]
        h_perm = _cat([hq + pq * init for hq, pq in zip(hs, ps)])
        for s in range(n_slab):
            hperm_ref[s] = h_perm[:, s * LANES:(s + 1) * LANES]

    def recurrent_out(c):
        r0 = c * CHUNK
        yg = st['proj'][r0:r0 + CHUNK, D_RG:2 * D_RG]
        subs = range(r0 // sub_len, (r0 + CHUNK) // sub_len)
        h = _cat([_cat([hperm_ref[s, pl.ds(p, sub_len, stride=n_sub), :] for p in subs]) for s in range(n_slab)], 1)
        rg_out = h * _gelu_tanh(yg)
        merged_ref[r0:r0 + CHUNK, 0:D_RG] = _rms(rg_out, v512[_V_G_RG:_V_G_RG + 1, :]).astype(BF16)

    def spatial_pre(c):
        r0 = c * CHUNK
        lane = lax.broadcasted_iota(jnp.int32, (CHUNK, 2 * SG_HEAD_DIM), 1)
        v = st['proj'][r0:r0 + CHUNK, 2 * D_RG + D_SG:]
        vn = _layer_norm(v, v512[_V_LN_G:_V_LN_G + 1, :], v512[_V_LN_B:_V_LN_B + 1, :])
        rhs = []
        for p in range(SG_HEADS // 2):
            vp = vn[:, p * 2 * SG_HEAD_DIM:(p + 1) * 2 * SG_HEAD_DIM]
            rhs.append(jnp.concatenate([jnp.where(lane < SG_HEAD_DIM, vp, 0.0),
                                        jnp.where(lane >= SG_HEAD_DIM, vp, 0.0)], axis=0).astype(BF16))
        st['rhs_%d' % c] = rhs
        if r0 + CHUNK == tt:
            vrows_ref[slot] = v

    def spatial_mm(c):
        t_idx = lax.broadcasted_iota(jnp.int32, (CHUNK, 2 * CHUNK), 0)
        s_idx = lax.broadcasted_iota(jnp.int32, (CHUNK, 2 * CHUNK), 1) & (CHUNK - 1)
        outs = []
        for p, rhs in enumerate(st.pop('rhs_%d' % c)):
            ws_pair = jnp.where(s_idx <= t_idx, wsp_ref[p], jnp.zeros((), BF16))
            outs.append(_dot(ws_pair, rhs))
        st['mixed_%d' % c] = _cat(outs, 1)

    def spatial_post(c):
        r0 = c * CHUNK
        u = st['proj'][r0:r0 + CHUNK, 2 * D_RG:2 * D_RG + D_SG]
        sg_out = u * (st.pop('mixed_%d' % c) + bias_s_ref[...])
        merged_ref[r0:r0 + CHUNK, D_RG:] = _rms(sg_out, v512[_V_G_SG:_V_G_SG + 1, :]).astype(BF16)

    thunks = {'norm': norm, 'w_in': project, 'conv': conv, 'gate_mm': gate_mm, 'coef': coefficients,
              'rec': recurrence}
    for c in range(tt // CHUNK):
        thunks['sp_pre_%d' % c] = functools.partial(spatial_pre, c)
        thunks['sp_mm_%d' % c] = functools.partial(spatial_mm, c)
        thunks['sp_post_%d' % c] = functools.partial(spatial_post, c)
        thunks['rg_out_%d' % c] = functools.partial(recurrent_out, c)
    return thunks


N_PROMPT_CHUNKS = PROMPT_TILE // CHUNK
MIX_ORDER = (['norm', 'w_in', 'conv', 'gate_mm', 'coef', 'rec'] +
             [f'{name}_{c}' for c in range(N_PROMPT_CHUNKS) for name in ('sp_pre', 'sp_mm', 'sp_post', 'rg_out')])

assert N_PROMPT_CHUNKS == 4 and N_FF_CHUNKS == 4
INTERLEAVED_ORDER = [
    ('a', 'norm'), ('b', 'w_out'), ('a', 'w_in'), ('b', 'hf'), ('b', 'ff1_0'), ('a', 'conv'),
    ('a', 'sp_pre_0'), ('a', 'sp_pre_1'), ('b', 'ff1_1'), ('a', 'gate_mm'), ('a', 'sp_pre_2'), ('a', 'sp_pre_3'),
    ('b', 'ff1_2'), ('a', 'coef'), ('a', 'sp_mm_0'), ('a', 'sp_mm_1'), ('b', 'ff1_3'), ('a', 'rec'),
    ('a', 'sp_mm_2'), ('a', 'sp_mm_3'), ('a', 'sp_post_0'), ('a', 'sp_post_1'), ('b', 'ff2_0'),
    ('a', 'rg_out_0'), ('a', 'rg_out_1'), ('a', 'sp_post_2'), ('a', 'sp_post_3'), ('b', 'ff2_1'),
    ('a', 'rg_out_2'), ('a', 'rg_out_3'), ('b', 'ff2_2'), ('b', 'ff2_3'), ('b', 'final'),
]


def _tail_weight_copies(hbm_refs, vmem_refs, sem):
    return [pltpu.make_async_copy(src, dst, sem.at[k]) for k, (src, dst) in enumerate(zip(hbm_refs, vmem_refs))]


def _prompt_kernel(x_ref, xprev_ref, ada_ref, v1024_ref, v512_ref, w_in_ref, wg_ref, wsp_ref,
                   bias_s_ref, w_out_hbm, w_ff1_hbm, w_ff2_hbm,
                   y_ref, conv_ref, lru_ref, vrows_ref,
                   xperm_ref, hperm_ref, tail_ref, h_ref, merged_ref, w_out_ref, w_ff1_ref, w_ff2_ref, w_sem,
                   *, tiles_per_seq):
    s = pl.program_id(0)
    n_tiles = pl.num_programs(0) - 1
    j = lax.rem(s, tiles_per_seq)
    seq_cur = jnp.minimum(s, n_tiles - 1) // tiles_per_seq
    seq_prev = jnp.maximum(s - 1, 0) // tiles_per_seq

    @pl.when(j == 0)
    def _():
        tail_ref[...] = jnp.zeros((SUBLANES, D_RG), F32)
        h_ref[...] = jnp.zeros((SUBLANES, D_RG), F32)

    v1024 = v1024_ref[...]
    v512 = v512_ref[...]

    def store_y(_, val):
        y_ref[...] = val

    def stage_a():
        return _prompt_mix_thunks(x_ref, ada_ref[pl.ds(seq_cur, 1), :], v1024, v512, w_in_ref, wg_ref, wsp_ref,
                                  bias_s_ref, conv_ref, lru_ref, vrows_ref, xperm_ref, hperm_ref, tail_ref, h_ref,
                                  merged_ref, j, seq_cur)

    def stage_b():
        return _tail_thunks([xprev_ref[...]], lambda: merged_ref[...], _split_mods(ada_ref[pl.ds(seq_prev, 1), :]),
                            v1024, w_out_ref, w_ff1_ref, w_ff2_ref, store_y)

    copies = _tail_weight_copies((w_out_hbm, w_ff1_hbm, w_ff2_hbm), (w_out_ref, w_ff1_ref, w_ff2_ref), w_sem)

    @pl.when(s == 0)
    def _():
        for cp in copies:
            cp.start()
        _run(stage_a(), MIX_ORDER)

    @pl.when(s == 1)
    def _():
        for cp in copies:
            cp.wait()

    @pl.when(jnp.logical_and(s > 0, s < n_tiles))
    def _():
        stages = {'a': stage_a(), 'b': stage_b()}
        assert sorted(n for k, n in INTERLEAVED_ORDER if k == 'a') == sorted(stages['a'])
        assert sorted(n for k, n in INTERLEAVED_ORDER if k == 'b') == sorted(stages['b'])
        for stage, name in INTERLEAVED_ORDER:
            stages[stage][name]()

    @pl.when(s == n_tiles)
    def _():
        _run(stage_b(), TAIL_ORDER)


def _sample_kernel(x_ref, ada_ref, cs_ref, h0_ref, v1024_ref, v512_ref, w_in_ref, wg_ref, w8_ref, b8_ref,
                   w_out_hbm, w_ff1_hbm, w_ff2_hbm,
                   y_ref, conv_ref, lru_ref, vrows_ref,
                   w_out_ref, w_ff1_ref, w_ff2_ref, w_sem):
    nb, n_t = x_ref.shape[0], x_ref.shape[1] // D_MODEL
    copies = _tail_weight_copies((w_out_hbm, w_ff1_hbm, w_ff2_hbm), (w_out_ref, w_ff1_ref, w_ff2_ref), w_sem)

    @pl.when(pl.program_id(0) == 0)
    def _():
        for cp in copies:
            cp.start()

    mods = _split_mods(ada_ref[...])
    sh_m, sc_m = mods[0], mods[1]
    v1024 = v1024_ref[...]
    v512 = v512_ref[...]

    def blk(arr, t):
        return arr[t * nb:(t + 1) * nb, :]

    x_blocks = [x_ref[:, t * D_MODEL:(t + 1) * D_MODEL] for t in range(n_t)]
    hm = _modulated_norm(x_blocks, v1024[0:1, :], sc_m, sh_m)
    proj = _dot(hm, w_in_ref[...])
    xr = proj[:, 0:D_RG]
    yg = proj[:, D_RG:2 * D_RG]
    u = proj[:, 2 * D_RG:2 * D_RG + D_SG]
    v = proj[:, 2 * D_RG + D_SG:]

    xp = [cs_ref[:, k * D_RG:(k + 1) * D_RG] for k in range(CONV_W - 1)] + [blk(xr, t) for t in range(n_t)]
    xc_blocks = []
    for t in range(n_t):
        acc = v512[_V_CONV_B:_V_CONV_B + 1, :]
        for k in range(CONV_W):
            acc = acc + xp[t + k] * v512[_V_CONV_W + k:_V_CONV_W + k + 1, :]
        xc_blocks.append(acc)
    for k in range(CONV_W - 1):
        conv_ref[:, k * D_RG:(k + 1) * D_RG] = xp[n_t + k]
    xc = _cat(xc_blocks)

    r, i_gate = _gates(_gate_preacts(xc, wg_ref), v512)
    a, mult = _lru_coeffs(r, v512)
    bterm = mult * i_gate * xc
    h = h0_ref[...]
    h_blocks = []
    for t in range(n_t):
        h = blk(a, t) * h + blk(bterm, t)
        h_blocks.append(h)
    lru_ref[...] = h
    rg_out = _cat(h_blocks) * _gelu_tanh(yg)

    vn = _layer_norm(v, v512[_V_LN_G:_V_LN_G + 1, :], v512[_V_LN_B:_V_LN_B + 1, :])
    sg_blocks = []
    for t in range(n_t):
        vrows_ref[:, t * D_SG:(t + 1) * D_SG] = blk(v, t)
        mixed = b8_ref[t:t + 1, :]
        for s in range(t + 1):
            mixed = mixed + w8_ref[t, s:s + 1, :] * blk(vn, s)
        sg_blocks.append(blk(u, t) * mixed)
    sg_out = _cat(sg_blocks)

    merged = jnp.concatenate(
        [_rms(rg_out, v512[_V_G_RG:_V_G_RG + 1, :]).astype(BF16),
         _rms(sg_out, v512[_V_G_SG:_V_G_SG + 1, :]).astype(BF16)], axis=1)

    def store_y(t, val):
        y_ref[:, t * D_MODEL:(t + 1) * D_MODEL] = val

    @pl.when(pl.program_id(0) == 0)
    def _():
        for cp in copies:
            cp.wait()

    _run(_tail_thunks(x_blocks, lambda: merged, mods, v1024, w_out_ref, w_ff1_ref, w_ff2_ref, store_y), TAIL_ORDER)


def _prep_kernel(cs_ref, cp_ref, w_ada_ref, b_ada_ref, w_in_ref, w_out_ref, w_ff1_ref, w_ff2_ref,
                 ada_ref, w_in_o, w_out_o, w_ff1_o, w_ff2_o):
    c = jnp.concatenate([cs_ref[...], cp_ref[...]], axis=0)
    s = (c * jax.nn.sigmoid(c)).astype(BF16)
    ada_ref[...] = _dot(s, w_ada_ref[...].astype(BF16)) + b_ada_ref[...]
    w_in_o[...] = w_in_ref[...].astype(BF16)
    w_out_o[...] = w_out_ref[...].astype(BF16)
    w_ff1_o[...] = w_ff1_ref[...].astype(BF16)
    w_ff2_o[...] = w_ff2_ref[...].astype(BF16)


def _resident(shape):
    nd = len(shape)
    return pl.BlockSpec(shape, lambda *_: (0,) * nd, pipeline_mode=pl.Buffered(1))


def _block_diag_groups(w):
    per = RG_HEADS // GATE_GROUPS
    w4 = w.reshape(GATE_GROUPS, per, RG_HEAD_DIM, RG_HEAD_DIM)
    bd = jnp.einsum('ghij,hk->ghikj', w4, jnp.eye(per, dtype=w.dtype))
    return bd.reshape(GATE_GROUPS, GATE_W, GATE_W)


def kernel(x_prompt, x_sample, c_prompt, c_sample, state_conv, state_rglru, w_ada, b_ada, g_mix, g_ffn, w_in, conv_w, conv_b, w_ra, b_ra, w_ri, b_ri, lru_l, ln_v_g, ln_v_b, w_s, b_s, g_rg, g_sg, w_out, w_ff1, w_ff2, g_final):
    batch, seq, _ = x_prompt.shape
    dec_batch, dec_seq, _ = x_sample.shape
    assert w_ada.shape[0] == 1, "single-layer step"
    assert seq % PROMPT_TILE == 0 and PROMPT_TILE % CHUNK == 0 and dec_batch % SAMPLE_NB == 0
    assert batch * (seq // PROMPT_TILE) >= 2, "the prompt pipeline needs at least two tiles"
    assert dec_seq <= CHUNK

    n_c = dec_batch + batch
    assert dec_batch % SUBLANES == 0 and dec_batch % batch == 0
    steps = PREP_STEPS

    def row_block(w):
        return pl.BlockSpec((w.shape[0] // steps, w.shape[1]), lambda n: (n, 0))

    mats = (w_in[0], w_out[0], w_ff1[0], w_ff2[0])
    ada, w_in_b, w_out_b, w_ff1_b, w_ff2_b = pl.pallas_call(
        _prep_kernel,
        out_shape=(jax.ShapeDtypeStruct((n_c, 6 * D_MODEL), F32),) +
                  tuple(jax.ShapeDtypeStruct(w.shape, BF16) for w in mats),
        grid=(steps,),
        in_specs=[pl.BlockSpec((dec_batch, D_MODEL), lambda n: (0, 0)),
                  pl.BlockSpec((batch, D_MODEL), lambda n: (0, 0)),
                  pl.BlockSpec((D_MODEL, 6 * D_MODEL // steps), lambda n: (0, n)),
                  pl.BlockSpec((1, 6 * D_MODEL // steps), lambda n: (0, n))] + [row_block(w) for w in mats],
        out_specs=(pl.BlockSpec((n_c, 6 * D_MODEL // steps), lambda n: (0, n)),) +
                  tuple(row_block(w) for w in mats),
        compiler_params=pltpu.CompilerParams(dimension_semantics=("arbitrary",),
                                             vmem_limit_bytes=VMEM_LIMIT_BYTES),
        name="prep",
    )(c_sample, c_prompt, w_ada[0], b_ada, *mats)

    v1024 = jnp.stack([g_mix[0], g_ffn[0], g_final], axis=0)
    v512 = jnp.concatenate(
        [conv_w[0], conv_b, b_ra[0].reshape(1, D_RG), b_ri[0].reshape(1, D_RG), lru_l,
         ln_v_g, ln_v_b, g_rg, g_sg], axis=0)
    wg = jnp.concatenate([_block_diag_groups(w_ra[0]), _block_diag_groups(w_ri[0])], axis=-1).astype(BF16)
    wsp = (w_s[0].reshape(SG_HEADS // 2, 2, CHUNK, CHUNK).transpose(0, 2, 1, 3)
           .reshape(SG_HEADS // 2, CHUNK, 2 * CHUNK).astype(BF16))
    bias_s = jnp.repeat(b_s[0].T, SG_HEAD_DIM, axis=1)
    w8 = jnp.repeat(w_s[0, :, :dec_seq, :dec_seq].transpose(1, 2, 0), SG_HEAD_DIM, axis=2)
    b8 = bias_s[:dec_seq]

    weights = (w_in_b, wg)
    weight_specs = [_resident(w_in_b.shape), _resident(wg.shape)]
    tail_weights = (w_out_b, w_ff1_b, w_ff2_b)
    cparams = functools.partial(pltpu.CompilerParams, vmem_limit_bytes=VMEM_LIMIT_BYTES)

    tt = PROMPT_TILE
    nt = seq // tt
    n_tiles = batch * nt

    def cur(s):
        return jnp.minimum(s, n_tiles - 1)

    def prev(s):
        return jnp.maximum(s - 1, 0)

    y_p, conv8_p, lru_p, vrows_p = pl.pallas_call(
        functools.partial(_prompt_kernel, tiles_per_seq=nt),
        out_shape=(jax.ShapeDtypeStruct((batch, seq, D_MODEL), F32),
                   jax.ShapeDtypeStruct((batch, SUBLANES, D_RG), F32),
                   jax.ShapeDtypeStruct((batch, 1, D_RG), F32),
                   jax.ShapeDtypeStruct((batch, CHUNK, D_SG), F32)),
        grid=(n_tiles + 1,),
        in_specs=[pl.BlockSpec((None, tt, D_MODEL), lambda s: (cur(s) // nt, cur(s) % nt, 0)),
                  pl.BlockSpec((None, tt, D_MODEL), lambda s: (prev(s) // nt, prev(s) % nt, 0)),
                  pl.BlockSpec((batch, 6 * D_MODEL), lambda s: (dec_batch // batch, 0),
                               pipeline_mode=pl.Buffered(1)),
                  _resident(v1024.shape), _resident(v512.shape)] + weight_specs +
                 [_resident(wsp.shape), _resident(bias_s.shape)] + [pl.BlockSpec(memory_space=pl.ANY)] * 3,
        out_specs=(pl.BlockSpec((None, tt, D_MODEL), lambda s: (prev(s) // nt, prev(s) % nt, 0)),
                   pl.BlockSpec((batch, SUBLANES, D_RG), lambda s: (0, 0, 0)),
                   pl.BlockSpec((batch, 1, D_RG), lambda s: (0, 0, 0)),
                   pl.BlockSpec((batch, CHUNK, D_SG), lambda s: (0, 0, 0))),
        scratch_shapes=[pltpu.VMEM((D_RG // LANES, tt, LANES), F32), pltpu.VMEM((D_RG // LANES, tt, LANES), F32),
                        pltpu.VMEM((SUBLANES, D_RG), F32), pltpu.VMEM((SUBLANES, D_RG), F32),
                        pltpu.VMEM((tt, D_MODEL), BF16)] +
                       [pltpu.VMEM(w.shape, BF16) for w in tail_weights] + [pltpu.SemaphoreType.DMA((3,))],
        compiler_params=cparams(dimension_semantics=("arbitrary",)),
        name="prompt_trunk",
    )(x_prompt, x_prompt, ada, v1024, v512, *weights, wsp, bias_s, *tail_weights)
    conv_p = conv8_p[:, SUBLANES - (CONV_W - 1):, :][None]
    lru_p = lru_p.reshape(1, batch, D_RG)
    vrows_p = vrows_p[None]

    nb = SAMPLE_NB
    n_cs = CONV_W - 1

    def seq_rows(width):
        return pl.BlockSpec((nb, width), lambda i: (i, 0))

    y_s, conv_s, lru_s, vrows_s = pl.pallas_call(
        _sample_kernel,
        out_shape=(jax.ShapeDtypeStruct((dec_batch, dec_seq * D_MODEL), F32),
                   jax.ShapeDtypeStruct((dec_batch, n_cs * D_RG), F32),
                   jax.ShapeDtypeStruct((dec_batch, D_RG), F32),
                   jax.ShapeDtypeStruct((dec_batch, dec_seq * D_SG), F32)),
        grid=(dec_batch // nb,),
        in_specs=[seq_rows(dec_seq * D_MODEL), seq_rows(6 * D_MODEL), seq_rows(n_cs * D_RG), seq_rows(D_RG),
                  _resident(v1024.shape), _resident(v512.shape)] + weight_specs +
                 [_resident(w8.shape), _resident(b8.shape)] + [pl.BlockSpec(memory_space=pl.ANY)] * 3,
        out_specs=(seq_rows(dec_seq * D_MODEL), seq_rows(n_cs * D_RG), seq_rows(D_RG), seq_rows(dec_seq * D_SG)),
        scratch_shapes=[pltpu.VMEM(w.shape, BF16) for w in tail_weights] + [pltpu.SemaphoreType.DMA((3,))],
        compiler_params=cparams(dimension_semantics=("arbitrary",)),
        name="sample_trunk",
    )(x_sample.reshape(dec_batch, dec_seq * D_MODEL), ada, state_conv[0].reshape(dec_batch, n_cs * D_RG),
      state_rglru[0], v1024, v512, *weights, w8, b8, *tail_weights)

    return (y_p, y_s.reshape(dec_batch, dec_seq, D_MODEL), conv_p, lru_p, vrows_p,
            conv_s.reshape(1, dec_batch, n_cs, D_RG), lru_s[None], vrows_s.reshape(1, dec_batch, dec_seq, D_SG))
```

```python
import functools

import jax
import jax.numpy as jnp
from jax import lax
from jax.experimental import pallas as pl
from jax.experimental.pallas import tpu as pltpu

D_MODEL = 1024
D_RG = 512
D_SG = 512
D_IN = 2 * D_RG + 2 * D_SG
D_FF = 4 * D_MODEL
RG_HEADS = 8
RG_HEAD_DIM = D_RG // RG_HEADS
SG_HEADS = 8
SG_HEAD_DIM = D_SG // SG_HEADS
CHUNK = 128
CONV_W = 4
LRU_C = 8.0
EPS = 1e-6

SUBLANES = 8
LANES = 128
GATE_GROUPS = 2
GATE_W = D_RG // GATE_GROUPS
PROMPT_TILE = 512
SAMPLE_NB = 64
FF_CHUNK = 1024
N_FF_CHUNKS = D_FF // FF_CHUNK
PACE_BLOCK = 256
COEF_CHUNKS = 8
PREP_STEPS = 8
VMEM_LIMIT_BYTES = 60000 * 1024

F32 = jnp.float32
BF16 = jnp.bfloat16

_V_CONV_W, _V_CONV_B, _V_B_RA, _V_B_RI, _V_LRU_L, _V_LN_G, _V_LN_B, _V_G_RG, _V_G_SG = 0, 4, 5, 6, 7, 8, 9, 10, 11


def _dot(a, b):
    return jnp.dot(a, b, preferred_element_type=F32)


def _rms(x, g):
    ms = jnp.mean(x * x, axis=-1, keepdims=True)
    return (x * lax.rsqrt(ms + EPS)) * g


def _sigmoid(x):
    return 0.5 * jnp.tanh(0.5 * x) + 0.5


def _gelu_tanh(x, zero=None):
    k = 0.7978845608028654
    k_row = _anchored(jnp.full((1, x.shape[1]), k, F32), zero)
    u = x * (k_row + (k * 0.044715) * (x * x))
    return (0.5 * x) * (1.0 + jnp.tanh(u))


def _layer_norm(x, g, b, zero=None):
    mu = jnp.mean(x, axis=-1, keepdims=True)
    if zero is not None:
        mu = mu + zero[:, 0:1]
    xc = x - mu
    var = jnp.mean(xc * xc, axis=-1, keepdims=True)
    return (xc * lax.rsqrt(var + EPS)) * g + b


def _cat(parts, axis=0):
    return parts[0] if len(parts) == 1 else jnp.concatenate(parts, axis=axis)


def _split_mods(ada):
    return [ada[:, k * D_MODEL:(k + 1) * D_MODEL] for k in range(6)]


def _modulated_norm(x_blocks, g, sc, sh, zero=None):
    gain = _anchored(g * (1.0 + sc), zero)
    return _cat([(_rms(xb, gain) + sh).astype(BF16) for xb in x_blocks])


def _gate_preacts(xc, wg_ref):
    xcb = xc.astype(BF16)
    return _cat([_dot(xcb[:, g * GATE_W:(g + 1) * GATE_W], wg_ref[g]) for g in range(GATE_GROUPS)], 1)


def _gates(pre, v512, zero=None):
    b_ra = _anchored(v512[_V_B_RA:_V_B_RA + 1, :], zero)
    b_ri = _anchored(v512[_V_B_RI:_V_B_RI + 1, :], zero)
    r_parts, i_parts = [], []
    for g in range(GATE_GROUPS):
        lo, hi = g * GATE_W, (g + 1) * GATE_W
        ra = pre[:, 2 * lo:2 * lo + GATE_W]
        ri = pre[:, 2 * lo + GATE_W:2 * hi]
        r_parts.append(_sigmoid(ra + b_ra[:, lo:hi]))
        i_parts.append(_sigmoid(ri + b_ri[:, lo:hi]))
    return _cat(r_parts, 1), _cat(i_parts, 1)


def _lru_coeffs(r, v512):
    lam = v512[_V_LRU_L:_V_LRU_L + 1, :]
    log_sig = jnp.minimum(lam, 0.0) - jnp.log1p(jnp.exp(-jnp.abs(lam)))
    log_a = (LRU_C * r) * log_sig
    a = jnp.exp(log_a)
    z = 1.0 - a * a
    mult = jnp.where(z > 0.0, z * lax.rsqrt(z), 0.0)
    return a, mult


def _publish(pace, name, result):
    if pace is not None:
        pace[name] = [result[0:SUBLANES, n * PACE_BLOCK:n * PACE_BLOCK + LANES]
                      for n in range(result.shape[1] // PACE_BLOCK)]


def _zero_after(pace, plan, name):
    if pace is None or name not in plan:
        return None
    mm, blk = plan[name]
    bits = lax.shift_right_logical(pltpu.bitcast(pace[mm][blk], jnp.uint32), jnp.uint32(16))
    return pltpu.bitcast(lax.shift_right_logical(bits, jnp.uint32(16)), F32)[0:1, :]


def _anchored(row, zero):
    return row if zero is None else row + _cat([zero] * (row.shape[1] // LANES), 1)


def _tail_thunks(x_blocks, merged_fn, mods, v1024, w_out_ref, w_ff1_ref, w_ff2_ref, store_y, pace=None):
    _, _, gt_m, sh_f, sc_f, gt_f = mods
    rows = x_blocks[0].shape[0]
    st = {}

    def w_out():
        mm = _dot(merged_fn(), w_out_ref[...])
        _publish(pace, 'w_out', mm)
        st['x1'] = [xb + gt_m * mm[i * rows:(i + 1) * rows] for i, xb in enumerate(x_blocks)]

    def hf():
        st['hf'] = _modulated_norm(st['x1'], v1024[1:2, :], sc_f, sh_f)

    def ff1(c):
        pre = _dot(st['hf'], w_ff1_ref[:, c * FF_CHUNK:(c + 1) * FF_CHUNK])
        _publish(pace, 'ff1_%d' % c, pre)
        h1 = jnp.maximum(pre.astype(BF16), 0.0)
        st['h1_%d' % c] = h1 * h1

    def ff2(c):
        part = _dot(st.pop('h1_%d' % c), w_ff2_ref[c * FF_CHUNK:(c + 1) * FF_CHUNK, :])
        _publish(pace, 'ff2_%d' % c, part)
        st['acc'] = part if c == 0 else st['acc'] + part

    def final():
        for i, xb in enumerate(st['x1']):
            x2 = xb + gt_f * st['acc'][i * rows:(i + 1) * rows]
            store_y(i, _rms(x2, v1024[2:3, :]))

    thunks = {'w_out': w_out, 'hf': hf, 'final': final}
    for c in range(N_FF_CHUNKS):
        thunks['ff1_%d' % c] = functools.partial(ff1, c)
        thunks['ff2_%d' % c] = functools.partial(ff2, c)
    return thunks


TAIL_ORDER = (['w_out', 'hf', 'ff1_0'] +
              [name for c in range(N_FF_CHUNKS)
               for name in (['ff1_%d' % (c + 1)] if c + 1 < N_FF_CHUNKS else []) + ['ff2_%d' % c]] +
              ['final'])


def _run(thunks, order):
    assert sorted(order) == sorted(thunks), (sorted(order), sorted(thunks))
    for name in order:
        thunks[name]()


def _prompt_proj_thunks(x_ref, ada, v1024, w_in_ref, proj_ref, xperm_ref, xtail_ref, pace=None):
    tt = x_ref.shape[0]
    sub_len = tt // SUBLANES
    st = {}

    def norm():
        mods = _split_mods(ada)
        st['hm'] = _modulated_norm([x_ref[...]], v1024[0:1, :], mods[1], mods[0],
                                   _zero_after(pace, PACE_PLAN, 'p_norm'))

    def project():
        proj = _dot(st['hm'], w_in_ref[...])
        proj_ref[...] = proj[:, D_RG:]
        xr = proj[:, 0:D_RG]
        xtail_ref[...] = xr[tt - SUBLANES:, :]
        for p in range(SUBLANES):
            for s in range(D_RG // LANES):
                xperm_ref[s, pl.ds(p, sub_len, stride=SUBLANES), :] = (
                    xr[p * sub_len:(p + 1) * sub_len, s * LANES:(s + 1) * LANES])

    return {'norm': norm, 'w_in': project}


PROJ_ORDER = ['norm', 'w_in']


def _prompt_mix_thunks(tt, v512, wg_ref, wsp_ref, bias_s_ref, proj_ref, xperm_ref, xtail_ref,
                       conv_ref, lru_ref, vrows_ref, hperm_ref, tail_ref, h_ref, merged_ref, j, slot, pace=None):
    n_sub = SUBLANES
    sub_len = tt // n_sub
    n_slab = D_RG // LANES
    st = {}
    col_yg, col_u, col_v = 0, D_RG, D_RG + D_SG

    def conv():
        xr_p = _cat([xperm_ref[s] for s in range(n_slab)], 1)
        sub = lax.broadcasted_iota(jnp.int32, (SUBLANES, D_RG), 0)
        prev_tail = tail_ref[...]
        head = []
        for k in range(CONV_W - 1, 0, -1):
            grp = pltpu.roll(xr_p[(sub_len - k) * SUBLANES:(sub_len - k + 1) * SUBLANES, :], 1, 0)
            head.append(jnp.where(sub == 0, prev_tail[SUBLANES - k:SUBLANES - k + 1, :], grp))
        ext = _cat(head + [xr_p])
        xc = v512[_V_CONV_B:_V_CONV_B + 1, :]
        for k in range(CONV_W):
            xc = xc + ext[k * SUBLANES:k * SUBLANES + tt, :] * v512[_V_CONV_W + k:_V_CONV_W + k + 1, :]
        st['xc'] = xc
        xr_tail = xtail_ref[...]
        tail_ref[...] = xr_tail
        conv_ref[slot] = xr_tail

    def gate_mm():
        st['gate_pre'] = _gate_preacts(st['xc'], wg_ref)

    rows_c = tt // COEF_CHUNKS

    def coefficients(k):
        r0 = k * rows_c
        r, i_gate = _gates(st['gate_pre'][r0:r0 + rows_c, :], v512, _zero_after(pace, PACE_PLAN, 'coef_%d' % k))
        a, mult = _lru_coeffs(r, v512)
        if k == 0:
            row = lax.broadcasted_iota(jnp.int32, (rows_c, D_RG), 0)
            mult = jnp.where(row + j * tt == 0, 1.0, mult)
        st['a_%d' % k] = a
        st['b_%d' % k] = mult * i_gate * st['xc'][r0:r0 + rows_c, :]

    def recurrence():
        a = _cat([st.pop('a_%d' % k) for k in range(COEF_CHUNKS)])
        b = _cat([st.pop('b_%d' % k) for k in range(COEF_CHUNKS)])
        sub = lax.broadcasted_iota(jnp.int32, (SUBLANES, D_RG), 0)
        hs, ps = [], []
        for q in range(sub_len):
            aq = a[q * SUBLANES:(q + 1) * SUBLANES, :]
            bq = b[q * SUBLANES:(q + 1) * SUBLANES, :]
            hs.append(bq if q == 0 else aq * hs[-1] + bq)
            ps.append(aq if q == 0 else aq * ps[-1])
        init = jnp.where(sub == 0, h_ref[SUBLANES - 1:SUBLANES, :], 0.0)
        for p in range(1, n_sub):
            end = hs[-1] + ps[-1] * init
            init = jnp.where(sub == p, pltpu.roll(end, 1, 0), init)
        end = hs[-1] + ps[-1] * init
        h_ref[...] = end
        lru_ref[slot] = end[SUBLANES - 1:SUBLANES, :]
        h_perm = _cat([hq + pq * init for hq, pq in zip(hs, ps)])
        for s in range(n_slab):
            hperm_ref[s] = h_perm[:, s * LANES:(s + 1) * LANES]

    def recurrent_out(c):
        r0 = c * CHUNK
        yg = proj_ref[r0:r0 + CHUNK, col_yg:col_yg + D_RG]
        subs = range(r0 // sub_len, (r0 + CHUNK) // sub_len)
        h = _cat([_cat([hperm_ref[s, pl.ds(p, sub_len, stride=n_sub), :] for p in subs]) for s in range(n_slab)], 1)
        rg_out = h * _gelu_tanh(yg, _zero_after(pace, PACE_PLAN, 'rg_out_%d' % c))
        merged_ref[r0:r0 + CHUNK, 0:D_RG] = _rms(rg_out, v512[_V_G_RG:_V_G_RG + 1, :]).astype(BF16)

    def spatial_pre(c):
        r0 = c * CHUNK
        lane = lax.broadcasted_iota(jnp.int32, (CHUNK, 2 * SG_HEAD_DIM), 1)
        v = proj_ref[r0:r0 + CHUNK, col_v:col_v + D_SG]
        vn = _layer_norm(v, v512[_V_LN_G:_V_LN_G + 1, :], v512[_V_LN_B:_V_LN_B + 1, :],
                         _zero_after(pace, PACE_PLAN, 'sp_pre_%d' % c))
        rhs = []
        for p in range(SG_HEADS // 2):
            vp = vn[:, p * 2 * SG_HEAD_DIM:(p + 1) * 2 * SG_HEAD_DIM]
            rhs.append(jnp.concatenate([jnp.where(lane < SG_HEAD_DIM, vp, 0.0),
                                        jnp.where(lane >= SG_HEAD_DIM, vp, 0.0)], axis=0).astype(BF16))
        st['rhs_%d' % c] = rhs
        if r0 + CHUNK == tt:
            vrows_ref[slot] = v

    def spatial_mm(c):
        t_idx = lax.broadcasted_iota(jnp.int32, (CHUNK, 2 * CHUNK), 0)
        s_idx = lax.broadcasted_iota(jnp.int32, (CHUNK, 2 * CHUNK), 1) & (CHUNK - 1)
        outs = []
        for p, rhs in enumerate(st.pop('rhs_%d' % c)):
            ws_pair = jnp.where(s_idx <= t_idx, wsp_ref[p], jnp.zeros((), BF16))
            outs.append(_dot(ws_pair, rhs))
        st['mixed_%d' % c] = _cat(outs, 1)

    def spatial_post(c):
        r0 = c * CHUNK
        u = proj_ref[r0:r0 + CHUNK, col_u:col_u + D_SG]
        sg_out = u * (st.pop('mixed_%d' % c) + bias_s_ref[...])
        merged_ref[r0:r0 + CHUNK, D_RG:] = _rms(sg_out, v512[_V_G_SG:_V_G_SG + 1, :]).astype(BF16)

    thunks = {'conv': conv, 'gate_mm': gate_mm, 'rec': recurrence}
    for k in range(COEF_CHUNKS):
        thunks['coef_%d' % k] = functools.partial(coefficients, k)
    for c in range(tt // CHUNK):
        thunks['sp_pre_%d' % c] = functools.partial(spatial_pre, c)
        thunks['sp_mm_%d' % c] = functools.partial(spatial_mm, c)
        thunks['sp_post_%d' % c] = functools.partial(spatial_post, c)
        thunks['rg_out_%d' % c] = functools.partial(recurrent_out, c)
    return thunks


N_PROMPT_CHUNKS = PROMPT_TILE // CHUNK
MIX_ORDER = (['conv', 'gate_mm'] + ['coef_%d' % k for k in range(COEF_CHUNKS)] + ['rec'] +
             [f'{name}_{c}' for c in range(N_PROMPT_CHUNKS) for name in ('sp_pre', 'sp_mm', 'sp_post', 'rg_out')])

assert N_PROMPT_CHUNKS == 4 and N_FF_CHUNKS == 4 and COEF_CHUNKS == 8 and FF_CHUNK // PACE_BLOCK == 4
INTERLEAVED_ORDER = [
    ('a', 'conv'), ('b', 'w_out'), ('a', 'gate_mm'), ('b', 'hf'), ('a', 'sp_pre_0'), ('a', 'sp_pre_1'),
    ('b', 'ff1_0'), ('a', 'coef_0'), ('a', 'coef_1'), ('a', 'coef_2'),
    ('b', 'ff1_1'), ('a', 'coef_3'), ('a', 'coef_4'), ('a', 'coef_5'), ('a', 'sp_mm_0'), ('a', 'sp_mm_1'),
    ('b', 'ff1_2'), ('a', 'coef_6'), ('a', 'coef_7'), ('a', 'rec'), ('a', 'sp_post_0'), ('a', 'sp_post_1'),
    ('a', 'sp_pre_2'), ('a', 'sp_pre_3'),
    ('b', 'ff1_3'), ('a', 'sp_mm_2'), ('a', 'sp_mm_3'), ('a', 'rg_out_0'), ('a', 'rg_out_1'),
    ('b', 'ff2_0'), ('a', 'sp_post_2'), ('a', 'sp_post_3'), ('a', 'rg_out_2'),
    ('b', 'ff2_1'), ('a', 'rg_out_3'), ('p', 'norm'),
    ('b', 'ff2_2'), ('b', 'ff2_3'), ('p', 'w_in'), ('b', 'final'),
]
PACE_PLAN = {
    'sp_pre_0': ('w_out', 1), 'sp_pre_1': ('w_out', 3),
    'coef_0': ('ff1_0', 1), 'coef_1': ('ff1_0', 2), 'coef_2': ('ff1_0', 3),
    'coef_3': ('ff1_1', 0), 'coef_4': ('ff1_1', 1), 'coef_5': ('ff1_1', 3),
    'coef_6': ('ff1_2', 0), 'coef_7': ('ff1_2', 1),
    'sp_pre_2': ('ff1_2', 2), 'sp_pre_3': ('ff1_2', 3),
    'rg_out_0': ('ff1_3', 1), 'rg_out_1': ('ff1_3', 3), 'rg_out_2': ('ff2_0', 2), 'rg_out_3': ('ff2_1', 1),
    'p_norm': ('ff2_1', 3),
}


def _run_interleaved(stages):
    for key, thunks in stages.items():
        assert sorted(n for k, n in INTERLEAVED_ORDER if k == key) == sorted(thunks), key
    for key, name in INTERLEAVED_ORDER:
        if key in stages:
            stages[key][name]()


def _tail_weight_copies(hbm_refs, vmem_refs, sem):
    return [pltpu.make_async_copy(src, dst, sem.at[k]) for k, (src, dst) in enumerate(zip(hbm_refs, vmem_refs))]


def _prompt_kernel(x_ref, xprev_ref, ada_ref, v1024_ref, v512_ref, w_in_ref, wg_ref, wsp_ref,
                   bias_s_ref, w_out_hbm, w_ff1_hbm, w_ff2_hbm,
                   y_ref, conv_ref, lru_ref, vrows_ref,
                   proj_ref, xperm_ref, xtail_ref, hperm_ref, tail_ref, h_ref, merged_ref,
                   w_out_ref, w_ff1_ref, w_ff2_ref, w_sem, *, tiles_per_seq):
    tt = x_ref.shape[0]
    g = pl.program_id(0)
    n_tiles = pl.num_programs(0) - 2
    seq_proj = jnp.minimum(g, n_tiles - 1) // tiles_per_seq
    t_mix = jnp.clip(g - 1, 0, n_tiles - 1)
    seq_mix = t_mix // tiles_per_seq
    j_mix = lax.rem(t_mix, tiles_per_seq)
    seq_tail = jnp.clip(g - 2, 0, n_tiles - 1) // tiles_per_seq

    @pl.when(jnp.logical_and(j_mix == 0, g >= 1))
    def _():
        tail_ref[...] = jnp.zeros((SUBLANES, D_RG), F32)
        h_ref[...] = jnp.zeros((SUBLANES, D_RG), F32)

    v1024 = v1024_ref[...]
    v512 = v512_ref[...]

    def store_y(_, val):
        y_ref[...] = val

    def stage_p(pace=None):
        return _prompt_proj_thunks(x_ref, ada_ref[pl.ds(seq_proj, 1), :], v1024, w_in_ref,
                                   proj_ref, xperm_ref, xtail_ref, pace)

    def stage_a(pace=None):
        return _prompt_mix_thunks(tt, v512, wg_ref, wsp_ref, bias_s_ref, proj_ref, xperm_ref, xtail_ref,
                                  conv_ref, lru_ref, vrows_ref, hperm_ref, tail_ref, h_ref, merged_ref,
                                  j_mix, seq_mix, pace)

    def stage_b(pace=None):
        return _tail_thunks([xprev_ref[...]], lambda: merged_ref[...], _split_mods(ada_ref[pl.ds(seq_tail, 1), :]),
                            v1024, w_out_ref, w_ff1_ref, w_ff2_ref, store_y, pace)

    copies = _tail_weight_copies((w_out_hbm, w_ff1_hbm, w_ff2_hbm), (w_out_ref, w_ff1_ref, w_ff2_ref), w_sem)

    @pl.when(g == 0)
    def _():
        for cp in copies:
            cp.start()
        _run(stage_p(), PROJ_ORDER)

    @pl.when(g == 1)
    def _():
        _run_interleaved({'a': stage_a(), 'p': stage_p()})

    @pl.when(g == 2)
    def _():
        for cp in copies:
            cp.wait()

    @pl.when(jnp.logical_and(g >= 2, g < n_tiles))
    def _():
        pace = {}
        _run_interleaved({'a': stage_a(pace), 'b': stage_b(pace), 'p': stage_p(pace)})

    @pl.when(g == n_tiles)
    def _():
        pace = {}
        _run_interleaved({'a': stage_a(pace), 'b': stage_b(pace)})

    @pl.when(g == n_tiles + 1)
    def _():
        _run(stage_b(), TAIL_ORDER)


def _sample_kernel(x_ref, ada_ref, cs_ref, h0_ref, v1024_ref, v512_ref, w_in_ref, wg_ref, w8_ref, b8_ref,
                   w_out_hbm, w_ff1_hbm, w_ff2_hbm,
                   y_ref, conv_ref, lru_ref, vrows_ref,
                   w_out_ref, w_ff1_ref, w_ff2_ref, w_sem):
    nb, n_t = x_ref.shape[0], x_ref.shape[1] // D_MODEL
    copies = _tail_weight_copies((w_out_hbm, w_ff1_hbm, w_ff2_hbm), (w_out_ref, w_ff1_ref, w_ff2_ref), w_sem)

    @pl.when(pl.program_id(0) == 0)
    def _():
        for cp in copies:
            cp.start()

    mods = _split_mods(ada_ref[...])
    sh_m, sc_m = mods[0], mods[1]
    v1024 = v1024_ref[...]
    v512 = v512_ref[...]

    def blk(arr, t):
        return arr[t * nb:(t + 1) * nb, :]

    x_blocks = [x_ref[:, t * D_MODEL:(t + 1) * D_MODEL] for t in range(n_t)]
    hm = _modulated_norm(x_blocks, v1024[0:1, :], sc_m, sh_m)
    proj = _dot(hm, w_in_ref[...])
    xr = proj[:, 0:D_RG]
    yg = proj[:, D_RG:2 * D_RG]
    u = proj[:, 2 * D_RG:2 * D_RG + D_SG]
    v = proj[:, 2 * D_RG + D_SG:]

    xp = [cs_ref[:, k * D_RG:(k + 1) * D_RG] for k in range(CONV_W - 1)] + [blk(xr, t) for t in range(n_t)]
    xc_blocks = []
    for t in range(n_t):
        acc = v512[_V_CONV_B:_V_CONV_B + 1, :]
        for k in range(CONV_W):
            acc = acc + xp[t + k] * v512[_V_CONV_W + k:_V_CONV_W + k + 1, :]
        xc_blocks.append(acc)
    for k in range(CONV_W - 1):
        conv_ref[:, k * D_RG:(k + 1) * D_RG] = xp[n_t + k]
    xc = _cat(xc_blocks)

    r, i_gate = _gates(_gate_preacts(xc, wg_ref), v512)
    a, mult = _lru_coeffs(r, v512)
    bterm = mult * i_gate * xc
    h = h0_ref[...]
    h_blocks = []
    for t in range(n_t):
        h = blk(a, t) * h + blk(bterm, t)
        h_blocks.append(h)
    lru_ref[...] = h
    rg_out = _cat(h_blocks) * _gelu_tanh(yg)

    vn = _layer_norm(v, v512[_V_LN_G:_V_LN_G + 1, :], v512[_V_LN_B:_V_LN_B + 1, :])
    sg_blocks = []
    for t in range(n_t):
        vrows_ref[:, t * D_SG:(t + 1) * D_SG] = blk(v, t)
        mixed = b8_ref[t:t + 1, :]
        for s in range(t + 1):
            mixed = mixed + w8_ref[t, s:s + 1, :] * blk(vn, s)
        sg_blocks.append(blk(u, t) * mixed)
    sg_out = _cat(sg_blocks)

    merged = jnp.concatenate(
        [_rms(rg_out, v512[_V_G_RG:_V_G_RG + 1, :]).astype(BF16),
         _rms(sg_out, v512[_V_G_SG:_V_G_SG + 1, :]).astype(BF16)], axis=1)

    def store_y(t, val):
        y_ref[:, t * D_MODEL:(t + 1) * D_MODEL] = val

    @pl.when(pl.program_id(0) == 0)
    def _():
        for cp in copies:
            cp.wait()

    _run(_tail_thunks(x_blocks, lambda: merged, mods, v1024, w_out_ref, w_ff1_ref, w_ff2_ref, store_y), TAIL_ORDER)


def _prep_kernel(cs_ref, cp_ref, w_ada_ref, b_ada_ref, w_in_ref, w_out_ref, w_ff1_ref, w_ff2_ref,
                 ada_ref, w_in_o, w_out_o, w_ff1_o, w_ff2_o):
    c = jnp.concatenate([cs_ref[...], cp_ref[...]], axis=0)
    s = (c * jax.nn.sigmoid(c)).astype(BF16)
    ada_ref[...] = _dot(s, w_ada_ref[...].astype(BF16)) + b_ada_ref[...]
    w_in_o[...] = w_in_ref[...].astype(BF16)
    w_out_o[...] = w_out_ref[...].astype(BF16)
    w_ff1_o[...] = w_ff1_ref[...].astype(BF16)
    w_ff2_o[...] = w_ff2_ref[...].astype(BF16)


def _resident(shape):
    nd = len(shape)
    return pl.BlockSpec(shape, lambda *_: (0,) * nd, pipeline_mode=pl.Buffered(1))


def _block_diag_groups(w):
    per = RG_HEADS // GATE_GROUPS
    w4 = w.reshape(GATE_GROUPS, per, RG_HEAD_DIM, RG_HEAD_DIM)
    bd = jnp.einsum('ghij,hk->ghikj', w4, jnp.eye(per, dtype=w.dtype))
    return bd.reshape(GATE_GROUPS, GATE_W, GATE_W)


def kernel(x_prompt, x_sample, c_prompt, c_sample, state_conv, state_rglru, w_ada, b_ada, g_mix, g_ffn, w_in, conv_w, conv_b, w_ra, b_ra, w_ri, b_ri, lru_l, ln_v_g, ln_v_b, w_s, b_s, g_rg, g_sg, w_out, w_ff1, w_ff2, g_final):
    batch, seq, _ = x_prompt.shape
    dec_batch, dec_seq, _ = x_sample.shape
    assert w_ada.shape[0] == 1, "single-layer step"
    assert seq % PROMPT_TILE == 0 and PROMPT_TILE % CHUNK == 0 and dec_batch % SAMPLE_NB == 0
    assert batch * (seq // PROMPT_TILE) >= 3, "the prompt pipeline needs at least three tiles"
    assert dec_seq <= CHUNK

    n_c = dec_batch + batch
    assert dec_batch % SUBLANES == 0 and dec_batch % batch == 0
    steps = PREP_STEPS

    def row_block(w):
        return pl.BlockSpec((w.shape[0] // steps, w.shape[1]), lambda n: (n, 0))

    mats = (w_in[0], w_out[0], w_ff1[0], w_ff2[0])
    ada, w_in_b, w_out_b, w_ff1_b, w_ff2_b = pl.pallas_call(
        _prep_kernel,
        out_shape=(jax.ShapeDtypeStruct((n_c, 6 * D_MODEL), F32),) +
                  tuple(jax.ShapeDtypeStruct(w.shape, BF16) for w in mats),
        grid=(steps,),
        in_specs=[pl.BlockSpec((dec_batch, D_MODEL), lambda n: (0, 0)),
                  pl.BlockSpec((batch, D_MODEL), lambda n: (0, 0)),
                  pl.BlockSpec((D_MODEL, 6 * D_MODEL // steps), lambda n: (0, n)),
                  pl.BlockSpec((1, 6 * D_MODEL // steps), lambda n: (0, n))] + [row_block(w) for w in mats],
        out_specs=(pl.BlockSpec((n_c, 6 * D_MODEL // steps), lambda n: (0, n)),) +
                  tuple(row_block(w) for w in mats),
        compiler_params=pltpu.CompilerParams(dimension_semantics=("arbitrary",),
                                             vmem_limit_bytes=VMEM_LIMIT_BYTES),
        name="prep",
    )(c_sample, c_prompt, w_ada[0], b_ada, *mats)

    v1024 = jnp.stack([g_mix[0], g_ffn[0], g_final], axis=0)
    v512 = jnp.concatenate(
        [conv_w[0], conv_b, b_ra[0].reshape(1, D_RG), b_ri[0].reshape(1, D_RG), lru_l,
         ln_v_g, ln_v_b, g_rg, g_sg], axis=0)
    wg = jnp.concatenate([_block_diag_groups(w_ra[0]), _block_diag_groups(w_ri[0])], axis=-1).astype(BF16)
    wsp = (w_s[0].reshape(SG_HEADS // 2, 2, CHUNK, CHUNK).transpose(0, 2, 1, 3)
           .reshape(SG_HEADS // 2, CHUNK, 2 * CHUNK).astype(BF16))
    bias_s = jnp.repeat(b_s[0].T, SG_HEAD_DIM, axis=1)
    w8 = jnp.repeat(w_s[0, :, :dec_seq, :dec_seq].transpose(1, 2, 0), SG_HEAD_DIM, axis=2)
    b8 = bias_s[:dec_seq]

    weights = (w_in_b, wg)
    weight_specs = [_resident(w_in_b.shape), _resident(wg.shape)]
    tail_weights = (w_out_b, w_ff1_b, w_ff2_b)
    cparams = functools.partial(pltpu.CompilerParams, vmem_limit_bytes=VMEM_LIMIT_BYTES)

    tt = PROMPT_TILE
    nt = seq // tt
    n_tiles = batch * nt

    def cur(g):
        return jnp.minimum(g, n_tiles - 1)

    def prev(g):
        return jnp.clip(g - 2, 0, n_tiles - 1)

    y_p, conv8_p, lru_p, vrows_p = pl.pallas_call(
        functools.partial(_prompt_kernel, tiles_per_seq=nt),
        out_shape=(jax.ShapeDtypeStruct((batch, seq, D_MODEL), F32),
                   jax.ShapeDtypeStruct((batch, SUBLANES, D_RG), F32),
                   jax.ShapeDtypeStruct((batch, 1, D_RG), F32),
                   jax.ShapeDtypeStruct((batch, CHUNK, D_SG), F32)),
        grid=(n_tiles + 2,),
        in_specs=[pl.BlockSpec((None, tt, D_MODEL), lambda s: (cur(s) // nt, cur(s) % nt, 0)),
                  pl.BlockSpec((None, tt, D_MODEL), lambda s: (prev(s) // nt, prev(s) % nt, 0)),
                  pl.BlockSpec((batch, 6 * D_MODEL), lambda s: (dec_batch // batch, 0),
                               pipeline_mode=pl.Buffered(1)),
                  _resident(v1024.shape), _resident(v512.shape)] + weight_specs +
                 [_resident(wsp.shape), _resident(bias_s.shape)] + [pl.BlockSpec(memory_space=pl.ANY)] * 3,
        out_specs=(pl.BlockSpec((None, tt, D_MODEL), lambda s: (prev(s) // nt, prev(s) % nt, 0)),
                   pl.BlockSpec((batch, SUBLANES, D_RG), lambda s: (0, 0, 0)),
                   pl.BlockSpec((batch, 1, D_RG), lambda s: (0, 0, 0)),
                   pl.BlockSpec((batch, CHUNK, D_SG), lambda s: (0, 0, 0))),
        scratch_shapes=[pltpu.VMEM((tt, D_IN - D_RG), F32),
                        pltpu.VMEM((D_RG // LANES, tt, LANES), F32), pltpu.VMEM((SUBLANES, D_RG), F32),
                        pltpu.VMEM((D_RG // LANES, tt, LANES), F32),
                        pltpu.VMEM((SUBLANES, D_RG), F32), pltpu.VMEM((SUBLANES, D_RG), F32),
                        pltpu.VMEM((tt, D_MODEL), BF16)] +
                       [pltpu.VMEM(w.shape, BF16) for w in tail_weights] + [pltpu.SemaphoreType.DMA((3,))],
        compiler_params=cparams(dimension_semantics=("arbitrary",)),
        name="prompt_trunk",
    )(x_prompt, x_prompt, ada, v1024, v512, *weights, wsp, bias_s, *tail_weights)
    conv_p = conv8_p[:, SUBLANES - (CONV_W - 1):, :][None]
    lru_p = lru_p.reshape(1, batch, D_RG)
    vrows_p = vrows_p[None]

    nb = SAMPLE_NB
    n_cs = CONV_W - 1

    def seq_rows(width):
        return pl.BlockSpec((nb, width), lambda i: (i, 0))

    y_s, conv_s, lru_s, vrows_s = pl.pallas_call(
        _sample_kernel,
        out_shape=(jax.ShapeDtypeStruct((dec_batch, dec_seq * D_MODEL), F32),
                   jax.ShapeDtypeStruct((dec_batch, n_cs * D_RG), F32),
                   jax.ShapeDtypeStruct((dec_batch, D_RG), F32),
                   jax.ShapeDtypeStruct((dec_batch, dec_seq * D_SG), F32)),
        grid=(dec_batch // nb,),
        in_specs=[seq_rows(dec_seq * D_MODEL), seq_rows(6 * D_MODEL), seq_rows(n_cs * D_RG), seq_rows(D_RG),
                  _resident(v1024.shape), _resident(v512.shape)] + weight_specs +
                 [_resident(w8.shape), _resident(b8.shape)] + [pl.BlockSpec(memory_space=pl.ANY)] * 3,
        out_specs=(seq_rows(dec_seq * D_MODEL), seq_rows(n_cs * D_RG), seq_rows(D_RG), seq_rows(dec_seq * D_SG)),
        scratch_shapes=[pltpu.VMEM(w.shape, BF16) for w in tail_weights] + [pltpu.SemaphoreType.DMA((3,))],
        compiler_params=cparams(dimension_semantics=("arbitrary",)),
        name="sample_trunk",
    )(x_sample.reshape(dec_batch, dec_seq * D_MODEL), ada, state_conv[0].reshape(dec_batch, n_cs * D_RG),
      state_rglru[0], v1024, v512, *weights, w8, b8, *tail_weights)

    return (y_p, y_s.reshape(dec_batch, dec_seq, D_MODEL), conv_p, lru_p, vrows_p,
            conv_s.reshape(1, dec_batch, n_cs, D_RG), lru_s[None], vrows_s.reshape(1, dec_batch, dec_seq, D_SG))
```

```python
import functools

import jax
import jax.numpy as jnp
from jax import lax
from jax.experimental import pallas as pl
from jax.experimental.pallas import tpu as pltpu

D_MODEL = 1024
D_RG = 512
D_SG = 512
D_IN = 2 * D_RG + 2 * D_SG
D_FF = 4 * D_MODEL
RG_HEADS = 8
RG_HEAD_DIM = D_RG // RG_HEADS
SG_HEADS = 8
SG_HEAD_DIM = D_SG // SG_HEADS
CHUNK = 128
CONV_W = 4
LRU_C = 8.0
EPS = 1e-6

SUBLANES = 8
LANES = 128
GATE_GROUPS = 2
GATE_W = D_RG // GATE_GROUPS
PROMPT_TILE = 512
SAMPLE_NB = 64
FF_CHUNK = 1024
N_FF_CHUNKS = D_FF // FF_CHUNK
PACE_BLOCK = 256
COEF_CHUNKS = 8
PREP_STEPS = 8
VMEM_LIMIT_BYTES = 60000 * 1024

F32 = jnp.float32
BF16 = jnp.bfloat16

_V_CONV_W, _V_CONV_B, _V_B_RA, _V_B_RI, _V_LRU_L, _V_LN_G, _V_LN_B, _V_G_RG, _V_G_SG = 0, 4, 5, 6, 7, 8, 9, 10, 11


def _dot(a, b):
    return jnp.dot(a, b, preferred_element_type=F32)


def _rms(x, g):
    ms = jnp.mean(x * x, axis=-1, keepdims=True)
    return (x * lax.rsqrt(ms + EPS)) * g


def _sigmoid(x):
    return 0.5 * jnp.tanh(0.5 * x) + 0.5


def _gelu_tanh(x):
    k = 0.7978845608028654
    u = x * (k + (k * 0.044715) * (x * x))
    return (0.5 * x) * (1.0 + jnp.tanh(u))


def _layer_norm(x, g, b):
    mu = jnp.mean(x, axis=-1, keepdims=True)
    xc = x - mu
    var = jnp.mean(xc * xc, axis=-1, keepdims=True)
    return (xc * lax.rsqrt(var + EPS)) * g + b


def _cat(parts, axis=0):
    return parts[0] if len(parts) == 1 else jnp.concatenate(parts, axis=axis)


def _split_mods(ada):
    return [ada[:, k * D_MODEL:(k + 1) * D_MODEL] for k in range(6)]


def _modulated_norm(x_blocks, g, sc, sh):
    gain = g * (1.0 + sc)
    return _cat([(_rms(xb, gain) + sh).astype(BF16) for xb in x_blocks])


def _gate_preacts(xc, wg_ref):
    xcb = xc.astype(BF16)
    return _cat([_dot(xcb[:, g * GATE_W:(g + 1) * GATE_W], wg_ref[g]) for g in range(GATE_GROUPS)], 1)


def _gates(pre, v512, zero=None):
    b_ra = _anchored(v512[_V_B_RA:_V_B_RA + 1, :], zero)
    b_ri = _anchored(v512[_V_B_RI:_V_B_RI + 1, :], zero)
    r_parts, i_parts = [], []
    for g in range(GATE_GROUPS):
        lo, hi = g * GATE_W, (g + 1) * GATE_W
        ra = pre[:, 2 * lo:2 * lo + GATE_W]
        ri = pre[:, 2 * lo + GATE_W:2 * hi]
        r_parts.append(_sigmoid(ra + b_ra[:, lo:hi]))
        i_parts.append(_sigmoid(ri + b_ri[:, lo:hi]))
    return _cat(r_parts, 1), _cat(i_parts, 1)


def _lru_coeffs(r, v512):
    lam = v512[_V_LRU_L:_V_LRU_L + 1, :]
    log_sig = jnp.minimum(lam, 0.0) - jnp.log1p(jnp.exp(-jnp.abs(lam)))
    log_a = (LRU_C * r) * log_sig
    a = jnp.exp(log_a)
    z = 1.0 - a * a
    mult = jnp.where(z > 0.0, z * lax.rsqrt(z), 0.0)
    return a, mult


def _publish(pace, name, result):
    if pace is not None:
        pace[name] = [result[0:SUBLANES, n * PACE_BLOCK:n * PACE_BLOCK + LANES]
                      for n in range(result.shape[1] // PACE_BLOCK)]


def _zero_after(pace, plan, name):
    if pace is None or name not in plan:
        return None
    mm, blk = plan[name]
    bits = lax.shift_right_logical(pltpu.bitcast(pace[mm][blk], jnp.uint32), jnp.uint32(16))
    return pltpu.bitcast(lax.shift_right_logical(bits, jnp.uint32(16)), F32)[0:1, :]


def _anchored(row, zero):
    return row if zero is None else row + _cat([zero] * (row.shape[1] // LANES), 1)


def _tail_thunks(x_blocks, merged_fn, mods, v1024, w_out_ref, w_ff1_ref, w_ff2_ref, store_y, pace=None):
    _, _, gt_m, sh_f, sc_f, gt_f = mods
    rows = x_blocks[0].shape[0]
    st = {}

    def w_out():
        mm = _dot(merged_fn(), w_out_ref[...])
        st['x1'] = [xb + gt_m * mm[i * rows:(i + 1) * rows] for i, xb in enumerate(x_blocks)]

    def hf():
        st['hf'] = _modulated_norm(st['x1'], v1024[1:2, :], sc_f, sh_f)

    def ff1(c):
        pre = _dot(st['hf'], w_ff1_ref[:, c * FF_CHUNK:(c + 1) * FF_CHUNK])
        _publish(pace, 'ff1_%d' % c, pre)
        h1 = jnp.maximum(pre.astype(BF16), 0.0)
        st['h1_%d' % c] = h1 * h1

    def ff2(c):
        part = _dot(st.pop('h1_%d' % c), w_ff2_ref[c * FF_CHUNK:(c + 1) * FF_CHUNK, :])
        st['acc'] = part if c == 0 else st['acc'] + part

    def final():
        for i, xb in enumerate(st['x1']):
            x2 = xb + gt_f * st['acc'][i * rows:(i + 1) * rows]
            store_y(i, _rms(x2, v1024[2:3, :]))

    thunks = {'w_out': w_out, 'hf': hf, 'final': final}
    for c in range(N_FF_CHUNKS):
        thunks['ff1_%d' % c] = functools.partial(ff1, c)
        thunks['ff2_%d' % c] = functools.partial(ff2, c)
    return thunks


TAIL_ORDER = (['w_out', 'hf', 'ff1_0'] +
              [name for c in range(N_FF_CHUNKS)
               for name in (['ff1_%d' % (c + 1)] if c + 1 < N_FF_CHUNKS else []) + ['ff2_%d' % c]] +
              ['final'])


def _run(thunks, order):
    assert sorted(order) == sorted(thunks), (sorted(order), sorted(thunks))
    for name in order:
        thunks[name]()


def _prompt_mix_thunks(x_ref, ada, v1024, v512, w_in_ref, wg_ref, wsp_ref, bias_s_ref,
                       conv_ref, lru_ref, vrows_ref, xperm_ref, hperm_ref, tail_ref, h_ref, merged_ref, j, slot,
                       pace=None):
    tt = x_ref.shape[0]
    n_sub = SUBLANES
    sub_len = tt // n_sub
    n_slab = D_RG // LANES
    st = {}

    def norm():
        mods = _split_mods(ada)
        st['hm'] = _modulated_norm([x_ref[...]], v1024[0:1, :], mods[1], mods[0])

    def project():
        st['proj'] = proj = _dot(st['hm'], w_in_ref[...])
        xr = proj[:, 0:D_RG]
        st['xr_tail'] = xr[tt - SUBLANES:, :]
        for p in range(n_sub):
            for s in range(n_slab):
                xperm_ref[s, pl.ds(p, sub_len, stride=n_sub), :] = (
                    xr[p * sub_len:(p + 1) * sub_len, s * LANES:(s + 1) * LANES])

    def conv():
        xr_p = _cat([xperm_ref[s] for s in range(n_slab)], 1)
        sub = lax.broadcasted_iota(jnp.int32, (SUBLANES, D_RG), 0)
        prev_tail = tail_ref[...]
        head = []
        for k in range(CONV_W - 1, 0, -1):
            grp = pltpu.roll(xr_p[(sub_len - k) * SUBLANES:(sub_len - k + 1) * SUBLANES, :], 1, 0)
            head.append(jnp.where(sub == 0, prev_tail[SUBLANES - k:SUBLANES - k + 1, :], grp))
        ext = _cat(head + [xr_p])
        xc = v512[_V_CONV_B:_V_CONV_B + 1, :]
        for k in range(CONV_W):
            xc = xc + ext[k * SUBLANES:k * SUBLANES + tt, :] * v512[_V_CONV_W + k:_V_CONV_W + k + 1, :]
        st['xc'] = xc
        tail_ref[...] = st['xr_tail']
        conv_ref[slot] = st['xr_tail']

    def gate_mm():
        st['gate_pre'] = _gate_preacts(st['xc'], wg_ref)

    rows_c = tt // COEF_CHUNKS

    def coefficients(k):
        r0 = k * rows_c
        r, i_gate = _gates(st['gate_pre'][r0:r0 + rows_c, :], v512, _zero_after(pace, PACE_PLAN, 'coef_%d' % k))
        a, mult = _lru_coeffs(r, v512)
        if k == 0:
            row = lax.broadcasted_iota(jnp.int32, (rows_c, D_RG), 0)
            mult = jnp.where(row + j * tt == 0, 1.0, mult)
        st['a_%d' % k] = a
        st['b_%d' % k] = mult * i_gate * st['xc'][r0:r0 + rows_c, :]

    def recurrence():
        a = _cat([st.pop('a_%d' % k) for k in range(COEF_CHUNKS)])
        b = _cat([st.pop('b_%d' % k) for k in range(COEF_CHUNKS)])
        sub = lax.broadcasted_iota(jnp.int32, (SUBLANES, D_RG), 0)
        hs, ps = [], []
        for q in range(sub_len):
            aq = a[q * SUBLANES:(q + 1) * SUBLANES, :]
            bq = b[q * SUBLANES:(q + 1) * SUBLANES, :]
            hs.append(bq if q == 0 else aq * hs[-1] + bq)
            ps.append(aq if q == 0 else aq * ps[-1])
        init = jnp.where(sub == 0, h_ref[SUBLANES - 1:SUBLANES, :], 0.0)
        for p in range(1, n_sub):
            end = hs[-1] + ps[-1] * init
            init = jnp.where(sub == p, pltpu.roll(end, 1, 0), init)
        end = hs[-1] + ps[-1] * init
        h_ref[...] = end
        lru_ref[slot] = end[SUBLANES - 1:SUBLANES, :]
        h_perm = _cat([hq + pq * init for hq, pq in zip(hs, ps)])
        for s in range(n_slab):
            hperm_ref[s] = h_perm[:, s * LANES:(s + 1) * LANES]

    def recurrent_out(c):
        r0 = c * CHUNK
        yg = st['proj'][r0:r0 + CHUNK, D_RG:2 * D_RG]
        subs = range(r0 // sub_len, (r0 + CHUNK) // sub_len)
        h = _cat([_cat([hperm_ref[s, pl.ds(p, sub_len, stride=n_sub), :] for p in subs]) for s in range(n_slab)], 1)
        rg_out = h * _gelu_tanh(yg)
        merged_ref[r0:r0 + CHUNK, 0:D_RG] = _rms(rg_out, v512[_V_G_RG:_V_G_RG + 1, :]).astype(BF16)

    def spatial_pre(c):
        r0 = c * CHUNK
        lane = lax.broadcasted_iota(jnp.int32, (CHUNK, 2 * SG_HEAD_DIM), 1)
        v = st['proj'][r0:r0 + CHUNK, 2 * D_RG + D_SG:]
        vn = _layer_norm(v, v512[_V_LN_G:_V_LN_G + 1, :], v512[_V_LN_B:_V_LN_B + 1, :])
        rhs = []
        for p in range(SG_HEADS // 2):
            vp = vn[:, p * 2 * SG_HEAD_DIM:(p + 1) * 2 * SG_HEAD_DIM]
            rhs.append(jnp.concatenate([jnp.where(lane < SG_HEAD_DIM, vp, 0.0),
                                        jnp.where(lane >= SG_HEAD_DIM, vp, 0.0)], axis=0).astype(BF16))
        st['rhs_%d' % c] = rhs
        if r0 + CHUNK == tt:
            vrows_ref[slot] = v

    def spatial_mm(c):
        t_idx = lax.broadcasted_iota(jnp.int32, (CHUNK, 2 * CHUNK), 0)
        s_idx = lax.broadcasted_iota(jnp.int32, (CHUNK, 2 * CHUNK), 1) & (CHUNK - 1)
        outs = []
        for p, rhs in enumerate(st.pop('rhs_%d' % c)):
            ws_pair = jnp.where(s_idx <= t_idx, wsp_ref[p], jnp.zeros((), BF16))
            outs.append(_dot(ws_pair, rhs))
        st['mixed_%d' % c] = _cat(outs, 1)

    def spatial_post(c):
        r0 = c * CHUNK
        u = st['proj'][r0:r0 + CHUNK, 2 * D_RG:2 * D_RG + D_SG]
        sg_out = u * (st.pop('mixed_%d' % c) + bias_s_ref[...])
        merged_ref[r0:r0 + CHUNK, D_RG:] = _rms(sg_out, v512[_V_G_SG:_V_G_SG + 1, :]).astype(BF16)

    thunks = {'norm': norm, 'w_in': project, 'conv': conv, 'gate_mm': gate_mm, 'rec': recurrence}
    for k in range(COEF_CHUNKS):
        thunks['coef_%d' % k] = functools.partial(coefficients, k)
    for c in range(tt // CHUNK):
        thunks['sp_pre_%d' % c] = functools.partial(spatial_pre, c)
        thunks['sp_mm_%d' % c] = functools.partial(spatial_mm, c)
        thunks['sp_post_%d' % c] = functools.partial(spatial_post, c)
        thunks['rg_out_%d' % c] = functools.partial(recurrent_out, c)
    return thunks


N_PROMPT_CHUNKS = PROMPT_TILE // CHUNK
MIX_ORDER = (['norm', 'w_in', 'conv', 'gate_mm'] + ['coef_%d' % k for k in range(COEF_CHUNKS)] + ['rec'] +
             [f'{name}_{c}' for c in range(N_PROMPT_CHUNKS) for name in ('sp_pre', 'sp_mm', 'sp_post', 'rg_out')])

assert N_PROMPT_CHUNKS == 4 and N_FF_CHUNKS == 4 and COEF_CHUNKS == 8 and FF_CHUNK // PACE_BLOCK == 4
INTERLEAVED_ORDER = [
    ('a', 'norm'), ('b', 'w_out'), ('a', 'w_in'), ('b', 'hf'), ('b', 'ff1_0'), ('a', 'conv'),
    ('a', 'sp_pre_0'), ('a', 'sp_pre_1'), ('b', 'ff1_1'), ('a', 'gate_mm'), ('a', 'sp_pre_2'), ('a', 'sp_pre_3'),
    ('b', 'ff1_2'), ('a', 'coef_0'), ('a', 'coef_1'), ('a', 'coef_2'), ('a', 'coef_3'), ('a', 'sp_mm_0'), ('a', 'sp_mm_1'),
    ('b', 'ff1_3'), ('a', 'coef_4'), ('a', 'coef_5'), ('a', 'coef_6'), ('a', 'coef_7'), ('a', 'rec'),
    ('a', 'sp_mm_2'), ('a', 'sp_mm_3'), ('a', 'sp_post_0'), ('a', 'sp_post_1'), ('b', 'ff2_0'),
    ('a', 'rg_out_0'), ('a', 'rg_out_1'), ('a', 'sp_post_2'), ('a', 'sp_post_3'), ('b', 'ff2_1'),
    ('a', 'rg_out_2'), ('a', 'rg_out_3'), ('b', 'ff2_2'), ('b', 'ff2_3'), ('b', 'final'),
]
PACE_PLAN = {'coef_%d' % k: ('ff1_%d' % (2 + k // 4), k % 4) for k in range(COEF_CHUNKS)}


def _tail_weight_copies(hbm_refs, vmem_refs, sem):
    return [pltpu.make_async_copy(src, dst, sem.at[k]) for k, (src, dst) in enumerate(zip(hbm_refs, vmem_refs))]


def _prompt_kernel(x_ref, xprev_ref, ada_ref, v1024_ref, v512_ref, w_in_ref, wg_ref, wsp_ref,
                   bias_s_ref, w_out_hbm, w_ff1_hbm, w_ff2_hbm,
                   y_ref, conv_ref, lru_ref, vrows_ref,
                   xperm_ref, hperm_ref, tail_ref, h_ref, merged_ref, w_out_ref, w_ff1_ref, w_ff2_ref, w_sem,
                   *, tiles_per_seq):
    s = pl.program_id(0)
    n_tiles = pl.num_programs(0) - 1
    j = lax.rem(s, tiles_per_seq)
    seq_cur = jnp.minimum(s, n_tiles - 1) // tiles_per_seq
    seq_prev = jnp.maximum(s - 1, 0) // tiles_per_seq

    @pl.when(j == 0)
    def _():
        tail_ref[...] = jnp.zeros((SUBLANES, D_RG), F32)
        h_ref[...] = jnp.zeros((SUBLANES, D_RG), F32)

    v1024 = v1024_ref[...]
    v512 = v512_ref[...]

    def store_y(_, val):
        y_ref[...] = val

    def stage_a(pace=None):
        return _prompt_mix_thunks(x_ref, ada_ref[pl.ds(seq_cur, 1), :], v1024, v512, w_in_ref, wg_ref, wsp_ref,
                                  bias_s_ref, conv_ref, lru_ref, vrows_ref, xperm_ref, hperm_ref, tail_ref, h_ref,
                                  merged_ref, j, seq_cur, pace)

    def stage_b(pace=None):
        return _tail_thunks([xprev_ref[...]], lambda: merged_ref[...], _split_mods(ada_ref[pl.ds(seq_prev, 1), :]),
                            v1024, w_out_ref, w_ff1_ref, w_ff2_ref, store_y, pace)

    copies = _tail_weight_copies((w_out_hbm, w_ff1_hbm, w_ff2_hbm), (w_out_ref, w_ff1_ref, w_ff2_ref), w_sem)

    @pl.when(s == 0)
    def _():
        for cp in copies:
            cp.start()
        _run(stage_a(), MIX_ORDER)

    @pl.when(s == 1)
    def _():
        for cp in copies:
            cp.wait()

    @pl.when(jnp.logical_and(s > 0, s < n_tiles))
    def _():
        pace = {}
        stages = {'a': stage_a(pace), 'b': stage_b(pace)}
        assert sorted(n for k, n in INTERLEAVED_ORDER if k == 'a') == sorted(stages['a'])
        assert sorted(n for k, n in INTERLEAVED_ORDER if k == 'b') == sorted(stages['b'])
        for stage, name in INTERLEAVED_ORDER:
            stages[stage][name]()

    @pl.when(s == n_tiles)
    def _():
        _run(stage_b(), TAIL_ORDER)


def _sample_kernel(x_ref, ada_ref, cs_ref, h0_ref, v1024_ref, v512_ref, w_in_ref, wg_ref, w8_ref, b8_ref,
                   w_out_hbm, w_ff1_hbm, w_ff2_hbm,
                   y_ref, conv_ref, lru_ref, vrows_ref,
                   w_out_ref, w_ff1_ref, w_ff2_ref, w_sem):
    nb, n_t = x_ref.shape[0], x_ref.shape[1] // D_MODEL
    copies = _tail_weight_copies((w_out_hbm, w_ff1_hbm, w_ff2_hbm), (w_out_ref, w_ff1_ref, w_ff2_ref), w_sem)

    @pl.when(pl.program_id(0) == 0)
    def _():
        for cp in copies:
            cp.start()

    mods = _split_mods(ada_ref[...])
    sh_m, sc_m = mods[0], mods[1]
    v1024 = v1024_ref[...]
    v512 = v512_ref[...]

    def blk(arr, t):
        return arr[t * nb:(t + 1) * nb, :]

    x_blocks = [x_ref[:, t * D_MODEL:(t + 1) * D_MODEL] for t in range(n_t)]
    hm = _modulated_norm(x_blocks, v1024[0:1, :], sc_m, sh_m)
    proj = _dot(hm, w_in_ref[...])
    xr = proj[:, 0:D_RG]
    yg = proj[:, D_RG:2 * D_RG]
    u = proj[:, 2 * D_RG:2 * D_RG + D_SG]
    v = proj[:, 2 * D_RG + D_SG:]

    xp = [cs_ref[:, k * D_RG:(k + 1) * D_RG] for k in range(CONV_W - 1)] + [blk(xr, t) for t in range(n_t)]
    xc_blocks = []
    for t in range(n_t):
        acc = v512[_V_CONV_B:_V_CONV_B + 1, :]
        for k in range(CONV_W):
            acc = acc + xp[t + k] * v512[_V_CONV_W + k:_V_CONV_W + k + 1, :]
        xc_blocks.append(acc)
    for k in range(CONV_W - 1):
        conv_ref[:, k * D_RG:(k + 1) * D_RG] = xp[n_t + k]
    xc = _cat(xc_blocks)

    r, i_gate = _gates(_gate_preacts(xc, wg_ref), v512)
    a, mult = _lru_coeffs(r, v512)
    bterm = mult * i_gate * xc
    h = h0_ref[...]
    h_blocks = []
    for t in range(n_t):
        h = blk(a, t) * h + blk(bterm, t)
        h_blocks.append(h)
    lru_ref[...] = h
    rg_out = _cat(h_blocks) * _gelu_tanh(yg)

    vn = _layer_norm(v, v512[_V_LN_G:_V_LN_G + 1, :], v512[_V_LN_B:_V_LN_B + 1, :])
    sg_blocks = []
    for t in range(n_t):
        vrows_ref[:, t * D_SG:(t + 1) * D_SG] = blk(v, t)
        mixed = b8_ref[t:t + 1, :]
        for s in range(t + 1):
            mixed = mixed + w8_ref[t, s:s + 1, :] * blk(vn, s)
        sg_blocks.append(blk(u, t) * mixed)
    sg_out = _cat(sg_blocks)

    merged = jnp.concatenate(
        [_rms(rg_out, v512[_V_G_RG:_V_G_RG + 1, :]).astype(BF16),
         _rms(sg_out, v512[_V_G_SG:_V_G_SG + 1, :]).astype(BF16)], axis=1)

    def store_y(t, val):
        y_ref[:, t * D_MODEL:(t + 1) * D_MODEL] = val

    @pl.when(pl.program_id(0) == 0)
    def _():
        for cp in copies:
            cp.wait()

    _run(_tail_thunks(x_blocks, lambda: merged, mods, v1024, w_out_ref, w_ff1_ref, w_ff2_ref, store_y), TAIL_ORDER)


def _prep_kernel(cs_ref, cp_ref, w_ada_ref, b_ada_ref, w_in_ref, w_out_ref, w_ff1_ref, w_ff2_ref,
                 ada_ref, w_in_o, w_out_o, w_ff1_o, w_ff2_o):
    c = jnp.concatenate([cs_ref[...], cp_ref[...]], axis=0)
    s = (c * jax.nn.sigmoid(c)).astype(BF16)
    ada_ref[...] = _dot(s, w_ada_ref[...].astype(BF16)) + b_ada_ref[...]
    w_in_o[...] = w_in_ref[...].astype(BF16)
    w_out_o[...] = w_out_ref[...].astype(BF16)
    w_ff1_o[...] = w_ff1_ref[...].astype(BF16)
    w_ff2_o[...] = w_ff2_ref[...].astype(BF16)


def _resident(shape):
    nd = len(shape)
    return pl.BlockSpec(shape, lambda *_: (0,) * nd, pipeline_mode=pl.Buffered(1))


def _block_diag_groups(w):
    per = RG_HEADS // GATE_GROUPS
    w4 = w.reshape(GATE_GROUPS, per, RG_HEAD_DIM, RG_HEAD_DIM)
    bd = jnp.einsum('ghij,hk->ghikj', w4, jnp.eye(per, dtype=w.dtype))
    return bd.reshape(GATE_GROUPS, GATE_W, GATE_W)


def kernel(x_prompt, x_sample, c_prompt, c_sample, state_conv, state_rglru, w_ada, b_ada, g_mix, g_ffn, w_in, conv_w, conv_b, w_ra, b_ra, w_ri, b_ri, lru_l, ln_v_g, ln_v_b, w_s, b_s, g_rg, g_sg, w_out, w_ff1, w_ff2, g_final):
    batch, seq, _ = x_prompt.shape
    dec_batch, dec_seq, _ = x_sample.shape
    assert w_ada.shape[0] == 1, "single-layer step"
    assert seq % PROMPT_TILE == 0 and PROMPT_TILE % CHUNK == 0 and dec_batch % SAMPLE_NB == 0
    assert batch * (seq // PROMPT_TILE) >= 2, "the prompt pipeline needs at least two tiles"
    assert dec_seq <= CHUNK

    n_c = dec_batch + batch
    assert dec_batch % SUBLANES == 0 and dec_batch % batch == 0
    steps = PREP_STEPS

    def row_block(w):
        return pl.BlockSpec((w.shape[0] // steps, w.shape[1]), lambda n: (n, 0))

    mats = (w_in[0], w_out[0], w_ff1[0], w_ff2[0])
    ada, w_in_b, w_out_b, w_ff1_b, w_ff2_b = pl.pallas_call(
        _prep_kernel,
        out_shape=(jax.ShapeDtypeStruct((n_c, 6 * D_MODEL), F32),) +
                  tuple(jax.ShapeDtypeStruct(w.shape, BF16) for w in mats),
        grid=(steps,),
        in_specs=[pl.BlockSpec((dec_batch, D_MODEL), lambda n: (0, 0)),
                  pl.BlockSpec((batch, D_MODEL), lambda n: (0, 0)),
                  pl.BlockSpec((D_MODEL, 6 * D_MODEL // steps), lambda n: (0, n)),
                  pl.BlockSpec((1, 6 * D_MODEL // steps), lambda n: (0, n))] + [row_block(w) for w in mats],
        out_specs=(pl.BlockSpec((n_c, 6 * D_MODEL // steps), lambda n: (0, n)),) +
                  tuple(row_block(w) for w in mats),
        compiler_params=pltpu.CompilerParams(dimension_semantics=("arbitrary",),
                                             vmem_limit_bytes=VMEM_LIMIT_BYTES),
        name="prep",
    )(c_sample, c_prompt, w_ada[0], b_ada, *mats)

    v1024 = jnp.stack([g_mix[0], g_ffn[0], g_final], axis=0)
    v512 = jnp.concatenate(
        [conv_w[0], conv_b, b_ra[0].reshape(1, D_RG), b_ri[0].reshape(1, D_RG), lru_l,
         ln_v_g, ln_v_b, g_rg, g_sg], axis=0)
    wg = jnp.concatenate([_block_diag_groups(w_ra[0]), _block_diag_groups(w_ri[0])], axis=-1).astype(BF16)
    wsp = (w_s[0].reshape(SG_HEADS // 2, 2, CHUNK, CHUNK).transpose(0, 2, 1, 3)
           .reshape(SG_HEADS // 2, CHUNK, 2 * CHUNK).astype(BF16))
    bias_s = jnp.repeat(b_s[0].T, SG_HEAD_DIM, axis=1)
    w8 = jnp.repeat(w_s[0, :, :dec_seq, :dec_seq].transpose(1, 2, 0), SG_HEAD_DIM, axis=2)
    b8 = bias_s[:dec_seq]

    weights = (w_in_b, wg)
    weight_specs = [_resident(w_in_b.shape), _resident(wg.shape)]
    tail_weights = (w_out_b, w_ff1_b, w_ff2_b)
    cparams = functools.partial(pltpu.CompilerParams, vmem_limit_bytes=VMEM_LIMIT_BYTES)

    tt = PROMPT_TILE
    nt = seq // tt
    n_tiles = batch * nt

    def cur(s):
        return jnp.minimum(s, n_tiles - 1)

    def prev(s):
        return jnp.maximum(s - 1, 0)

    y_p, conv8_p, lru_p, vrows_p = pl.pallas_call(
        functools.partial(_prompt_kernel, tiles_per_seq=nt),
        out_shape=(jax.ShapeDtypeStruct((batch, seq, D_MODEL), F32),
                   jax.ShapeDtypeStruct((batch, SUBLANES, D_RG), F32),
                   jax.ShapeDtypeStruct((batch, 1, D_RG), F32),
                   jax.ShapeDtypeStruct((batch, CHUNK, D_SG), F32)),
        grid=(n_tiles + 1,),
        in_specs=[pl.BlockSpec((None, tt, D_MODEL), lambda s: (cur(s) // nt, cur(s) % nt, 0)),
                  pl.BlockSpec((None, tt, D_MODEL), lambda s: (prev(s) // nt, prev(s) % nt, 0)),
                  pl.BlockSpec((batch, 6 * D_MODEL), lambda s: (dec_batch // batch, 0),
                               pipeline_mode=pl.Buffered(1)),
                  _resident(v1024.shape), _resident(v512.shape)] + weight_specs +
                 [_resident(wsp.shape), _resident(bias_s.shape)] + [pl.BlockSpec(memory_space=pl.ANY)] * 3,
        out_specs=(pl.BlockSpec((None, tt, D_MODEL), lambda s: (prev(s) // nt, prev(s) % nt, 0)),
                   pl.BlockSpec((batch, SUBLANES, D_RG), lambda s: (0, 0, 0)),
                   pl.BlockSpec((batch, 1, D_RG), lambda s: (0, 0, 0)),
                   pl.BlockSpec((batch, CHUNK, D_SG), lambda s: (0, 0, 0))),
        scratch_shapes=[pltpu.VMEM((D_RG // LANES, tt, LANES), F32), pltpu.VMEM((D_RG // LANES, tt, LANES), F32),
                        pltpu.VMEM((SUBLANES, D_RG), F32), pltpu.VMEM((SUBLANES, D_RG), F32),
                        pltpu.VMEM((tt, D_MODEL), BF16)] +
                       [pltpu.VMEM(w.shape, BF16) for w in tail_weights] + [pltpu.SemaphoreType.DMA((3,))],
        compiler_params=cparams(dimension_semantics=("arbitrary",)),
        name="prompt_trunk",
    )(x_prompt, x_prompt, ada, v1024, v512, *weights, wsp, bias_s, *tail_weights)
    conv_p = conv8_p[:, SUBLANES - (CONV_W - 1):, :][None]
    lru_p = lru_p.reshape(1, batch, D_RG)
    vrows_p = vrows_p[None]

    nb = SAMPLE_NB
    n_cs = CONV_W - 1

    def seq_rows(width):
        return pl.BlockSpec((nb, width), lambda i: (i, 0))

    y_s, conv_s, lru_s, vrows_s = pl.pallas_call(
        _sample_kernel,
        out_shape=(jax.ShapeDtypeStruct((dec_batch, dec_seq * D_MODEL), F32),
                   jax.ShapeDtypeStruct((dec_batch, n_cs * D_RG), F32),
                   jax.ShapeDtypeStruct((dec_batch, D_RG), F32),
                   jax.ShapeDtypeStruct((dec_batch, dec_seq * D_SG), F32)),
        grid=(dec_batch // nb,),
        in_specs=[seq_rows(dec_seq * D_MODEL), seq_rows(6 * D_MODEL), seq_rows(n_cs * D_RG), seq_rows(D_RG),
                  _resident(v1024.shape), _resident(v512.shape)] + weight_specs +
                 [_resident(w8.shape), _resident(b8.shape)] + [pl.BlockSpec(memory_space=pl.ANY)] * 3,
        out_specs=(seq_rows(dec_seq * D_MODEL), seq_rows(n_cs * D_RG), seq_rows(D_RG), seq_rows(dec_seq * D_SG)),
        scratch_shapes=[pltpu.VMEM(w.shape, BF16) for w in tail_weights] + [pltpu.SemaphoreType.DMA((3,))],
        compiler_params=cparams(dimension_semantics=("arbitrary",)),
        name="sample_trunk",
    )(x_sample.reshape(dec_batch, dec_seq * D_MODEL), ada, state_conv[0].reshape(dec_batch, n_cs * D_RG),
      state_rglru[0], v1024, v512, *weights, w8, b8, *tail_weights)

    return (y_p, y_s.reshape(dec_batch, dec_seq, D_MODEL), conv_p, lru_p, vrows_p,
            conv_s.reshape(1, dec_batch, n_cs, D_RG), lru_s[None], vrows_s.reshape(1, dec_batch, dec_seq, D_SG))
```

```python
import functools

import jax
import jax.numpy as jnp
from jax import lax
from jax.experimental import pallas as pl
from jax.experimental.pallas import tpu as pltpu

D_MODEL = 1024
D_RG = 512
D_SG = 512
D_IN = 2 * D_RG + 2 * D_SG
D_FF = 4 * D_MODEL
RG_HEADS = 8
RG_HEAD_DIM = D_RG // RG_HEADS
SG_HEADS = 8
SG_HEAD_DIM = D_SG // SG_HEADS
CHUNK = 128
CONV_W = 4
LRU_C = 8.0
EPS = 1e-6

SUBLANES = 8
LANES = 128
GATE_GROUPS = 2
GATE_W = D_RG // GATE_GROUPS
PROMPT_TILE = 512
SAMPLE_NB = 64
FF_CHUNK = 2048
N_FF_CHUNKS = D_FF // FF_CHUNK
PREP_STEPS = 8
VMEM_LIMIT_BYTES = 60000 * 1024

F32 = jnp.float32
BF16 = jnp.bfloat16

_V_CONV_W, _V_CONV_B, _V_B_RA, _V_B_RI, _V_LRU_L, _V_LN_G, _V_LN_B, _V_G_RG, _V_G_SG = 0, 4, 5, 6, 7, 8, 9, 10, 11


def _dot(a, b):
    return jnp.dot(a, b, preferred_element_type=F32)


def _rms(x, g):
    ms = jnp.mean(x * x, axis=-1, keepdims=True)
    return (x * lax.rsqrt(ms + EPS)) * g


def _sigmoid(x):
    return 0.5 * jnp.tanh(0.5 * x) + 0.5


def _gelu_tanh(x):
    k = 0.7978845608028654
    u = x * (k + (k * 0.044715) * (x * x))
    return (0.5 * x) * (1.0 + jnp.tanh(u))


def _layer_norm(x, g, b):
    mu = jnp.mean(x, axis=-1, keepdims=True)
    xc = x - mu
    var = jnp.mean(xc * xc, axis=-1, keepdims=True)
    return (xc * lax.rsqrt(var + EPS)) * g + b


def _cat(parts, axis=0):
    return parts[0] if len(parts) == 1 else jnp.concatenate(parts, axis=axis)


def _split_mods(ada):
    return [ada[:, k * D_MODEL:(k + 1) * D_MODEL] for k in range(6)]


def _modulated_norm(x_blocks, g, sc, sh):
    gain = g * (1.0 + sc)
    return _cat([(_rms(xb, gain) + sh).astype(BF16) for xb in x_blocks])


def _gate_preacts(xc, wg_ref):
    xcb = xc.astype(BF16)
    return _cat([_dot(xcb[:, g * GATE_W:(g + 1) * GATE_W], wg_ref[g]) for g in range(GATE_GROUPS)], 1)


def _gates(pre, v512):
    r_parts, i_parts = [], []
    for g in range(GATE_GROUPS):
        lo, hi = g * GATE_W, (g + 1) * GATE_W
        ra = pre[:, 2 * lo:2 * lo + GATE_W]
        ri = pre[:, 2 * lo + GATE_W:2 * hi]
        r_parts.append(_sigmoid(ra + v512[_V_B_RA:_V_B_RA + 1, lo:hi]))
        i_parts.append(_sigmoid(ri + v512[_V_B_RI:_V_B_RI + 1, lo:hi]))
    return _cat(r_parts, 1), _cat(i_parts, 1)


def _lru_coeffs(r, v512):
    lam = v512[_V_LRU_L:_V_LRU_L + 1, :]
    log_sig = jnp.minimum(lam, 0.0) - jnp.log1p(jnp.exp(-jnp.abs(lam)))
    log_a = (LRU_C * r) * log_sig
    a = jnp.exp(log_a)
    z = 1.0 - a * a
    mult = jnp.where(z > 0.0, z * lax.rsqrt(z), 0.0)
    return a, mult


def _tail_thunks(x_blocks, merged_fn, mods, v1024, w_out_ref, w_ff1_ref, w_ff2_ref, store_y):
    _, _, gt_m, sh_f, sc_f, gt_f = mods
    rows = x_blocks[0].shape[0]
    st = {}

    def w_out():
        mm = _dot(merged_fn(), w_out_ref[...])
        st['x1'] = [xb + gt_m * mm[i * rows:(i + 1) * rows] for i, xb in enumerate(x_blocks)]

    def hf():
        st['hf'] = _modulated_norm(st['x1'], v1024[1:2, :], sc_f, sh_f)

    def ff1(c):
        pre = _dot(st['hf'], w_ff1_ref[:, c * FF_CHUNK:(c + 1) * FF_CHUNK])
        h1 = jnp.maximum(pre.astype(BF16), 0.0)
        st['h1_%d' % c] = h1 * h1

    def ff2(c):
        part = _dot(st.pop('h1_%d' % c), w_ff2_ref[c * FF_CHUNK:(c + 1) * FF_CHUNK, :])
        st['acc'] = part if c == 0 else st['acc'] + part

    def final():
        for i, xb in enumerate(st['x1']):
            x2 = xb + gt_f * st['acc'][i * rows:(i + 1) * rows]
            store_y(i, _rms(x2, v1024[2:3, :]))

    thunks = {'w_out': w_out, 'hf': hf, 'final': final}
    for c in range(N_FF_CHUNKS):
        thunks['ff1_%d' % c] = functools.partial(ff1, c)
        thunks['ff2_%d' % c] = functools.partial(ff2, c)
    return thunks


TAIL_ORDER = (['w_out', 'hf', 'ff1_0'] +
              [name for c in range(N_FF_CHUNKS)
               for name in (['ff1_%d' % (c + 1)] if c + 1 < N_FF_CHUNKS else []) + ['ff2_%d' % c]] +
              ['final'])


def _run(thunks, order):
    assert sorted(order) == sorted(thunks), (sorted(order), sorted(thunks))
    for name in order:
        thunks[name]()


def _prompt_mix_thunks(x_ref, ada, v1024, v512, w_in_ref, wg_ref, wsp_ref, bias_s_ref,
                       conv_ref, lru_ref, vrows_ref, xperm_ref, hperm_ref, tail_ref, h_ref, merged_ref, j, slot):
    tt = x_ref.shape[0]
    n_sub = SUBLANES
    sub_len = tt // n_sub
    n_slab = D_RG // LANES
    st = {}

    def norm():
        mods = _split_mods(ada)
        st['hm'] = _modulated_norm([x_ref[...]], v1024[0:1, :], mods[1], mods[0])

    def project():
        st['proj'] = proj = _dot(st['hm'], w_in_ref[...])
        xr = proj[:, 0:D_RG]
        st['xr_tail'] = xr[tt - SUBLANES:, :]
        for p in range(n_sub):
            for s in range(n_slab):
                xperm_ref[s, pl.ds(p, sub_len, stride=n_sub), :] = (
                    xr[p * sub_len:(p + 1) * sub_len, s * LANES:(s + 1) * LANES])

    def conv():
        xr_p = _cat([xperm_ref[s] for s in range(n_slab)], 1)
        sub = lax.broadcasted_iota(jnp.int32, (SUBLANES, D_RG), 0)
        prev_tail = tail_ref[...]
        head = []
        for k in range(CONV_W - 1, 0, -1):
            grp = pltpu.roll(xr_p[(sub_len - k) * SUBLANES:(sub_len - k + 1) * SUBLANES, :], 1, 0)
            head.append(jnp.where(sub == 0, prev_tail[SUBLANES - k:SUBLANES - k + 1, :], grp))
        ext = _cat(head + [xr_p])
        xc = v512[_V_CONV_B:_V_CONV_B + 1, :]
        for k in range(CONV_W):
            xc = xc + ext[k * SUBLANES:k * SUBLANES + tt, :] * v512[_V_CONV_W + k:_V_CONV_W + k + 1, :]
        st['xc'] = xc
        tail_ref[...] = st['xr_tail']
        conv_ref[slot] = st['xr_tail']

    def gate_mm():
        st['gate_pre'] = _gate_preacts(st['xc'], wg_ref)

    def coefficients():
        r, i_gate = _gates(st.pop('gate_pre'), v512)
        a, mult = _lru_coeffs(r, v512)
        row = lax.broadcasted_iota(jnp.int32, (tt, D_RG), 0)
        mult = jnp.where(row + j * tt == 0, 1.0, mult)
        st['a'] = a
        st['b'] = mult * i_gate * st.pop('xc')

    def recurrence():
        a, b = st['a'], st['b']
        sub = lax.broadcasted_iota(jnp.int32, (SUBLANES, D_RG), 0)
        hs, ps = [], []
        for q in range(sub_len):
            aq = a[q * SUBLANES:(q + 1) * SUBLANES, :]
            bq = b[q * SUBLANES:(q + 1) * SUBLANES, :]
            hs.append(bq if q == 0 else aq * hs[-1] + bq)
            ps.append(aq if q == 0 else aq * ps[-1])
        init = jnp.where(sub == 0, h_ref[SUBLANES - 1:SUBLANES, :], 0.0)
        for p in range(1, n_sub):
            end = hs[-1] + ps[-1] * init
            init = jnp.where(sub == p, pltpu.roll(end, 1, 0), init)
        end = hs[-1] + ps[-1] * init
        h_ref[...] = end
        lru_ref[slot] = end[SUBLANES - 1:SUBLANES, :]
        h_perm = _cat([hq + pq * init for hq, pq in zip(hs, ps)])
        for s in range(n_slab):
            hperm_ref[s] = h_perm[:, s * LANES:(s + 1) * LANES]

    def recurrent_out(c):
        r0 = c * CHUNK
        yg = st['proj'][r0:r0 + CHUNK, D_RG:2 * D_RG]
        subs = range(r0 // sub_len, (r0 + CHUNK) // sub_len)
        h = _cat([_cat([hperm_ref[s, pl.ds(p, sub_len, stride=n_sub), :] for p in subs]) for s in range(n_slab)], 1)
        rg_out = h * _gelu_tanh(yg)
        merged_ref[r0:r0 + CHUNK, 0:D_RG] = _rms(rg_out, v512[_V_G_RG:_V_G_RG + 1, :]).astype(BF16)

    def spatial_pre(c):
        r0 = c * CHUNK
        lane = lax.broadcasted_iota(jnp.int32, (CHUNK, 2 * SG_HEAD_DIM), 1)
        v = st['proj'][r0:r0 + CHUNK, 2 * D_RG + D_SG:]
        vn = _layer_norm(v, v512[_V_LN_G:_V_LN_G + 1, :], v512[_V_LN_B:_V_LN_B + 1, :])
        rhs = []
        for p in range(SG_HEADS // 2):
            vp = vn[:, p * 2 * SG_HEAD_DIM:(p + 1) * 2 * SG_HEAD_DIM]
            rhs.append(jnp.concatenate([jnp.where(lane < SG_HEAD_DIM, vp, 0.0),
                                        jnp.where(lane >= SG_HEAD_DIM, vp, 0.0)], axis=0).astype(BF16))
        st['rhs_%d' % c] = rhs
        if r0 + CHUNK == tt:
            vrows_ref[slot] = v

    def spatial_mm(c):
        t_idx = lax.broadcasted_iota(jnp.int32, (CHUNK, 2 * CHUNK), 0)
        s_idx = lax.broadcasted_iota(jnp.int32, (CHUNK, 2 * CHUNK), 1) & (CHUNK - 1)
        outs = []
        for p, rhs in enumerate(st.pop('rhs_%d' % c)):
            ws_pair = jnp.where(s_idx <= t_idx, wsp_ref[p], jnp.zeros((), BF16))
            outs.append(_dot(ws_pair, rhs))
        st['mixed_%d' % c] = _cat(outs, 1)

    def spatial_post(c):
        r0 = c * CHUNK
        u = st['proj'][r0:r0 + CHUNK, 2 * D_RG:2 * D_RG + D_SG]
        sg_out = u * (st.pop('mixed_%d' % c) + bias_s_ref[...])
        merged_ref[r0:r0 + CHUNK, D_RG:] = _rms(sg_out, v512[_V_G_SG:_V_G_SG + 1, :]).astype(BF16)

    thunks = {'norm': norm, 'w_in': project, 'conv': conv, 'gate_mm': gate_mm, 'coef': coefficients,
              'rec': recurrence}
    for c in range(tt // CHUNK):
        thunks['sp_pre_%d' % c] = functools.partial(spatial_pre, c)
        thunks['sp_mm_%d' % c] = functools.partial(spatial_mm, c)
        thunks['sp_post_%d' % c] = functools.partial(spatial_post, c)
        thunks['rg_out_%d' % c] = functools.partial(recurrent_out, c)
    return thunks


N_PROMPT_CHUNKS = PROMPT_TILE // CHUNK
MIX_ORDER = (['norm', 'w_in', 'conv', 'gate_mm', 'coef', 'rec'] +
             [f'{name}_{c}' for c in range(N_PROMPT_CHUNKS) for name in ('sp_pre', 'sp_mm', 'sp_post', 'rg_out')])

assert N_PROMPT_CHUNKS == 4 and N_FF_CHUNKS == 2
INTERLEAVED_ORDER = [
    ('a', 'norm'), ('b', 'w_out'), ('a', 'w_in'), ('b', 'hf'), ('b', 'ff1_0'), ('a', 'conv'),
    ('a', 'sp_pre_0'), ('a', 'sp_pre_1'), ('a', 'gate_mm'), ('a', 'sp_pre_2'), ('a', 'sp_pre_3'),
    ('b', 'ff1_1'), ('a', 'coef'), ('a', 'sp_mm_0'), ('a', 'sp_mm_1'), ('a', 'rec'),
    ('a', 'sp_mm_2'), ('a', 'sp_mm_3'), ('a', 'sp_post_0'), ('a', 'sp_post_1'), ('b', 'ff2_0'),
    ('a', 'rg_out_0'), ('a', 'rg_out_1'), ('a', 'sp_post_2'), ('a', 'sp_post_3'),
    ('a', 'rg_out_2'), ('a', 'rg_out_3'), ('b', 'ff2_1'), ('b', 'final'),
]


def _tail_weight_copies(hbm_refs, vmem_refs, sem):
    return [pltpu.make_async_copy(src, dst, sem.at[k]) for k, (src, dst) in enumerate(zip(hbm_refs, vmem_refs))]


def _prompt_kernel(x_ref, xprev_ref, ada_ref, v1024_ref, v512_ref, w_in_ref, wg_ref, wsp_ref,
                   bias_s_ref, w_out_hbm, w_ff1_hbm, w_ff2_hbm,
                   y_ref, conv_ref, lru_ref, vrows_ref,
                   xperm_ref, hperm_ref, tail_ref, h_ref, merged_ref, w_out_ref, w_ff1_ref, w_ff2_ref, w_sem,
                   *, tiles_per_seq):
    s = pl.program_id(0)
    n_tiles = pl.num_programs(0) - 1
    j = lax.rem(s, tiles_per_seq)
    seq_cur = jnp.minimum(s, n_tiles - 1) // tiles_per_seq
    seq_prev = jnp.maximum(s - 1, 0) // tiles_per_seq

    @pl.when(j == 0)
    def _():
        tail_ref[...] = jnp.zeros((SUBLANES, D_RG), F32)
        h_ref[...] = jnp.zeros((SUBLANES, D_RG), F32)

    v1024 = v1024_ref[...]
    v512 = v512_ref[...]

    def store_y(_, val):
        y_ref[...] = val

    def stage_a():
        return _prompt_mix_thunks(x_ref, ada_ref[pl.ds(seq_cur, 1), :], v1024, v512, w_in_ref, wg_ref, wsp_ref,
                                  bias_s_ref, conv_ref, lru_ref, vrows_ref, xperm_ref, hperm_ref, tail_ref, h_ref,
                                  merged_ref, j, seq_cur)

    def stage_b():
        return _tail_thunks([xprev_ref[...]], lambda: merged_ref[...], _split_mods(ada_ref[pl.ds(seq_prev, 1), :]),
                            v1024, w_out_ref, w_ff1_ref, w_ff2_ref, store_y)

    copies = _tail_weight_copies((w_out_hbm, w_ff1_hbm, w_ff2_hbm), (w_out_ref, w_ff1_ref, w_ff2_ref), w_sem)

    @pl.when(s == 0)
    def _():
        for cp in copies:
            cp.start()
        _run(stage_a(), MIX_ORDER)

    @pl.when(s == 1)
    def _():
        for cp in copies:
            cp.wait()

    @pl.when(jnp.logical_and(s > 0, s < n_tiles))
    def _():
        stages = {'a': stage_a(), 'b': stage_b()}
        assert sorted(n for k, n in INTERLEAVED_ORDER if k == 'a') == sorted(stages['a'])
        assert sorted(n for k, n in INTERLEAVED_ORDER if k == 'b') == sorted(stages['b'])
        for stage, name in INTERLEAVED_ORDER:
            stages[stage][name]()

    @pl.when(s == n_tiles)
    def _():
        _run(stage_b(), TAIL_ORDER)


def _sample_kernel(x_ref, ada_ref, cs_ref, h0_ref, v1024_ref, v512_ref, w_in_ref, wg_ref, w8_ref, b8_ref,
                   w_out_hbm, w_ff1_hbm, w_ff2_hbm,
                   y_ref, conv_ref, lru_ref, vrows_ref,
                   w_out_ref, w_ff1_ref, w_ff2_ref, w_sem):
    nb, n_t = x_ref.shape[0], x_ref.shape[1] // D_MODEL
    copies = _tail_weight_copies((w_out_hbm, w_ff1_hbm, w_ff2_hbm), (w_out_ref, w_ff1_ref, w_ff2_ref), w_sem)

    @pl.when(pl.program_id(0) == 0)
    def _():
        for cp in copies:
            cp.start()

    mods = _split_mods(ada_ref[...])
    sh_m, sc_m = mods[0], mods[1]
    v1024 = v1024_ref[...]
    v512 = v512_ref[...]

    def blk(arr, t):
        return arr[t * nb:(t + 1) * nb, :]

    x_blocks = [x_ref[:, t * D_MODEL:(t + 1) * D_MODEL] for t in range(n_t)]
    hm = _modulated_norm(x_blocks, v1024[0:1, :], sc_m, sh_m)
    proj = _dot(hm, w_in_ref[...])
    xr = proj[:, 0:D_RG]
    yg = proj[:, D_RG:2 * D_RG]
    u = proj[:, 2 * D_RG:2 * D_RG + D_SG]
    v = proj[:, 2 * D_RG + D_SG:]

    xp = [cs_ref[:, k * D_RG:(k + 1) * D_RG] for k in range(CONV_W - 1)] + [blk(xr, t) for t in range(n_t)]
    xc_blocks = []
    for t in range(n_t):
        acc = v512[_V_CONV_B:_V_CONV_B + 1, :]
        for k in range(CONV_W):
            acc = acc + xp[t + k] * v512[_V_CONV_W + k:_V_CONV_W + k + 1, :]
        xc_blocks.append(acc)
    for k in range(CONV_W - 1):
        conv_ref[:, k * D_RG:(k + 1) * D_RG] = xp[n_t + k]
    xc = _cat(xc_blocks)

    r, i_gate = _gates(_gate_preacts(xc, wg_ref), v512)
    a, mult = _lru_coeffs(r, v512)
    bterm = mult * i_gate * xc
    h = h0_ref[...]
    h_blocks = []
    for t in range(n_t):
        h = blk(a, t) * h + blk(bterm, t)
        h_blocks.append(h)
    lru_ref[...] = h
    rg_out = _cat(h_blocks) * _gelu_tanh(yg)

    vn = _layer_norm(v, v512[_V_LN_G:_V_LN_G + 1, :], v512[_V_LN_B:_V_LN_B + 1, :])
    sg_blocks = []
    for t in range(n_t):
        vrows_ref[:, t * D_SG:(t + 1) * D_SG] = blk(v, t)
        mixed = b8_ref[t:t + 1, :]
        for s in range(t + 1):
            mixed = mixed + w8_ref[t, s:s + 1, :] * blk(vn, s)
        sg_blocks.append(blk(u, t) * mixed)
    sg_out = _cat(sg_blocks)

    merged = jnp.concatenate(
        [_rms(rg_out, v512[_V_G_RG:_V_G_RG + 1, :]).astype(BF16),
         _rms(sg_out, v512[_V_G_SG:_V_G_SG + 1, :]).astype(BF16)], axis=1)

    def store_y(t, val):
        y_ref[:, t * D_MODEL:(t + 1) * D_MODEL] = val

    @pl.when(pl.program_id(0) == 0)
    def _():
        for cp in copies:
            cp.wait()

    _run(_tail_thunks(x_blocks, lambda: merged, mods, v1024, w_out_ref, w_ff1_ref, w_ff2_ref, store_y), TAIL_ORDER)


def _prep_kernel(cs_ref, cp_ref, w_ada_ref, b_ada_ref, w_in_ref, w_out_ref, w_ff1_ref, w_ff2_ref,
                 ada_ref, w_in_o, w_out_o, w_ff1_o, w_ff2_o):
    c = jnp.concatenate([cs_ref[...], cp_ref[...]], axis=0)
    s = (c * jax.nn.sigmoid(c)).astype(BF16)
    ada_ref[...] = _dot(s, w_ada_ref[...].astype(BF16)) + b_ada_ref[...]
    w_in_o[...] = w_in_ref[...].astype(BF16)
    w_out_o[...] = w_out_ref[...].astype(BF16)
    w_ff1_o[...] = w_ff1_ref[...].astype(BF16)
    w_ff2_o[...] = w_ff2_ref[...].astype(BF16)


def _resident(shape):
    nd = len(shape)
    return pl.BlockSpec(shape, lambda *_: (0,) * nd, pipeline_mode=pl.Buffered(1))


def _block_diag_groups(w):
    per = RG_HEADS // GATE_GROUPS
    w4 = w.reshape(GATE_GROUPS, per, RG_HEAD_DIM, RG_HEAD_DIM)
    bd = jnp.einsum('ghij,hk->ghikj', w4, jnp.eye(per, dtype=w.dtype))
    return bd.reshape(GATE_GROUPS, GATE_W, GATE_W)


def kernel(x_prompt, x_sample, c_prompt, c_sample, state_conv, state_rglru, w_ada, b_ada, g_mix, g_ffn, w_in, conv_w, conv_b, w_ra, b_ra, w_ri, b_ri, lru_l, ln_v_g, ln_v_b, w_s, b_s, g_rg, g_sg, w_out, w_ff1, w_ff2, g_final):
    batch, seq, _ = x_prompt.shape
    dec_batch, dec_seq, _ = x_sample.shape
    assert w_ada.shape[0] == 1, "single-layer step"
    assert seq % PROMPT_TILE == 0 and PROMPT_TILE % CHUNK == 0 and dec_batch % SAMPLE_NB == 0
    assert batch * (seq // PROMPT_TILE) >= 2, "the prompt pipeline needs at least two tiles"
    assert dec_seq <= CHUNK

    n_c = dec_batch + batch
    assert dec_batch % SUBLANES == 0 and dec_batch % batch == 0
    steps = PREP_STEPS

    def row_block(w):
        return pl.BlockSpec((w.shape[0] // steps, w.shape[1]), lambda n: (n, 0))

    mats = (w_in[0], w_out[0], w_ff1[0], w_ff2[0])
    ada, w_in_b, w_out_b, w_ff1_b, w_ff2_b = pl.pallas_call(
        _prep_kernel,
        out_shape=(jax.ShapeDtypeStruct((n_c, 6 * D_MODEL), F32),) +
                  tuple(jax.ShapeDtypeStruct(w.shape, BF16) for w in mats),
        grid=(steps,),
        in_specs=[pl.BlockSpec((dec_batch, D_MODEL), lambda n: (0, 0)),
                  pl.BlockSpec((batch, D_MODEL), lambda n: (0, 0)),
                  pl.BlockSpec((D_MODEL, 6 * D_MODEL // steps), lambda n: (0, n)),
                  pl.BlockSpec((1, 6 * D_MODEL // steps), lambda n: (0, n))] + [row_block(w) for w in mats],
        out_specs=(pl.BlockSpec((n_c, 6 * D_MODEL // steps), lambda n: (0, n)),) +
                  tuple(row_block(w) for w in mats),
        compiler_params=pltpu.CompilerParams(dimension_semantics=("arbitrary",),
                                             vmem_limit_bytes=VMEM_LIMIT_BYTES),
        name="prep",
    )(c_sample, c_prompt, w_ada[0], b_ada, *mats)

    v1024 = jnp.stack([g_mix[0], g_ffn[0], g_final], axis=0)
    v512 = jnp.concatenate(
        [conv_w[0], conv_b, b_ra[0].reshape(1, D_RG), b_ri[0].reshape(1, D_RG), lru_l,
         ln_v_g, ln_v_b, g_rg, g_sg], axis=0)
    wg = jnp.concatenate([_block_diag_groups(w_ra[0]), _block_diag_groups(w_ri[0])], axis=-1).astype(BF16)
    wsp = (w_s[0].reshape(SG_HEADS // 2, 2, CHUNK, CHUNK).transpose(0, 2, 1, 3)
           .reshape(SG_HEADS // 2, CHUNK, 2 * CHUNK).astype(BF16))
    bias_s = jnp.repeat(b_s[0].T, SG_HEAD_DIM, axis=1)
    w8 = jnp.repeat(w_s[0, :, :dec_seq, :dec_seq].transpose(1, 2, 0), SG_HEAD_DIM, axis=2)
    b8 = bias_s[:dec_seq]

    weights = (w_in_b, wg)
    weight_specs = [_resident(w_in_b.shape), _resident(wg.shape)]
    tail_weights = (w_out_b, w_ff1_b, w_ff2_b)
    cparams = functools.partial(pltpu.CompilerParams, vmem_limit_bytes=VMEM_LIMIT_BYTES)

    tt = PROMPT_TILE
    nt = seq // tt
    n_tiles = batch * nt

    def cur(s):
        return jnp.minimum(s, n_tiles - 1)

    def prev(s):
        return jnp.maximum(s - 1, 0)

    y_p, conv8_p, lru_p, vrows_p = pl.pallas_call(
        functools.partial(_prompt_kernel, tiles_per_seq=nt),
        out_shape=(jax.ShapeDtypeStruct((batch, seq, D_MODEL), F32),
                   jax.ShapeDtypeStruct((batch, SUBLANES, D_RG), F32),
                   jax.ShapeDtypeStruct((batch, 1, D_RG), F32),
                   jax.ShapeDtypeStruct((batch, CHUNK, D_SG), F32)),
        grid=(n_tiles + 1,),
        in_specs=[pl.BlockSpec((None, tt, D_MODEL), lambda s: (cur(s) // nt, cur(s) % nt, 0)),
                  pl.BlockSpec((None, tt, D_MODEL), lambda s: (prev(s) // nt, prev(s) % nt, 0)),
                  pl.BlockSpec((batch, 6 * D_MODEL), lambda s: (dec_batch // batch, 0),
                               pipeline_mode=pl.Buffered(1)),
                  _resident(v1024.shape), _resident(v512.shape)] + weight_specs +
                 [_resident(wsp.shape), _resident(bias_s.shape)] + [pl.BlockSpec(memory_space=pl.ANY)] * 3,
        out_specs=(pl.BlockSpec((None, tt, D_MODEL), lambda s: (prev(s) // nt, prev(s) % nt, 0)),
                   pl.BlockSpec((batch, SUBLANES, D_RG), lambda s: (0, 0, 0)),
                   pl.BlockSpec((batch, 1, D_RG), lambda s: (0, 0, 0)),
                   pl.BlockSpec((batch, CHUNK, D_SG), lambda s: (0, 0, 0))),
        scratch_shapes=[pltpu.VMEM((D_RG // LANES, tt, LANES), F32), pltpu.VMEM((D_RG // LANES, tt, LANES), F32),
                        pltpu.VMEM((SUBLANES, D_RG), F32), pltpu.VMEM((SUBLANES, D_RG), F32),
                        pltpu.VMEM((tt, D_MODEL), BF16)] +
                       [pltpu.VMEM(w.shape, BF16) for w in tail_weights] + [pltpu.SemaphoreType.DMA((3,))],
        compiler_params=cparams(dimension_semantics=("arbitrary",)),
        name="prompt_trunk",
    )(x_prompt, x_prompt, ada, v1024, v512, *weights, wsp, bias_s, *tail_weights)
    conv_p = conv8_p[:, SUBLANES - (CONV_W - 1):, :][None]
    lru_p = lru_p.reshape(1, batch, D_RG)
    vrows_p = vrows_p[None]

    nb = SAMPLE_NB
    n_cs = CONV_W - 1

    def seq_rows(width):
        return pl.BlockSpec((nb, width), lambda i: (i, 0))

    y_s, conv_s, lru_s, vrows_s = pl.pallas_call(
        _sample_kernel,
        out_shape=(jax.ShapeDtypeStruct((dec_batch, dec_seq * D_MODEL), F32),
                   jax.ShapeDtypeStruct((dec_batch, n_cs * D_RG), F32),
                   jax.ShapeDtypeStruct((dec_batch, D_RG), F32),
                   jax.ShapeDtypeStruct((dec_batch, dec_seq * D_SG), F32)),
        grid=(dec_batch // nb,),
        in_specs=[seq_rows(dec_seq * D_MODEL), seq_rows(6 * D_MODEL), seq_rows(n_cs * D_RG), seq_rows(D_RG),
                  _resident(v1024.shape), _resident(v512.shape)] + weight_specs +
                 [_resident(w8.shape), _resident(b8.shape)] + [pl.BlockSpec(memory_space=pl.ANY)] * 3,
        out_specs=(seq_rows(dec_seq * D_MODEL), seq_rows(n_cs * D_RG), seq_rows(D_RG), seq_rows(dec_seq * D_SG)),
        scratch_shapes=[pltpu.VMEM(w.shape, BF16) for w in tail_weights] + [pltpu.SemaphoreType.DMA((3,))],
        compiler_params=cparams(dimension_semantics=("arbitrary",)),
        name="sample_trunk",
    )(x_sample.reshape(dec_batch, dec_seq * D_MODEL), ada, state_conv[0].reshape(dec_batch, n_cs * D_RG),
      state_rglru[0], v1024, v512, *weights, w8, b8, *tail_weights)

    return (y_p, y_s.reshape(dec_batch, dec_seq, D_MODEL), conv_p, lru_p, vrows_p,
            conv_s.reshape(1, dec_batch, n_cs, D_RG), lru_s[None], vrows_s.reshape(1, dec_batch, dec_seq, D_SG))
```

```python
import functools

import jax
import jax.numpy as jnp
from jax import lax
from jax.experimental import pallas as pl
from jax.experimental.pallas import tpu as pltpu

D_MODEL = 1024
D_RG = 512
D_SG = 512
D_IN = 2 * D_RG + 2 * D_SG
D_FF = 4 * D_MODEL
RG_HEADS = 8
RG_HEAD_DIM = D_RG // RG_HEADS
SG_HEADS = 8
SG_HEAD_DIM = D_SG // SG_HEADS
CHUNK = 128
CONV_W = 4
LRU_C = 8.0
EPS = 1e-6

SUBLANES = 8
LANES = 128
GATE_GROUPS = 2
GATE_W = D_RG // GATE_GROUPS
PROMPT_TILE = 512
SAMPLE_NB = 64
FF_CHUNK = 1024
N_FF_CHUNKS = D_FF // FF_CHUNK
PREP_STEPS = 8
VMEM_LIMIT_BYTES = 60000 * 1024

F32 = jnp.float32
BF16 = jnp.bfloat16

_V_CONV_W, _V_CONV_B, _V_B_RA, _V_B_RI, _V_LRU_L, _V_LN_G, _V_LN_B, _V_G_RG, _V_G_SG = 0, 4, 5, 6, 7, 8, 9, 10, 11


def _dot(a, b):
    return jnp.dot(a, b, preferred_element_type=F32)


def _rms(x, g):
    ms = jnp.mean(x * x, axis=-1, keepdims=True)
    return (x * lax.rsqrt(ms + EPS)) * g


def _sigmoid(x):
    return 0.5 * jnp.tanh(0.5 * x) + 0.5


def _gelu_tanh(x):
    k = 0.7978845608028654
    u = x * (k + (k * 0.044715) * (x * x))
    return (0.5 * x) * (1.0 + jnp.tanh(u))


def _layer_norm(x, g, b):
    mu = jnp.mean(x, axis=-1, keepdims=True)
    xc = x - mu
    var = jnp.mean(xc * xc, axis=-1, keepdims=True)
    return (xc * lax.rsqrt(var + EPS)) * g + b


def _cat(parts, axis=0):
    return parts[0] if len(parts) == 1 else jnp.concatenate(parts, axis=axis)


def _split_mods(ada):
    return [ada[:, k * D_MODEL:(k + 1) * D_MODEL] for k in range(6)]


def _modulated_norm(x_blocks, g, sc, sh):
    gain = g * (1.0 + sc)
    return _cat([(_rms(xb, gain) + sh).astype(BF16) for xb in x_blocks])


def _gate_preacts(xc, wg_ref):
    xcb = xc.astype(BF16)
    return _cat([_dot(xcb[:, g * GATE_W:(g + 1) * GATE_W], wg_ref[g]) for g in range(GATE_GROUPS)], 1)


def _gates(pre, v512):
    r_parts, i_parts = [], []
    for g in range(GATE_GROUPS):
        lo, hi = g * GATE_W, (g + 1) * GATE_W
        ra = pre[:, 2 * lo:2 * lo + GATE_W]
        ri = pre[:, 2 * lo + GATE_W:2 * hi]
        r_parts.append(_sigmoid(ra + v512[_V_B_RA:_V_B_RA + 1, lo:hi]))
        i_parts.append(_sigmoid(ri + v512[_V_B_RI:_V_B_RI + 1, lo:hi]))
    return _cat(r_parts, 1), _cat(i_parts, 1)


def _lru_coeffs(r, v512):
    lam = v512[_V_LRU_L:_V_LRU_L + 1, :]
    log_sig = jnp.minimum(lam, 0.0) - jnp.log1p(jnp.exp(-jnp.abs(lam)))
    log_a = (LRU_C * r) * log_sig
    a = jnp.exp(log_a)
    z = 1.0 - a * a
    mult = jnp.where(z > 0.0, z * lax.rsqrt(z), 0.0)
    return a, mult


def _tail_thunks(x_blocks, merged_fn, mods, v1024, w_out_ref, w_ff1_ref, w_ff2_ref, store_y):
    _, _, gt_m, sh_f, sc_f, gt_f = mods
    rows = x_blocks[0].shape[0]
    st = {}

    def w_out():
        mm = _dot(merged_fn(), w_out_ref[...])
        st['x1'] = [xb + gt_m * mm[i * rows:(i + 1) * rows] for i, xb in enumerate(x_blocks)]

    def hf():
        st['hf'] = _modulated_norm(st['x1'], v1024[1:2, :], sc_f, sh_f)

    def ff1(c):
        pre = _dot(st['hf'], w_ff1_ref[:, c * FF_CHUNK:(c + 1) * FF_CHUNK])
        h1 = jnp.maximum(pre.astype(BF16), 0.0)
        st['h1_%d' % c] = h1 * h1

    def ff2(c):
        part = _dot(st.pop('h1_%d' % c), w_ff2_ref[c * FF_CHUNK:(c + 1) * FF_CHUNK, :])
        st['acc'] = part if c == 0 else st['acc'] + part

    def final():
        for i, xb in enumerate(st['x1']):
            x2 = xb + gt_f * st['acc'][i * rows:(i + 1) * rows]
            store_y(i, _rms(x2, v1024[2:3, :]))

    thunks = {'w_out': w_out, 'hf': hf, 'final': final}
    for c in range(N_FF_CHUNKS):
        thunks['ff1_%d' % c] = functools.partial(ff1, c)
        thunks['ff2_%d' % c] = functools.partial(ff2, c)
    return thunks


TAIL_ORDER = (['w_out', 'hf', 'ff1_0'] +
              [name for c in range(N_FF_CHUNKS)
               for name in (['ff1_%d' % (c + 1)] if c + 1 < N_FF_CHUNKS else []) + ['ff2_%d' % c]] +
              ['final'])


def _run(thunks, order):
    assert sorted(order) == sorted(thunks), (sorted(order), sorted(thunks))
    for name in order:
        thunks[name]()


def _prompt_mix_thunks(x_ref, ada, v1024, v512, w_in_ref, wg_ref, wsp_ref, bias_s_ref,
                       conv_ref, lru_ref, vrows_ref, xperm_ref, hperm_ref, tail_ref, h_ref, merged_ref, j, slot):
    tt = x_ref.shape[0]
    n_sub = SUBLANES
    sub_len = tt // n_sub
    n_slab = D_RG // LANES
    st = {}

    def norm():
        mods = _split_mods(ada)
        st['hm'] = _modulated_norm([x_ref[...]], v1024[0:1, :], mods[1], mods[0])

    def project():
        st['proj'] = proj = _dot(st['hm'], w_in_ref[...])
        xr = proj[:, 0:D_RG]
        st['xr_tail'] = xr[tt - SUBLANES:, :]
        for p in range(n_sub):
            for s in range(n_slab):
                xperm_ref[s, pl.ds(p, sub_len, stride=n_sub), :] = (
                    xr[p * sub_len:(p + 1) * sub_len, s * LANES:(s + 1) * LANES])

    def conv():
        xr_p = _cat([xperm_ref[s] for s in range(n_slab)], 1)
        sub = lax.broadcasted_iota(jnp.int32, (SUBLANES, D_RG), 0)
        prev_tail = tail_ref[...]
        head = []
        for k in range(CONV_W - 1, 0, -1):
            grp = pltpu.roll(xr_p[(sub_len - k) * SUBLANES:(sub_len - k + 1) * SUBLANES, :], 1, 0)
            head.append(jnp.where(sub == 0, prev_tail[SUBLANES - k:SUBLANES - k + 1, :], grp))
        ext = _cat(head + [xr_p])
        xc = v512[_V_CONV_B:_V_CONV_B + 1, :]
        for k in range(CONV_W):
            xc = xc + ext[k * SUBLANES:k * SUBLANES + tt, :] * v512[_V_CONV_W + k:_V_CONV_W + k + 1, :]
        st['xc'] = xc
        tail_ref[...] = st['xr_tail']
        conv_ref[slot] = st['xr_tail']

    def gate_mm():
        st['gate_pre'] = _gate_preacts(st['xc'], wg_ref)

    def coefficients():
        r, i_gate = _gates(st.pop('gate_pre'), v512)
        a, mult = _lru_coeffs(r, v512)
        row = lax.broadcasted_iota(jnp.int32, (tt, D_RG), 0)
        mult = jnp.where(row + j * tt == 0, 1.0, mult)
        st['a'] = a
        st['b'] = mult * i_gate * st.pop('xc')

    def recurrence():
        a, b = st['a'], st['b']
        sub = lax.broadcasted_iota(jnp.int32, (SUBLANES, D_RG), 0)
        hs, ps = [], []
        for q in range(sub_len):
            aq = a[q * SUBLANES:(q + 1) * SUBLANES, :]
            bq = b[q * SUBLANES:(q + 1) * SUBLANES, :]
            hs.append(bq if q == 0 else aq * hs[-1] + bq)
            ps.append(aq if q == 0 else aq * ps[-1])
        init = jnp.where(sub == 0, h_ref[SUBLANES - 1:SUBLANES, :], 0.0)
        for p in range(1, n_sub):
            end = hs[-1] + ps[-1] * init
            init = jnp.where(sub == p, pltpu.roll(end, 1, 0), init)
        end = hs[-1] + ps[-1] * init
        h_ref[...] = end
        lru_ref[slot] = end[SUBLANES - 1:SUBLANES, :]
        h_perm = _cat([hq + pq * init for hq, pq in zip(hs, ps)])
        for s in range(n_slab):
            hperm_ref[s] = h_perm[:, s * LANES:(s + 1) * LANES]

    def recurrent_out(c):
        r0 = c * CHUNK
        yg = st['proj'][r0:r0 + CHUNK, D_RG:2 * D_RG]
        subs = range(r0 // sub_len, (r0 + CHUNK) // sub_len)
        h = _cat([_cat([hperm_ref[s, pl.ds(p, sub_len, stride=n_sub), :] for p in subs]) for s in range(n_slab)], 1)
        rg_out = h * _gelu_tanh(yg)
        merged_ref[r0:r0 + CHUNK, 0:D_RG] = _rms(rg_out, v512[_V_G_RG:_V_G_RG + 1, :]).astype(BF16)

    def spatial_pre(c):
        r0 = c * CHUNK
        lane = lax.broadcasted_iota(jnp.int32, (CHUNK, 2 * SG_HEAD_DIM), 1)
        v = st['proj'][r0:r0 + CHUNK, 2 * D_RG + D_SG:]
        vn = _layer_norm(v, v512[_V_LN_G:_V_LN_G + 1, :], v512[_V_LN_B:_V_LN_B + 1, :])
        rhs = []
        for p in range(SG_HEADS // 2):
            vp = vn[:, p * 2 * SG_HEAD_DIM:(p + 1) * 2 * SG_HEAD_DIM]
            rhs.append(jnp.concatenate([jnp.where(lane < SG_HEAD_DIM, vp, 0.0),
                                        jnp.where(lane >= SG_HEAD_DIM, vp, 0.0)], axis=0).astype(BF16))
        st['rhs_%d' % c] = rhs
        if r0 + CHUNK == tt:
            vrows_ref[slot] = v

    def spatial_mm(c):
        t_idx = lax.broadcasted_iota(jnp.int32, (CHUNK, 2 * CHUNK), 0)
        s_idx = lax.broadcasted_iota(jnp.int32, (CHUNK, 2 * CHUNK), 1) & (CHUNK - 1)
        outs = []
        for p, rhs in enumerate(st.pop('rhs_%d' % c)):
            ws_pair = jnp.where(s_idx <= t_idx, wsp_ref[p], jnp.zeros((), BF16))
            outs.append(_dot(ws_pair, rhs))
        st['mixed_%d' % c] = _cat(outs, 1)

    def spatial_post(c):
        r0 = c * CHUNK
        u = st['proj'][r0:r0 + CHUNK, 2 * D_RG:2 * D_RG + D_SG]
        sg_out = u * (st.pop('mixed_%d' % c) + bias_s_ref[...])
        merged_ref[r0:r0 + CHUNK, D_RG:] = _rms(sg_out, v512[_V_G_SG:_V_G_SG + 1, :]).astype(BF16)

    thunks = {'norm': norm, 'w_in': project, 'conv': conv, 'gate_mm': gate_mm, 'coef': coefficients,
              'rec': recurrence}
    for c in range(tt // CHUNK):
        thunks['sp_pre_%d' % c] = functools.partial(spatial_pre, c)
        thunks['sp_mm_%d' % c] = functools.partial(spatial_mm, c)
        thunks['sp_post_%d' % c] = functools.partial(spatial_post, c)
        thunks['rg_out_%d' % c] = functools.partial(recurrent_out, c)
    return thunks


N_PROMPT_CHUNKS = PROMPT_TILE // CHUNK
MIX_ORDER = (['norm', 'w_in', 'conv', 'gate_mm', 'coef', 'rec'] +
             [f'{name}_{c}' for c in range(N_PROMPT_CHUNKS) for name in ('sp_pre', 'sp_mm', 'sp_post', 'rg_out')])

assert N_PROMPT_CHUNKS == 4 and N_FF_CHUNKS == 4
INTERLEAVED_ORDER = [
    ('a', 'norm'), ('b', 'w_out'), ('a', 'w_in'), ('b', 'hf'), ('b', 'ff1_0'), ('a', 'conv'),
    ('a', 'sp_pre_0'), ('a', 'sp_pre_1'), ('b', 'ff1_1'), ('a', 'gate_mm'), ('a', 'sp_pre_2'), ('a', 'sp_pre_3'),
    ('b', 'ff1_2'), ('a', 'coef'), ('a', 'sp_mm_0'), ('a', 'sp_mm_1'), ('b', 'ff1_3'), ('a', 'rec'),
    ('a', 'sp_mm_2'), ('a', 'sp_mm_3'), ('a', 'sp_post_0'), ('a', 'sp_post_1'), ('b', 'ff2_0'),
    ('a', 'rg_out_0'), ('a', 'rg_out_1'), ('a', 'sp_post_2'), ('a', 'sp_post_3'), ('b', 'ff2_1'),
    ('a', 'rg_out_2'), ('a', 'rg_out_3'), ('b', 'ff2_2'), ('b', 'ff2_3'), ('b', 'final'),
]


def _tail_weight_copies(hbm_refs, vmem_refs, sem):
    return [pltpu.make_async_copy(src, dst, sem.at[k]) for k, (src, dst) in enumerate(zip(hbm_refs, vmem_refs))]


def _prompt_kernel(x_ref, xprev_ref, ada_ref, v1024_ref, v512_ref, w_in_ref, wg_ref, wsp_ref,
                   bias_s_ref, w_out_hbm, w_ff1_hbm, w_ff2_hbm,
                   y_ref, conv_ref, lru_ref, vrows_ref,
                   xperm_ref, hperm_ref, tail_ref, h_ref, merged_ref, w_out_ref, w_ff1_ref, w_ff2_ref, w_sem,
                   *, tiles_per_seq):
    s = pl.program_id(0)
    n_tiles = pl.num_programs(0) - 1
    j = lax.rem(s, tiles_per_seq)
    seq_cur = jnp.minimum(s, n_tiles - 1) // tiles_per_seq
    seq_prev = jnp.maximum(s - 1, 0) // tiles_per_seq

    @pl.when(j == 0)
    def _():
        tail_ref[...] = jnp.zeros((SUBLANES, D_RG), F32)
        h_ref[...] = jnp.zeros((SUBLANES, D_RG), F32)

    v1024 = v1024_ref[...]
    v512 = v512_ref[...]

    def store_y(_, val):
        y_ref[...] = val

    def stage_a():
        return _prompt_mix_thunks(x_ref, ada_ref[pl.ds(seq_cur, 1), :], v1024, v512, w_in_ref, wg_ref, wsp_ref,
                                  bias_s_ref, conv_ref, lru_ref, vrows_ref, xperm_ref, hperm_ref, tail_ref, h_ref,
                                  merged_ref, j, seq_cur)

    def stage_b():
        return _tail_thunks([xprev_ref[...]], lambda: merged_ref[...], _split_mods(ada_ref[pl.ds(seq_prev, 1), :]),
                            v1024, w_out_ref, w_ff1_ref, w_ff2_ref, store_y)

    copies = _tail_weight_copies((w_out_hbm, w_ff1_hbm, w_ff2_hbm), (w_out_ref, w_ff1_ref, w_ff2_ref), w_sem)

    @pl.when(s == 0)
    def _():
        for cp in copies:
            cp.start()
        _run(stage_a(), MIX_ORDER)

    @pl.when(s == 1)
    def _():
        for cp in copies:
            cp.wait()

    @pl.when(jnp.logical_and(s > 0, s < n_tiles))
    def _():
        stages = {'a': stage_a(), 'b': stage_b()}
        assert sorted(n for k, n in INTERLEAVED_ORDER if k == 'a') == sorted(stages['a'])
        assert sorted(n for k, n in INTERLEAVED_ORDER if k == 'b') == sorted(stages['b'])
        for stage, name in INTERLEAVED_ORDER:
            stages[stage][name]()

    @pl.when(s == n_tiles)
    def _():
        _run(stage_b(), TAIL_ORDER)


def _sample_kernel(x_ref, ada_ref, cs_ref, h0_ref, v1024_ref, v512_ref, w_in_ref, wg_ref, w8_ref, b8_ref,
                   w_out_hbm, w_ff1_hbm, w_ff2_hbm,
                   y_ref, conv_ref, lru_ref, vrows_ref,
                   xslab_ref, yslab_ref, vslab_ref, w_out_ref, w_ff1_ref, w_ff2_ref, w_sem):
    n_t = SUBLANES
    nb = x_ref.shape[0] // n_t
    copies = _tail_weight_copies((w_out_hbm, w_ff1_hbm, w_ff2_hbm), (w_out_ref, w_ff1_ref, w_ff2_ref), w_sem)

    @pl.when(pl.program_id(0) == 0)
    def _():
        for cp in copies:
            cp.start()

    mods = _split_mods(ada_ref[...])
    sh_m, sc_m = mods[0], mods[1]
    v1024 = v1024_ref[...]
    v512 = v512_ref[...]

    def blk(arr, t):
        return arr[t * nb:(t + 1) * nb, :]

    for s in range(D_MODEL // LANES):
        xslab_ref[s] = x_ref[:, s * LANES:(s + 1) * LANES]
    x_blocks = [_cat([xslab_ref[s, pl.ds(t, nb, stride=n_t), :] for s in range(D_MODEL // LANES)], 1)
                for t in range(n_t)]
    hm = _modulated_norm(x_blocks, v1024[0:1, :], sc_m, sh_m)
    proj = _dot(hm, w_in_ref[...])
    xr = proj[:, 0:D_RG]
    yg = proj[:, D_RG:2 * D_RG]
    u = proj[:, 2 * D_RG:2 * D_RG + D_SG]
    v = proj[:, 2 * D_RG + D_SG:]

    xp = [cs_ref[:, k, :] for k in range(CONV_W - 1)] + [blk(xr, t) for t in range(n_t)]
    xc_blocks = []
    for t in range(n_t):
        acc = v512[_V_CONV_B:_V_CONV_B + 1, :]
        for k in range(CONV_W):
            acc = acc + xp[t + k] * v512[_V_CONV_W + k:_V_CONV_W + k + 1, :]
        xc_blocks.append(acc)
    for k in range(CONV_W - 1):
        conv_ref[:, k, :] = xp[n_t + k]
    xc = _cat(xc_blocks)

    r, i_gate = _gates(_gate_preacts(xc, wg_ref), v512)
    a, mult = _lru_coeffs(r, v512)
    bterm = mult * i_gate * xc
    h = h0_ref[...]
    h_blocks = []
    for t in range(n_t):
        h = blk(a, t) * h + blk(bterm, t)
        h_blocks.append(h)
    lru_ref[...] = h
    rg_out = _cat(h_blocks) * _gelu_tanh(yg)

    vn = _layer_norm(v, v512[_V_LN_G:_V_LN_G + 1, :], v512[_V_LN_B:_V_LN_B + 1, :])
    sg_blocks = []
    for t in range(n_t):
        for s in range(D_SG // LANES):
            vslab_ref[s, pl.ds(t, nb, stride=n_t), :] = blk(v, t)[:, s * LANES:(s + 1) * LANES]
        mixed = b8_ref[t:t + 1, :]
        for s in range(t + 1):
            mixed = mixed + w8_ref[t, s:s + 1, :] * blk(vn, s)
        sg_blocks.append(blk(u, t) * mixed)
    sg_out = _cat(sg_blocks)
    for s in range(D_SG // LANES):
        vrows_ref[:, s * LANES:(s + 1) * LANES] = vslab_ref[s]

    merged = jnp.concatenate(
        [_rms(rg_out, v512[_V_G_RG:_V_G_RG + 1, :]).astype(BF16),
         _rms(sg_out, v512[_V_G_SG:_V_G_SG + 1, :]).astype(BF16)], axis=1)

    def store_y(t, val):
        for s in range(D_MODEL // LANES):
            yslab_ref[s, pl.ds(t, nb, stride=n_t), :] = val[:, s * LANES:(s + 1) * LANES]

    @pl.when(pl.program_id(0) == 0)
    def _():
        for cp in copies:
            cp.wait()

    _run(_tail_thunks(x_blocks, lambda: merged, mods, v1024, w_out_ref, w_ff1_ref, w_ff2_ref, store_y), TAIL_ORDER)
    for s in range(D_MODEL // LANES):
        y_ref[:, s * LANES:(s + 1) * LANES] = yslab_ref[s]


def _prep_kernel(cs_ref, cp_ref, w_ada_ref, b_ada_ref, w_in_ref, w_out_ref, w_ff1_ref, w_ff2_ref,
                 ada_ref, w_in_o, w_out_o, w_ff1_o, w_ff2_o):
    c = jnp.concatenate([cs_ref[...], cp_ref[...]], axis=0)
    s = (c * jax.nn.sigmoid(c)).astype(BF16)
    ada_ref[...] = _dot(s, w_ada_ref[...].astype(BF16)) + b_ada_ref[...]
    w_in_o[...] = w_in_ref[...].astype(BF16)
    w_out_o[...] = w_out_ref[...].astype(BF16)
    w_ff1_o[...] = w_ff1_ref[...].astype(BF16)
    w_ff2_o[...] = w_ff2_ref[...].astype(BF16)


def _resident(shape):
    nd = len(shape)
    return pl.BlockSpec(shape, lambda *_: (0,) * nd, pipeline_mode=pl.Buffered(1))


def _block_diag_groups(w):
    per = RG_HEADS // GATE_GROUPS
    w4 = w.reshape(GATE_GROUPS, per, RG_HEAD_DIM, RG_HEAD_DIM)
    bd = jnp.einsum('ghij,hk->ghikj', w4, jnp.eye(per, dtype=w.dtype))
    return bd.reshape(GATE_GROUPS, GATE_W, GATE_W)


def kernel(x_prompt, x_sample, c_prompt, c_sample, state_conv, state_rglru, w_ada, b_ada, g_mix, g_ffn, w_in, conv_w, conv_b, w_ra, b_ra, w_ri, b_ri, lru_l, ln_v_g, ln_v_b, w_s, b_s, g_rg, g_sg, w_out, w_ff1, w_ff2, g_final):
    batch, seq, _ = x_prompt.shape
    dec_batch, dec_seq, _ = x_sample.shape
    assert w_ada.shape[0] == 1, "single-layer step"
    assert seq % PROMPT_TILE == 0 and PROMPT_TILE % CHUNK == 0 and dec_batch % SAMPLE_NB == 0
    assert batch * (seq // PROMPT_TILE) >= 2, "the prompt pipeline needs at least two tiles"
    assert dec_seq <= CHUNK

    n_c = dec_batch + batch
    assert dec_batch % SUBLANES == 0 and dec_batch % batch == 0
    steps = PREP_STEPS

    def row_block(w):
        return pl.BlockSpec((w.shape[0] // steps, w.shape[1]), lambda n: (n, 0))

    mats = (w_in[0], w_out[0], w_ff1[0], w_ff2[0])
    ada, w_in_b, w_out_b, w_ff1_b, w_ff2_b = pl.pallas_call(
        _prep_kernel,
        out_shape=(jax.ShapeDtypeStruct((n_c, 6 * D_MODEL), F32),) +
                  tuple(jax.ShapeDtypeStruct(w.shape, BF16) for w in mats),
        grid=(steps,),
        in_specs=[pl.BlockSpec((dec_batch, D_MODEL), lambda n: (0, 0)),
                  pl.BlockSpec((batch, D_MODEL), lambda n: (0, 0)),
                  pl.BlockSpec((D_MODEL, 6 * D_MODEL // steps), lambda n: (0, n)),
                  pl.BlockSpec((1, 6 * D_MODEL // steps), lambda n: (0, n))] + [row_block(w) for w in mats],
        out_specs=(pl.BlockSpec((n_c, 6 * D_MODEL // steps), lambda n: (0, n)),) +
                  tuple(row_block(w) for w in mats),
        compiler_params=pltpu.CompilerParams(dimension_semantics=("arbitrary",),
                                             vmem_limit_bytes=VMEM_LIMIT_BYTES),
        name="prep",
    )(c_sample, c_prompt, w_ada[0], b_ada, *mats)

    v1024 = jnp.stack([g_mix[0], g_ffn[0], g_final], axis=0)
    v512 = jnp.concatenate(
        [conv_w[0], conv_b, b_ra[0].reshape(1, D_RG), b_ri[0].reshape(1, D_RG), lru_l,
         ln_v_g, ln_v_b, g_rg, g_sg], axis=0)
    wg = jnp.concatenate([_block_diag_groups(w_ra[0]), _block_diag_groups(w_ri[0])], axis=-1).astype(BF16)
    wsp = (w_s[0].reshape(SG_HEADS // 2, 2, CHUNK, CHUNK).transpose(0, 2, 1, 3)
           .reshape(SG_HEADS // 2, CHUNK, 2 * CHUNK).astype(BF16))
    bias_s = jnp.repeat(b_s[0].T, SG_HEAD_DIM, axis=1)
    w8 = jnp.repeat(w_s[0, :, :dec_seq, :dec_seq].transpose(1, 2, 0), SG_HEAD_DIM, axis=2)
    b8 = bias_s[:dec_seq]

    weights = (w_in_b, wg)
    weight_specs = [_resident(w_in_b.shape), _resident(wg.shape)]
    tail_weights = (w_out_b, w_ff1_b, w_ff2_b)
    cparams = functools.partial(pltpu.CompilerParams, vmem_limit_bytes=VMEM_LIMIT_BYTES)

    tt = PROMPT_TILE
    nt = seq // tt
    n_tiles = batch * nt

    def cur(s):
        return jnp.minimum(s, n_tiles - 1)

    def prev(s):
        return jnp.maximum(s - 1, 0)

    y_p, conv8_p, lru_p, vrows_p = pl.pallas_call(
        functools.partial(_prompt_kernel, tiles_per_seq=nt),
        out_shape=(jax.ShapeDtypeStruct((batch, seq, D_MODEL), F32),
                   jax.ShapeDtypeStruct((batch, SUBLANES, D_RG), F32),
                   jax.ShapeDtypeStruct((batch, 1, D_RG), F32),
                   jax.ShapeDtypeStruct((batch, CHUNK, D_SG), F32)),
        grid=(n_tiles + 1,),
        in_specs=[pl.BlockSpec((None, tt, D_MODEL), lambda s: (cur(s) // nt, cur(s) % nt, 0)),
                  pl.BlockSpec((None, tt, D_MODEL), lambda s: (prev(s) // nt, prev(s) % nt, 0)),
                  pl.BlockSpec((batch, 6 * D_MODEL), lambda s: (dec_batch // batch, 0),
                               pipeline_mode=pl.Buffered(1)),
                  _resident(v1024.shape), _resident(v512.shape)] + weight_specs +
                 [_resident(wsp.shape), _resident(bias_s.shape)] + [pl.BlockSpec(memory_space=pl.ANY)] * 3,
        out_specs=(pl.BlockSpec((None, tt, D_MODEL), lambda s: (prev(s) // nt, prev(s) % nt, 0)),
                   pl.BlockSpec((batch, SUBLANES, D_RG), lambda s: (0, 0, 0)),
                   pl.BlockSpec((batch, 1, D_RG), lambda s: (0, 0, 0)),
                   pl.BlockSpec((batch, CHUNK, D_SG), lambda s: (0, 0, 0))),
        scratch_shapes=[pltpu.VMEM((D_RG // LANES, tt, LANES), F32), pltpu.VMEM((D_RG // LANES, tt, LANES), F32),
                        pltpu.VMEM((SUBLANES, D_RG), F32), pltpu.VMEM((SUBLANES, D_RG), F32),
                        pltpu.VMEM((tt, D_MODEL), BF16)] +
                       [pltpu.VMEM(w.shape, BF16) for w in tail_weights] + [pltpu.SemaphoreType.DMA((3,))],
        compiler_params=cparams(dimension_semantics=("arbitrary",)),
        name="prompt_trunk",
    )(x_prompt, x_prompt, ada, v1024, v512, *weights, wsp, bias_s, *tail_weights)
    conv_p = conv8_p[:, SUBLANES - (CONV_W - 1):, :][None]
    lru_p = lru_p.reshape(1, batch, D_RG)
    vrows_p = vrows_p[None]

    nb = SAMPLE_NB
    n_cs = CONV_W - 1
    assert dec_seq == SUBLANES, "one decode sequence per 8-row group"
    rows = nb * dec_seq

    y_s, conv_s, lru_s, vrows_s = pl.pallas_call(
        _sample_kernel,
        out_shape=(jax.ShapeDtypeStruct((dec_batch * dec_seq, D_MODEL), F32),
                   jax.ShapeDtypeStruct((dec_batch, n_cs, D_RG), F32),
                   jax.ShapeDtypeStruct((dec_batch, D_RG), F32),
                   jax.ShapeDtypeStruct((dec_batch * dec_seq, D_SG), F32)),
        grid=(dec_batch // nb,),
        in_specs=[pl.BlockSpec((rows, D_MODEL), lambda i: (i, 0)),
                  pl.BlockSpec((nb, 6 * D_MODEL), lambda i: (i, 0)),
                  pl.BlockSpec((nb, n_cs, D_RG), lambda i: (i, 0, 0)),
                  pl.BlockSpec((nb, D_RG), lambda i: (i, 0)),
                  _resident(v1024.shape), _resident(v512.shape)] + weight_specs +
                 [_resident(w8.shape), _resident(b8.shape)] + [pl.BlockSpec(memory_space=pl.ANY)] * 3,
        out_specs=(pl.BlockSpec((rows, D_MODEL), lambda i: (i, 0)),
                   pl.BlockSpec((nb, n_cs, D_RG), lambda i: (i, 0, 0)),
                   pl.BlockSpec((nb, D_RG), lambda i: (i, 0)),
                   pl.BlockSpec((rows, D_SG), lambda i: (i, 0))),
        scratch_shapes=[pltpu.VMEM((D_MODEL // LANES, rows, LANES), F32),
                        pltpu.VMEM((D_MODEL // LANES, rows, LANES), F32),
                        pltpu.VMEM((D_SG // LANES, rows, LANES), F32)] +
                       [pltpu.VMEM(w.shape, BF16) for w in tail_weights] + [pltpu.SemaphoreType.DMA((3,))],
        compiler_params=cparams(dimension_semantics=("arbitrary",)),
        name="sample_trunk",
    )(x_sample.reshape(dec_batch * dec_seq, D_MODEL), ada, state_conv[0], state_rglru[0], v1024, v512,
      *weights, w8, b8, *tail_weights)

    return (y_p, y_s.reshape(dec_batch, dec_seq, D_MODEL), conv_p, lru_p, vrows_p,
            conv_s[None], lru_s[None], vrows_s.reshape(1, dec_batch, dec_seq, D_SG))
```

```python
import functools

import jax
import jax.numpy as jnp
from jax import lax
from jax.experimental import pallas as pl
from jax.experimental.pallas import tpu as pltpu

D_MODEL = 1024
D_RG = 512
D_SG = 512
D_IN = 2 * D_RG + 2 * D_SG
D_FF = 4 * D_MODEL
RG_HEADS = 8
RG_HEAD_DIM = D_RG // RG_HEADS
SG_HEADS = 8
SG_HEAD_DIM = D_SG // SG_HEADS
CHUNK = 128
CONV_W = 4
LRU_C = 8.0
EPS = 1e-6

SUBLANES = 8
LANES = 128
GATE_GROUPS = 2
GATE_W = D_RG // GATE_GROUPS
PROMPT_TILE = 512
SAMPLE_NB = 64
FF_CHUNK = 1024
N_FF_CHUNKS = D_FF // FF_CHUNK
PREP_STEPS = 8
VMEM_LIMIT_BYTES = 60000 * 1024

F32 = jnp.float32
BF16 = jnp.bfloat16

_V_CONV_W, _V_CONV_B, _V_B_RA, _V_B_RI, _V_LRU_L, _V_LN_G, _V_LN_B, _V_G_RG, _V_G_SG = 0, 4, 5, 6, 7, 8, 9, 10, 11


def _dot(a, b):
    return jnp.dot(a, b, preferred_element_type=F32)


def _rms(x, g):
    ms = jnp.mean(x * x, axis=-1, keepdims=True)
    return (x * lax.rsqrt(ms + EPS)) * g


def _sigmoid(x):
    return 0.5 * jnp.tanh(0.5 * x) + 0.5


def _gelu_tanh(x):
    k = 0.7978845608028654
    u = x * (k + (k * 0.044715) * (x * x))
    return (0.5 * x) * (1.0 + jnp.tanh(u))


def _layer_norm(x, g, b):
    mu = jnp.mean(x, axis=-1, keepdims=True)
    xc = x - mu
    var = jnp.mean(xc * xc, axis=-1, keepdims=True)
    return (xc * lax.rsqrt(var + EPS)) * g + b


def _cat(parts, axis=0):
    return parts[0] if len(parts) == 1 else jnp.concatenate(parts, axis=axis)


def _split_mods(ada):
    return [ada[:, k * D_MODEL:(k + 1) * D_MODEL] for k in range(6)]


def _modulated_norm(x_blocks, g, sc, sh):
    gain = g * (1.0 + sc)
    return _cat([(_rms(xb, gain) + sh).astype(BF16) for xb in x_blocks])


def _gate_preacts(xc, wg_ref):
    xcb = xc.astype(BF16)
    return _cat([_dot(xcb[:, g * GATE_W:(g + 1) * GATE_W], wg_ref[g]) for g in range(GATE_GROUPS)], 1)


def _gates(pre, v512):
    r_parts, i_parts = [], []
    for g in range(GATE_GROUPS):
        lo, hi = g * GATE_W, (g + 1) * GATE_W
        ra = pre[:, 2 * lo:2 * lo + GATE_W]
        ri = pre[:, 2 * lo + GATE_W:2 * hi]
        r_parts.append(_sigmoid(ra + v512[_V_B_RA:_V_B_RA + 1, lo:hi]))
        i_parts.append(_sigmoid(ri + v512[_V_B_RI:_V_B_RI + 1, lo:hi]))
    return _cat(r_parts, 1), _cat(i_parts, 1)


def _lru_coeffs(r, v512):
    lam = v512[_V_LRU_L:_V_LRU_L + 1, :]
    log_sig = jnp.minimum(lam, 0.0) - jnp.log1p(jnp.exp(-jnp.abs(lam)))
    log_a = (LRU_C * r) * log_sig
    a = jnp.exp(log_a)
    z = 1.0 - a * a
    mult = jnp.where(z > 0.0, z * lax.rsqrt(z), 0.0)
    return a, mult


def _tail_thunks(x_blocks, merged_fn, mods, v1024, w_out_ref, w_ff1_ref, w_ff2_ref, store_y):
    _, _, gt_m, sh_f, sc_f, gt_f = mods
    rows = x_blocks[0].shape[0]
    st = {}

    def w_out():
        mm = _dot(merged_fn(), w_out_ref[...])
        st['x1'] = [xb + gt_m * mm[i * rows:(i + 1) * rows] for i, xb in enumerate(x_blocks)]

    def hf():
        st['hf'] = _modulated_norm(st['x1'], v1024[1:2, :], sc_f, sh_f)

    def ff1(c):
        pre = _dot(st['hf'], w_ff1_ref[:, c * FF_CHUNK:(c + 1) * FF_CHUNK])
        h1 = jnp.maximum(pre.astype(BF16), 0.0)
        st['h1_%d' % c] = h1 * h1

    def ff2():
        h1 = _cat([st.pop('h1_%d' % c) for c in range(N_FF_CHUNKS)], 1)
        st['acc'] = _dot(h1, w_ff2_ref[...])

    def final():
        for i, xb in enumerate(st['x1']):
            x2 = xb + gt_f * st['acc'][i * rows:(i + 1) * rows]
            store_y(i, _rms(x2, v1024[2:3, :]))

    thunks = {'w_out': w_out, 'hf': hf, 'ff2': ff2, 'final': final}
    for c in range(N_FF_CHUNKS):
        thunks['ff1_%d' % c] = functools.partial(ff1, c)
    return thunks


TAIL_ORDER = ['w_out', 'hf'] + ['ff1_%d' % c for c in range(N_FF_CHUNKS)] + ['ff2', 'final']


def _run(thunks, order):
    assert sorted(order) == sorted(thunks), (sorted(order), sorted(thunks))
    for name in order:
        thunks[name]()


def _prompt_mix_thunks(x_ref, ada, v1024, v512, w_in_ref, wg_ref, wsp_ref, bias_s_ref,
                       conv_ref, lru_ref, vrows_ref, xperm_ref, hperm_ref, tail_ref, h_ref, merged_ref, j, slot):
    tt = x_ref.shape[0]
    n_sub = SUBLANES
    sub_len = tt // n_sub
    n_slab = D_RG // LANES
    st = {}

    def norm():
        mods = _split_mods(ada)
        st['hm'] = _modulated_norm([x_ref[...]], v1024[0:1, :], mods[1], mods[0])

    def project():
        st['proj'] = proj = _dot(st['hm'], w_in_ref[...])
        xr = proj[:, 0:D_RG]
        st['xr_tail'] = xr[tt - SUBLANES:, :]
        for p in range(n_sub):
            for s in range(n_slab):
                xperm_ref[s, pl.ds(p, sub_len, stride=n_sub), :] = (
                    xr[p * sub_len:(p + 1) * sub_len, s * LANES:(s + 1) * LANES])

    def conv():
        xr_p = _cat([xperm_ref[s] for s in range(n_slab)], 1)
        sub = lax.broadcasted_iota(jnp.int32, (SUBLANES, D_RG), 0)
        prev_tail = tail_ref[...]
        head = []
        for k in range(CONV_W - 1, 0, -1):
            grp = pltpu.roll(xr_p[(sub_len - k) * SUBLANES:(sub_len - k + 1) * SUBLANES, :], 1, 0)
            head.append(jnp.where(sub == 0, prev_tail[SUBLANES - k:SUBLANES - k + 1, :], grp))
        ext = _cat(head + [xr_p])
        xc = v512[_V_CONV_B:_V_CONV_B + 1, :]
        for k in range(CONV_W):
            xc = xc + ext[k * SUBLANES:k * SUBLANES + tt, :] * v512[_V_CONV_W + k:_V_CONV_W + k + 1, :]
        st['xc'] = xc
        tail_ref[...] = st['xr_tail']
        conv_ref[slot] = st['xr_tail']

    def gate_mm():
        st['gate_pre'] = _gate_preacts(st['xc'], wg_ref)

    def coefficients():
        r, i_gate = _gates(st.pop('gate_pre'), v512)
        a, mult = _lru_coeffs(r, v512)
        row = lax.broadcasted_iota(jnp.int32, (tt, D_RG), 0)
        mult = jnp.where(row + j * tt == 0, 1.0, mult)
        st['a'] = a
        st['b'] = mult * i_gate * st.pop('xc')

    def recurrence():
        a, b = st['a'], st['b']
        sub = lax.broadcasted_iota(jnp.int32, (SUBLANES, D_RG), 0)
        hs, ps = [], []
        for q in range(sub_len):
            aq = a[q * SUBLANES:(q + 1) * SUBLANES, :]
            bq = b[q * SUBLANES:(q + 1) * SUBLANES, :]
            hs.append(bq if q == 0 else aq * hs[-1] + bq)
            ps.append(aq if q == 0 else aq * ps[-1])
        init = jnp.where(sub == 0, h_ref[SUBLANES - 1:SUBLANES, :], 0.0)
        for p in range(1, n_sub):
            end = hs[-1] + ps[-1] * init
            init = jnp.where(sub == p, pltpu.roll(end, 1, 0), init)
        end = hs[-1] + ps[-1] * init
        h_ref[...] = end
        lru_ref[slot] = end[SUBLANES - 1:SUBLANES, :]
        h_perm = _cat([hq + pq * init for hq, pq in zip(hs, ps)])
        for s in range(n_slab):
            hperm_ref[s] = h_perm[:, s * LANES:(s + 1) * LANES]

    def recurrent_out(c):
        r0 = c * CHUNK
        yg = st['proj'][r0:r0 + CHUNK, D_RG:2 * D_RG]
        subs = range(r0 // sub_len, (r0 + CHUNK) // sub_len)
        h = _cat([_cat([hperm_ref[s, pl.ds(p, sub_len, stride=n_sub), :] for p in subs]) for s in range(n_slab)], 1)
        rg_out = h * _gelu_tanh(yg)
        merged_ref[r0:r0 + CHUNK, 0:D_RG] = _rms(rg_out, v512[_V_G_RG:_V_G_RG + 1, :]).astype(BF16)

    def spatial_pre(c):
        r0 = c * CHUNK
        lane = lax.broadcasted_iota(jnp.int32, (CHUNK, 2 * SG_HEAD_DIM), 1)
        v = st['proj'][r0:r0 + CHUNK, 2 * D_RG + D_SG:]
        vn = _layer_norm(v, v512[_V_LN_G:_V_LN_G + 1, :], v512[_V_LN_B:_V_LN_B + 1, :])
        rhs = []
        for p in range(SG_HEADS // 2):
            vp = vn[:, p * 2 * SG_HEAD_DIM:(p + 1) * 2 * SG_HEAD_DIM]
            rhs.append(jnp.concatenate([jnp.where(lane < SG_HEAD_DIM, vp, 0.0),
                                        jnp.where(lane >= SG_HEAD_DIM, vp, 0.0)], axis=0).astype(BF16))
        st['rhs_%d' % c] = rhs
        if r0 + CHUNK == tt:
            vrows_ref[slot] = v

    def spatial_mm(c):
        t_idx = lax.broadcasted_iota(jnp.int32, (CHUNK, 2 * CHUNK), 0)
        s_idx = lax.broadcasted_iota(jnp.int32, (CHUNK, 2 * CHUNK), 1) & (CHUNK - 1)
        outs = []
        for p, rhs in enumerate(st.pop('rhs_%d' % c)):
            ws_pair = jnp.where(s_idx <= t_idx, wsp_ref[p], jnp.zeros((), BF16))
            outs.append(_dot(ws_pair, rhs))
        st['mixed_%d' % c] = _cat(outs, 1)

    def spatial_post(c):
        r0 = c * CHUNK
        u = st['proj'][r0:r0 + CHUNK, 2 * D_RG:2 * D_RG + D_SG]
        sg_out = u * (st.pop('mixed_%d' % c) + bias_s_ref[...])
        merged_ref[r0:r0 + CHUNK, D_RG:] = _rms(sg_out, v512[_V_G_SG:_V_G_SG + 1, :]).astype(BF16)

    thunks = {'norm': norm, 'w_in': project, 'conv': conv, 'gate_mm': gate_mm, 'coef': coefficients,
              'rec': recurrence}
    for c in range(tt // CHUNK):
        thunks['sp_pre_%d' % c] = functools.partial(spatial_pre, c)
        thunks['sp_mm_%d' % c] = functools.partial(spatial_mm, c)
        thunks['sp_post_%d' % c] = functools.partial(spatial_post, c)
        thunks['rg_out_%d' % c] = functools.partial(recurrent_out, c)
    return thunks


N_PROMPT_CHUNKS = PROMPT_TILE // CHUNK
MIX_ORDER = (['norm', 'w_in', 'conv', 'gate_mm', 'coef', 'rec'] +
             [f'{name}_{c}' for c in range(N_PROMPT_CHUNKS) for name in ('sp_pre', 'sp_mm', 'sp_post', 'rg_out')])

assert N_PROMPT_CHUNKS == 4 and N_FF_CHUNKS == 4
INTERLEAVED_ORDER = [
    ('a', 'norm'), ('b', 'w_out'), ('a', 'w_in'), ('b', 'hf'), ('b', 'ff1_0'), ('a', 'conv'),
    ('a', 'sp_pre_0'), ('a', 'sp_pre_1'), ('b', 'ff1_1'), ('a', 'gate_mm'), ('a', 'sp_pre_2'), ('a', 'sp_pre_3'),
    ('b', 'ff1_2'), ('a', 'coef'), ('a', 'sp_mm_0'), ('a', 'sp_mm_1'), ('b', 'ff1_3'), ('a', 'rec'),
    ('a', 'sp_mm_2'), ('a', 'sp_mm_3'), ('a', 'sp_post_0'), ('a', 'sp_post_1'),
    ('a', 'rg_out_0'), ('a', 'rg_out_1'), ('a', 'sp_post_2'), ('a', 'sp_post_3'),
    ('a', 'rg_out_2'), ('a', 'rg_out_3'), ('b', 'ff2'), ('b', 'final'),
]


def _tail_weight_copies(hbm_refs, vmem_refs, sem):
    return [pltpu.make_async_copy(src, dst, sem.at[k]) for k, (src, dst) in enumerate(zip(hbm_refs, vmem_refs))]


def _prompt_kernel(x_ref, xprev_ref, ada_ref, v1024_ref, v512_ref, w_in_ref, wg_ref, wsp_ref,
                   bias_s_ref, w_out_hbm, w_ff1_hbm, w_ff2_hbm,
                   y_ref, conv_ref, lru_ref, vrows_ref,
                   xperm_ref, hperm_ref, tail_ref, h_ref, merged_ref, w_out_ref, w_ff1_ref, w_ff2_ref, w_sem,
                   *, tiles_per_seq):
    s = pl.program_id(0)
    n_tiles = pl.num_programs(0) - 1
    j = lax.rem(s, tiles_per_seq)
    seq_cur = jnp.minimum(s, n_tiles - 1) // tiles_per_seq
    seq_prev = jnp.maximum(s - 1, 0) // tiles_per_seq

    @pl.when(j == 0)
    def _():
        tail_ref[...] = jnp.zeros((SUBLANES, D_RG), F32)
        h_ref[...] = jnp.zeros((SUBLANES, D_RG), F32)

    v1024 = v1024_ref[...]
    v512 = v512_ref[...]

    def store_y(_, val):
        y_ref[...] = val

    def stage_a():
        return _prompt_mix_thunks(x_ref, ada_ref[pl.ds(seq_cur, 1), :], v1024, v512, w_in_ref, wg_ref, wsp_ref,
                                  bias_s_ref, conv_ref, lru_ref, vrows_ref, xperm_ref, hperm_ref, tail_ref, h_ref,
                                  merged_ref, j, seq_cur)

    def stage_b():
        return _tail_thunks([xprev_ref[...]], lambda: merged_ref[...], _split_mods(ada_ref[pl.ds(seq_prev, 1), :]),
                            v1024, w_out_ref, w_ff1_ref, w_ff2_ref, store_y)

    copies = _tail_weight_copies((w_out_hbm, w_ff1_hbm, w_ff2_hbm), (w_out_ref, w_ff1_ref, w_ff2_ref), w_sem)

    @pl.when(s == 0)
    def _():
        for cp in copies:
            cp.start()
        _run(stage_a(), MIX_ORDER)

    @pl.when(s == 1)
    def _():
        for cp in copies:
            cp.wait()

    @pl.when(jnp.logical_and(s > 0, s < n_tiles))
    def _():
        stages = {'a': stage_a(), 'b': stage_b()}
        assert sorted(n for k, n in INTERLEAVED_ORDER if k == 'a') == sorted(stages['a'])
        assert sorted(n for k, n in INTERLEAVED_ORDER if k == 'b') == sorted(stages['b'])
        for stage, name in INTERLEAVED_ORDER:
            stages[stage][name]()

    @pl.when(s == n_tiles)
    def _():
        _run(stage_b(), TAIL_ORDER)


def _sample_kernel(x_ref, ada_ref, cs_ref, h0_ref, v1024_ref, v512_ref, w_in_ref, wg_ref, w8_ref, b8_ref,
                   w_out_hbm, w_ff1_hbm, w_ff2_hbm,
                   y_ref, conv_ref, lru_ref, vrows_ref,
                   xslab_ref, yslab_ref, vslab_ref, w_out_ref, w_ff1_ref, w_ff2_ref, w_sem):
    n_t = SUBLANES
    nb = x_ref.shape[0] // n_t
    copies = _tail_weight_copies((w_out_hbm, w_ff1_hbm, w_ff2_hbm), (w_out_ref, w_ff1_ref, w_ff2_ref), w_sem)

    @pl.when(pl.program_id(0) == 0)
    def _():
        for cp in copies:
            cp.start()

    mods = _split_mods(ada_ref[...])
    sh_m, sc_m = mods[0], mods[1]
    v1024 = v1024_ref[...]
    v512 = v512_ref[...]

    def blk(arr, t):
        return arr[t * nb:(t + 1) * nb, :]

    for s in range(D_MODEL // LANES):
        xslab_ref[s] = x_ref[:, s * LANES:(s + 1) * LANES]
    x_blocks = [_cat([xslab_ref[s, pl.ds(t, nb, stride=n_t), :] for s in range(D_MODEL // LANES)], 1)
                for t in range(n_t)]
    hm = _modulated_norm(x_blocks, v1024[0:1, :], sc_m, sh_m)
    proj = _dot(hm, w_in_ref[...])
    xr = proj[:, 0:D_RG]
    yg = proj[:, D_RG:2 * D_RG]
    u = proj[:, 2 * D_RG:2 * D_RG + D_SG]
    v = proj[:, 2 * D_RG + D_SG:]

    xp = [cs_ref[:, k, :] for k in range(CONV_W - 1)] + [blk(xr, t) for t in range(n_t)]
    xc_blocks = []
    for t in range(n_t):
        acc = v512[_V_CONV_B:_V_CONV_B + 1, :]
        for k in range(CONV_W):
            acc = acc + xp[t + k] * v512[_V_CONV_W + k:_V_CONV_W + k + 1, :]
        xc_blocks.append(acc)
    for k in range(CONV_W - 1):
        conv_ref[:, k, :] = xp[n_t + k]
    xc = _cat(xc_blocks)

    r, i_gate = _gates(_gate_preacts(xc, wg_ref), v512)
    a, mult = _lru_coeffs(r, v512)
    bterm = mult * i_gate * xc
    h = h0_ref[...]
    h_blocks = []
    for t in range(n_t):
        h = blk(a, t) * h + blk(bterm, t)
        h_blocks.append(h)
    lru_ref[...] = h
    rg_out = _cat(h_blocks) * _gelu_tanh(yg)

    vn = _layer_norm(v, v512[_V_LN_G:_V_LN_G + 1, :], v512[_V_LN_B:_V_LN_B + 1, :])
    sg_blocks = []
    for t in range(n_t):
        for s in range(D_SG // LANES):
            vslab_ref[s, pl.ds(t, nb, stride=n_t), :] = blk(v, t)[:, s * LANES:(s + 1) * LANES]
        mixed = b8_ref[t:t + 1, :]
        for s in range(t + 1):
            mixed = mixed + w8_ref[t, s:s + 1, :] * blk(vn, s)
        sg_blocks.append(blk(u, t) * mixed)
    sg_out = _cat(sg_blocks)
    for s in range(D_SG // LANES):
        vrows_ref[:, s * LANES:(s + 1) * LANES] = vslab_ref[s]

    merged = jnp.concatenate(
        [_rms(rg_out, v512[_V_G_RG:_V_G_RG + 1, :]).astype(BF16),
         _rms(sg_out, v512[_V_G_SG:_V_G_SG + 1, :]).astype(BF16)], axis=1)

    def store_y(t, val):
        for s in range(D_MODEL // LANES):
            yslab_ref[s, pl.ds(t, nb, stride=n_t), :] = val[:, s * LANES:(s + 1) * LANES]

    @pl.when(pl.program_id(0) == 0)
    def _():
        for cp in copies:
            cp.wait()

    _run(_tail_thunks(x_blocks, lambda: merged, mods, v1024, w_out_ref, w_ff1_ref, w_ff2_ref, store_y), TAIL_ORDER)
    for s in range(D_MODEL // LANES):
        y_ref[:, s * LANES:(s + 1) * LANES] = yslab_ref[s]


def _prep_kernel(cs_ref, cp_ref, w_ada_ref, b_ada_ref, w_in_ref, w_out_ref, w_ff1_ref, w_ff2_ref,
                 ada_ref, w_in_o, w_out_o, w_ff1_o, w_ff2_o):
    c = jnp.concatenate([cs_ref[...], cp_ref[...]], axis=0)
    s = (c * jax.nn.sigmoid(c)).astype(BF16)
    ada_ref[...] = _dot(s, w_ada_ref[...].astype(BF16)) + b_ada_ref[...]
    w_in_o[...] = w_in_ref[...].astype(BF16)
    w_out_o[...] = w_out_ref[...].astype(BF16)
    w_ff1_o[...] = w_ff1_ref[...].astype(BF16)
    w_ff2_o[...] = w_ff2_ref[...].astype(BF16)


def _resident(shape):
    nd = len(shape)
    return pl.BlockSpec(shape, lambda *_: (0,) * nd, pipeline_mode=pl.Buffered(1))


def _block_diag_groups(w):
    per = RG_HEADS // GATE_GROUPS
    w4 = w.reshape(GATE_GROUPS, per, RG_HEAD_DIM, RG_HEAD_DIM)
    bd = jnp.einsum('ghij,hk->ghikj', w4, jnp.eye(per, dtype=w.dtype))
    return bd.reshape(GATE_GROUPS, GATE_W, GATE_W)


def kernel(x_prompt, x_sample, c_prompt, c_sample, state_conv, state_rglru, w_ada, b_ada, g_mix, g_ffn, w_in, conv_w, conv_b, w_ra, b_ra, w_ri, b_ri, lru_l, ln_v_g, ln_v_b, w_s, b_s, g_rg, g_sg, w_out, w_ff1, w_ff2, g_final):
    batch, seq, _ = x_prompt.shape
    dec_batch, dec_seq, _ = x_sample.shape
    assert w_ada.shape[0] == 1, "single-layer step"
    assert seq % PROMPT_TILE == 0 and PROMPT_TILE % CHUNK == 0 and dec_batch % SAMPLE_NB == 0
    assert batch * (seq // PROMPT_TILE) >= 2, "the prompt pipeline needs at least two tiles"
    assert dec_seq <= CHUNK

    n_c = dec_batch + batch
    assert dec_batch % SUBLANES == 0 and dec_batch % batch == 0
    steps = PREP_STEPS

    def row_block(w):
        return pl.BlockSpec((w.shape[0] // steps, w.shape[1]), lambda n: (n, 0))

    mats = (w_in[0], w_out[0], w_ff1[0], w_ff2[0])
    ada, w_in_b, w_out_b, w_ff1_b, w_ff2_b = pl.pallas_call(
        _prep_kernel,
        out_shape=(jax.ShapeDtypeStruct((n_c, 6 * D_MODEL), F32),) +
                  tuple(jax.ShapeDtypeStruct(w.shape, BF16) for w in mats),
        grid=(steps,),
        in_specs=[pl.BlockSpec((dec_batch, D_MODEL), lambda n: (0, 0)),
                  pl.BlockSpec((batch, D_MODEL), lambda n: (0, 0)),
                  pl.BlockSpec((D_MODEL, 6 * D_MODEL // steps), lambda n: (0, n)),
                  pl.BlockSpec((1, 6 * D_MODEL // steps), lambda n: (0, n))] + [row_block(w) for w in mats],
        out_specs=(pl.BlockSpec((n_c, 6 * D_MODEL // steps), lambda n: (0, n)),) +
                  tuple(row_block(w) for w in mats),
        compiler_params=pltpu.CompilerParams(dimension_semantics=("arbitrary",),
                                             vmem_limit_bytes=VMEM_LIMIT_BYTES),
        name="prep",
    )(c_sample, c_prompt, w_ada[0], b_ada, *mats)

    v1024 = jnp.stack([g_mix[0], g_ffn[0], g_final], axis=0)
    v512 = jnp.concatenate(
        [conv_w[0], conv_b, b_ra[0].reshape(1, D_RG), b_ri[0].reshape(1, D_RG), lru_l,
         ln_v_g, ln_v_b, g_rg, g_sg], axis=0)
    wg = jnp.concatenate([_block_diag_groups(w_ra[0]), _block_diag_groups(w_ri[0])], axis=-1).astype(BF16)
    wsp = (w_s[0].reshape(SG_HEADS // 2, 2, CHUNK, CHUNK).transpose(0, 2, 1, 3)
           .reshape(SG_HEADS // 2, CHUNK, 2 * CHUNK).astype(BF16))
    bias_s = jnp.repeat(b_s[0].T, SG_HEAD_DIM, axis=1)
    w8 = jnp.repeat(w_s[0, :, :dec_seq, :dec_seq].transpose(1, 2, 0), SG_HEAD_DIM, axis=2)
    b8 = bias_s[:dec_seq]

    weights = (w_in_b, wg)
    weight_specs = [_resident(w_in_b.shape), _resident(wg.shape)]
    tail_weights = (w_out_b, w_ff1_b, w_ff2_b)
    cparams = functools.partial(pltpu.CompilerParams, vmem_limit_bytes=VMEM_LIMIT_BYTES)

    tt = PROMPT_TILE
    nt = seq // tt
    n_tiles = batch * nt

    def cur(s):
        return jnp.minimum(s, n_tiles - 1)

    def prev(s):
        return jnp.maximum(s - 1, 0)

    y_p, conv8_p, lru_p, vrows_p = pl.pallas_call(
        functools.partial(_prompt_kernel, tiles_per_seq=nt),
        out_shape=(jax.ShapeDtypeStruct((batch, seq, D_MODEL), F32),
                   jax.ShapeDtypeStruct((batch, SUBLANES, D_RG), F32),
                   jax.ShapeDtypeStruct((batch, 1, D_RG), F32),
                   jax.ShapeDtypeStruct((batch, CHUNK, D_SG), F32)),
        grid=(n_tiles + 1,),
        in_specs=[pl.BlockSpec((None, tt, D_MODEL), lambda s: (cur(s) // nt, cur(s) % nt, 0)),
                  pl.BlockSpec((None, tt, D_MODEL), lambda s: (prev(s) // nt, prev(s) % nt, 0)),
                  pl.BlockSpec((batch, 6 * D_MODEL), lambda s: (dec_batch // batch, 0),
                               pipeline_mode=pl.Buffered(1)),
                  _resident(v1024.shape), _resident(v512.shape)] + weight_specs +
                 [_resident(wsp.shape), _resident(bias_s.shape)] + [pl.BlockSpec(memory_space=pl.ANY)] * 3,
        out_specs=(pl.BlockSpec((None, tt, D_MODEL), lambda s: (prev(s) // nt, prev(s) % nt, 0)),
                   pl.BlockSpec((batch, SUBLANES, D_RG), lambda s: (0, 0, 0)),
                   pl.BlockSpec((batch, 1, D_RG), lambda s: (0, 0, 0)),
                   pl.BlockSpec((batch, CHUNK, D_SG), lambda s: (0, 0, 0))),
        scratch_shapes=[pltpu.VMEM((D_RG // LANES, tt, LANES), F32), pltpu.VMEM((D_RG // LANES, tt, LANES), F32),
                        pltpu.VMEM((SUBLANES, D_RG), F32), pltpu.VMEM((SUBLANES, D_RG), F32),
                        pltpu.VMEM((tt, D_MODEL), BF16)] +
                       [pltpu.VMEM(w.shape, BF16) for w in tail_weights] + [pltpu.SemaphoreType.DMA((3,))],
        compiler_params=cparams(dimension_semantics=("arbitrary",)),
        name="prompt_trunk",
    )(x_prompt, x_prompt, ada, v1024, v512, *weights, wsp, bias_s, *tail_weights)
    conv_p = conv8_p[:, SUBLANES - (CONV_W - 1):, :][None]
    lru_p = lru_p.reshape(1, batch, D_RG)
    vrows_p = vrows_p[None]

    nb = SAMPLE_NB
    n_cs = CONV_W - 1
    assert dec_seq == SUBLANES, "one decode sequence per 8-row group"
    rows = nb * dec_seq

    y_s, conv_s, lru_s, vrows_s = pl.pallas_call(
        _sample_kernel,
        out_shape=(jax.ShapeDtypeStruct((dec_batch * dec_seq, D_MODEL), F32),
                   jax.ShapeDtypeStruct((dec_batch, n_cs, D_RG), F32),
                   jax.ShapeDtypeStruct((dec_batch, D_RG), F32),
                   jax.ShapeDtypeStruct((dec_batch * dec_seq, D_SG), F32)),
        grid=(dec_batch // nb,),
        in_specs=[pl.BlockSpec((rows, D_MODEL), lambda i: (i, 0)),
                  pl.BlockSpec((nb, 6 * D_MODEL), lambda i: (i, 0)),
                  pl.BlockSpec((nb, n_cs, D_RG), lambda i: (i, 0, 0)),
                  pl.BlockSpec((nb, D_RG), lambda i: (i, 0)),
                  _resident(v1024.shape), _resident(v512.shape)] + weight_specs +
                 [_resident(w8.shape), _resident(b8.shape)] + [pl.BlockSpec(memory_space=pl.ANY)] * 3,
        out_specs=(pl.BlockSpec((rows, D_MODEL), lambda i: (i, 0)),
                   pl.BlockSpec((nb, n_cs, D_RG), lambda i: (i, 0, 0)),
                   pl.BlockSpec((nb, D_RG), lambda i: (i, 0)),
                   pl.BlockSpec((rows, D_SG), lambda i: (i, 0))),
        scratch_shapes=[pltpu.VMEM((D_MODEL // LANES, rows, LANES), F32),
                        pltpu.VMEM((D_MODEL // LANES, rows, LANES), F32),
                        pltpu.VMEM((D_SG // LANES, rows, LANES), F32)] +
                       [pltpu.VMEM(w.shape, BF16) for w in tail_weights] + [pltpu.SemaphoreType.DMA((3,))],
        compiler_params=cparams(dimension_semantics=("arbitrary",)),
        name="sample_trunk",
    )(x_sample.reshape(dec_batch * dec_seq, D_MODEL), ada, state_conv[0], state_rglru[0], v1024, v512,
      *weights, w8, b8, *tail_weights)

    return (y_p, y_s.reshape(dec_batch, dec_seq, D_MODEL), conv_p, lru_p, vrows_p,
            conv_s[None], lru_s[None], vrows_s.reshape(1, dec_batch, dec_seq, D_SG))
```

```python
import functools

import jax
import jax.numpy as jnp
from jax import lax
from jax.experimental import pallas as pl
from jax.experimental.pallas import tpu as pltpu

D_MODEL = 1024
D_RG = 512
D_SG = 512
D_IN = 2 * D_RG + 2 * D_SG
D_FF = 4 * D_MODEL
RG_HEADS = 8
RG_HEAD_DIM = D_RG // RG_HEADS
SG_HEADS = 8
SG_HEAD_DIM = D_SG // SG_HEADS
CHUNK = 128
CONV_W = 4
LRU_C = 8.0
EPS = 1e-6

SUBLANES = 8
LANES = 128
GATE_GROUPS = 2
GATE_W = D_RG // GATE_GROUPS
PROMPT_TILE = 512
SAMPLE_NB = 64
FF_CHUNK = 1024
N_FF_CHUNKS = D_FF // FF_CHUNK
PREP_STEPS = 4
VMEM_LIMIT_BYTES = 60000 * 1024

F32 = jnp.float32
BF16 = jnp.bfloat16

_V_CONV_W, _V_CONV_B, _V_B_RA, _V_B_RI, _V_LRU_L, _V_LN_G, _V_LN_B, _V_G_RG, _V_G_SG = 0, 4, 5, 6, 7, 8, 9, 10, 11


def _dot(a, b):
    return jnp.dot(a, b, preferred_element_type=F32)


def _rms(x, g):
    ms = jnp.mean(x * x, axis=-1, keepdims=True)
    return (x * lax.rsqrt(ms + EPS)) * g


def _gelu_tanh(x):
    k = 0.7978845608028654
    u = x * (k + (k * 0.044715) * (x * x))
    return (0.5 * x) * (1.0 + jnp.tanh(u))


def _layer_norm(x, g, b):
    mu = jnp.mean(x, axis=-1, keepdims=True)
    xc = x - mu
    var = jnp.mean(xc * xc, axis=-1, keepdims=True)
    return (xc * lax.rsqrt(var + EPS)) * g + b


def _cat(parts, axis=0):
    return parts[0] if len(parts) == 1 else jnp.concatenate(parts, axis=axis)


def _split_mods(ada):
    return [ada[:, k * D_MODEL:(k + 1) * D_MODEL] for k in range(6)]


def _modulated_norm(x_blocks, g, sc, sh):
    gain = g * (1.0 + sc)
    return _cat([(_rms(xb, gain) + sh).astype(BF16) for xb in x_blocks])


def _gate_preacts(xc, wg_ref):
    xcb = xc.astype(BF16)
    return _cat([_dot(xcb[:, g * GATE_W:(g + 1) * GATE_W], wg_ref[g]) for g in range(GATE_GROUPS)], 1)


def _gate_tanh(pre_half, v512):
    r_parts, i_parts = [], []
    for g in range(GATE_GROUPS):
        lo, hi = g * GATE_W, (g + 1) * GATE_W
        ra = pre_half[:, 2 * lo:2 * lo + GATE_W]
        ri = pre_half[:, 2 * lo + GATE_W:2 * hi]
        r_parts.append(jnp.tanh(ra + 0.5 * v512[_V_B_RA:_V_B_RA + 1, lo:hi]))
        i_parts.append(jnp.tanh(ri + 0.5 * v512[_V_B_RI:_V_B_RI + 1, lo:hi]))
    return _cat(r_parts, 1), _cat(i_parts, 1)


def _lru_coeffs(t_r, v512):
    lam = v512[_V_LRU_L:_V_LRU_L + 1, :]
    log_sig = jnp.minimum(lam, 0.0) - jnp.log1p(jnp.exp(-jnp.abs(lam)))
    k = (0.5 * LRU_C) * log_sig
    log_a = k * t_r + k
    a = jnp.exp(log_a)
    z = 1.0 - a * a
    mult = jnp.where(z > 0.0, z * lax.rsqrt(z), 0.0)
    return a, mult


def _tail_thunks(x_blocks, merged_fn, mods, v1024, w_out_ref, w_ff1_ref, w_ff2_ref, store_y):
    _, _, gt_m, sh_f, sc_f, gt_f = mods
    rows = x_blocks[0].shape[0]
    st = {}

    def w_out():
        mm = _dot(merged_fn(), w_out_ref[...])
        st['x1'] = [xb + gt_m * mm[i * rows:(i + 1) * rows] for i, xb in enumerate(x_blocks)]

    def hf():
        st['hf'] = _modulated_norm(st['x1'], v1024[1:2, :], sc_f, sh_f)

    def ff1(c):
        pre = _dot(st['hf'], w_ff1_ref[:, c * FF_CHUNK:(c + 1) * FF_CHUNK])
        h1 = jnp.maximum(pre.astype(BF16), 0.0)
        st['h1_%d' % c] = h1 * h1

    def ff2():
        h1 = _cat([st.pop('h1_%d' % c) for c in range(N_FF_CHUNKS)], 1)
        st['acc'] = _dot(h1, w_ff2_ref[...])

    def final():
        for i, xb in enumerate(st['x1']):
            x2 = xb + gt_f * st['acc'][i * rows:(i + 1) * rows]
            store_y(i, _rms(x2, v1024[2:3, :]))

    thunks = {'w_out': w_out, 'hf': hf, 'ff2': ff2, 'final': final}
    for c in range(N_FF_CHUNKS):
        thunks['ff1_%d' % c] = functools.partial(ff1, c)
    return thunks


TAIL_ORDER = ['w_out', 'hf'] + ['ff1_%d' % c for c in range(N_FF_CHUNKS)] + ['ff2', 'final']


def _run(thunks, order):
    assert sorted(order) == sorted(thunks), (sorted(order), sorted(thunks))
    for name in order:
        thunks[name]()


def _prompt_mix_thunks(x_ref, ada, v1024, v512, w_in_ref, wg_ref, wsp_ref, bias_s_ref,
                       conv_ref, lru_ref, vrows_ref, xperm_ref, hperm_ref, tail_ref, h_ref, merged_ref, j, slot):
    tt = x_ref.shape[0]
    n_sub = SUBLANES
    sub_len = tt // n_sub
    n_slab = D_RG // LANES
    st = {}

    def norm():
        mods = _split_mods(ada)
        st['hm'] = _modulated_norm([x_ref[...]], v1024[0:1, :], mods[1], mods[0])

    def project():
        st['proj'] = proj = _dot(st['hm'], w_in_ref[...])
        xr = proj[:, 0:D_RG]
        st['xr_tail'] = xr[tt - SUBLANES:, :]
        for p in range(n_sub):
            for s in range(n_slab):
                xperm_ref[s, pl.ds(p, sub_len, stride=n_sub), :] = (
                    xr[p * sub_len:(p + 1) * sub_len, s * LANES:(s + 1) * LANES])

    def conv():
        xr_p = _cat([xperm_ref[s] for s in range(n_slab)], 1)
        sub = lax.broadcasted_iota(jnp.int32, (SUBLANES, D_RG), 0)
        prev_tail = tail_ref[...]
        head = []
        for k in range(CONV_W - 1, 0, -1):
            grp = pltpu.roll(xr_p[(sub_len - k) * SUBLANES:(sub_len - k + 1) * SUBLANES, :], 1, 0)
            head.append(jnp.where(sub == 0, prev_tail[SUBLANES - k:SUBLANES - k + 1, :], grp))
        ext = _cat(head + [xr_p])
        xc = v512[_V_CONV_B:_V_CONV_B + 1, :]
        for k in range(CONV_W):
            xc = xc + ext[k * SUBLANES:k * SUBLANES + tt, :] * v512[_V_CONV_W + k:_V_CONV_W + k + 1, :]
        st['xc'] = xc
        tail_ref[...] = st['xr_tail']
        conv_ref[slot] = st['xr_tail'][SUBLANES - (CONV_W - 1):, :]

    def gate_mm():
        st['gate_pre'] = _gate_preacts(st['xc'], wg_ref)

    def coefficients():
        t_r, t_i = _gate_tanh(st.pop('gate_pre'), v512)
        a, mult = _lru_coeffs(t_r, v512)
        row = lax.broadcasted_iota(jnp.int32, (tt, D_RG), 0)
        mult = jnp.where(row + j * tt == 0, 1.0, mult)
        st['a'] = a
        st['b'] = mult * (0.5 * t_i + 0.5) * st.pop('xc')

    def recurrence():
        a, b = st['a'], st['b']
        sub = lax.broadcasted_iota(jnp.int32, (SUBLANES, D_RG), 0)
        hs, ps = [], []
        for q in range(sub_len):
            aq = a[q * SUBLANES:(q + 1) * SUBLANES, :]
            bq = b[q * SUBLANES:(q + 1) * SUBLANES, :]
            hs.append(bq if q == 0 else aq * hs[-1] + bq)
            ps.append(aq if q == 0 else aq * ps[-1])
        init = jnp.where(sub == 0, h_ref[SUBLANES - 1:SUBLANES, :], 0.0)
        for p in range(1, n_sub):
            end = hs[-1] + ps[-1] * init
            init = jnp.where(sub == p, pltpu.roll(end, 1, 0), init)
        end = hs[-1] + ps[-1] * init
        h_ref[...] = end
        lru_ref[pl.ds(slot, 1), :] = end[SUBLANES - 1:SUBLANES, :]
        h_perm = _cat([hq + pq * init for hq, pq in zip(hs, ps)])
        for s in range(n_slab):
            hperm_ref[s] = h_perm[:, s * LANES:(s + 1) * LANES]

    def recurrent_out(c):
        r0 = c * CHUNK
        yg = st['proj'][r0:r0 + CHUNK, D_RG:2 * D_RG]
        subs = range(r0 // sub_len, (r0 + CHUNK) // sub_len)
        h = _cat([_cat([hperm_ref[s, pl.ds(p, sub_len, stride=n_sub), :] for p in subs]) for s in range(n_slab)], 1)
        rg_out = h * _gelu_tanh(yg)
        merged_ref[r0:r0 + CHUNK, 0:D_RG] = _rms(rg_out, v512[_V_G_RG:_V_G_RG + 1, :]).astype(BF16)

    def spatial_pre(c):
        r0 = c * CHUNK
        lane = lax.broadcasted_iota(jnp.int32, (CHUNK, 2 * SG_HEAD_DIM), 1)
        v = st['proj'][r0:r0 + CHUNK, 2 * D_RG + D_SG:]
        vn = _layer_norm(v, v512[_V_LN_G:_V_LN_G + 1, :], v512[_V_LN_B:_V_LN_B + 1, :])
        rhs = []
        for p in range(SG_HEADS // 2):
            vp = vn[:, p * 2 * SG_HEAD_DIM:(p + 1) * 2 * SG_HEAD_DIM]
            rhs.append(jnp.concatenate([jnp.where(lane < SG_HEAD_DIM, vp, 0.0),
                                        jnp.where(lane >= SG_HEAD_DIM, vp, 0.0)], axis=0).astype(BF16))
        st['rhs_%d' % c] = rhs
        if r0 + CHUNK == tt:
            vrows_ref[slot] = v

    def spatial_mm(c):
        t_idx = lax.broadcasted_iota(jnp.int32, (CHUNK, 2 * CHUNK), 0)
        s_idx = lax.broadcasted_iota(jnp.int32, (CHUNK, 2 * CHUNK), 1) & (CHUNK - 1)
        outs = []
        for p, rhs in enumerate(st.pop('rhs_%d' % c)):
            ws_pair = jnp.where(s_idx <= t_idx, wsp_ref[p], jnp.zeros((), BF16))
            outs.append(_dot(ws_pair, rhs))
        st['mixed_%d' % c] = _cat(outs, 1)

    def spatial_post(c):
        r0 = c * CHUNK
        u = st['proj'][r0:r0 + CHUNK, 2 * D_RG:2 * D_RG + D_SG]
        sg_out = u * (st.pop('mixed_%d' % c) + bias_s_ref[...])
        merged_ref[r0:r0 + CHUNK, D_RG:] = _rms(sg_out, v512[_V_G_SG:_V_G_SG + 1, :]).astype(BF16)

    thunks = {'norm': norm, 'w_in': project, 'conv': conv, 'gate_mm': gate_mm, 'coef': coefficients,
              'rec': recurrence}
    for c in range(tt // CHUNK):
        thunks['sp_pre_%d' % c] = functools.partial(spatial_pre, c)
        thunks['sp_mm_%d' % c] = functools.partial(spatial_mm, c)
        thunks['sp_post_%d' % c] = functools.partial(spatial_post, c)
        thunks['rg_out_%d' % c] = functools.partial(recurrent_out, c)
    return thunks


N_PROMPT_CHUNKS = PROMPT_TILE // CHUNK
MIX_ORDER = (['norm', 'w_in', 'conv', 'gate_mm', 'coef', 'rec'] +
             [f'{name}_{c}' for c in range(N_PROMPT_CHUNKS) for name in ('sp_pre', 'sp_mm', 'sp_post', 'rg_out')])

assert N_PROMPT_CHUNKS == 4 and N_FF_CHUNKS == 4
INTERLEAVED_ORDER = [
    ('a', 'norm'), ('b', 'w_out'), ('a', 'w_in'), ('b', 'hf'), ('b', 'ff1_0'), ('a', 'conv'),
    ('a', 'sp_pre_0'), ('a', 'sp_pre_1'), ('b', 'ff1_1'), ('a', 'gate_mm'), ('a', 'sp_pre_2'), ('a', 'sp_pre_3'),
    ('b', 'ff1_2'), ('a', 'coef'), ('a', 'sp_mm_0'), ('a', 'sp_mm_1'), ('b', 'ff1_3'), ('a', 'rec'),
    ('a', 'sp_mm_2'), ('a', 'sp_mm_3'), ('a', 'sp_post_0'), ('a', 'sp_post_1'),
    ('a', 'rg_out_0'), ('a', 'rg_out_1'), ('a', 'sp_post_2'), ('a', 'sp_post_3'),
    ('a', 'rg_out_2'), ('a', 'rg_out_3'), ('b', 'ff2'), ('b', 'final'),
]


def _tail_weight_copies(hbm_refs, vmem_refs, sem):
    return [pltpu.make_async_copy(src, dst, sem.at[k]) for k, (src, dst) in enumerate(zip(hbm_refs, vmem_refs))]


def _prompt_kernel(x_ref, xprev_ref, ada_ref, v1024_ref, v512_ref, w_in_ref, wg_ref, wsp_ref,
                   bias_s_ref, w_out_hbm, w_ff1_hbm, w_ff2_hbm,
                   y_ref, conv_ref, lru_ref, vrows_ref,
                   xperm_ref, hperm_ref, tail_ref, h_ref, merged_ref, w_out_ref, w_ff1_ref, w_ff2_ref, w_sem,
                   *, tiles_per_seq):
    s = pl.program_id(0)
    n_tiles = pl.num_programs(0) - 1
    j = lax.rem(s, tiles_per_seq)
    seq_cur = jnp.minimum(s, n_tiles - 1) // tiles_per_seq
    seq_prev = jnp.maximum(s - 1, 0) // tiles_per_seq

    @pl.when(j == 0)
    def _():
        tail_ref[...] = jnp.zeros((SUBLANES, D_RG), F32)
        h_ref[...] = jnp.zeros((SUBLANES, D_RG), F32)

    v1024 = v1024_ref[...]
    v512 = v512_ref[...]

    def store_y(_, val):
        y_ref[...] = val

    def stage_a():
        return _prompt_mix_thunks(x_ref, ada_ref[pl.ds(seq_cur, 1), :], v1024, v512, w_in_ref, wg_ref, wsp_ref,
                                  bias_s_ref, conv_ref, lru_ref, vrows_ref, xperm_ref, hperm_ref, tail_ref, h_ref,
                                  merged_ref, j, seq_cur)

    def stage_b():
        return _tail_thunks([xprev_ref[...]], lambda: merged_ref[...], _split_mods(ada_ref[pl.ds(seq_prev, 1), :]),
                            v1024, w_out_ref, w_ff1_ref, w_ff2_ref, store_y)

    copies = _tail_weight_copies((w_out_hbm, w_ff1_hbm, w_ff2_hbm), (w_out_ref, w_ff1_ref, w_ff2_ref), w_sem)

    @pl.when(s == 0)
    def _():
        for cp in copies:
            cp.start()
        _run(stage_a(), MIX_ORDER)

    @pl.when(s == 1)
    def _():
        for cp in copies:
            cp.wait()

    @pl.when(jnp.logical_and(s > 0, s < n_tiles))
    def _():
        stages = {'a': stage_a(), 'b': stage_b()}
        assert sorted(n for k, n in INTERLEAVED_ORDER if k == 'a') == sorted(stages['a'])
        assert sorted(n for k, n in INTERLEAVED_ORDER if k == 'b') == sorted(stages['b'])
        for stage, name in INTERLEAVED_ORDER:
            stages[stage][name]()

    @pl.when(s == n_tiles)
    def _():
        _run(stage_b(), TAIL_ORDER)


def _sample_kernel(x_ref, ada_ref, cs_ref, h0_ref, v1024_ref, v512_ref, w_in_ref, wg_ref, w8_ref, b8_ref,
                   w_out_hbm, w_ff1_hbm, w_ff2_hbm,
                   y_ref, conv_ref, lru_ref, vrows_ref,
                   xslab_ref, yslab_ref, vslab_ref, w_out_ref, w_ff1_ref, w_ff2_ref, w_sem):
    n_t = SUBLANES
    nb = x_ref.shape[0] // n_t
    copies = _tail_weight_copies((w_out_hbm, w_ff1_hbm, w_ff2_hbm), (w_out_ref, w_ff1_ref, w_ff2_ref), w_sem)

    @pl.when(pl.program_id(0) == 0)
    def _():
        for cp in copies:
            cp.start()

    mods = _split_mods(ada_ref[...])
    sh_m, sc_m = mods[0], mods[1]
    v1024 = v1024_ref[...]
    v512 = v512_ref[...]

    def blk(arr, t):
        return arr[t * nb:(t + 1) * nb, :]

    for s in range(D_MODEL // LANES):
        xslab_ref[s] = x_ref[:, s * LANES:(s + 1) * LANES]
    x_blocks = [_cat([xslab_ref[s, pl.ds(t, nb, stride=n_t), :] for s in range(D_MODEL // LANES)], 1)
                for t in range(n_t)]
    hm = _modulated_norm(x_blocks, v1024[0:1, :], sc_m, sh_m)
    proj = _dot(hm, w_in_ref[...])
    xr = proj[:, 0:D_RG]
    yg = proj[:, D_RG:2 * D_RG]
    u = proj[:, 2 * D_RG:2 * D_RG + D_SG]
    v = proj[:, 2 * D_RG + D_SG:]

    xp = [cs_ref[:, k, :] for k in range(CONV_W - 1)] + [blk(xr, t) for t in range(n_t)]
    xc_blocks = []
    for t in range(n_t):
        acc = v512[_V_CONV_B:_V_CONV_B + 1, :]
        for k in range(CONV_W):
            acc = acc + xp[t + k] * v512[_V_CONV_W + k:_V_CONV_W + k + 1, :]
        xc_blocks.append(acc)
    for k in range(CONV_W - 1):
        conv_ref[:, k, :] = xp[n_t + k]
    xc = _cat(xc_blocks)

    t_r, t_i = _gate_tanh(_gate_preacts(xc, wg_ref), v512)
    a, mult = _lru_coeffs(t_r, v512)
    bterm = mult * (0.5 * t_i + 0.5) * xc
    h = h0_ref[...]
    h_blocks = []
    for t in range(n_t):
        h = blk(a, t) * h + blk(bterm, t)
        h_blocks.append(h)
    lru_ref[...] = h
    rg_out = _cat(h_blocks) * _gelu_tanh(yg)

    vn = _layer_norm(v, v512[_V_LN_G:_V_LN_G + 1, :], v512[_V_LN_B:_V_LN_B + 1, :])
    sg_blocks = []
    for t in range(n_t):
        for s in range(D_SG // LANES):
            vslab_ref[s, pl.ds(t, nb, stride=n_t), :] = blk(v, t)[:, s * LANES:(s + 1) * LANES]
        mixed = b8_ref[t:t + 1, :]
        for s in range(t + 1):
            mixed = mixed + w8_ref[t, s:s + 1, :] * blk(vn, s)
        sg_blocks.append(blk(u, t) * mixed)
    sg_out = _cat(sg_blocks)
    for s in range(D_SG // LANES):
        vrows_ref[:, s * LANES:(s + 1) * LANES] = vslab_ref[s]

    merged = jnp.concatenate(
        [_rms(rg_out, v512[_V_G_RG:_V_G_RG + 1, :]).astype(BF16),
         _rms(sg_out, v512[_V_G_SG:_V_G_SG + 1, :]).astype(BF16)], axis=1)

    def store_y(t, val):
        for s in range(D_MODEL // LANES):
            yslab_ref[s, pl.ds(t, nb, stride=n_t), :] = val[:, s * LANES:(s + 1) * LANES]

    @pl.when(pl.program_id(0) == 0)
    def _():
        for cp in copies:
            cp.wait()

    _run(_tail_thunks(x_blocks, lambda: merged, mods, v1024, w_out_ref, w_ff1_ref, w_ff2_ref, store_y), TAIL_ORDER)
    for s in range(D_MODEL // LANES):
        y_ref[:, s * LANES:(s + 1) * LANES] = yslab_ref[s]


def _pack_params(g_mix_ref, g_ffn_ref, g_final_ref, conv_w_ref, conv_b_ref, b_ra_ref, b_ri_ref, lru_l_ref,
                 ln_v_g_ref, ln_v_b_ref, g_rg_ref, g_sg_ref, w_ra_ref, w_ri_ref, w_s_ref, b_s_ref,
                 v1024_o, v512_o, wg_o, wsp_o, bias_s_o, w8_o, b8_o, wg_f):
    v1024_o[0:1, :] = g_mix_ref[...]
    v1024_o[1:2, :] = g_ffn_ref[...]
    v1024_o[2:3, :] = g_final_ref[...]
    v512_o[_V_CONV_W:_V_CONV_W + CONV_W, :] = conv_w_ref[...]
    for row, ref in ((_V_CONV_B, conv_b_ref), (_V_LRU_L, lru_l_ref), (_V_LN_G, ln_v_g_ref), (_V_LN_B, ln_v_b_ref),
                     (_V_G_RG, g_rg_ref), (_V_G_SG, g_sg_ref)):
        v512_o[row:row + 1, :] = ref[...]
    wg_f[...] = jnp.zeros(wg_f.shape, F32)
    per = RG_HEADS // GATE_GROUPS
    for h in range(RG_HEADS):
        g, lo = h // per, (h % per) * RG_HEAD_DIM
        v512_o[_V_B_RA:_V_B_RA + 1, h * RG_HEAD_DIM:(h + 1) * RG_HEAD_DIM] = b_ra_ref[h:h + 1, :]
        v512_o[_V_B_RI:_V_B_RI + 1, h * RG_HEAD_DIM:(h + 1) * RG_HEAD_DIM] = b_ri_ref[h:h + 1, :]
        wg_f[g, lo:lo + RG_HEAD_DIM, lo:lo + RG_HEAD_DIM] = w_ra_ref[h]
        wg_f[g, lo:lo + RG_HEAD_DIM, GATE_W + lo:GATE_W + lo + RG_HEAD_DIM] = w_ri_ref[h]
    wg_o[...] = (0.5 * wg_f[...]).astype(BF16)
    for h in range(SG_HEADS):
        wsp_o[h // 2, :, (h % 2) * CHUNK:(h % 2 + 1) * CHUNK] = w_s_ref[h].astype(BF16)
    head = lax.broadcasted_iota(jnp.int32, (1, D_SG), 1) // SG_HEAD_DIM
    b_t = b_s_ref[...].T
    bias = jnp.zeros((CHUNK, D_SG), F32)
    w_t = []
    for h in range(SG_HEADS):
        bias = jnp.where(head == h, b_t[:, h:h + 1], bias)
        w_t.append(w_s_ref[h, 0:SUBLANES, :].T[0:SUBLANES, :])
    bias_s_o[...] = bias
    b8_o[...] = bias[0:SUBLANES, :]
    for t in range(SUBLANES):
        acc = jnp.zeros((SUBLANES, D_SG), F32)
        for h in range(SG_HEADS):
            acc = jnp.where(head == h, w_t[h][:, t:t + 1], acc)
        w8_o[t] = acc


def _prep_kernel(cs_ref, cp_ref, w_ada_ref, b_ada_ref, w_in_ref, w_out_ref, w_ff1_ref, w_ff2_ref, *rest):
    small, (ada_ref, w_in_o, w_out_o, w_ff1_o, w_ff2_o), packed = rest[:16], rest[16:21], rest[21:]
    c = jnp.concatenate([cs_ref[...], cp_ref[...]], axis=0)
    s = (c * jax.nn.sigmoid(c)).astype(BF16)
    ada_ref[...] = _dot(s, w_ada_ref[...].astype(BF16)) + b_ada_ref[...]
    w_in_o[...] = w_in_ref[...].astype(BF16)
    w_out_o[...] = w_out_ref[...].astype(BF16)
    w_ff1_o[...] = w_ff1_ref[...].astype(BF16)
    w_ff2_o[...] = w_ff2_ref[...].astype(BF16)

    @pl.when(pl.program_id(0) == 0)
    def _():
        _pack_params(*small, *packed)


def _resident(shape):
    nd = len(shape)
    return pl.BlockSpec(shape, lambda *_: (0,) * nd, pipeline_mode=pl.Buffered(1))


def kernel(x_prompt, x_sample, c_prompt, c_sample, state_conv, state_rglru, w_ada, b_ada, g_mix, g_ffn, w_in, conv_w, conv_b, w_ra, b_ra, w_ri, b_ri, lru_l, ln_v_g, ln_v_b, w_s, b_s, g_rg, g_sg, w_out, w_ff1, w_ff2, g_final):
    batch, seq, _ = x_prompt.shape
    dec_batch, dec_seq, _ = x_sample.shape
    assert w_ada.shape[0] == 1, "single-layer step"
    assert seq % PROMPT_TILE == 0 and PROMPT_TILE % CHUNK == 0 and dec_batch % SAMPLE_NB == 0
    assert batch * (seq // PROMPT_TILE) >= 2, "the prompt pipeline needs at least two tiles"
    assert dec_seq <= CHUNK

    n_c = dec_batch + batch
    assert dec_batch % SUBLANES == 0 and dec_batch % batch == 0
    steps = PREP_STEPS

    def row_block(w):
        return pl.BlockSpec((w.shape[0] // steps, w.shape[1]), lambda n: (n, 0))

    def whole(a, lead=0):
        nd = a.ndim
        return pl.BlockSpec((None,) * lead + a.shape[lead:], lambda n: (0,) * nd)

    mats = (w_in[0], w_out[0], w_ff1[0], w_ff2[0])
    gains = (g_mix, g_ffn, g_final.reshape(1, D_MODEL))
    rows512 = (conv_b, lru_l, ln_v_g, ln_v_b, g_rg, g_sg)
    small = gains + (conv_w, rows512[0], b_ra, b_ri) + rows512[1:] + (w_ra, w_ri, w_s, b_s)
    small_specs = [whole(a) for a in gains] + [whole(conv_w, 1), whole(conv_b), whole(b_ra, 1), whole(b_ri, 1)] + \
                  [whole(a) for a in rows512[1:]] + [whole(w_ra, 1), whole(w_ri, 1), whole(w_s, 1), whole(b_s, 1)]
    packed_shapes = ((3, D_MODEL), (12, D_RG), (GATE_GROUPS, GATE_W, 2 * GATE_W), (SG_HEADS // 2, CHUNK, 2 * CHUNK),
                     (CHUNK, D_SG), (SUBLANES, SUBLANES, D_SG), (SUBLANES, D_SG))
    packed_dtypes = (F32, F32, BF16, BF16, F32, F32, F32)
    packed_out = tuple(jax.ShapeDtypeStruct(s, d) for s, d in zip(packed_shapes, packed_dtypes))
    ada, w_in_b, w_out_b, w_ff1_b, w_ff2_b, v1024, v512, wg, wsp, bias_s, w8, b8 = pl.pallas_call(
        _prep_kernel,
        out_shape=(jax.ShapeDtypeStruct((n_c, 6 * D_MODEL), F32),) +
                  tuple(jax.ShapeDtypeStruct(w.shape, BF16) for w in mats) + packed_out,
        grid=(steps,),
        in_specs=[pl.BlockSpec((dec_batch, D_MODEL), lambda n: (0, 0)),
                  pl.BlockSpec((batch, D_MODEL), lambda n: (0, 0)),
                  pl.BlockSpec((D_MODEL, 6 * D_MODEL // steps), lambda n: (0, n)),
                  pl.BlockSpec((1, 6 * D_MODEL // steps), lambda n: (0, n))] + [row_block(w) for w in mats] +
                 small_specs,
        out_specs=(pl.BlockSpec((n_c, 6 * D_MODEL // steps), lambda n: (0, n)),) +
                  tuple(row_block(w) for w in mats) + tuple(whole(o) for o in packed_out),
        scratch_shapes=[pltpu.VMEM(packed_shapes[2], F32)],
        compiler_params=pltpu.CompilerParams(dimension_semantics=("arbitrary",),
                                             vmem_limit_bytes=VMEM_LIMIT_BYTES),
        name="prep",
    )(c_sample, c_prompt, w_ada[0], b_ada, *mats, *small)

    weights = (w_in_b, wg)
    weight_specs = [_resident(w_in_b.shape), _resident(wg.shape)]
    tail_weights = (w_out_b, w_ff1_b, w_ff2_b)
    cparams = functools.partial(pltpu.CompilerParams, vmem_limit_bytes=VMEM_LIMIT_BYTES)

    tt = PROMPT_TILE
    nt = seq // tt
    n_tiles = batch * nt

    def cur(s):
        return jnp.minimum(s, n_tiles - 1)

    def prev(s):
        return jnp.maximum(s - 1, 0)

    y_p, conv_p, lru_p, vrows_p = pl.pallas_call(
        functools.partial(_prompt_kernel, tiles_per_seq=nt),
        out_shape=(jax.ShapeDtypeStruct((batch, seq, D_MODEL), F32),
                   jax.ShapeDtypeStruct((batch, CONV_W - 1, D_RG), F32),
                   jax.ShapeDtypeStruct((batch, D_RG), F32),
                   jax.ShapeDtypeStruct((batch, CHUNK, D_SG), F32)),
        grid=(n_tiles + 1,),
        in_specs=[pl.BlockSpec((None, tt, D_MODEL), lambda s: (cur(s) // nt, cur(s) % nt, 0)),
                  pl.BlockSpec((None, tt, D_MODEL), lambda s: (prev(s) // nt, prev(s) % nt, 0)),
                  pl.BlockSpec((batch, 6 * D_MODEL), lambda s: (dec_batch // batch, 0),
                               pipeline_mode=pl.Buffered(1)),
                  _resident(v1024.shape), _resident(v512.shape)] + weight_specs +
                 [_resident(wsp.shape), _resident(bias_s.shape)] + [pl.BlockSpec(memory_space=pl.ANY)] * 3,
        out_specs=(pl.BlockSpec((None, tt, D_MODEL), lambda s: (prev(s) // nt, prev(s) % nt, 0)),
                   pl.BlockSpec((batch, CONV_W - 1, D_RG), lambda s: (0, 0, 0)),
                   pl.BlockSpec((batch, D_RG), lambda s: (0, 0)),
                   pl.BlockSpec((batch, CHUNK, D_SG), lambda s: (0, 0, 0))),
        scratch_shapes=[pltpu.VMEM((D_RG // LANES, tt, LANES), F32), pltpu.VMEM((D_RG // LANES, tt, LANES), F32),
                        pltpu.VMEM((SUBLANES, D_RG), F32), pltpu.VMEM((SUBLANES, D_RG), F32),
                        pltpu.VMEM((tt, D_MODEL), BF16)] +
                       [pltpu.VMEM(w.shape, BF16) for w in tail_weights] + [pltpu.SemaphoreType.DMA((3,))],
        compiler_params=cparams(dimension_semantics=("arbitrary",)),
        name="prompt_trunk",
    )(x_prompt, x_prompt, ada, v1024, v512, *weights, wsp, bias_s, *tail_weights)

    nb = SAMPLE_NB
    n_cs = CONV_W - 1
    assert dec_seq == SUBLANES, "one decode sequence per 8-row group"
    rows = nb * dec_seq

    y_s, conv_s, lru_s, vrows_s = pl.pallas_call(
        _sample_kernel,
        out_shape=(jax.ShapeDtypeStruct((dec_batch * dec_seq, D_MODEL), F32),
                   jax.ShapeDtypeStruct((dec_batch, n_cs, D_RG), F32),
                   jax.ShapeDtypeStruct((dec_batch, D_RG), F32),
                   jax.ShapeDtypeStruct((dec_batch * dec_seq, D_SG), F32)),
        grid=(dec_batch // nb,),
        in_specs=[pl.BlockSpec((rows, D_MODEL), lambda i: (i, 0)),
                  pl.BlockSpec((nb, 6 * D_MODEL), lambda i: (i, 0)),
                  pl.BlockSpec((nb, n_cs, D_RG), lambda i: (i, 0, 0)),
                  pl.BlockSpec((nb, D_RG), lambda i: (i, 0)),
                  _resident(v1024.shape), _resident(v512.shape)] + weight_specs +
                 [_resident(w8.shape), _resident(b8.shape)] + [pl.BlockSpec(memory_space=pl.ANY)] * 3,
        out_specs=(pl.BlockSpec((rows, D_MODEL), lambda i: (i, 0)),
                   pl.BlockSpec((nb, n_cs, D_RG), lambda i: (i, 0, 0)),
                   pl.BlockSpec((nb, D_RG), lambda i: (i, 0)),
                   pl.BlockSpec((rows, D_SG), lambda i: (i, 0))),
        scratch_shapes=[pltpu.VMEM((D_MODEL // LANES, rows, LANES), F32),
                        pltpu.VMEM((D_MODEL // LANES, rows, LANES), F32),
                        pltpu.VMEM((D_SG // LANES, rows, LANES), F32)] +
                       [pltpu.VMEM(w.shape, BF16) for w in tail_weights] + [pltpu.SemaphoreType.DMA((3,))],
        compiler_params=cparams(dimension_semantics=("arbitrary",)),
        name="sample_trunk",
    )(x_sample.reshape(dec_batch * dec_seq, D_MODEL), ada, state_conv[0], state_rglru[0], v1024, v512,
      *weights, w8, b8, *tail_weights)

    return (y_p, y_s.reshape(dec_batch, dec_seq, D_MODEL), conv_p[None], lru_p[None], vrows_p[None],
            conv_s[None], lru_s[None], vrows_s.reshape(1, dec_batch, dec_seq, D_SG))
```

```python
import functools

import jax
import jax.numpy as jnp
from jax import lax
from jax.experimental import pallas as pl
from jax.experimental.pallas import tpu as pltpu

D_MODEL = 1024
D_RG = 512
D_SG = 512
D_IN = 2 * D_RG + 2 * D_SG
D_FF = 4 * D_MODEL
RG_HEADS = 8
RG_HEAD_DIM = D_RG // RG_HEADS
SG_HEADS = 8
SG_HEAD_DIM = D_SG // SG_HEADS
CHUNK = 128
CONV_W = 4
LRU_C = 8.0
EPS = 1e-6

SUBLANES = 8
LANES = 128
GATE_GROUPS = 2
GATE_W = D_RG // GATE_GROUPS
PROMPT_TILE = 512
SAMPLE_NB = 64
FF_CHUNK = 1024
N_FF_CHUNKS = D_FF // FF_CHUNK
PREP_STEPS = 4
VMEM_LIMIT_BYTES = 60000 * 1024

F32 = jnp.float32
BF16 = jnp.bfloat16

_V_CONV_W, _V_CONV_B, _V_B_RA, _V_B_RI, _V_LRU_L, _V_LN_G, _V_LN_B, _V_G_RG, _V_G_SG = 0, 4, 5, 6, 7, 8, 9, 10, 11


def _dot(a, b):
    return jnp.dot(a, b, preferred_element_type=F32)


def _rms(x, g):
    ms = jnp.mean(x * x, axis=-1, keepdims=True)
    return (x * lax.rsqrt(ms + EPS)) * g


def _gelu_tanh(x):
    k = 0.7978845608028654
    u = x * (k + (k * 0.044715) * (x * x))
    return (0.5 * x) * (1.0 + jnp.tanh(u))


def _layer_norm(x, g, b):
    mu = jnp.mean(x, axis=-1, keepdims=True)
    xc = x - mu
    var = jnp.mean(xc * xc, axis=-1, keepdims=True)
    return (xc * lax.rsqrt(var + EPS)) * g + b


def _cat(parts, axis=0):
    return parts[0] if len(parts) == 1 else jnp.concatenate(parts, axis=axis)


def _split_mods(ada):
    return [ada[:, k * D_MODEL:(k + 1) * D_MODEL] for k in range(6)]


def _modulated_norm(x_blocks, g, sc, sh):
    gain = g * (1.0 + sc)
    return _cat([(_rms(xb, gain) + sh).astype(BF16) for xb in x_blocks])


def _gate_preacts(xc, wg_ref):
    xcb = xc.astype(BF16)
    return _cat([_dot(xcb[:, g * GATE_W:(g + 1) * GATE_W], wg_ref[g]) for g in range(GATE_GROUPS)], 1)


def _gate_tanh(pre_half, v512):
    r_parts, i_parts = [], []
    for g in range(GATE_GROUPS):
        lo, hi = g * GATE_W, (g + 1) * GATE_W
        ra = pre_half[:, 2 * lo:2 * lo + GATE_W]
        ri = pre_half[:, 2 * lo + GATE_W:2 * hi]
        r_parts.append(jnp.tanh(ra + 0.5 * v512[_V_B_RA:_V_B_RA + 1, lo:hi]))
        i_parts.append(jnp.tanh(ri + 0.5 * v512[_V_B_RI:_V_B_RI + 1, lo:hi]))
    return _cat(r_parts, 1), _cat(i_parts, 1)


def _lru_coeffs(t_r, v512):
    lam = v512[_V_LRU_L:_V_LRU_L + 1, :]
    log_sig = jnp.minimum(lam, 0.0) - jnp.log1p(jnp.exp(-jnp.abs(lam)))
    k = (0.5 * LRU_C) * log_sig
    log_a = k * t_r + k
    a = jnp.exp(log_a)
    z = 1.0 - a * a
    mult = jnp.where(z > 0.0, z * lax.rsqrt(z), 0.0)
    return a, mult


def _tail_thunks(x_blocks, merged_fn, mods, v1024, w_out_ref, w_ff1_ref, w_ff2_ref, store_y):
    _, _, gt_m, sh_f, sc_f, gt_f = mods
    rows = x_blocks[0].shape[0]
    st = {}

    def w_out():
        mm = _dot(merged_fn(), w_out_ref[...])
        st['x1'] = [xb + gt_m * mm[i * rows:(i + 1) * rows] for i, xb in enumerate(x_blocks)]

    def hf():
        st['hf'] = _modulated_norm(st['x1'], v1024[1:2, :], sc_f, sh_f)

    def ff1(c):
        pre = _dot(st['hf'], w_ff1_ref[:, c * FF_CHUNK:(c + 1) * FF_CHUNK])
        h1 = jnp.maximum(pre.astype(BF16), 0.0)
        st['h1_%d' % c] = h1 * h1

    def ff2():
        h1 = _cat([st.pop('h1_%d' % c) for c in range(N_FF_CHUNKS)], 1)
        st['acc'] = _dot(h1, w_ff2_ref[...])

    def final():
        for i, xb in enumerate(st['x1']):
            x2 = xb + gt_f * st['acc'][i * rows:(i + 1) * rows]
            store_y(i, _rms(x2, v1024[2:3, :]))

    thunks = {'w_out': w_out, 'hf': hf, 'ff2': ff2, 'final': final}
    for c in range(N_FF_CHUNKS):
        thunks['ff1_%d' % c] = functools.partial(ff1, c)
    return thunks


TAIL_ORDER = ['w_out', 'hf'] + ['ff1_%d' % c for c in range(N_FF_CHUNKS)] + ['ff2', 'final']


def _run(thunks, order):
    assert sorted(order) == sorted(thunks), (sorted(order), sorted(thunks))
    for name in order:
        thunks[name]()


def _prompt_mix_thunks(x_ref, ada, v1024, v512, w_in_ref, wg_ref, wsp_ref, bias_s_ref,
                       conv_ref, lru_ref, vrows_ref, xperm_ref, hperm_ref, tail_ref, h_ref, merged_ref, j, slot):
    tt = x_ref.shape[0]
    n_sub = SUBLANES
    sub_len = tt // n_sub
    n_slab = D_RG // LANES
    st = {}

    def norm():
        mods = _split_mods(ada)
        st['hm'] = _modulated_norm([x_ref[...]], v1024[0:1, :], mods[1], mods[0])

    def project():
        st['proj'] = proj = _dot(st['hm'], w_in_ref[...])
        xr = proj[:, 0:D_RG]
        st['xr_tail'] = xr[tt - SUBLANES:, :]
        for p in range(n_sub):
            for s in range(n_slab):
                xperm_ref[s, pl.ds(p, sub_len, stride=n_sub), :] = (
                    xr[p * sub_len:(p + 1) * sub_len, s * LANES:(s + 1) * LANES])

    def conv():
        xr_p = _cat([xperm_ref[s] for s in range(n_slab)], 1)
        sub = lax.broadcasted_iota(jnp.int32, (SUBLANES, D_RG), 0)
        prev_tail = tail_ref[...]
        head = []
        for k in range(CONV_W - 1, 0, -1):
            grp = pltpu.roll(xr_p[(sub_len - k) * SUBLANES:(sub_len - k + 1) * SUBLANES, :], 1, 0)
            head.append(jnp.where(sub == 0, prev_tail[SUBLANES - k:SUBLANES - k + 1, :], grp))
        ext = _cat(head + [xr_p])
        xc = v512[_V_CONV_B:_V_CONV_B + 1, :]
        for k in range(CONV_W):
            xc = xc + ext[k * SUBLANES:k * SUBLANES + tt, :] * v512[_V_CONV_W + k:_V_CONV_W + k + 1, :]
        st['xc'] = xc
        tail_ref[...] = st['xr_tail']
        for k in range(CONV_W - 1):
            row = SUBLANES - (CONV_W - 1) + k
            conv_ref[k, pl.ds(slot, 1), :] = st['xr_tail'][row:row + 1, :]

    def gate_mm():
        st['gate_pre'] = _gate_preacts(st['xc'], wg_ref)

    def coefficients():
        t_r, t_i = _gate_tanh(st.pop('gate_pre'), v512)
        a, mult = _lru_coeffs(t_r, v512)
        row = lax.broadcasted_iota(jnp.int32, (tt, D_RG), 0)
        mult = jnp.where(row + j * tt == 0, 1.0, mult)
        st['a'] = a
        st['b'] = mult * (0.5 * t_i + 0.5) * st.pop('xc')

    def recurrence():
        a, b = st['a'], st['b']
        sub = lax.broadcasted_iota(jnp.int32, (SUBLANES, D_RG), 0)
        hs, ps = [], []
        for q in range(sub_len):
            aq = a[q * SUBLANES:(q + 1) * SUBLANES, :]
            bq = b[q * SUBLANES:(q + 1) * SUBLANES, :]
            hs.append(bq if q == 0 else aq * hs[-1] + bq)
            ps.append(aq if q == 0 else aq * ps[-1])
        init = jnp.where(sub == 0, h_ref[SUBLANES - 1:SUBLANES, :], 0.0)
        for p in range(1, n_sub):
            end = hs[-1] + ps[-1] * init
            init = jnp.where(sub == p, pltpu.roll(end, 1, 0), init)
        end = hs[-1] + ps[-1] * init
        h_ref[...] = end
        lru_ref[pl.ds(slot, 1), :] = end[SUBLANES - 1:SUBLANES, :]
        h_perm = _cat([hq + pq * init for hq, pq in zip(hs, ps)])
        for s in range(n_slab):
            hperm_ref[s] = h_perm[:, s * LANES:(s + 1) * LANES]

    def recurrent_out(c):
        r0 = c * CHUNK
        yg = st['proj'][r0:r0 + CHUNK, D_RG:2 * D_RG]
        subs = range(r0 // sub_len, (r0 + CHUNK) // sub_len)
        h = _cat([_cat([hperm_ref[s, pl.ds(p, sub_len, stride=n_sub), :] for p in subs]) for s in range(n_slab)], 1)
        rg_out = h * _gelu_tanh(yg)
        merged_ref[r0:r0 + CHUNK, 0:D_RG] = _rms(rg_out, v512[_V_G_RG:_V_G_RG + 1, :]).astype(BF16)

    def spatial_pre(c):
        r0 = c * CHUNK
        lane = lax.broadcasted_iota(jnp.int32, (CHUNK, 2 * SG_HEAD_DIM), 1)
        v = st['proj'][r0:r0 + CHUNK, 2 * D_RG + D_SG:]
        vn = _layer_norm(v, v512[_V_LN_G:_V_LN_G + 1, :], v512[_V_LN_B:_V_LN_B + 1, :])
        rhs = []
        for p in range(SG_HEADS // 2):
            vp = vn[:, p * 2 * SG_HEAD_DIM:(p + 1) * 2 * SG_HEAD_DIM]
            rhs.append(jnp.concatenate([jnp.where(lane < SG_HEAD_DIM, vp, 0.0),
                                        jnp.where(lane >= SG_HEAD_DIM, vp, 0.0)], axis=0).astype(BF16))
        st['rhs_%d' % c] = rhs
        if r0 + CHUNK == tt:
            vrows_ref[slot] = v

    def spatial_mm(c):
        t_idx = lax.broadcasted_iota(jnp.int32, (CHUNK, 2 * CHUNK), 0)
        s_idx = lax.broadcasted_iota(jnp.int32, (CHUNK, 2 * CHUNK), 1) & (CHUNK - 1)
        outs = []
        for p, rhs in enumerate(st.pop('rhs_%d' % c)):
            ws_pair = jnp.where(s_idx <= t_idx, wsp_ref[p], jnp.zeros((), BF16))
            outs.append(_dot(ws_pair, rhs))
        st['mixed_%d' % c] = _cat(outs, 1)

    def spatial_post(c):
        r0 = c * CHUNK
        u = st['proj'][r0:r0 + CHUNK, 2 * D_RG:2 * D_RG + D_SG]
        sg_out = u * (st.pop('mixed_%d' % c) + bias_s_ref[...])
        merged_ref[r0:r0 + CHUNK, D_RG:] = _rms(sg_out, v512[_V_G_SG:_V_G_SG + 1, :]).astype(BF16)

    thunks = {'norm': norm, 'w_in': project, 'conv': conv, 'gate_mm': gate_mm, 'coef': coefficients,
              'rec': recurrence}
    for c in range(tt // CHUNK):
        thunks['sp_pre_%d' % c] = functools.partial(spatial_pre, c)
        thunks['sp_mm_%d' % c] = functools.partial(spatial_mm, c)
        thunks['sp_post_%d' % c] = functools.partial(spatial_post, c)
        thunks['rg_out_%d' % c] = functools.partial(recurrent_out, c)
    return thunks


N_PROMPT_CHUNKS = PROMPT_TILE // CHUNK
MIX_ORDER = (['norm', 'w_in', 'conv', 'gate_mm', 'coef', 'rec'] +
             [f'{name}_{c}' for c in range(N_PROMPT_CHUNKS) for name in ('sp_pre', 'sp_mm', 'sp_post', 'rg_out')])

assert N_PROMPT_CHUNKS == 4 and N_FF_CHUNKS == 4
INTERLEAVED_ORDER = [
    ('a', 'norm'), ('b', 'w_out'), ('a', 'w_in'), ('b', 'hf'), ('b', 'ff1_0'), ('a', 'conv'),
    ('a', 'sp_pre_0'), ('a', 'sp_pre_1'), ('b', 'ff1_1'), ('a', 'gate_mm'), ('a', 'sp_pre_2'), ('a', 'sp_pre_3'),
    ('b', 'ff1_2'), ('a', 'coef'), ('a', 'sp_mm_0'), ('a', 'sp_mm_1'), ('b', 'ff1_3'), ('a', 'rec'),
    ('a', 'sp_mm_2'), ('a', 'sp_mm_3'), ('a', 'sp_post_0'), ('a', 'sp_post_1'),
    ('a', 'rg_out_0'), ('a', 'rg_out_1'), ('a', 'sp_post_2'), ('a', 'sp_post_3'),
    ('a', 'rg_out_2'), ('a', 'rg_out_3'), ('b', 'ff2'), ('b', 'final'),
]


def _tail_weight_copies(hbm_refs, vmem_refs, sem):
    return [pltpu.make_async_copy(src, dst, sem.at[k]) for k, (src, dst) in enumerate(zip(hbm_refs, vmem_refs))]


def _prompt_kernel(x_ref, xprev_ref, ada_ref, v1024_ref, v512_ref, w_in_ref, wg_ref, wsp_ref,
                   bias_s_ref, w_out_hbm, w_ff1_hbm, w_ff2_hbm,
                   y_ref, conv_ref, lru_ref, vrows_ref,
                   xperm_ref, hperm_ref, tail_ref, h_ref, merged_ref, w_out_ref, w_ff1_ref, w_ff2_ref, w_sem,
                   *, tiles_per_seq):
    s = pl.program_id(0)
    n_tiles = pl.num_programs(0) - 1
    j = lax.rem(s, tiles_per_seq)
    seq_cur = jnp.minimum(s, n_tiles - 1) // tiles_per_seq
    seq_prev = jnp.maximum(s - 1, 0) // tiles_per_seq

    @pl.when(j == 0)
    def _():
        tail_ref[...] = jnp.zeros((SUBLANES, D_RG), F32)
        h_ref[...] = jnp.zeros((SUBLANES, D_RG), F32)

    v1024 = v1024_ref[...]
    v512 = v512_ref[...]

    def store_y(_, val):
        y_ref[...] = val

    def stage_a():
        return _prompt_mix_thunks(x_ref, ada_ref[pl.ds(seq_cur, 1), :], v1024, v512, w_in_ref, wg_ref, wsp_ref,
                                  bias_s_ref, conv_ref, lru_ref, vrows_ref, xperm_ref, hperm_ref, tail_ref, h_ref,
                                  merged_ref, j, seq_cur)

    def stage_b():
        return _tail_thunks([xprev_ref[...]], lambda: merged_ref[...], _split_mods(ada_ref[pl.ds(seq_prev, 1), :]),
                            v1024, w_out_ref, w_ff1_ref, w_ff2_ref, store_y)

    copies = _tail_weight_copies((w_out_hbm, w_ff1_hbm, w_ff2_hbm), (w_out_ref, w_ff1_ref, w_ff2_ref), w_sem)

    @pl.when(s == 0)
    def _():
        for cp in copies:
            cp.start()
        _run(stage_a(), MIX_ORDER)

    @pl.when(s == 1)
    def _():
        for cp in copies:
            cp.wait()

    @pl.when(jnp.logical_and(s > 0, s < n_tiles))
    def _():
        stages = {'a': stage_a(), 'b': stage_b()}
        assert sorted(n for k, n in INTERLEAVED_ORDER if k == 'a') == sorted(stages['a'])
        assert sorted(n for k, n in INTERLEAVED_ORDER if k == 'b') == sorted(stages['b'])
        for stage, name in INTERLEAVED_ORDER:
            stages[stage][name]()

    @pl.when(s == n_tiles)
    def _():
        _run(stage_b(), TAIL_ORDER)


def _sample_kernel(x_ref, ada_ref, cs_ref, h0_ref, v1024_ref, v512_ref, w_in_ref, wg_ref, w8_ref, b8_ref,
                   w_out_hbm, w_ff1_hbm, w_ff2_hbm,
                   y_ref, conv_ref, lru_ref, vrows_ref,
                   xslab_ref, yslab_ref, vslab_ref, w_out_ref, w_ff1_ref, w_ff2_ref, w_sem):
    n_t = SUBLANES
    nb = x_ref.shape[0] // n_t
    copies = _tail_weight_copies((w_out_hbm, w_ff1_hbm, w_ff2_hbm), (w_out_ref, w_ff1_ref, w_ff2_ref), w_sem)

    @pl.when(pl.program_id(0) == 0)
    def _():
        for cp in copies:
            cp.start()

    mods = _split_mods(ada_ref[...])
    sh_m, sc_m = mods[0], mods[1]
    v1024 = v1024_ref[...]
    v512 = v512_ref[...]

    def blk(arr, t):
        return arr[t * nb:(t + 1) * nb, :]

    for s in range(D_MODEL // LANES):
        xslab_ref[s] = x_ref[:, s * LANES:(s + 1) * LANES]
    x_blocks = [_cat([xslab_ref[s, pl.ds(t, nb, stride=n_t), :] for s in range(D_MODEL // LANES)], 1)
                for t in range(n_t)]
    hm = _modulated_norm(x_blocks, v1024[0:1, :], sc_m, sh_m)
    proj = _dot(hm, w_in_ref[...])
    xr = proj[:, 0:D_RG]
    yg = proj[:, D_RG:2 * D_RG]
    u = proj[:, 2 * D_RG:2 * D_RG + D_SG]
    v = proj[:, 2 * D_RG + D_SG:]

    xp = [cs_ref[k] for k in range(CONV_W - 1)] + [blk(xr, t) for t in range(n_t)]
    xc_blocks = []
    for t in range(n_t):
        acc = v512[_V_CONV_B:_V_CONV_B + 1, :]
        for k in range(CONV_W):
            acc = acc + xp[t + k] * v512[_V_CONV_W + k:_V_CONV_W + k + 1, :]
        xc_blocks.append(acc)
    for k in range(CONV_W - 1):
        conv_ref[k] = xp[n_t + k]
    xc = _cat(xc_blocks)

    t_r, t_i = _gate_tanh(_gate_preacts(xc, wg_ref), v512)
    a, mult = _lru_coeffs(t_r, v512)
    bterm = mult * (0.5 * t_i + 0.5) * xc
    h = h0_ref[...]
    h_blocks = []
    for t in range(n_t):
        h = blk(a, t) * h + blk(bterm, t)
        h_blocks.append(h)
    lru_ref[...] = h
    rg_out = _cat(h_blocks) * _gelu_tanh(yg)

    vn = _layer_norm(v, v512[_V_LN_G:_V_LN_G + 1, :], v512[_V_LN_B:_V_LN_B + 1, :])
    sg_blocks = []
    for t in range(n_t):
        for s in range(D_SG // LANES):
            vslab_ref[s, pl.ds(t, nb, stride=n_t), :] = blk(v, t)[:, s * LANES:(s + 1) * LANES]
        mixed = b8_ref[t:t + 1, :]
        for s in range(t + 1):
            mixed = mixed + w8_ref[t, s:s + 1, :] * blk(vn, s)
        sg_blocks.append(blk(u, t) * mixed)
    sg_out = _cat(sg_blocks)
    for s in range(D_SG // LANES):
        vrows_ref[:, s * LANES:(s + 1) * LANES] = vslab_ref[s]

    merged = jnp.concatenate(
        [_rms(rg_out, v512[_V_G_RG:_V_G_RG + 1, :]).astype(BF16),
         _rms(sg_out, v512[_V_G_SG:_V_G_SG + 1, :]).astype(BF16)], axis=1)

    def store_y(t, val):
        for s in range(D_MODEL // LANES):
            yslab_ref[s, pl.ds(t, nb, stride=n_t), :] = val[:, s * LANES:(s + 1) * LANES]

    @pl.when(pl.program_id(0) == 0)
    def _():
        for cp in copies:
            cp.wait()

    _run(_tail_thunks(x_blocks, lambda: merged, mods, v1024, w_out_ref, w_ff1_ref, w_ff2_ref, store_y), TAIL_ORDER)
    for s in range(D_MODEL // LANES):
        y_ref[:, s * LANES:(s + 1) * LANES] = yslab_ref[s]


def _pack_params(g_mix_ref, g_ffn_ref, g_final_ref, conv_w_ref, conv_b_ref, b_ra_ref, b_ri_ref, lru_l_ref,
                 ln_v_g_ref, ln_v_b_ref, g_rg_ref, g_sg_ref, w_ra_ref, w_ri_ref, w_s_ref, b_s_ref,
                 v1024_o, v512_o, wg_o, wsp_o, bias_s_o, w8_o, b8_o, wg_f):
    v1024_o[0:1, :] = g_mix_ref[...]
    v1024_o[1:2, :] = g_ffn_ref[...]
    v1024_o[2:3, :] = g_final_ref[...]
    v512_o[_V_CONV_W:_V_CONV_W + CONV_W, :] = conv_w_ref[...]
    for row, ref in ((_V_CONV_B, conv_b_ref), (_V_LRU_L, lru_l_ref), (_V_LN_G, ln_v_g_ref), (_V_LN_B, ln_v_b_ref),
                     (_V_G_RG, g_rg_ref), (_V_G_SG, g_sg_ref)):
        v512_o[row:row + 1, :] = ref[...]
    wg_f[...] = jnp.zeros(wg_f.shape, F32)
    per = RG_HEADS // GATE_GROUPS
    for h in range(RG_HEADS):
        g, lo = h // per, (h % per) * RG_HEAD_DIM
        v512_o[_V_B_RA:_V_B_RA + 1, h * RG_HEAD_DIM:(h + 1) * RG_HEAD_DIM] = b_ra_ref[h:h + 1, :]
        v512_o[_V_B_RI:_V_B_RI + 1, h * RG_HEAD_DIM:(h + 1) * RG_HEAD_DIM] = b_ri_ref[h:h + 1, :]
        wg_f[g, lo:lo + RG_HEAD_DIM, lo:lo + RG_HEAD_DIM] = w_ra_ref[h]
        wg_f[g, lo:lo + RG_HEAD_DIM, GATE_W + lo:GATE_W + lo + RG_HEAD_DIM] = w_ri_ref[h]
    wg_o[...] = (0.5 * wg_f[...]).astype(BF16)
    for h in range(SG_HEADS):
        wsp_o[h // 2, :, (h % 2) * CHUNK:(h % 2 + 1) * CHUNK] = w_s_ref[h].astype(BF16)
    head = lax.broadcasted_iota(jnp.int32, (1, D_SG), 1) // SG_HEAD_DIM
    b_t = b_s_ref[...].T
    bias = jnp.zeros((CHUNK, D_SG), F32)
    w_t = []
    for h in range(SG_HEADS):
        bias = jnp.where(head == h, b_t[:, h:h + 1], bias)
        w_t.append(w_s_ref[h, 0:SUBLANES, :].T[0:SUBLANES, :])
    bias_s_o[...] = bias
    b8_o[...] = bias[0:SUBLANES, :]
    for t in range(SUBLANES):
        acc = jnp.zeros((SUBLANES, D_SG), F32)
        for h in range(SG_HEADS):
            acc = jnp.where(head == h, w_t[h][:, t:t + 1], acc)
        w8_o[t] = acc


def _prep_kernel(cs_ref, cp_ref, w_ada_ref, b_ada_ref, w_in_ref, w_out_ref, w_ff1_ref, w_ff2_ref, *rest):
    small, (ada_ref, w_in_o, w_out_o, w_ff1_o, w_ff2_o), packed = rest[:16], rest[16:21], rest[21:]
    c = jnp.concatenate([cs_ref[...], cp_ref[...]], axis=0)
    s = (c * jax.nn.sigmoid(c)).astype(BF16)
    ada_ref[...] = _dot(s, w_ada_ref[...].astype(BF16)) + b_ada_ref[...]
    w_in_o[...] = w_in_ref[...].astype(BF16)
    w_out_o[...] = w_out_ref[...].astype(BF16)
    w_ff1_o[...] = w_ff1_ref[...].astype(BF16)
    w_ff2_o[...] = w_ff2_ref[...].astype(BF16)

    @pl.when(pl.program_id(0) == 0)
    def _():
        _pack_params(*small, *packed)


def _resident(shape):
    nd = len(shape)
    return pl.BlockSpec(shape, lambda *_: (0,) * nd, pipeline_mode=pl.Buffered(1))


def kernel(x_prompt, x_sample, c_prompt, c_sample, state_conv, state_rglru, w_ada, b_ada, g_mix, g_ffn, w_in, conv_w, conv_b, w_ra, b_ra, w_ri, b_ri, lru_l, ln_v_g, ln_v_b, w_s, b_s, g_rg, g_sg, w_out, w_ff1, w_ff2, g_final):
    batch, seq, _ = x_prompt.shape
    dec_batch, dec_seq, _ = x_sample.shape
    assert w_ada.shape[0] == 1, "single-layer step"
    assert seq % PROMPT_TILE == 0 and PROMPT_TILE % CHUNK == 0 and dec_batch % SAMPLE_NB == 0
    assert batch * (seq // PROMPT_TILE) >= 2, "the prompt pipeline needs at least two tiles"
    assert dec_seq <= CHUNK

    n_c = dec_batch + batch
    assert dec_batch % SUBLANES == 0 and dec_batch % batch == 0
    steps = PREP_STEPS

    def row_block(w):
        return pl.BlockSpec((w.shape[0] // steps, w.shape[1]), lambda n: (n, 0))

    def whole(a, lead=0):
        nd = a.ndim
        return pl.BlockSpec((None,) * lead + a.shape[lead:], lambda n: (0,) * nd)

    mats = (w_in[0], w_out[0], w_ff1[0], w_ff2[0])
    gains = (g_mix, g_ffn, g_final.reshape(1, D_MODEL))
    rows512 = (conv_b, lru_l, ln_v_g, ln_v_b, g_rg, g_sg)
    small = gains + (conv_w, rows512[0], b_ra, b_ri) + rows512[1:] + (w_ra, w_ri, w_s, b_s)
    small_specs = [whole(a) for a in gains] + [whole(conv_w, 1), whole(conv_b), whole(b_ra, 1), whole(b_ri, 1)] + \
                  [whole(a) for a in rows512[1:]] + [whole(w_ra, 1), whole(w_ri, 1), whole(w_s, 1), whole(b_s, 1)]
    packed_shapes = ((3, D_MODEL), (12, D_RG), (GATE_GROUPS, GATE_W, 2 * GATE_W), (SG_HEADS // 2, CHUNK, 2 * CHUNK),
                     (CHUNK, D_SG), (SUBLANES, SUBLANES, D_SG), (SUBLANES, D_SG))
    packed_dtypes = (F32, F32, BF16, BF16, F32, F32, F32)
    packed_out = tuple(jax.ShapeDtypeStruct(s, d) for s, d in zip(packed_shapes, packed_dtypes))
    ada, w_in_b, w_out_b, w_ff1_b, w_ff2_b, v1024, v512, wg, wsp, bias_s, w8, b8 = pl.pallas_call(
        _prep_kernel,
        out_shape=(jax.ShapeDtypeStruct((n_c, 6 * D_MODEL), F32),) +
                  tuple(jax.ShapeDtypeStruct(w.shape, BF16) for w in mats) + packed_out,
        grid=(steps,),
        in_specs=[pl.BlockSpec((dec_batch, D_MODEL), lambda n: (0, 0)),
                  pl.BlockSpec((batch, D_MODEL), lambda n: (0, 0)),
                  pl.BlockSpec((D_MODEL, 6 * D_MODEL // steps), lambda n: (0, n)),
                  pl.BlockSpec((1, 6 * D_MODEL // steps), lambda n: (0, n))] + [row_block(w) for w in mats] +
                 small_specs,
        out_specs=(pl.BlockSpec((n_c, 6 * D_MODEL // steps), lambda n: (0, n)),) +
                  tuple(row_block(w) for w in mats) + tuple(whole(o) for o in packed_out),
        scratch_shapes=[pltpu.VMEM(packed_shapes[2], F32)],
        compiler_params=pltpu.CompilerParams(dimension_semantics=("arbitrary",),
                                             vmem_limit_bytes=VMEM_LIMIT_BYTES),
        name="prep",
    )(c_sample, c_prompt, w_ada[0], b_ada, *mats, *small)

    weights = (w_in_b, wg)
    weight_specs = [_resident(w_in_b.shape), _resident(wg.shape)]
    tail_weights = (w_out_b, w_ff1_b, w_ff2_b)
    cparams = functools.partial(pltpu.CompilerParams, vmem_limit_bytes=VMEM_LIMIT_BYTES)

    tt = PROMPT_TILE
    nt = seq // tt
    n_tiles = batch * nt

    def cur(s):
        return jnp.minimum(s, n_tiles - 1)

    def prev(s):
        return jnp.maximum(s - 1, 0)

    y_p, conv_p, lru_p, vrows_p = pl.pallas_call(
        functools.partial(_prompt_kernel, tiles_per_seq=nt),
        out_shape=(jax.ShapeDtypeStruct((batch, seq, D_MODEL), F32),
                   jax.ShapeDtypeStruct((CONV_W - 1, batch, D_RG), F32),
                   jax.ShapeDtypeStruct((batch, D_RG), F32),
                   jax.ShapeDtypeStruct((batch, CHUNK, D_SG), F32)),
        grid=(n_tiles + 1,),
        in_specs=[pl.BlockSpec((None, tt, D_MODEL), lambda s: (cur(s) // nt, cur(s) % nt, 0)),
                  pl.BlockSpec((None, tt, D_MODEL), lambda s: (prev(s) // nt, prev(s) % nt, 0)),
                  pl.BlockSpec((batch, 6 * D_MODEL), lambda s: (dec_batch // batch, 0),
                               pipeline_mode=pl.Buffered(1)),
                  _resident(v1024.shape), _resident(v512.shape)] + weight_specs +
                 [_resident(wsp.shape), _resident(bias_s.shape)] + [pl.BlockSpec(memory_space=pl.ANY)] * 3,
        out_specs=(pl.BlockSpec((None, tt, D_MODEL), lambda s: (prev(s) // nt, prev(s) % nt, 0)),
                   pl.BlockSpec((CONV_W - 1, batch, D_RG), lambda s: (0, 0, 0)),
                   pl.BlockSpec((batch, D_RG), lambda s: (0, 0)),
                   pl.BlockSpec((batch, CHUNK, D_SG), lambda s: (0, 0, 0))),
        scratch_shapes=[pltpu.VMEM((D_RG // LANES, tt, LANES), F32), pltpu.VMEM((D_RG // LANES, tt, LANES), F32),
                        pltpu.VMEM((SUBLANES, D_RG), F32), pltpu.VMEM((SUBLANES, D_RG), F32),
                        pltpu.VMEM((tt, D_MODEL), BF16)] +
                       [pltpu.VMEM(w.shape, BF16) for w in tail_weights] + [pltpu.SemaphoreType.DMA((3,))],
        compiler_params=cparams(dimension_semantics=("arbitrary",)),
        name="prompt_trunk",
    )(x_prompt, x_prompt, ada, v1024, v512, *weights, wsp, bias_s, *tail_weights)

    nb = SAMPLE_NB
    n_cs = CONV_W - 1
    assert dec_seq == SUBLANES, "one decode sequence per 8-row group"
    rows = nb * dec_seq

    y_s, conv_s, lru_s, vrows_s = pl.pallas_call(
        _sample_kernel,
        out_shape=(jax.ShapeDtypeStruct((dec_batch * dec_seq, D_MODEL), F32),
                   jax.ShapeDtypeStruct((n_cs, dec_batch, D_RG), F32),
                   jax.ShapeDtypeStruct((dec_batch, D_RG), F32),
                   jax.ShapeDtypeStruct((dec_batch * dec_seq, D_SG), F32)),
        grid=(dec_batch // nb,),
        in_specs=[pl.BlockSpec((rows, D_MODEL), lambda i: (i, 0)),
                  pl.BlockSpec((nb, 6 * D_MODEL), lambda i: (i, 0)),
                  pl.BlockSpec((n_cs, nb, D_RG), lambda i: (0, i, 0)),
                  pl.BlockSpec((nb, D_RG), lambda i: (i, 0)),
                  _resident(v1024.shape), _resident(v512.shape)] + weight_specs +
                 [_resident(w8.shape), _resident(b8.shape)] + [pl.BlockSpec(memory_space=pl.ANY)] * 3,
        out_specs=(pl.BlockSpec((rows, D_MODEL), lambda i: (i, 0)),
                   pl.BlockSpec((n_cs, nb, D_RG), lambda i: (0, i, 0)),
                   pl.BlockSpec((nb, D_RG), lambda i: (i, 0)),
                   pl.BlockSpec((rows, D_SG), lambda i: (i, 0))),
        scratch_shapes=[pltpu.VMEM((D_MODEL // LANES, rows, LANES), F32),
                        pltpu.VMEM((D_MODEL // LANES, rows, LANES), F32),
                        pltpu.VMEM((D_SG // LANES, rows, LANES), F32)] +
                       [pltpu.VMEM(w.shape, BF16) for w in tail_weights] + [pltpu.SemaphoreType.DMA((3,))],
        compiler_params=cparams(dimension_semantics=("arbitrary",)),
        name="sample_trunk",
    )(x_sample.reshape(dec_batch * dec_seq, D_MODEL), ada, state_conv[0].transpose(1, 0, 2), state_rglru[0],
      v1024, v512, *weights, w8, b8, *tail_weights)

    return (y_p, y_s.reshape(dec_batch, dec_seq, D_MODEL), conv_p.transpose(1, 0, 2)[None], lru_p[None],
            vrows_p[None], conv_s.transpose(1, 0, 2)[None], lru_s[None],
            vrows_s.reshape(1, dec_batch, dec_seq, D_SG))
```

```python
import functools

import jax
import jax.numpy as jnp
from jax import lax
from jax.experimental import pallas as pl
from jax.experimental.pallas import tpu as pltpu

D_MODEL = 1024
D_RG = 512
D_SG = 512
D_IN = 2 * D_RG + 2 * D_SG
D_FF = 4 * D_MODEL
RG_HEADS = 8
RG_HEAD_DIM = D_RG // RG_HEADS
SG_HEADS = 8
SG_HEAD_DIM = D_SG // SG_HEADS
CHUNK = 128
CONV_W = 4
LRU_C = 8.0
EPS = 1e-6

SUBLANES = 8
LANES = 128
GATE_GROUPS = 2
GATE_W = D_RG // GATE_GROUPS
PROMPT_TILE = 512
SAMPLE_NB = 64
FF_CHUNK = 1024
N_FF_CHUNKS = D_FF // FF_CHUNK
PREP_STEPS = 4
CAST_ROWS = 256
CAST_BUFS = 4
CAST_SEM0 = 3
VMEM_LIMIT_BYTES = 60000 * 1024

F32 = jnp.float32
BF16 = jnp.bfloat16

_V_CONV_W, _V_CONV_B, _V_B_RA, _V_B_RI, _V_LRU_L, _V_LN_G, _V_LN_B, _V_G_RG, _V_G_SG = 0, 4, 5, 6, 7, 8, 9, 10, 11


def _dot(a, b):
    return jnp.dot(a, b, preferred_element_type=F32)


def _rms(x, g):
    ms = jnp.mean(x * x, axis=-1, keepdims=True)
    return (x * lax.rsqrt(ms + EPS)) * g


def _gelu_tanh(x):
    k = 0.7978845608028654
    u = x * (k + (k * 0.044715) * (x * x))
    return (0.5 * x) * (1.0 + jnp.tanh(u))


def _layer_norm(x, g, b):
    mu = jnp.mean(x, axis=-1, keepdims=True)
    xc = x - mu
    var = jnp.mean(xc * xc, axis=-1, keepdims=True)
    return (xc * lax.rsqrt(var + EPS)) * g + b


def _cat(parts, axis=0):
    return parts[0] if len(parts) == 1 else jnp.concatenate(parts, axis=axis)


def _split_mods(ada):
    return [ada[:, k * D_MODEL:(k + 1) * D_MODEL] for k in range(6)]


def _modulated_norm(x_blocks, g, sc, sh):
    gain = g * (1.0 + sc)
    return _cat([(_rms(xb, gain) + sh).astype(BF16) for xb in x_blocks])


def _gate_preacts(xc, wg_ref):
    xcb = xc.astype(BF16)
    return _cat([_dot(xcb[:, g * GATE_W:(g + 1) * GATE_W], wg_ref[g]) for g in range(GATE_GROUPS)], 1)


def _gate_tanh(pre_half, v512):
    r_parts, i_parts = [], []
    for g in range(GATE_GROUPS):
        lo, hi = g * GATE_W, (g + 1) * GATE_W
        ra = pre_half[:, 2 * lo:2 * lo + GATE_W]
        ri = pre_half[:, 2 * lo + GATE_W:2 * hi]
        r_parts.append(jnp.tanh(ra + 0.5 * v512[_V_B_RA:_V_B_RA + 1, lo:hi]))
        i_parts.append(jnp.tanh(ri + 0.5 * v512[_V_B_RI:_V_B_RI + 1, lo:hi]))
    return _cat(r_parts, 1), _cat(i_parts, 1)


def _lru_coeffs(t_r, v512):
    lam = v512[_V_LRU_L:_V_LRU_L + 1, :]
    log_sig = jnp.minimum(lam, 0.0) - jnp.log1p(jnp.exp(-jnp.abs(lam)))
    k = (0.5 * LRU_C) * log_sig
    log_a = k * t_r + k
    a = jnp.exp(log_a)
    z = 1.0 - a * a
    mult = jnp.where(z > 0.0, z * lax.rsqrt(z), 0.0)
    return a, mult


def _tail_thunks(x_blocks, merged_fn, mods, v1024, w_out_ref, w_ff1_ref, w_ff2_ref, store_y):
    _, _, gt_m, sh_f, sc_f, gt_f = mods
    rows = x_blocks[0].shape[0]
    st = {}

    def w_out():
        mm = _dot(merged_fn(), w_out_ref[...])
        st['x1'] = [xb + gt_m * mm[i * rows:(i + 1) * rows] for i, xb in enumerate(x_blocks)]

    def hf():
        st['hf'] = _modulated_norm(st['x1'], v1024[1:2, :], sc_f, sh_f)

    def ff1(c):
        pre = _dot(st['hf'], w_ff1_ref[:, c * FF_CHUNK:(c + 1) * FF_CHUNK])
        h1 = jnp.maximum(pre.astype(BF16), 0.0)
        st['h1_%d' % c] = h1 * h1

    def ff2():
        h1 = _cat([st.pop('h1_%d' % c) for c in range(N_FF_CHUNKS)], 1)
        st['acc'] = _dot(h1, w_ff2_ref[...])

    def final():
        for i, xb in enumerate(st['x1']):
            x2 = xb + gt_f * st['acc'][i * rows:(i + 1) * rows]
            store_y(i, _rms(x2, v1024[2:3, :]))

    thunks = {'w_out': w_out, 'hf': hf, 'ff2': ff2, 'final': final}
    for c in range(N_FF_CHUNKS):
        thunks['ff1_%d' % c] = functools.partial(ff1, c)
    return thunks


TAIL_ORDER = ['w_out', 'hf'] + ['ff1_%d' % c for c in range(N_FF_CHUNKS)] + ['ff2', 'final']


def _run(thunks, order):
    assert sorted(order) == sorted(thunks), (sorted(order), sorted(thunks))
    for name in order:
        thunks[name]()


def _prompt_mix_thunks(x_ref, ada, v1024, v512, w_in_ref, wg_ref, wsp_ref, bias_s_ref,
                       conv_ref, lru_ref, vrows_ref, xperm_ref, hperm_ref, tail_ref, h_ref, merged_ref, j, slot):
    tt = x_ref.shape[0]
    n_sub = SUBLANES
    sub_len = tt // n_sub
    n_slab = D_RG // LANES
    st = {}

    def norm():
        mods = _split_mods(ada)
        st['hm'] = _modulated_norm([x_ref[...]], v1024[0:1, :], mods[1], mods[0])

    def project():
        st['proj'] = proj = _dot(st['hm'], w_in_ref[...])
        xr = proj[:, 0:D_RG]
        st['xr_tail'] = xr[tt - SUBLANES:, :]
        for p in range(n_sub):
            for s in range(n_slab):
                xperm_ref[s, pl.ds(p, sub_len, stride=n_sub), :] = (
                    xr[p * sub_len:(p + 1) * sub_len, s * LANES:(s + 1) * LANES])

    def conv():
        xr_p = _cat([xperm_ref[s] for s in range(n_slab)], 1)
        sub = lax.broadcasted_iota(jnp.int32, (SUBLANES, D_RG), 0)
        prev_tail = tail_ref[...]
        head = []
        for k in range(CONV_W - 1, 0, -1):
            grp = pltpu.roll(xr_p[(sub_len - k) * SUBLANES:(sub_len - k + 1) * SUBLANES, :], 1, 0)
            head.append(jnp.where(sub == 0, prev_tail[SUBLANES - k:SUBLANES - k + 1, :], grp))
        ext = _cat(head + [xr_p])
        xc = v512[_V_CONV_B:_V_CONV_B + 1, :]
        for k in range(CONV_W):
            xc = xc + ext[k * SUBLANES:k * SUBLANES + tt, :] * v512[_V_CONV_W + k:_V_CONV_W + k + 1, :]
        st['xc'] = xc
        tail_ref[...] = st['xr_tail']
        for k in range(CONV_W - 1):
            row = SUBLANES - (CONV_W - 1) + k
            conv_ref[k, pl.ds(slot, 1), :] = st['xr_tail'][row:row + 1, :]

    def gate_mm():
        st['gate_pre'] = _gate_preacts(st['xc'], wg_ref)

    def coefficients():
        t_r, t_i = _gate_tanh(st.pop('gate_pre'), v512)
        a, mult = _lru_coeffs(t_r, v512)
        row = lax.broadcasted_iota(jnp.int32, (tt, D_RG), 0)
        mult = jnp.where(row + j * tt == 0, 1.0, mult)
        st['a'] = a
        st['b'] = mult * (0.5 * t_i + 0.5) * st.pop('xc')

    def recurrence():
        a, b = st['a'], st['b']
        sub = lax.broadcasted_iota(jnp.int32, (SUBLANES, D_RG), 0)
        hs, ps = [], []
        for q in range(sub_len):
            aq = a[q * SUBLANES:(q + 1) * SUBLANES, :]
            bq = b[q * SUBLANES:(q + 1) * SUBLANES, :]
            hs.append(bq if q == 0 else aq * hs[-1] + bq)
            ps.append(aq if q == 0 else aq * ps[-1])
        init = jnp.where(sub == 0, h_ref[SUBLANES - 1:SUBLANES, :], 0.0)
        for p in range(1, n_sub):
            end = hs[-1] + ps[-1] * init
            init = jnp.where(sub == p, pltpu.roll(end, 1, 0), init)
        end = hs[-1] + ps[-1] * init
        h_ref[...] = end
        lru_ref[pl.ds(slot, 1), :] = end[SUBLANES - 1:SUBLANES, :]
        h_perm = _cat([hq + pq * init for hq, pq in zip(hs, ps)])
        for s in range(n_slab):
            hperm_ref[s] = h_perm[:, s * LANES:(s + 1) * LANES]

    def recurrent_out(c):
        r0 = c * CHUNK
        yg = st['proj'][r0:r0 + CHUNK, D_RG:2 * D_RG]
        subs = range(r0 // sub_len, (r0 + CHUNK) // sub_len)
        h = _cat([_cat([hperm_ref[s, pl.ds(p, sub_len, stride=n_sub), :] for p in subs]) for s in range(n_slab)], 1)
        rg_out = h * _gelu_tanh(yg)
        merged_ref[r0:r0 + CHUNK, 0:D_RG] = _rms(rg_out, v512[_V_G_RG:_V_G_RG + 1, :]).astype(BF16)

    def spatial_pre(c):
        r0 = c * CHUNK
        lane = lax.broadcasted_iota(jnp.int32, (CHUNK, 2 * SG_HEAD_DIM), 1)
        v = st['proj'][r0:r0 + CHUNK, 2 * D_RG + D_SG:]
        vn = _layer_norm(v, v512[_V_LN_G:_V_LN_G + 1, :], v512[_V_LN_B:_V_LN_B + 1, :])
        rhs = []
        for p in range(SG_HEADS // 2):
            vp = vn[:, p * 2 * SG_HEAD_DIM:(p + 1) * 2 * SG_HEAD_DIM]
            rhs.append(jnp.concatenate([jnp.where(lane < SG_HEAD_DIM, vp, 0.0),
                                        jnp.where(lane >= SG_HEAD_DIM, vp, 0.0)], axis=0).astype(BF16))
        st['rhs_%d' % c] = rhs
        if r0 + CHUNK == tt:
            vrows_ref[slot] = v

    def spatial_mm(c):
        t_idx = lax.broadcasted_iota(jnp.int32, (CHUNK, 2 * CHUNK), 0)
        s_idx = lax.broadcasted_iota(jnp.int32, (CHUNK, 2 * CHUNK), 1) & (CHUNK - 1)
        outs = []
        for p, rhs in enumerate(st.pop('rhs_%d' % c)):
            ws_pair = jnp.where(s_idx <= t_idx, wsp_ref[p], jnp.zeros((), BF16))
            outs.append(_dot(ws_pair, rhs))
        st['mixed_%d' % c] = _cat(outs, 1)

    def spatial_post(c):
        r0 = c * CHUNK
        u = st['proj'][r0:r0 + CHUNK, 2 * D_RG:2 * D_RG + D_SG]
        sg_out = u * (st.pop('mixed_%d' % c) + bias_s_ref[...])
        merged_ref[r0:r0 + CHUNK, D_RG:] = _rms(sg_out, v512[_V_G_SG:_V_G_SG + 1, :]).astype(BF16)

    thunks = {'norm': norm, 'w_in': project, 'conv': conv, 'gate_mm': gate_mm, 'coef': coefficients,
              'rec': recurrence}
    for c in range(tt // CHUNK):
        thunks['sp_pre_%d' % c] = functools.partial(spatial_pre, c)
        thunks['sp_mm_%d' % c] = functools.partial(spatial_mm, c)
        thunks['sp_post_%d' % c] = functools.partial(spatial_post, c)
        thunks['rg_out_%d' % c] = functools.partial(recurrent_out, c)
    return thunks


N_PROMPT_CHUNKS = PROMPT_TILE // CHUNK
MIX_ORDER = (['norm', 'w_in', 'conv', 'gate_mm', 'coef', 'rec'] +
             [f'{name}_{c}' for c in range(N_PROMPT_CHUNKS) for name in ('sp_pre', 'sp_mm', 'sp_post', 'rg_out')])

assert N_PROMPT_CHUNKS == 4 and N_FF_CHUNKS == 4
INTERLEAVED_ORDER = [
    ('a', 'norm'), ('b', 'w_out'), ('a', 'w_in'), ('b', 'hf'), ('b', 'ff1_0'), ('a', 'conv'),
    ('a', 'sp_pre_0'), ('a', 'sp_pre_1'), ('b', 'ff1_1'), ('a', 'gate_mm'), ('a', 'sp_pre_2'), ('a', 'sp_pre_3'),
    ('b', 'ff1_2'), ('a', 'coef'), ('a', 'sp_mm_0'), ('a', 'sp_mm_1'), ('b', 'ff1_3'), ('a', 'rec'),
    ('a', 'sp_mm_2'), ('a', 'sp_mm_3'), ('a', 'sp_post_0'), ('a', 'sp_post_1'),
    ('a', 'rg_out_0'), ('a', 'rg_out_1'), ('a', 'sp_post_2'), ('a', 'sp_post_3'),
    ('a', 'rg_out_2'), ('a', 'rg_out_3'), ('b', 'ff2'), ('b', 'final'),
]


def _tail_weight_copies(src_refs, dst_refs, sem):
    return [pltpu.make_async_copy(src, dst, sem.at[k]) for k, (src, dst) in enumerate(zip(src_refs, dst_refs))]


def _weight_cast_thunks(f32_hbm_refs, bf16_refs, stage_ref, sem):
    windows = []
    for src, dst in zip(f32_hbm_refs, bf16_refs):
        rows, cols = dst.shape
        for r in range(0, rows, CAST_ROWS):
            for c in range(0, cols, D_MODEL):
                windows.append((src, dst, pl.ds(r, CAST_ROWS), pl.ds(c, D_MODEL)))
    copies = [pltpu.make_async_copy(src.at[rs, cs], stage_ref.at[k % CAST_BUFS], sem.at[CAST_SEM0 + k % CAST_BUFS])
              for k, (src, _, rs, cs) in enumerate(windows)]

    def make(k):
        def run():
            if k == 0:
                for cp in copies[:CAST_BUFS]:
                    cp.start()
            copies[k].wait()
            _, dst, rs, cs = windows[k]
            dst[rs, cs] = stage_ref[k % CAST_BUFS].astype(BF16)
            if k + CAST_BUFS < len(copies):
                copies[k + CAST_BUFS].start()
        return run

    return [make(k) for k in range(len(copies))]


def _prompt_kernel(x_ref, xprev_ref, ada_ref, v1024_ref, v512_ref, w_in_ref, wg_ref, wsp_ref,
                   bias_s_ref, w_out_hbm, w_ff1_hbm, w_ff2_hbm,
                   y_ref, conv_ref, lru_ref, vrows_ref, w_out_o, w_ff1_o, w_ff2_o,
                   xperm_ref, hperm_ref, tail_ref, h_ref, merged_ref, w_out_ref, w_ff1_ref, w_ff2_ref, stage_ref, w_sem,
                   *, tiles_per_seq):
    s = pl.program_id(0)
    n_tiles = pl.num_programs(0) - 1
    j = lax.rem(s, tiles_per_seq)
    seq_cur = jnp.minimum(s, n_tiles - 1) // tiles_per_seq
    seq_prev = jnp.maximum(s - 1, 0) // tiles_per_seq

    @pl.when(j == 0)
    def _():
        tail_ref[...] = jnp.zeros((SUBLANES, D_RG), F32)
        h_ref[...] = jnp.zeros((SUBLANES, D_RG), F32)

    v1024 = v1024_ref[...]
    v512 = v512_ref[...]

    def store_y(_, val):
        y_ref[...] = val

    def stage_a():
        return _prompt_mix_thunks(x_ref, ada_ref[pl.ds(seq_cur, 1), :], v1024, v512, w_in_ref, wg_ref, wsp_ref,
                                  bias_s_ref, conv_ref, lru_ref, vrows_ref, xperm_ref, hperm_ref, tail_ref, h_ref,
                                  merged_ref, j, seq_cur)

    def stage_b():
        return _tail_thunks([xprev_ref[...]], lambda: merged_ref[...], _split_mods(ada_ref[pl.ds(seq_prev, 1), :]),
                            v1024, w_out_ref, w_ff1_ref, w_ff2_ref, store_y)

    tail_refs = (w_out_ref, w_ff1_ref, w_ff2_ref)
    out_copies = _tail_weight_copies(tail_refs, (w_out_o, w_ff1_o, w_ff2_o), w_sem)

    @pl.when(s == 0)
    def _():
        casts = _weight_cast_thunks((w_out_hbm, w_ff1_hbm, w_ff2_hbm), tail_refs, stage_ref, w_sem)
        mix = stage_a()
        per = -(-len(casts) // len(MIX_ORDER))
        for n, name in enumerate(MIX_ORDER):
            for cast in casts[n * per:(n + 1) * per]:
                cast()
            mix[name]()
        for cp in out_copies:
            cp.start()

    @pl.when(jnp.logical_and(s > 0, s < n_tiles))
    def _():
        stages = {'a': stage_a(), 'b': stage_b()}
        assert sorted(n for k, n in INTERLEAVED_ORDER if k == 'a') == sorted(stages['a'])
        assert sorted(n for k, n in INTERLEAVED_ORDER if k == 'b') == sorted(stages['b'])
        for stage, name in INTERLEAVED_ORDER:
            stages[stage][name]()

    @pl.when(s == n_tiles)
    def _():
        _run(stage_b(), TAIL_ORDER)
        for cp in out_copies:
            cp.wait()


def _sample_kernel(x_ref, ada_ref, cs_ref, h0_ref, v1024_ref, v512_ref, w_in_ref, wg_ref, w8_ref, b8_ref,
                   w_out_hbm, w_ff1_hbm, w_ff2_hbm,
                   y_ref, conv_ref, lru_ref, vrows_ref,
                   xslab_ref, yslab_ref, vslab_ref, w_out_ref, w_ff1_ref, w_ff2_ref, w_sem):
    n_t = SUBLANES
    nb = x_ref.shape[0] // n_t
    copies = _tail_weight_copies((w_out_hbm, w_ff1_hbm, w_ff2_hbm), (w_out_ref, w_ff1_ref, w_ff2_ref), w_sem)

    @pl.when(pl.program_id(0) == 0)
    def _():
        for cp in copies:
            cp.start()

    mods = _split_mods(ada_ref[...])
    sh_m, sc_m = mods[0], mods[1]
    v1024 = v1024_ref[...]
    v512 = v512_ref[...]

    def blk(arr, t):
        return arr[t * nb:(t + 1) * nb, :]

    for s in range(D_MODEL // LANES):
        xslab_ref[s] = x_ref[:, s * LANES:(s + 1) * LANES]
    x_blocks = [_cat([xslab_ref[s, pl.ds(t, nb, stride=n_t), :] for s in range(D_MODEL // LANES)], 1)
                for t in range(n_t)]
    hm = _modulated_norm(x_blocks, v1024[0:1, :], sc_m, sh_m)
    proj = _dot(hm, w_in_ref[...])
    xr = proj[:, 0:D_RG]
    yg = proj[:, D_RG:2 * D_RG]
    u = proj[:, 2 * D_RG:2 * D_RG + D_SG]
    v = proj[:, 2 * D_RG + D_SG:]

    xp = [cs_ref[k] for k in range(CONV_W - 1)] + [blk(xr, t) for t in range(n_t)]
    xc_blocks = []
    for t in range(n_t):
        acc = v512[_V_CONV_B:_V_CONV_B + 1, :]
        for k in range(CONV_W):
            acc = acc + xp[t + k] * v512[_V_CONV_W + k:_V_CONV_W + k + 1, :]
        xc_blocks.append(acc)
    for k in range(CONV_W - 1):
        conv_ref[k] = xp[n_t + k]
    xc = _cat(xc_blocks)

    t_r, t_i = _gate_tanh(_gate_preacts(xc, wg_ref), v512)
    a, mult = _lru_coeffs(t_r, v512)
    bterm = mult * (0.5 * t_i + 0.5) * xc
    h = h0_ref[...]
    h_blocks = []
    for t in range(n_t):
        h = blk(a, t) * h + blk(bterm, t)
        h_blocks.append(h)
    lru_ref[...] = h
    rg_out = _cat(h_blocks) * _gelu_tanh(yg)

    vn = _layer_norm(v, v512[_V_LN_G:_V_LN_G + 1, :], v512[_V_LN_B:_V_LN_B + 1, :])
    sg_blocks = []
    for t in range(n_t):
        for s in range(D_SG // LANES):
            vslab_ref[s, pl.ds(t, nb, stride=n_t), :] = blk(v, t)[:, s * LANES:(s + 1) * LANES]
        mixed = b8_ref[t:t + 1, :]
        for s in range(t + 1):
            mixed = mixed + w8_ref[t, s:s + 1, :] * blk(vn, s)
        sg_blocks.append(blk(u, t) * mixed)
    sg_out = _cat(sg_blocks)
    for s in range(D_SG // LANES):
        vrows_ref[:, s * LANES:(s + 1) * LANES] = vslab_ref[s]

    merged = jnp.concatenate(
        [_rms(rg_out, v512[_V_G_RG:_V_G_RG + 1, :]).astype(BF16),
         _rms(sg_out, v512[_V_G_SG:_V_G_SG + 1, :]).astype(BF16)], axis=1)

    def store_y(t, val):
        for s in range(D_MODEL // LANES):
            yslab_ref[s, pl.ds(t, nb, stride=n_t), :] = val[:, s * LANES:(s + 1) * LANES]

    @pl.when(pl.program_id(0) == 0)
    def _():
        for cp in copies:
            cp.wait()

    _run(_tail_thunks(x_blocks, lambda: merged, mods, v1024, w_out_ref, w_ff1_ref, w_ff2_ref, store_y), TAIL_ORDER)
    for s in range(D_MODEL // LANES):
        y_ref[:, s * LANES:(s + 1) * LANES] = yslab_ref[s]


def _pack_params(g_mix_ref, g_ffn_ref, g_final_ref, conv_w_ref, conv_b_ref, b_ra_ref, b_ri_ref, lru_l_ref,
                 ln_v_g_ref, ln_v_b_ref, g_rg_ref, g_sg_ref, w_ra_ref, w_ri_ref, w_s_ref, b_s_ref,
                 v1024_o, v512_o, wg_o, wsp_o, bias_s_o, w8_o, b8_o, wg_f):
    v1024_o[0:1, :] = g_mix_ref[...]
    v1024_o[1:2, :] = g_ffn_ref[...]
    v1024_o[2:3, :] = g_final_ref[...]
    v512_o[_V_CONV_W:_V_CONV_W + CONV_W, :] = conv_w_ref[...]
    for row, ref in ((_V_CONV_B, conv_b_ref), (_V_LRU_L, lru_l_ref), (_V_LN_G, ln_v_g_ref), (_V_LN_B, ln_v_b_ref),
                     (_V_G_RG, g_rg_ref), (_V_G_SG, g_sg_ref)):
        v512_o[row:row + 1, :] = ref[...]
    wg_f[...] = jnp.zeros(wg_f.shape, F32)
    per = RG_HEADS // GATE_GROUPS
    for h in range(RG_HEADS):
        g, lo = h // per, (h % per) * RG_HEAD_DIM
        v512_o[_V_B_RA:_V_B_RA + 1, h * RG_HEAD_DIM:(h + 1) * RG_HEAD_DIM] = b_ra_ref[h:h + 1, :]
        v512_o[_V_B_RI:_V_B_RI + 1, h * RG_HEAD_DIM:(h + 1) * RG_HEAD_DIM] = b_ri_ref[h:h + 1, :]
        wg_f[g, lo:lo + RG_HEAD_DIM, lo:lo + RG_HEAD_DIM] = w_ra_ref[h]
        wg_f[g, lo:lo + RG_HEAD_DIM, GATE_W + lo:GATE_W + lo + RG_HEAD_DIM] = w_ri_ref[h]
    wg_o[...] = (0.5 * wg_f[...]).astype(BF16)
    for h in range(SG_HEADS):
        wsp_o[h // 2, :, (h % 2) * CHUNK:(h % 2 + 1) * CHUNK] = w_s_ref[h].astype(BF16)
    head = lax.broadcasted_iota(jnp.int32, (1, D_SG), 1) // SG_HEAD_DIM
    b_t = b_s_ref[...].T
    bias = jnp.zeros((CHUNK, D_SG), F32)
    w_t = []
    for h in range(SG_HEADS):
        bias = jnp.where(head == h, b_t[:, h:h + 1], bias)
        w_t.append(w_s_ref[h, 0:SUBLANES, :].T[0:SUBLANES, :])
    bias_s_o[...] = bias
    b8_o[...] = bias[0:SUBLANES, :]
    for t in range(SUBLANES):
        acc = jnp.zeros((SUBLANES, D_SG), F32)
        for h in range(SG_HEADS):
            acc = jnp.where(head == h, w_t[h][:, t:t + 1], acc)
        w8_o[t] = acc


def _prep_kernel(cs_ref, cp_ref, w_ada_ref, b_ada_ref, w_in_ref, *rest):
    small, (ada_ref, w_in_o), packed = rest[:16], rest[16:18], rest[18:]
    c = jnp.concatenate([cs_ref[...], cp_ref[...]], axis=0)
    s = (c * jax.nn.sigmoid(c)).astype(BF16)
    ada_ref[...] = _dot(s, w_ada_ref[...].astype(BF16)) + b_ada_ref[...]
    w_in_o[...] = w_in_ref[...].astype(BF16)

    @pl.when(pl.program_id(0) == 0)
    def _():
        _pack_params(*small, *packed)


def _resident(shape):
    nd = len(shape)
    return pl.BlockSpec(shape, lambda *_: (0,) * nd, pipeline_mode=pl.Buffered(1))


def kernel(x_prompt, x_sample, c_prompt, c_sample, state_conv, state_rglru, w_ada, b_ada, g_mix, g_ffn, w_in, conv_w, conv_b, w_ra, b_ra, w_ri, b_ri, lru_l, ln_v_g, ln_v_b, w_s, b_s, g_rg, g_sg, w_out, w_ff1, w_ff2, g_final):
    batch, seq, _ = x_prompt.shape
    dec_batch, dec_seq, _ = x_sample.shape
    assert w_ada.shape[0] == 1, "single-layer step"
    assert seq % PROMPT_TILE == 0 and PROMPT_TILE % CHUNK == 0 and dec_batch % SAMPLE_NB == 0
    assert batch * (seq // PROMPT_TILE) >= 2, "the prompt pipeline needs at least two tiles"
    assert dec_seq <= CHUNK

    n_c = dec_batch + batch
    assert dec_batch % SUBLANES == 0 and dec_batch % batch == 0
    steps = PREP_STEPS

    def row_block(w):
        return pl.BlockSpec((w.shape[0] // steps, w.shape[1]), lambda n: (n, 0))

    def whole(a, lead=0):
        nd = a.ndim
        return pl.BlockSpec((None,) * lead + a.shape[lead:], lambda n: (0,) * nd)

    mats = (w_in[0],)
    gains =(g_mix, g_ffn, g_final.reshape(1, D_MODEL))
    rows512 = (conv_b, lru_l, ln_v_g, ln_v_b, g_rg, g_sg)
    small = gains + (conv_w, rows512[0], b_ra, b_ri) + rows512[1:] + (w_ra, w_ri, w_s, b_s)
    small_specs = [whole(a) for a in gains] + [whole(conv_w, 1), whole(conv_b), whole(b_ra, 1), whole(b_ri, 1)] + \
                  [whole(a) for a in rows512[1:]] + [whole(w_ra, 1), whole(w_ri, 1), whole(w_s, 1), whole(b_s, 1)]
    packed_shapes = ((3, D_MODEL), (12, D_RG), (GATE_GROUPS, GATE_W, 2 * GATE_W), (SG_HEADS // 2, CHUNK, 2 * CHUNK),
                     (CHUNK, D_SG), (SUBLANES, SUBLANES, D_SG), (SUBLANES, D_SG))
    packed_dtypes = (F32, F32, BF16, BF16, F32, F32, F32)
    packed_out = tuple(jax.ShapeDtypeStruct(s, d) for s, d in zip(packed_shapes, packed_dtypes))
    ada, w_in_b, v1024, v512, wg, wsp, bias_s, w8, b8 = pl.pallas_call(
        _prep_kernel,
        out_shape=(jax.ShapeDtypeStruct((n_c, 6 * D_MODEL), F32),) +
                  tuple(jax.ShapeDtypeStruct(w.shape, BF16) for w in mats) + packed_out,
        grid=(steps,),
        in_specs=[pl.BlockSpec((dec_batch, D_MODEL), lambda n: (0, 0)),
                  pl.BlockSpec((batch, D_MODEL), lambda n: (0, 0)),
                  pl.BlockSpec((D_MODEL, 6 * D_MODEL // steps), lambda n: (0, n)),
                  pl.BlockSpec((1, 6 * D_MODEL // steps), lambda n: (0, n))] + [row_block(w) for w in mats] +
                 small_specs,
        out_specs=(pl.BlockSpec((n_c, 6 * D_MODEL // steps), lambda n: (0, n)),) +
                  tuple(row_block(w) for w in mats) + tuple(whole(o) for o in packed_out),
        scratch_shapes=[pltpu.VMEM(packed_shapes[2], F32)],
        compiler_params=pltpu.CompilerParams(dimension_semantics=("arbitrary",),
                                             vmem_limit_bytes=VMEM_LIMIT_BYTES),
        name="prep",
    )(c_sample, c_prompt, w_ada[0], b_ada, *mats, *small)

    weights = (w_in_b, wg)
    weight_specs = [_resident(w_in_b.shape), _resident(wg.shape)]
    tail_f32 = (w_out[0], w_ff1[0], w_ff2[0])
    tail_shapes = tuple(jax.ShapeDtypeStruct(w.shape, BF16) for w in tail_f32)
    cparams = functools.partial(pltpu.CompilerParams, vmem_limit_bytes=VMEM_LIMIT_BYTES)

    tt = PROMPT_TILE
    nt = seq // tt
    n_tiles = batch * nt

    def cur(s):
        return jnp.minimum(s, n_tiles - 1)

    def prev(s):
        return jnp.maximum(s - 1, 0)

    y_p, conv_p, lru_p, vrows_p, *tail_weights = pl.pallas_call(
        functools.partial(_prompt_kernel, tiles_per_seq=nt),
        out_shape=(jax.ShapeDtypeStruct((batch, seq, D_MODEL), F32),
                   jax.ShapeDtypeStruct((CONV_W - 1, batch, D_RG), F32),
                   jax.ShapeDtypeStruct((batch, D_RG), F32),
                   jax.ShapeDtypeStruct((batch, CHUNK, D_SG), F32)) + tail_shapes,
        grid=(n_tiles + 1,),
        in_specs=[pl.BlockSpec((None, tt, D_MODEL), lambda s: (cur(s) // nt, cur(s) % nt, 0)),
                  pl.BlockSpec((None, tt, D_MODEL), lambda s: (prev(s) // nt, prev(s) % nt, 0)),
                  pl.BlockSpec((batch, 6 * D_MODEL), lambda s: (dec_batch // batch, 0),
                               pipeline_mode=pl.Buffered(1)),
                  _resident(v1024.shape), _resident(v512.shape)] + weight_specs +
                 [_resident(wsp.shape), _resident(bias_s.shape)] + [pl.BlockSpec(memory_space=pl.ANY)] * 3,
        out_specs=(pl.BlockSpec((None, tt, D_MODEL), lambda s: (prev(s) // nt, prev(s) % nt, 0)),
                   pl.BlockSpec((CONV_W - 1, batch, D_RG), lambda s: (0, 0, 0)),
                   pl.BlockSpec((batch, D_RG), lambda s: (0, 0)),
                   pl.BlockSpec((batch, CHUNK, D_SG), lambda s: (0, 0, 0))) + (pl.BlockSpec(memory_space=pl.ANY),) * 3,
        scratch_shapes=[pltpu.VMEM((D_RG // LANES, tt, LANES), F32), pltpu.VMEM((D_RG // LANES, tt, LANES), F32),
                        pltpu.VMEM((SUBLANES, D_RG), F32), pltpu.VMEM((SUBLANES, D_RG), F32),
                        pltpu.VMEM((tt, D_MODEL), BF16)] +
                       [pltpu.VMEM(w.shape, BF16) for w in tail_f32] +
                       [pltpu.VMEM((CAST_BUFS, CAST_ROWS, D_MODEL), F32),
                        pltpu.SemaphoreType.DMA((CAST_SEM0 + CAST_BUFS,))],
        compiler_params=cparams(dimension_semantics=("arbitrary",)),
        name="prompt_trunk",
    )(x_prompt, x_prompt, ada, v1024, v512, *weights, wsp, bias_s, *tail_f32)

    nb = SAMPLE_NB
    n_cs = CONV_W - 1
    assert dec_seq == SUBLANES, "one decode sequence per 8-row group"
    rows = nb * dec_seq

    y_s, conv_s, lru_s, vrows_s = pl.pallas_call(
        _sample_kernel,
        out_shape=(jax.ShapeDtypeStruct((dec_batch * dec_seq, D_MODEL), F32),
                   jax.ShapeDtypeStruct((n_cs, dec_batch, D_RG), F32),
                   jax.ShapeDtypeStruct((dec_batch, D_RG), F32),
                   jax.ShapeDtypeStruct((dec_batch * dec_seq, D_SG), F32)),
        grid=(dec_batch // nb,),
        in_specs=[pl.BlockSpec((rows, D_MODEL), lambda i: (i, 0)),
                  pl.BlockSpec((nb, 6 * D_MODEL), lambda i: (i, 0)),
                  pl.BlockSpec((n_cs, nb, D_RG), lambda i: (0, i, 0)),
                  pl.BlockSpec((nb, D_RG), lambda i: (i, 0)),
                  _resident(v1024.shape), _resident(v512.shape)] + weight_specs +
                 [_resident(w8.shape), _resident(b8.shape)] + [pl.BlockSpec(memory_space=pl.ANY)] * 3,
        out_specs=(pl.BlockSpec((rows, D_MODEL), lambda i: (i, 0)),
                   pl.BlockSpec((n_cs, nb, D_RG), lambda i: (0, i, 0)),
                   pl.BlockSpec((nb, D_RG), lambda i: (i, 0)),
                   pl.BlockSpec((rows, D_SG), lambda i: (i, 0))),
        scratch_shapes=[pltpu.VMEM((D_MODEL // LANES, rows, LANES), F32),
                        pltpu.VMEM((D_MODEL // LANES, rows, LANES), F32),
                        pltpu.VMEM((D_SG // LANES, rows, LANES), F32)] +
                       [pltpu.VMEM(w.shape, BF16) for w in tail_weights] + [pltpu.SemaphoreType.DMA((3,))],
        compiler_params=cparams(dimension_semantics=("arbitrary",)),
        name="sample_trunk",
    )(x_sample.reshape(dec_batch * dec_seq, D_MODEL), ada, state_conv[0].transpose(1, 0, 2), state_rglru[0],
      v1024, v512, *weights, w8, b8, *tail_weights)

    return (y_p, y_s.reshape(dec_batch, dec_seq, D_MODEL), conv_p.transpose(1, 0, 2)[None], lru_p[None],
            vrows_p[None], conv_s.transpose(1, 0, 2)[None], lru_s[None],
            vrows_s.reshape(1, dec_batch, dec_seq, D_SG))
```

```python
import functools

import jax
import jax.numpy as jnp
from jax import lax
from jax.experimental import pallas as pl
from jax.experimental.pallas import tpu as pltpu

D_MODEL = 1024
D_RG = 512
D_SG = 512
D_IN = 2 * D_RG + 2 * D_SG
D_FF = 4 * D_MODEL
RG_HEADS = 8
RG_HEAD_DIM = D_RG // RG_HEADS
SG_HEADS = 8
SG_HEAD_DIM = D_SG // SG_HEADS
CHUNK = 128
CONV_W = 4
LRU_C = 8.0
EPS = 1e-6

SUBLANES = 8
LANES = 128
GATE_GROUPS = 2
GATE_W = D_RG // GATE_GROUPS
PROMPT_TILE = 512
SAMPLE_NB = 64
FF_CHUNK = 1024
N_FF_CHUNKS = D_FF // FF_CHUNK
PREP_STEPS = 4
CAST_ROWS = 256
CAST_BUFS = 4
CAST_SEM0 = 3
VMEM_LIMIT_BYTES = 60000 * 1024

F32 = jnp.float32
BF16 = jnp.bfloat16

_V_CONV_W, _V_CONV_B, _V_B_RA, _V_B_RI, _V_LRU_L, _V_LN_G, _V_LN_B, _V_G_RG, _V_G_SG = 0, 4, 5, 6, 7, 8, 9, 10, 11


def _dot(a, b):
    return jnp.dot(a, b, preferred_element_type=F32)


def _rms(x, g):
    ms = jnp.mean(x * x, axis=-1, keepdims=True)
    return (x * lax.rsqrt(ms + EPS)) * g


def _gelu_tanh(x):
    k = 0.7978845608028654
    u = x * (k + (k * 0.044715) * (x * x))
    return (0.5 * x) * (1.0 + jnp.tanh(u))


def _layer_norm(x, g, b):
    mu = jnp.mean(x, axis=-1, keepdims=True)
    xc = x - mu
    var = jnp.mean(xc * xc, axis=-1, keepdims=True)
    return (xc * lax.rsqrt(var + EPS)) * g + b


def _cat(parts, axis=0):
    return parts[0] if len(parts) == 1 else jnp.concatenate(parts, axis=axis)


def _split_mods(ada):
    return [ada[:, k * D_MODEL:(k + 1) * D_MODEL] for k in range(6)]


def _modulated_norm(x_blocks, g, sc, sh):
    gain = g * (1.0 + sc)
    return _cat([(_rms(xb, gain) + sh).astype(BF16) for xb in x_blocks])


def _gate_preacts(xc, wg_ref):
    xcb = xc.astype(BF16)
    return _cat([_dot(xcb[:, g * GATE_W:(g + 1) * GATE_W], wg_ref[g]) for g in range(GATE_GROUPS)], 1)


def _gate_tanh(pre_half, v512):
    r_parts, i_parts = [], []
    for g in range(GATE_GROUPS):
        lo, hi = g * GATE_W, (g + 1) * GATE_W
        ra = pre_half[:, 2 * lo:2 * lo + GATE_W]
        ri = pre_half[:, 2 * lo + GATE_W:2 * hi]
        r_parts.append(jnp.tanh(ra + 0.5 * v512[_V_B_RA:_V_B_RA + 1, lo:hi]))
        i_parts.append(jnp.tanh(ri + 0.5 * v512[_V_B_RI:_V_B_RI + 1, lo:hi]))
    return _cat(r_parts, 1), _cat(i_parts, 1)


def _lru_coeffs(t_r, v512):
    lam = v512[_V_LRU_L:_V_LRU_L + 1, :]
    log_sig = jnp.minimum(lam, 0.0) - jnp.log1p(jnp.exp(-jnp.abs(lam)))
    k = (0.5 * LRU_C) * log_sig
    log_a = k * t_r + k
    a = jnp.exp(log_a)
    z = 1.0 - a * a
    mult = jnp.where(z > 0.0, z * lax.rsqrt(z), 0.0)
    return a, mult


def _tail_thunks(x_blocks, merged_fn, mods, v1024, w_out_ref, w_ff1_ref, w_ff2_ref, store_y):
    _, _, gt_m, sh_f, sc_f, gt_f = mods
    rows = x_blocks[0].shape[0]
    st = {}

    def w_out():
        mm = _dot(merged_fn(), w_out_ref[...])
        st['x1'] = [xb + gt_m * mm[i * rows:(i + 1) * rows] for i, xb in enumerate(x_blocks)]

    def hf():
        st['hf'] = _modulated_norm(st['x1'], v1024[1:2, :], sc_f, sh_f)

    def ff1(c):
        pre = _dot(st['hf'], w_ff1_ref[:, c * FF_CHUNK:(c + 1) * FF_CHUNK])
        h1 = jnp.maximum(pre.astype(BF16), 0.0)
        st['h1_%d' % c] = h1 * h1

    def ff2():
        h1 = _cat([st.pop('h1_%d' % c) for c in range(N_FF_CHUNKS)], 1)
        st['acc'] = _dot(h1, w_ff2_ref[...])

    def final():
        for i, xb in enumerate(st['x1']):
            x2 = xb + gt_f * st['acc'][i * rows:(i + 1) * rows]
            store_y(i, _rms(x2, v1024[2:3, :]))

    thunks = {'w_out': w_out, 'hf': hf, 'ff2': ff2, 'final': final}
    for c in range(N_FF_CHUNKS):
        thunks['ff1_%d' % c] = functools.partial(ff1, c)
    return thunks


TAIL_ORDER = ['w_out', 'hf'] + ['ff1_%d' % c for c in range(N_FF_CHUNKS)] + ['ff2', 'final']


def _run(thunks, order):
    assert sorted(order) == sorted(thunks), (sorted(order), sorted(thunks))
    for name in order:
        thunks[name]()


def _prompt_mix_thunks(x_ref, ada, v1024, v512, w_in_ref, wg_ref, wsp_ref, bias_s_ref,
                       conv_ref, lru_ref, vrows_ref, xperm_ref, hperm_ref, tail_ref, h_ref, merged_ref, j, slot):
    tt = x_ref.shape[0]
    n_sub = SUBLANES
    sub_len = tt // n_sub
    n_slab = D_RG // LANES
    st = {}

    def norm():
        mods = _split_mods(ada)
        st['hm'] = _modulated_norm([x_ref[...]], v1024[0:1, :], mods[1], mods[0])

    def project():
        st['proj'] = proj = _dot(st['hm'], w_in_ref[...])
        xr = proj[:, 0:D_RG]
        st['xr_tail'] = xr[tt - SUBLANES:, :]
        for p in range(n_sub):
            for s in range(n_slab):
                xperm_ref[s, pl.ds(p, sub_len, stride=n_sub), :] = (
                    xr[p * sub_len:(p + 1) * sub_len, s * LANES:(s + 1) * LANES])

    def conv():
        xr_p = _cat([xperm_ref[s] for s in range(n_slab)], 1)
        sub = lax.broadcasted_iota(jnp.int32, (SUBLANES, D_RG), 0)
        prev_tail = tail_ref[...]
        head = []
        for k in range(CONV_W - 1, 0, -1):
            grp = pltpu.roll(xr_p[(sub_len - k) * SUBLANES:(sub_len - k + 1) * SUBLANES, :], 1, 0)
            head.append(jnp.where(sub == 0, prev_tail[SUBLANES - k:SUBLANES - k + 1, :], grp))
        ext = _cat(head + [xr_p])
        xc = v512[_V_CONV_B:_V_CONV_B + 1, :]
        for k in range(CONV_W):
            xc = xc + ext[k * SUBLANES:k * SUBLANES + tt, :] * v512[_V_CONV_W + k:_V_CONV_W + k + 1, :]
        st['xc'] = xc
        tail_ref[...] = st['xr_tail']
        for k in range(CONV_W - 1):
            row = SUBLANES - (CONV_W - 1) + k
            conv_ref[k, pl.ds(slot, 1), :] = st['xr_tail'][row:row + 1, :]

    def gate_mm():
        st['gate_pre'] = _gate_preacts(st['xc'], wg_ref)

    def coefficients():
        t_r, t_i = _gate_tanh(st.pop('gate_pre'), v512)
        a, mult = _lru_coeffs(t_r, v512)
        row = lax.broadcasted_iota(jnp.int32, (tt, D_RG), 0)
        mult = jnp.where(row + j * tt == 0, 1.0, mult)
        st['a'] = a
        st['b'] = mult * (0.5 * t_i + 0.5) * st.pop('xc')

    def recurrence():
        a, b = st['a'], st['b']
        sub = lax.broadcasted_iota(jnp.int32, (SUBLANES, D_RG), 0)
        hs, ps = [], []
        for q in range(sub_len):
            aq = a[q * SUBLANES:(q + 1) * SUBLANES, :]
            bq = b[q * SUBLANES:(q + 1) * SUBLANES, :]
            hs.append(bq if q == 0 else aq * hs[-1] + bq)
            ps.append(aq if q == 0 else aq * ps[-1])
        init = jnp.where(sub == 0, h_ref[SUBLANES - 1:SUBLANES, :], 0.0)
        for p in range(1, n_sub):
            end = hs[-1] + ps[-1] * init
            init = jnp.where(sub == p, pltpu.roll(end, 1, 0), init)
        end = hs[-1] + ps[-1] * init
        h_ref[...] = end
        lru_ref[pl.ds(slot, 1), :] = end[SUBLANES - 1:SUBLANES, :]
        h_perm = _cat([hq + pq * init for hq, pq in zip(hs, ps)])
        for s in range(n_slab):
            hperm_ref[s] = h_perm[:, s * LANES:(s + 1) * LANES]

    def recurrent_out(c):
        r0 = c * CHUNK
        yg = st['proj'][r0:r0 + CHUNK, D_RG:2 * D_RG]
        subs = range(r0 // sub_len, (r0 + CHUNK) // sub_len)
        h = _cat([_cat([hperm_ref[s, pl.ds(p, sub_len, stride=n_sub), :] for p in subs]) for s in range(n_slab)], 1)
        rg_out = h * _gelu_tanh(yg)
        merged_ref[r0:r0 + CHUNK, 0:D_RG] = _rms(rg_out, v512[_V_G_RG:_V_G_RG + 1, :]).astype(BF16)

    def spatial_pre(c):
        r0 = c * CHUNK
        lane = lax.broadcasted_iota(jnp.int32, (CHUNK, 2 * SG_HEAD_DIM), 1)
        v = st['proj'][r0:r0 + CHUNK, 2 * D_RG + D_SG:]
        vn = _layer_norm(v, v512[_V_LN_G:_V_LN_G + 1, :], v512[_V_LN_B:_V_LN_B + 1, :])
        rhs = []
        for p in range(SG_HEADS // 2):
            vp = vn[:, p * 2 * SG_HEAD_DIM:(p + 1) * 2 * SG_HEAD_DIM]
            rhs.append(jnp.concatenate([jnp.where(lane < SG_HEAD_DIM, vp, 0.0),
                                        jnp.where(lane >= SG_HEAD_DIM, vp, 0.0)], axis=0).astype(BF16))
        st['rhs_%d' % c] = rhs
        if r0 + CHUNK == tt:
            vrows_ref[slot] = v

    def spatial_mm(c):
        t_idx = lax.broadcasted_iota(jnp.int32, (CHUNK, 2 * CHUNK), 0)
        s_idx = lax.broadcasted_iota(jnp.int32, (CHUNK, 2 * CHUNK), 1) & (CHUNK - 1)
        outs = []
        for p, rhs in enumerate(st.pop('rhs_%d' % c)):
            ws_pair = jnp.where(s_idx <= t_idx, wsp_ref[p], jnp.zeros((), BF16))
            outs.append(_dot(ws_pair, rhs))
        st['mixed_%d' % c] = _cat(outs, 1)

    def spatial_post(c):
        r0 = c * CHUNK
        u = st['proj'][r0:r0 + CHUNK, 2 * D_RG:2 * D_RG + D_SG]
        sg_out = u * (st.pop('mixed_%d' % c) + bias_s_ref[...])
        merged_ref[r0:r0 + CHUNK, D_RG:] = _rms(sg_out, v512[_V_G_SG:_V_G_SG + 1, :]).astype(BF16)

    thunks = {'norm': norm, 'w_in': project, 'conv': conv, 'gate_mm': gate_mm, 'coef': coefficients,
              'rec': recurrence}
    for c in range(tt // CHUNK):
        thunks['sp_pre_%d' % c] = functools.partial(spatial_pre, c)
        thunks['sp_mm_%d' % c] = functools.partial(spatial_mm, c)
        thunks['sp_post_%d' % c] = functools.partial(spatial_post, c)
        thunks['rg_out_%d' % c] = functools.partial(recurrent_out, c)
    return thunks


N_PROMPT_CHUNKS = PROMPT_TILE // CHUNK
MIX_ORDER = (['norm', 'w_in', 'conv', 'gate_mm', 'coef', 'rec'] +
             [f'{name}_{c}' for c in range(N_PROMPT_CHUNKS) for name in ('sp_pre', 'sp_mm', 'sp_post', 'rg_out')])

assert N_PROMPT_CHUNKS == 4 and N_FF_CHUNKS == 4
INTERLEAVED_ORDER = [
    ('a', 'norm'), ('b', 'w_out'), ('a', 'w_in'), ('b', 'hf'), ('b', 'ff1_0'), ('a', 'conv'),
    ('a', 'sp_pre_0'), ('a', 'sp_pre_1'), ('b', 'ff1_1'), ('a', 'gate_mm'), ('a', 'sp_pre_2'), ('a', 'sp_pre_3'),
    ('b', 'ff1_2'), ('a', 'coef'), ('a', 'sp_mm_0'), ('a', 'sp_mm_1'), ('b', 'ff1_3'), ('a', 'rec'),
    ('a', 'sp_mm_2'), ('a', 'sp_mm_3'), ('a', 'sp_post_0'), ('a', 'sp_post_1'),
    ('a', 'rg_out_0'), ('a', 'rg_out_1'), ('a', 'sp_post_2'), ('a', 'sp_post_3'),
    ('a', 'rg_out_2'), ('a', 'rg_out_3'), ('b', 'ff2'), ('b', 'final'),
]


def _tail_weight_copies(src_refs, dst_refs, sem):
    return [pltpu.make_async_copy(src, dst, sem.at[k]) for k, (src, dst) in enumerate(zip(src_refs, dst_refs))]


def _weight_cast_thunks(f32_hbm_refs, bf16_refs, stage_ref, sem):
    windows = []
    for src, dst in zip(f32_hbm_refs, bf16_refs):
        rows, cols = dst.shape
        for r in range(0, rows, CAST_ROWS):
            for c in range(0, cols, D_MODEL):
                windows.append((src, dst, pl.ds(r, CAST_ROWS), pl.ds(c, D_MODEL)))
    copies = [pltpu.make_async_copy(src.at[rs, cs], stage_ref.at[k % CAST_BUFS], sem.at[CAST_SEM0 + k % CAST_BUFS])
              for k, (src, _, rs, cs) in enumerate(windows)]

    def make(k):
        def run():
            if k == 0:
                for cp in copies[:CAST_BUFS]:
                    cp.start()
            copies[k].wait()
            _, dst, rs, cs = windows[k]
            dst[rs, cs] = stage_ref[k % CAST_BUFS].astype(BF16)
            if k + CAST_BUFS < len(copies):
                copies[k + CAST_BUFS].start()
        return run

    return [make(k) for k in range(len(copies))]


def _prompt_kernel(x_ref, xprev_ref, ada_ref, v1024_ref, v512_ref, w_in_ref, wg_ref, wsp_ref,
                   bias_s_ref, w_out_hbm, w_ff1_hbm, w_ff2_hbm,
                   y_ref, conv_ref, lru_ref, vrows_ref, w_out_o, w_ff1_o, w_ff2_o,
                   xperm_ref, hperm_ref, tail_ref, h_ref, merged_ref, w_out_ref, w_ff1_ref, w_ff2_ref, stage_ref, w_sem,
                   *, tiles_per_seq):
    s = pl.program_id(0)
    n_tiles = pl.num_programs(0) - 1
    j = lax.rem(s, tiles_per_seq)
    seq_cur = jnp.minimum(s, n_tiles - 1) // tiles_per_seq
    seq_prev = jnp.maximum(s - 1, 0) // tiles_per_seq

    @pl.when(j == 0)
    def _():
        tail_ref[...] = jnp.zeros((SUBLANES, D_RG), F32)
        h_ref[...] = jnp.zeros((SUBLANES, D_RG), F32)

    v1024 = v1024_ref[...]
    v512 = v512_ref[...]

    def store_y(_, val):
        y_ref[...] = val

    def stage_a():
        return _prompt_mix_thunks(x_ref, ada_ref[pl.ds(seq_cur, 1), :], v1024, v512, w_in_ref, wg_ref, wsp_ref,
                                  bias_s_ref, conv_ref, lru_ref, vrows_ref, xperm_ref, hperm_ref, tail_ref, h_ref,
                                  merged_ref, j, seq_cur)

    def stage_b():
        return _tail_thunks([xprev_ref[...]], lambda: merged_ref[...], _split_mods(ada_ref[pl.ds(seq_prev, 1), :]),
                            v1024, w_out_ref, w_ff1_ref, w_ff2_ref, store_y)

    tail_refs = (w_out_ref, w_ff1_ref, w_ff2_ref)
    out_copies = _tail_weight_copies(tail_refs, (w_out_o, w_ff1_o, w_ff2_o), w_sem)

    def spread(casts, thunks):
        per = -(-len(casts) // len(thunks))
        for n, thunk in enumerate(thunks):
            for cast in casts[n * per:(n + 1) * per]:
                cast()
            thunk()

    def interleaved(casts):
        stages = {'a': stage_a(), 'b': stage_b()}
        assert sorted(n for k, n in INTERLEAVED_ORDER if k == 'a') == sorted(stages['a'])
        assert sorted(n for k, n in INTERLEAVED_ORDER if k == 'b') == sorted(stages['b'])
        thunks = [stages[stage][name] for stage, name in INTERLEAVED_ORDER]
        cut = INTERLEAVED_ORDER.index(('b', 'ff2'))
        spread(casts, thunks[:cut])
        for thunk in thunks[cut:]:
            thunk()

    @pl.when(s == 0)
    def _():
        mix = stage_a()
        spread(_weight_cast_thunks((w_out_hbm, w_ff1_hbm), tail_refs[:2], stage_ref, w_sem),
               [mix[name] for name in MIX_ORDER])

    @pl.when(s == 1)
    def _():
        interleaved(_weight_cast_thunks((w_ff2_hbm,), tail_refs[2:], stage_ref, w_sem))
        for cp in out_copies:
            cp.start()

    @pl.when(jnp.logical_and(s > 1, s < n_tiles))
    def _():
        interleaved([])

    @pl.when(s == n_tiles)
    def _():
        _run(stage_b(), TAIL_ORDER)
        for cp in out_copies:
            cp.wait()


def _sample_kernel(x_ref, ada_ref, cs_ref, h0_ref, v1024_ref, v512_ref, w_in_ref, wg_ref, w8_ref, b8_ref,
                   w_out_hbm, w_ff1_hbm, w_ff2_hbm,
                   y_ref, conv_ref, lru_ref, vrows_ref,
                   xslab_ref, yslab_ref, vslab_ref, w_out_ref, w_ff1_ref, w_ff2_ref, w_sem):
    n_t = SUBLANES
    nb = x_ref.shape[0] // n_t
    copies = _tail_weight_copies((w_out_hbm, w_ff1_hbm, w_ff2_hbm), (w_out_ref, w_ff1_ref, w_ff2_ref), w_sem)

    @pl.when(pl.program_id(0) == 0)
    def _():
        for cp in copies:
            cp.start()

    mods = _split_mods(ada_ref[...])
    sh_m, sc_m = mods[0], mods[1]
    v1024 = v1024_ref[...]
    v512 = v512_ref[...]

    def blk(arr, t):
        return arr[t * nb:(t + 1) * nb, :]

    for s in range(D_MODEL // LANES):
        xslab_ref[s] = x_ref[:, s * LANES:(s + 1) * LANES]
    x_blocks = [_cat([xslab_ref[s, pl.ds(t, nb, stride=n_t), :] for s in range(D_MODEL // LANES)], 1)
                for t in range(n_t)]
    hm = _modulated_norm(x_blocks, v1024[0:1, :], sc_m, sh_m)
    proj = _dot(hm, w_in_ref[...])
    xr = proj[:, 0:D_RG]
    yg = proj[:, D_RG:2 * D_RG]
    u = proj[:, 2 * D_RG:2 * D_RG + D_SG]
    v = proj[:, 2 * D_RG + D_SG:]

    xp = [cs_ref[k] for k in range(CONV_W - 1)] + [blk(xr, t) for t in range(n_t)]
    xc_blocks = []
    for t in range(n_t):
        acc = v512[_V_CONV_B:_V_CONV_B + 1, :]
        for k in range(CONV_W):
            acc = acc + xp[t + k] * v512[_V_CONV_W + k:_V_CONV_W + k + 1, :]
        xc_blocks.append(acc)
    for k in range(CONV_W - 1):
        conv_ref[k] = xp[n_t + k]
    xc = _cat(xc_blocks)

    t_r, t_i = _gate_tanh(_gate_preacts(xc, wg_ref), v512)
    a, mult = _lru_coeffs(t_r, v512)
    bterm = mult * (0.5 * t_i + 0.5) * xc
    h = h0_ref[...]
    h_blocks = []
    for t in range(n_t):
        h = blk(a, t) * h + blk(bterm, t)
        h_blocks.append(h)
    lru_ref[...] = h
    rg_out = _cat(h_blocks) * _gelu_tanh(yg)

    vn = _layer_norm(v, v512[_V_LN_G:_V_LN_G + 1, :], v512[_V_LN_B:_V_LN_B + 1, :])
    sg_blocks = []
    for t in range(n_t):
        for s in range(D_SG // LANES):
            vslab_ref[s, pl.ds(t, nb, stride=n_t), :] = blk(v, t)[:, s * LANES:(s + 1) * LANES]
        mixed = b8_ref[t:t + 1, :]
        for s in range(t + 1):
            mixed = mixed + w8_ref[t, s:s + 1, :] * blk(vn, s)
        sg_blocks.append(blk(u, t) * mixed)
    sg_out = _cat(sg_blocks)
    for s in range(D_SG // LANES):
        vrows_ref[:, s * LANES:(s + 1) * LANES] = vslab_ref[s]

    merged = jnp.concatenate(
        [_rms(rg_out, v512[_V_G_RG:_V_G_RG + 1, :]).astype(BF16),
         _rms(sg_out, v512[_V_G_SG:_V_G_SG + 1, :]).astype(BF16)], axis=1)

    def store_y(t, val):
        for s in range(D_MODEL // LANES):
            yslab_ref[s, pl.ds(t, nb, stride=n_t), :] = val[:, s * LANES:(s + 1) * LANES]

    @pl.when(pl.program_id(0) == 0)
    def _():
        for cp in copies:
            cp.wait()

    _run(_tail_thunks(x_blocks, lambda: merged, mods, v1024, w_out_ref, w_ff1_ref, w_ff2_ref, store_y), TAIL_ORDER)
    for s in range(D_MODEL // LANES):
        y_ref[:, s * LANES:(s + 1) * LANES] = yslab_ref[s]


def _pack_params(g_mix_ref, g_ffn_ref, g_final_ref, conv_w_ref, conv_b_ref, b_ra_ref, b_ri_ref, lru_l_ref,
                 ln_v_g_ref, ln_v_b_ref, g_rg_ref, g_sg_ref, w_ra_ref, w_ri_ref, w_s_ref, b_s_ref,
                 v1024_o, v512_o, wg_o, wsp_o, bias_s_o, w8_o, b8_o, wg_f):
    v1024_o[0:1, :] = g_mix_ref[...]
    v1024_o[1:2, :] = g_ffn_ref[...]
    v1024_o[2:3, :] = g_final_ref[...]
    v512_o[_V_CONV_W:_V_CONV_W + CONV_W, :] = conv_w_ref[...]
    for row, ref in ((_V_CONV_B, conv_b_ref), (_V_LRU_L, lru_l_ref), (_V_LN_G, ln_v_g_ref), (_V_LN_B, ln_v_b_ref),
                     (_V_G_RG, g_rg_ref), (_V_G_SG, g_sg_ref)):
        v512_o[row:row + 1, :] = ref[...]
    wg_f[...] = jnp.zeros(wg_f.shape, F32)
    per = RG_HEADS // GATE_GROUPS
    for h in range(RG_HEADS):
        g, lo = h // per, (h % per) * RG_HEAD_DIM
        v512_o[_V_B_RA:_V_B_RA + 1, h * RG_HEAD_DIM:(h + 1) * RG_HEAD_DIM] = b_ra_ref[h:h + 1, :]
        v512_o[_V_B_RI:_V_B_RI + 1, h * RG_HEAD_DIM:(h + 1) * RG_HEAD_DIM] = b_ri_ref[h:h + 1, :]
        wg_f[g, lo:lo + RG_HEAD_DIM, lo:lo + RG_HEAD_DIM] = w_ra_ref[h]
        wg_f[g, lo:lo + RG_HEAD_DIM, GATE_W + lo:GATE_W + lo + RG_HEAD_DIM] = w_ri_ref[h]
    wg_o[...] = (0.5 * wg_f[...]).astype(BF16)
    for h in range(SG_HEADS):
        wsp_o[h // 2, :, (h % 2) * CHUNK:(h % 2 + 1) * CHUNK] = w_s_ref[h].astype(BF16)
    head = lax.broadcasted_iota(jnp.int32, (1, D_SG), 1) // SG_HEAD_DIM
    b_t = b_s_ref[...].T
    bias = jnp.zeros((CHUNK, D_SG), F32)
    w_t = []
    for h in range(SG_HEADS):
        bias = jnp.where(head == h, b_t[:, h:h + 1], bias)
        w_t.append(w_s_ref[h, 0:SUBLANES, :].T[0:SUBLANES, :])
    bias_s_o[...] = bias
    b8_o[...] = bias[0:SUBLANES, :]
    for t in range(SUBLANES):
        acc = jnp.zeros((SUBLANES, D_SG), F32)
        for h in range(SG_HEADS):
            acc = jnp.where(head == h, w_t[h][:, t:t + 1], acc)
        w8_o[t] = acc


def _prep_kernel(cs_ref, cp_ref, w_ada_ref, b_ada_ref, w_in_ref, *rest):
    small, (ada_ref, w_in_o), packed = rest[:16], rest[16:18], rest[18:]
    c = jnp.concatenate([cs_ref[...], cp_ref[...]], axis=0)
    s = (c * jax.nn.sigmoid(c)).astype(BF16)
    ada_ref[...] = _dot(s, w_ada_ref[...].astype(BF16)) + b_ada_ref[...]
    w_in_o[...] = w_in_ref[...].astype(BF16)

    @pl.when(pl.program_id(0) == 0)
    def _():
        _pack_params(*small, *packed)


def _resident(shape):
    nd = len(shape)
    return pl.BlockSpec(shape, lambda *_: (0,) * nd, pipeline_mode=pl.Buffered(1))


def kernel(x_prompt, x_sample, c_prompt, c_sample, state_conv, state_rglru, w_ada, b_ada, g_mix, g_ffn, w_in, conv_w, conv_b, w_ra, b_ra, w_ri, b_ri, lru_l, ln_v_g, ln_v_b, w_s, b_s, g_rg, g_sg, w_out, w_ff1, w_ff2, g_final):
    batch, seq, _ = x_prompt.shape
    dec_batch, dec_seq, _ = x_sample.shape
    assert w_ada.shape[0] == 1, "single-layer step"
    assert seq % PROMPT_TILE == 0 and PROMPT_TILE % CHUNK == 0 and dec_batch % SAMPLE_NB == 0
    assert batch * (seq // PROMPT_TILE) >= 2, "the prompt pipeline needs at least two tiles"
    assert dec_seq <= CHUNK

    n_c = dec_batch + batch
    assert dec_batch % SUBLANES == 0 and dec_batch % batch == 0
    steps = PREP_STEPS

    def row_block(w):
        return pl.BlockSpec((w.shape[0] // steps, w.shape[1]), lambda n: (n, 0))

    def whole(a, lead=0):
        nd = a.ndim
        return pl.BlockSpec((None,) * lead + a.shape[lead:], lambda n: (0,) * nd)

    mats = (w_in[0],)
    gains =(g_mix, g_ffn, g_final.reshape(1, D_MODEL))
    rows512 = (conv_b, lru_l, ln_v_g, ln_v_b, g_rg, g_sg)
    small = gains + (conv_w, rows512[0], b_ra, b_ri) + rows512[1:] + (w_ra, w_ri, w_s, b_s)
    small_specs = [whole(a) for a in gains] + [whole(conv_w, 1), whole(conv_b), whole(b_ra, 1), whole(b_ri, 1)] + \
                  [whole(a) for a in rows512[1:]] + [whole(w_ra, 1), whole(w_ri, 1), whole(w_s, 1), whole(b_s, 1)]
    packed_shapes = ((3, D_MODEL), (12, D_RG), (GATE_GROUPS, GATE_W, 2 * GATE_W), (SG_HEADS // 2, CHUNK, 2 * CHUNK),
                     (CHUNK, D_SG), (SUBLANES, SUBLANES, D_SG), (SUBLANES, D_SG))
    packed_dtypes = (F32, F32, BF16, BF16, F32, F32, F32)
    packed_out = tuple(jax.ShapeDtypeStruct(s, d) for s, d in zip(packed_shapes, packed_dtypes))
    ada, w_in_b, v1024, v512, wg, wsp, bias_s, w8, b8 = pl.pallas_call(
        _prep_kernel,
        out_shape=(jax.ShapeDtypeStruct((n_c, 6 * D_MODEL), F32),) +
                  tuple(jax.ShapeDtypeStruct(w.shape, BF16) for w in mats) + packed_out,
        grid=(steps,),
        in_specs=[pl.BlockSpec((dec_batch, D_MODEL), lambda n: (0, 0)),
                  pl.BlockSpec((batch, D_MODEL), lambda n: (0, 0)),
                  pl.BlockSpec((D_MODEL, 6 * D_MODEL // steps), lambda n: (0, n)),
                  pl.BlockSpec((1, 6 * D_MODEL // steps), lambda n: (0, n))] + [row_block(w) for w in mats] +
                 small_specs,
        out_specs=(pl.BlockSpec((n_c, 6 * D_MODEL // steps), lambda n: (0, n)),) +
                  tuple(row_block(w) for w in mats) + tuple(whole(o) for o in packed_out),
        scratch_shapes=[pltpu.VMEM(packed_shapes[2], F32)],
        compiler_params=pltpu.CompilerParams(dimension_semantics=("arbitrary",),
                                             vmem_limit_bytes=VMEM_LIMIT_BYTES),
        name="prep",
    )(c_sample, c_prompt, w_ada[0], b_ada, *mats, *small)

    weights = (w_in_b, wg)
    weight_specs = [_resident(w_in_b.shape), _resident(wg.shape)]
    tail_f32 = (w_out[0], w_ff1[0], w_ff2[0])
    tail_shapes = tuple(jax.ShapeDtypeStruct(w.shape, BF16) for w in tail_f32)
    cparams = functools.partial(pltpu.CompilerParams, vmem_limit_bytes=VMEM_LIMIT_BYTES)

    tt = PROMPT_TILE
    nt = seq // tt
    n_tiles = batch * nt

    def cur(s):
        return jnp.minimum(s, n_tiles - 1)

    def prev(s):
        return jnp.maximum(s - 1, 0)

    y_p, conv_p, lru_p, vrows_p, *tail_weights = pl.pallas_call(
        functools.partial(_prompt_kernel, tiles_per_seq=nt),
        out_shape=(jax.ShapeDtypeStruct((batch, seq, D_MODEL), F32),
                   jax.ShapeDtypeStruct((CONV_W - 1, batch, D_RG), F32),
                   jax.ShapeDtypeStruct((batch, D_RG), F32),
                   jax.ShapeDtypeStruct((batch, CHUNK, D_SG), F32)) + tail_shapes,
        grid=(n_tiles + 1,),
        in_specs=[pl.BlockSpec((None, tt, D_MODEL), lambda s: (cur(s) // nt, cur(s) % nt, 0)),
                  pl.BlockSpec((None, tt, D_MODEL), lambda s: (prev(s) // nt, prev(s) % nt, 0)),
                  pl.BlockSpec((batch, 6 * D_MODEL), lambda s: (dec_batch // batch, 0),
                               pipeline_mode=pl.Buffered(1)),
                  _resident(v1024.shape), _resident(v512.shape)] + weight_specs +
                 [_resident(wsp.shape), _resident(bias_s.shape)] + [pl.BlockSpec(memory_space=pl.ANY)] * 3,
        out_specs=(pl.BlockSpec((None, tt, D_MODEL), lambda s: (prev(s) // nt, prev(s) % nt, 0)),
                   pl.BlockSpec((CONV_W - 1, batch, D_RG), lambda s: (0, 0, 0)),
                   pl.BlockSpec((batch, D_RG), lambda s: (0, 0)),
                   pl.BlockSpec((batch, CHUNK, D_SG), lambda s: (0, 0, 0))) + (pl.BlockSpec(memory_space=pl.ANY),) * 3,
        scratch_shapes=[pltpu.VMEM((D_RG // LANES, tt, LANES), F32), pltpu.VMEM((D_RG // LANES, tt, LANES), F32),
                        pltpu.VMEM((SUBLANES, D_RG), F32), pltpu.VMEM((SUBLANES, D_RG), F32),
                        pltpu.VMEM((tt, D_MODEL), BF16)] +
                       [pltpu.VMEM(w.shape, BF16) for w in tail_f32] +
                       [pltpu.VMEM((CAST_BUFS, CAST_ROWS, D_MODEL), F32),
                        pltpu.SemaphoreType.DMA((CAST_SEM0 + CAST_BUFS,))],
        compiler_params=cparams(dimension_semantics=("arbitrary",)),
        name="prompt_trunk",
    )(x_prompt, x_prompt, ada, v1024, v512, *weights, wsp, bias_s, *tail_f32)

    nb = SAMPLE_NB
    n_cs = CONV_W - 1
    assert dec_seq == SUBLANES, "one decode sequence per 8-row group"
    rows = nb * dec_seq

    y_s, conv_s, lru_s, vrows_s = pl.pallas_call(
        _sample_kernel,
        out_shape=(jax.ShapeDtypeStruct((dec_batch * dec_seq, D_MODEL), F32),
                   jax.ShapeDtypeStruct((n_cs, dec_batch, D_RG), F32),
                   jax.ShapeDtypeStruct((dec_batch, D_RG), F32),
                   jax.ShapeDtypeStruct((dec_batch * dec_seq, D_SG), F32)),
        grid=(dec_batch // nb,),
        in_specs=[pl.BlockSpec((rows, D_MODEL), lambda i: (i, 0)),
                  pl.BlockSpec((nb, 6 * D_MODEL), lambda i: (i, 0)),
                  pl.BlockSpec((n_cs, nb, D_RG), lambda i: (0, i, 0)),
                  pl.BlockSpec((nb, D_RG), lambda i: (i, 0)),
                  _resident(v1024.shape), _resident(v512.shape)] + weight_specs +
                 [_resident(w8.shape), _resident(b8.shape)] + [pl.BlockSpec(memory_space=pl.ANY)] * 3,
        out_specs=(pl.BlockSpec((rows, D_MODEL), lambda i: (i, 0)),
                   pl.BlockSpec((n_cs, nb, D_RG), lambda i: (0, i, 0)),
                   pl.BlockSpec((nb, D_RG), lambda i: (i, 0)),
                   pl.BlockSpec((rows, D_SG), lambda i: (i, 0))),
        scratch_shapes=[pltpu.VMEM((D_MODEL // LANES, rows, LANES), F32),
                        pltpu.VMEM((D_MODEL // LANES, rows, LANES), F32),
                        pltpu.VMEM((D_SG // LANES, rows, LANES), F32)] +
                       [pltpu.VMEM(w.shape, BF16) for w in tail_weights] + [pltpu.SemaphoreType.DMA((3,))],
        compiler_params=cparams(dimension_semantics=("arbitrary",)),
        name="sample_trunk",
    )(x_sample.reshape(dec_batch * dec_seq, D_MODEL), ada, state_conv[0].transpose(1, 0, 2), state_rglru[0],
      v1024, v512, *weights, w8, b8, *tail_weights)

    return (y_p, y_s.reshape(dec_batch, dec_seq, D_MODEL), conv_p.transpose(1, 0, 2)[None], lru_p[None],
            vrows_p[None], conv_s.transpose(1, 0, 2)[None], lru_s[None],
            vrows_s.reshape(1, dec_batch, dec_seq, D_SG))
```

```python
import functools

import jax
import jax.numpy as jnp
from jax import lax
from jax.experimental import pallas as pl
from jax.experimental.pallas import tpu as pltpu

D_MODEL = 1024
D_RG = 512
D_SG = 512
D_IN = 2 * D_RG + 2 * D_SG
D_FF = 4 * D_MODEL
RG_HEADS = 8
RG_HEAD_DIM = D_RG // RG_HEADS
SG_HEADS = 8
SG_HEAD_DIM = D_SG // SG_HEADS
CHUNK = 128
CONV_W = 4
LRU_C = 8.0
EPS = 1e-6

SUBLANES = 8
LANES = 128
GATE_GROUPS = 2
GATE_W = D_RG // GATE_GROUPS
PROMPT_TILE = 512
SAMPLE_NB = 64
FF_CHUNK = 1024
N_FF_CHUNKS = D_FF // FF_CHUNK
PREP_STEPS = 4
CAST_ROWS = 256
CAST_BUFS = 4
CAST_SEM0 = 3
VMEM_LIMIT_BYTES = 60000 * 1024

F32 = jnp.float32
BF16 = jnp.bfloat16

_V_CONV_W, _V_CONV_B, _V_B_RA, _V_B_RI, _V_LRU_L, _V_LN_G, _V_LN_B, _V_G_RG, _V_G_SG = 0, 4, 5, 6, 7, 8, 9, 10, 11


def _dot(a, b):
    return jnp.dot(a, b, preferred_element_type=F32)


def _rms(x, g):
    ms = jnp.mean(x * x, axis=-1, keepdims=True)
    return (x * lax.rsqrt(ms + EPS)) * g


def _gelu_tanh(x):
    k = 0.7978845608028654
    u = x * (k + (k * 0.044715) * (x * x))
    return (0.5 * x) * (1.0 + jnp.tanh(u))


def _layer_norm(x, g, b):
    mu = jnp.mean(x, axis=-1, keepdims=True)
    xc = x - mu
    var = jnp.mean(xc * xc, axis=-1, keepdims=True)
    return (xc * lax.rsqrt(var + EPS)) * g + b


def _cat(parts, axis=0):
    return parts[0] if len(parts) == 1 else jnp.concatenate(parts, axis=axis)


def _split_mods(ada):
    return [ada[:, k * D_MODEL:(k + 1) * D_MODEL] for k in range(6)]


def _modulated_norm(x_blocks, g, sc, sh):
    gain = g * (1.0 + sc)
    return _cat([(_rms(xb, gain) + sh).astype(BF16) for xb in x_blocks])


def _gate_preacts(xc, wg_ref):
    xcb = xc.astype(BF16)
    return _cat([_dot(xcb[:, g * GATE_W:(g + 1) * GATE_W], wg_ref[g]) for g in range(GATE_GROUPS)], 1)


def _gate_tanh(pre_half, v512):
    r_parts, i_parts = [], []
    for g in range(GATE_GROUPS):
        lo, hi = g * GATE_W, (g + 1) * GATE_W
        ra = pre_half[:, 2 * lo:2 * lo + GATE_W]
        ri = pre_half[:, 2 * lo + GATE_W:2 * hi]
        r_parts.append(jnp.tanh(ra + 0.5 * v512[_V_B_RA:_V_B_RA + 1, lo:hi]))
        i_parts.append(jnp.tanh(ri + 0.5 * v512[_V_B_RI:_V_B_RI + 1, lo:hi]))
    return _cat(r_parts, 1), _cat(i_parts, 1)


def _lru_coeffs(t_r, v512):
    lam = v512[_V_LRU_L:_V_LRU_L + 1, :]
    log_sig = jnp.minimum(lam, 0.0) - jnp.log1p(jnp.exp(-jnp.abs(lam)))
    k = (0.5 * LRU_C) * log_sig
    log_a = k * t_r + k
    a = jnp.exp(log_a)
    z = 1.0 - a * a
    mult = jnp.where(z > 0.0, z * lax.rsqrt(z), 0.0)
    return a, mult


def _tail_thunks(x_blocks, merged_fn, mods, v1024, w_out_ref, w_ff1_ref, w_ff2_ref, store_y):
    _, _, gt_m, sh_f, sc_f, gt_f = mods
    rows = x_blocks[0].shape[0]
    st = {}

    def w_out():
        mm = _dot(merged_fn(), w_out_ref[...])
        st['x1'] = [xb + gt_m * mm[i * rows:(i + 1) * rows] for i, xb in enumerate(x_blocks)]

    def hf():
        st['hf'] = _modulated_norm(st['x1'], v1024[1:2, :], sc_f, sh_f)

    def ff1(c):
        pre = _dot(st['hf'], w_ff1_ref[:, c * FF_CHUNK:(c + 1) * FF_CHUNK])
        h1 = jnp.maximum(pre.astype(BF16), 0.0)
        st['h1_%d' % c] = h1 * h1

    def ff2():
        h1 = _cat([st.pop('h1_%d' % c) for c in range(N_FF_CHUNKS)], 1)
        st['acc'] = _dot(h1, w_ff2_ref[...])

    def final():
        for i, xb in enumerate(st['x1']):
            x2 = xb + gt_f * st['acc'][i * rows:(i + 1) * rows]
            store_y(i, _rms(x2, v1024[2:3, :]))

    thunks = {'w_out': w_out, 'hf': hf, 'ff2': ff2, 'final': final}
    for c in range(N_FF_CHUNKS):
        thunks['ff1_%d' % c] = functools.partial(ff1, c)
    return thunks


TAIL_ORDER = ['w_out', 'hf'] + ['ff1_%d' % c for c in range(N_FF_CHUNKS)] + ['ff2', 'final']


def _run(thunks, order):
    assert sorted(order) == sorted(thunks), (sorted(order), sorted(thunks))
    for name in order:
        thunks[name]()


def _store_if(valid, ref, idx, val):
    keep = jnp.zeros(val.shape, jnp.int32) + valid.astype(jnp.int32)
    ref[idx] = jnp.where(keep > 0, val, ref[idx])


def _prompt_mix_thunks(x_ref, ada, v1024, v512, w_in_ref, wg_ref, wsp_ref, bias_s_ref, conv_ref, lru_ref,
                       vrows_ref, xperm_ref, hperm_ref, tail_ref, h_ref, merged_ref, j, slot, valid):
    tt = x_ref.shape[0]
    n_sub = SUBLANES
    sub_len = tt // n_sub
    n_slab = D_RG // LANES
    st = {}

    def norm():
        mods = _split_mods(ada)
        st['hm'] = _modulated_norm([x_ref[...]], v1024[0:1, :], mods[1], mods[0])

    def project():
        st['proj'] = proj = _dot(st['hm'], w_in_ref[...])
        xr = proj[:, 0:D_RG]
        st['xr_tail'] = xr[tt - SUBLANES:, :]
        for p in range(n_sub):
            for s in range(n_slab):
                xperm_ref[s, pl.ds(p, sub_len, stride=n_sub), :] = (
                    xr[p * sub_len:(p + 1) * sub_len, s * LANES:(s + 1) * LANES])

    def conv():
        xr_p = _cat([xperm_ref[s] for s in range(n_slab)], 1)
        sub = lax.broadcasted_iota(jnp.int32, (SUBLANES, D_RG), 0)
        prev_tail = tail_ref[...]
        head = []
        for k in range(CONV_W - 1, 0, -1):
            grp = pltpu.roll(xr_p[(sub_len - k) * SUBLANES:(sub_len - k + 1) * SUBLANES, :], 1, 0)
            head.append(jnp.where(sub == 0, prev_tail[SUBLANES - k:SUBLANES - k + 1, :], grp))
        ext = _cat(head + [xr_p])
        xc = v512[_V_CONV_B:_V_CONV_B + 1, :]
        for k in range(CONV_W):
            xc = xc + ext[k * SUBLANES:k * SUBLANES + tt, :] * v512[_V_CONV_W + k:_V_CONV_W + k + 1, :]
        st['xc'] = xc
        tail_ref[...] = st['xr_tail']
        for k in range(CONV_W - 1):
            row = SUBLANES - (CONV_W - 1) + k
            _store_if(valid, conv_ref, (k, pl.ds(slot, 1), slice(None)), st['xr_tail'][row:row + 1, :])

    def gate_mm():
        st['gate_pre'] = _gate_preacts(st['xc'], wg_ref)

    def coefficients():
        t_r, t_i = _gate_tanh(st.pop('gate_pre'), v512)
        a, mult = _lru_coeffs(t_r, v512)
        row = lax.broadcasted_iota(jnp.int32, (tt, D_RG), 0)
        mult = jnp.where(row + j * tt == 0, 1.0, mult)
        st['a'] = a
        st['b'] = mult * (0.5 * t_i + 0.5) * st.pop('xc')

    def recurrence():
        a, b = st['a'], st['b']
        sub = lax.broadcasted_iota(jnp.int32, (SUBLANES, D_RG), 0)
        hs, ps = [], []
        for q in range(sub_len):
            aq = a[q * SUBLANES:(q + 1) * SUBLANES, :]
            bq = b[q * SUBLANES:(q + 1) * SUBLANES, :]
            hs.append(bq if q == 0 else aq * hs[-1] + bq)
            ps.append(aq if q == 0 else aq * ps[-1])
        init = jnp.where(sub == 0, h_ref[SUBLANES - 1:SUBLANES, :], 0.0)
        for p in range(1, n_sub):
            end = hs[-1] + ps[-1] * init
            init = jnp.where(sub == p, pltpu.roll(end, 1, 0), init)
        end = hs[-1] + ps[-1] * init
        h_ref[...] = end
        _store_if(valid, lru_ref, (pl.ds(slot, 1), slice(None)), end[SUBLANES - 1:SUBLANES, :])
        h_perm = _cat([hq + pq * init for hq, pq in zip(hs, ps)])
        for s in range(n_slab):
            hperm_ref[s] = h_perm[:, s * LANES:(s + 1) * LANES]

    def recurrent_out(c):
        r0 = c * CHUNK
        yg = st['proj'][r0:r0 + CHUNK, D_RG:2 * D_RG]
        subs = range(r0 // sub_len, (r0 + CHUNK) // sub_len)
        h = _cat([_cat([hperm_ref[s, pl.ds(p, sub_len, stride=n_sub), :] for p in subs]) for s in range(n_slab)], 1)
        rg_out = h * _gelu_tanh(yg)
        merged_ref[r0:r0 + CHUNK, 0:D_RG] = _rms(rg_out, v512[_V_G_RG:_V_G_RG + 1, :]).astype(BF16)

    def spatial_pre(c):
        r0 = c * CHUNK
        lane = lax.broadcasted_iota(jnp.int32, (CHUNK, 2 * SG_HEAD_DIM), 1)
        v = st['proj'][r0:r0 + CHUNK, 2 * D_RG + D_SG:]
        vn = _layer_norm(v, v512[_V_LN_G:_V_LN_G + 1, :], v512[_V_LN_B:_V_LN_B + 1, :])
        rhs = []
        for p in range(SG_HEADS // 2):
            vp = vn[:, p * 2 * SG_HEAD_DIM:(p + 1) * 2 * SG_HEAD_DIM]
            rhs.append(jnp.concatenate([jnp.where(lane < SG_HEAD_DIM, vp, 0.0),
                                        jnp.where(lane >= SG_HEAD_DIM, vp, 0.0)], axis=0).astype(BF16))
        st['rhs_%d' % c] = rhs
        if r0 + CHUNK == tt:
            _store_if(valid, vrows_ref, slot, v)

    def spatial_mm(c):
        t_idx = lax.broadcasted_iota(jnp.int32, (CHUNK, 2 * CHUNK), 0)
        s_idx = lax.broadcasted_iota(jnp.int32, (CHUNK, 2 * CHUNK), 1) & (CHUNK - 1)
        outs = []
        for p, rhs in enumerate(st.pop('rhs_%d' % c)):
            ws_pair = jnp.where(s_idx <= t_idx, wsp_ref[p], jnp.zeros((), BF16))
            outs.append(_dot(ws_pair, rhs))
        st['mixed_%d' % c] = _cat(outs, 1)

    def spatial_post(c):
        r0 = c * CHUNK
        u = st['proj'][r0:r0 + CHUNK, 2 * D_RG:2 * D_RG + D_SG]
        sg_out = u * (st.pop('mixed_%d' % c) + bias_s_ref[...])
        merged_ref[r0:r0 + CHUNK, D_RG:] = _rms(sg_out, v512[_V_G_SG:_V_G_SG + 1, :]).astype(BF16)

    thunks = {'norm': norm, 'w_in': project, 'conv': conv, 'gate_mm': gate_mm, 'coef': coefficients,
              'rec': recurrence}
    for c in range(tt // CHUNK):
        thunks['sp_pre_%d' % c] = functools.partial(spatial_pre, c)
        thunks['sp_mm_%d' % c] = functools.partial(spatial_mm, c)
        thunks['sp_post_%d' % c] = functools.partial(spatial_post, c)
        thunks['rg_out_%d' % c] = functools.partial(recurrent_out, c)
    return thunks


N_PROMPT_CHUNKS = PROMPT_TILE // CHUNK
MIX_ORDER = (['norm', 'w_in', 'conv', 'gate_mm', 'coef', 'rec'] +
             [f'{name}_{c}' for c in range(N_PROMPT_CHUNKS) for name in ('sp_pre', 'sp_mm', 'sp_post', 'rg_out')])

assert N_PROMPT_CHUNKS == 4 and N_FF_CHUNKS == 4
INTERLEAVED_ORDER = [
    ('a', 'norm'), ('b', 'w_out'), ('a', 'w_in'), ('b', 'hf'), ('b', 'ff1_0'), ('a', 'conv'),
    ('a', 'sp_pre_0'), ('a', 'sp_pre_1'), ('b', 'ff1_1'), ('a', 'gate_mm'), ('a', 'sp_pre_2'), ('a', 'sp_pre_3'),
    ('b', 'ff1_2'), ('a', 'coef'), ('a', 'sp_mm_0'), ('a', 'sp_mm_1'), ('b', 'ff1_3'), ('a', 'rec'),
    ('a', 'sp_mm_2'), ('a', 'sp_mm_3'), ('a', 'sp_post_0'), ('a', 'sp_post_1'),
    ('a', 'rg_out_0'), ('a', 'rg_out_1'), ('a', 'sp_post_2'), ('a', 'sp_post_3'),
    ('a', 'rg_out_2'), ('a', 'rg_out_3'), ('b', 'ff2'), ('b', 'final'),
]


def _tail_weight_copies(src_refs, dst_refs, sem):
    return [pltpu.make_async_copy(src, dst, sem.at[k]) for k, (src, dst) in enumerate(zip(src_refs, dst_refs))]


def _cast_weight(src_hbm, dst_ref, stage_ref, sem):
    rows, cols = dst_ref.shape
    n_col = cols // D_MODEL
    n = (rows // CAST_ROWS) * n_col
    assert rows % CAST_ROWS == 0 and cols % D_MODEL == 0 and n >= CAST_BUFS

    def window(k):
        r = pl.multiple_of((k // n_col) * CAST_ROWS, CAST_ROWS)
        c = pl.multiple_of((k % n_col) * D_MODEL, D_MODEL)
        return pl.ds(r, CAST_ROWS), pl.ds(c, D_MODEL)

    def copy(k):
        rs, cs = window(k)
        slot = k % CAST_BUFS
        return pltpu.make_async_copy(src_hbm.at[rs, cs], stage_ref.at[slot], sem.at[CAST_SEM0 + slot])

    for k in range(CAST_BUFS):
        copy(k).start()

    def body(k, carry):
        copy(k).wait()
        rs, cs = window(k)
        dst_ref[rs, cs] = stage_ref[k % CAST_BUFS].astype(BF16)

        @pl.when(k + CAST_BUFS < n)
        def _():
            copy(k + CAST_BUFS).start()

        return carry

    lax.fori_loop(0, n, body, 0)


def _prompt_kernel(x_ref, xprev_ref, ada_ref, v1024_ref, v512_ref, w_in_ref, wg_ref, wsp_ref,
                   bias_s_ref, w_out_hbm, w_ff1_hbm, w_ff2_hbm,
                   y_ref, conv_ref, lru_ref, vrows_ref, w_out_o, w_ff1_o, w_ff2_o,
                   xperm_ref, hperm_ref, tail_ref, h_ref, merged_ref, w_out_ref, w_ff1_ref, w_ff2_ref, stage_ref, w_sem,
                   *, tiles_per_seq):
    s = pl.program_id(0)
    n_tiles = pl.num_programs(0) - 1
    j = lax.rem(s, tiles_per_seq)
    seq_cur = jnp.minimum(s, n_tiles - 1) // tiles_per_seq
    seq_prev = jnp.maximum(s - 1, 0) // tiles_per_seq

    tail_refs = (w_out_ref, w_ff1_ref, w_ff2_ref)
    out_copies = _tail_weight_copies(tail_refs, (w_out_o, w_ff1_o, w_ff2_o), w_sem)

    @pl.when(s == 0)
    def _():
        merged_ref[...] = jnp.zeros(merged_ref.shape, BF16)
        conv_ref[...] = jnp.zeros(conv_ref.shape, F32)
        lru_ref[...] = jnp.zeros(lru_ref.shape, F32)
        vrows_ref[...] = jnp.zeros(vrows_ref.shape, F32)
        for src, dst in zip((w_out_hbm, w_ff1_hbm, w_ff2_hbm), tail_refs):
            _cast_weight(src, dst, stage_ref, w_sem)
        for cp in out_copies:
            cp.start()

    @pl.when(j == 0)
    def _():
        tail_ref[...] = jnp.zeros((SUBLANES, D_RG), F32)
        h_ref[...] = jnp.zeros((SUBLANES, D_RG), F32)

    v1024 = v1024_ref[...]
    v512 = v512_ref[...]

    def store_y(_, val):
        y_ref[...] = val

    def stage_a():
        return _prompt_mix_thunks(x_ref, ada_ref[pl.ds(seq_cur, 1), :], v1024, v512, w_in_ref, wg_ref, wsp_ref,
                                  bias_s_ref, conv_ref, lru_ref, vrows_ref, xperm_ref, hperm_ref, tail_ref, h_ref,
                                  merged_ref, j, seq_cur, s < n_tiles)

    def stage_b():
        return _tail_thunks([xprev_ref[...]], lambda: merged_ref[...], _split_mods(ada_ref[pl.ds(seq_prev, 1), :]),
                            v1024, w_out_ref, w_ff1_ref, w_ff2_ref, store_y)

    stages = {'a': stage_a(), 'b': stage_b()}
    assert sorted(n for k, n in INTERLEAVED_ORDER if k == 'a') == sorted(stages['a'])
    assert sorted(n for k, n in INTERLEAVED_ORDER if k == 'b') == sorted(stages['b'])
    for stage, name in INTERLEAVED_ORDER:
        stages[stage][name]()

    @pl.when(s == n_tiles)
    def _():
        for cp in out_copies:
            cp.wait()


def _sample_kernel(x_ref, ada_ref, cs_ref, h0_ref, v1024_ref, v512_ref, w_in_ref, wg_ref, w8_ref, b8_ref,
                   w_out_hbm, w_ff1_hbm, w_ff2_hbm,
                   y_ref, conv_ref, lru_ref, vrows_ref,
                   xslab_ref, yslab_ref, vslab_ref, w_out_ref, w_ff1_ref, w_ff2_ref, w_sem):
    n_t = SUBLANES
    nb = x_ref.shape[0] // n_t
    copies = _tail_weight_copies((w_out_hbm, w_ff1_hbm, w_ff2_hbm), (w_out_ref, w_ff1_ref, w_ff2_ref), w_sem)

    @pl.when(pl.program_id(0) == 0)
    def _():
        for cp in copies:
            cp.start()

    mods = _split_mods(ada_ref[...])
    sh_m, sc_m = mods[0], mods[1]
    v1024 = v1024_ref[...]
    v512 = v512_ref[...]

    def blk(arr, t):
        return arr[t * nb:(t + 1) * nb, :]

    for s in range(D_MODEL // LANES):
        xslab_ref[s] = x_ref[:, s * LANES:(s + 1) * LANES]
    x_blocks = [_cat([xslab_ref[s, pl.ds(t, nb, stride=n_t), :] for s in range(D_MODEL // LANES)], 1)
                for t in range(n_t)]
    hm = _modulated_norm(x_blocks, v1024[0:1, :], sc_m, sh_m)
    proj = _dot(hm, w_in_ref[...])
    xr = proj[:, 0:D_RG]
    yg = proj[:, D_RG:2 * D_RG]
    u = proj[:, 2 * D_RG:2 * D_RG + D_SG]
    v = proj[:, 2 * D_RG + D_SG:]

    xp = [cs_ref[k] for k in range(CONV_W - 1)] + [blk(xr, t) for t in range(n_t)]
    xc_blocks = []
    for t in range(n_t):
        acc = v512[_V_CONV_B:_V_CONV_B + 1, :]
        for k in range(CONV_W):
            acc = acc + xp[t + k] * v512[_V_CONV_W + k:_V_CONV_W + k + 1, :]
        xc_blocks.append(acc)
    for k in range(CONV_W - 1):
        conv_ref[k] = xp[n_t + k]
    xc = _cat(xc_blocks)

    t_r, t_i = _gate_tanh(_gate_preacts(xc, wg_ref), v512)
    a, mult = _lru_coeffs(t_r, v512)
    bterm = mult * (0.5 * t_i + 0.5) * xc
    h = h0_ref[...]
    h_blocks = []
    for t in range(n_t):
        h = blk(a, t) * h + blk(bterm, t)
        h_blocks.append(h)
    lru_ref[...] = h
    rg_out = _cat(h_blocks) * _gelu_tanh(yg)

    vn = _layer_norm(v, v512[_V_LN_G:_V_LN_G + 1, :], v512[_V_LN_B:_V_LN_B + 1, :])
    sg_blocks = []
    for t in range(n_t):
        for s in range(D_SG // LANES):
            vslab_ref[s, pl.ds(t, nb, stride=n_t), :] = blk(v, t)[:, s * LANES:(s + 1) * LANES]
        mixed = b8_ref[t:t + 1, :]
        for s in range(t + 1):
            mixed = mixed + w8_ref[t, s:s + 1, :] * blk(vn, s)
        sg_blocks.append(blk(u, t) * mixed)
    sg_out = _cat(sg_blocks)
    for s in range(D_SG // LANES):
        vrows_ref[:, s * LANES:(s + 1) * LANES] = vslab_ref[s]

    merged = jnp.concatenate(
        [_rms(rg_out, v512[_V_G_RG:_V_G_RG + 1, :]).astype(BF16),
         _rms(sg_out, v512[_V_G_SG:_V_G_SG + 1, :]).astype(BF16)], axis=1)

    def store_y(t, val):
        for s in range(D_MODEL // LANES):
            yslab_ref[s, pl.ds(t, nb, stride=n_t), :] = val[:, s * LANES:(s + 1) * LANES]

    @pl.when(pl.program_id(0) == 0)
    def _():
        for cp in copies:
            cp.wait()

    _run(_tail_thunks(x_blocks, lambda: merged, mods, v1024, w_out_ref, w_ff1_ref, w_ff2_ref, store_y), TAIL_ORDER)
    for s in range(D_MODEL // LANES):
        y_ref[:, s * LANES:(s + 1) * LANES] = yslab_ref[s]


def _pack_params(g_mix_ref, g_ffn_ref, g_final_ref, conv_w_ref, conv_b_ref, b_ra_ref, b_ri_ref, lru_l_ref,
                 ln_v_g_ref, ln_v_b_ref, g_rg_ref, g_sg_ref, w_ra_ref, w_ri_ref, w_s_ref, b_s_ref,
                 v1024_o, v512_o, wg_o, wsp_o, bias_s_o, w8_o, b8_o, wg_f):
    v1024_o[0:1, :] = g_mix_ref[...]
    v1024_o[1:2, :] = g_ffn_ref[...]
    v1024_o[2:3, :] = g_final_ref[...]
    v512_o[_V_CONV_W:_V_CONV_W + CONV_W, :] = conv_w_ref[...]
    for row, ref in ((_V_CONV_B, conv_b_ref), (_V_LRU_L, lru_l_ref), (_V_LN_G, ln_v_g_ref), (_V_LN_B, ln_v_b_ref),
                     (_V_G_RG, g_rg_ref), (_V_G_SG, g_sg_ref)):
        v512_o[row:row + 1, :] = ref[...]
    wg_f[...] = jnp.zeros(wg_f.shape, F32)
    per = RG_HEADS // GATE_GROUPS
    for h in range(RG_HEADS):
        g, lo = h // per, (h % per) * RG_HEAD_DIM
        v512_o[_V_B_RA:_V_B_RA + 1, h * RG_HEAD_DIM:(h + 1) * RG_HEAD_DIM] = b_ra_ref[h:h + 1, :]
        v512_o[_V_B_RI:_V_B_RI + 1, h * RG_HEAD_DIM:(h + 1) * RG_HEAD_DIM] = b_ri_ref[h:h + 1, :]
        wg_f[g, lo:lo + RG_HEAD_DIM, lo:lo + RG_HEAD_DIM] = w_ra_ref[h]
        wg_f[g, lo:lo + RG_HEAD_DIM, GATE_W + lo:GATE_W + lo + RG_HEAD_DIM] = w_ri_ref[h]
    wg_o[...] = (0.5 * wg_f[...]).astype(BF16)
    for h in range(SG_HEADS):
        wsp_o[h // 2, :, (h % 2) * CHUNK:(h % 2 + 1) * CHUNK] = w_s_ref[h].astype(BF16)
    head = lax.broadcasted_iota(jnp.int32, (1, D_SG), 1) // SG_HEAD_DIM
    b_t = b_s_ref[...].T
    bias = jnp.zeros((CHUNK, D_SG), F32)
    w_t = []
    for h in range(SG_HEADS):
        bias = jnp.where(head == h, b_t[:, h:h + 1], bias)
        w_t.append(w_s_ref[h, 0:SUBLANES, :].T[0:SUBLANES, :])
    bias_s_o[...] = bias
    b8_o[...] = bias[0:SUBLANES, :]
    for t in range(SUBLANES):
        acc = jnp.zeros((SUBLANES, D_SG), F32)
        for h in range(SG_HEADS):
            acc = jnp.where(head == h, w_t[h][:, t:t + 1], acc)
        w8_o[t] = acc


def _prep_kernel(cs_ref, cp_ref, w_ada_ref, b_ada_ref, w_in_ref, *rest):
    small, (ada_ref, w_in_o), packed = rest[:16], rest[16:18], rest[18:]
    c = jnp.concatenate([cs_ref[...], cp_ref[...]], axis=0)
    s = (c * jax.nn.sigmoid(c)).astype(BF16)
    ada_ref[...] = _dot(s, w_ada_ref[...].astype(BF16)) + b_ada_ref[...]
    w_in_o[...] = w_in_ref[...].astype(BF16)

    @pl.when(pl.program_id(0) == 0)
    def _():
        _pack_params(*small, *packed)


def _resident(shape):
    nd = len(shape)
    return pl.BlockSpec(shape, lambda *_: (0,) * nd, pipeline_mode=pl.Buffered(1))


def kernel(x_prompt, x_sample, c_prompt, c_sample, state_conv, state_rglru, w_ada, b_ada, g_mix, g_ffn, w_in, conv_w, conv_b, w_ra, b_ra, w_ri, b_ri, lru_l, ln_v_g, ln_v_b, w_s, b_s, g_rg, g_sg, w_out, w_ff1, w_ff2, g_final):
    batch, seq, _ = x_prompt.shape
    dec_batch, dec_seq, _ = x_sample.shape
    assert w_ada.shape[0] == 1, "single-layer step"
    assert seq % PROMPT_TILE == 0 and PROMPT_TILE % CHUNK == 0 and dec_batch % SAMPLE_NB == 0
    assert batch * (seq // PROMPT_TILE) >= 2, "the prompt pipeline needs at least two tiles"
    assert dec_seq <= CHUNK

    n_c = dec_batch + batch
    assert dec_batch % SUBLANES == 0 and dec_batch % batch == 0
    steps = PREP_STEPS

    def row_block(w):
        return pl.BlockSpec((w.shape[0] // steps, w.shape[1]), lambda n: (n, 0))

    def whole(a, lead=0):
        nd = a.ndim
        return pl.BlockSpec((None,) * lead + a.shape[lead:], lambda n: (0,) * nd)

    mats = (w_in[0],)
    gains =(g_mix, g_ffn, g_final.reshape(1, D_MODEL))
    rows512 = (conv_b, lru_l, ln_v_g, ln_v_b, g_rg, g_sg)
    small = gains + (conv_w, rows512[0], b_ra, b_ri) + rows512[1:] + (w_ra, w_ri, w_s, b_s)
    small_specs = [whole(a) for a in gains] + [whole(conv_w, 1), whole(conv_b), whole(b_ra, 1), whole(b_ri, 1)] + \
                  [whole(a) for a in rows512[1:]] + [whole(w_ra, 1), whole(w_ri, 1), whole(w_s, 1), whole(b_s, 1)]
    packed_shapes = ((3, D_MODEL), (12, D_RG), (GATE_GROUPS, GATE_W, 2 * GATE_W), (SG_HEADS // 2, CHUNK, 2 * CHUNK),
                     (CHUNK, D_SG), (SUBLANES, SUBLANES, D_SG), (SUBLANES, D_SG))
    packed_dtypes = (F32, F32, BF16, BF16, F32, F32, F32)
    packed_out = tuple(jax.ShapeDtypeStruct(s, d) for s, d in zip(packed_shapes, packed_dtypes))
    ada, w_in_b, v1024, v512, wg, wsp, bias_s, w8, b8 = pl.pallas_call(
        _prep_kernel,
        out_shape=(jax.ShapeDtypeStruct((n_c, 6 * D_MODEL), F32),) +
                  tuple(jax.ShapeDtypeStruct(w.shape, BF16) for w in mats) + packed_out,
        grid=(steps,),
        in_specs=[pl.BlockSpec((dec_batch, D_MODEL), lambda n: (0, 0)),
                  pl.BlockSpec((batch, D_MODEL), lambda n: (0, 0)),
                  pl.BlockSpec((D_MODEL, 6 * D_MODEL // steps), lambda n: (0, n)),
                  pl.BlockSpec((1, 6 * D_MODEL // steps), lambda n: (0, n))] + [row_block(w) for w in mats] +
                 small_specs,
        out_specs=(pl.BlockSpec((n_c, 6 * D_MODEL // steps), lambda n: (0, n)),) +
                  tuple(row_block(w) for w in mats) + tuple(whole(o) for o in packed_out),
        scratch_shapes=[pltpu.VMEM(packed_shapes[2], F32)],
        compiler_params=pltpu.CompilerParams(dimension_semantics=("arbitrary",),
                                             vmem_limit_bytes=VMEM_LIMIT_BYTES),
        name="prep",
    )(c_sample, c_prompt, w_ada[0], b_ada, *mats, *small)

    weights = (w_in_b, wg)
    weight_specs = [_resident(w_in_b.shape), _resident(wg.shape)]
    tail_f32 = (w_out[0], w_ff1[0], w_ff2[0])
    tail_shapes = tuple(jax.ShapeDtypeStruct(w.shape, BF16) for w in tail_f32)
    cparams = functools.partial(pltpu.CompilerParams, vmem_limit_bytes=VMEM_LIMIT_BYTES)

    tt = PROMPT_TILE
    nt = seq // tt
    n_tiles = batch * nt

    def cur(s):
        return jnp.minimum(s, n_tiles - 1)

    def prev(s):
        return jnp.maximum(s - 1, 0)

    y_p, conv_p, lru_p, vrows_p, *tail_weights = pl.pallas_call(
        functools.partial(_prompt_kernel, tiles_per_seq=nt),
        out_shape=(jax.ShapeDtypeStruct((batch, seq, D_MODEL), F32),
                   jax.ShapeDtypeStruct((CONV_W - 1, batch, D_RG), F32),
                   jax.ShapeDtypeStruct((batch, D_RG), F32),
                   jax.ShapeDtypeStruct((batch, CHUNK, D_SG), F32)) + tail_shapes,
        grid=(n_tiles + 1,),
        in_specs=[pl.BlockSpec((None, tt, D_MODEL), lambda s: (cur(s) // nt, cur(s) % nt, 0)),
                  pl.BlockSpec((None, tt, D_MODEL), lambda s: (prev(s) // nt, prev(s) % nt, 0)),
                  pl.BlockSpec((batch, 6 * D_MODEL), lambda s: (dec_batch // batch, 0),
                               pipeline_mode=pl.Buffered(1)),
                  _resident(v1024.shape), _resident(v512.shape)] + weight_specs +
                 [_resident(wsp.shape), _resident(bias_s.shape)] + [pl.BlockSpec(memory_space=pl.ANY)] * 3,
        out_specs=(pl.BlockSpec((None, tt, D_MODEL), lambda s: (prev(s) // nt, prev(s) % nt, 0)),
                   pl.BlockSpec((CONV_W - 1, batch, D_RG), lambda s: (0, 0, 0)),
                   pl.BlockSpec((batch, D_RG), lambda s: (0, 0)),
                   pl.BlockSpec((batch, CHUNK, D_SG), lambda s: (0, 0, 0))) + (pl.BlockSpec(memory_space=pl.ANY),) * 3,
        scratch_shapes=[pltpu.VMEM((D_RG // LANES, tt, LANES), F32), pltpu.VMEM((D_RG // LANES, tt, LANES), F32),
                        pltpu.VMEM((SUBLANES, D_RG), F32), pltpu.VMEM((SUBLANES, D_RG), F32),
                        pltpu.VMEM((tt, D_MODEL), BF16)] +
                       [pltpu.VMEM(w.shape, BF16) for w in tail_f32] +
                       [pltpu.VMEM((CAST_BUFS, CAST_ROWS, D_MODEL), F32),
                        pltpu.SemaphoreType.DMA((CAST_SEM0 + CAST_BUFS,))],
        compiler_params=cparams(dimension_semantics=("arbitrary",)),
        name="prompt_trunk",
    )(x_prompt, x_prompt, ada, v1024, v512, *weights, wsp, bias_s, *tail_f32)

    nb = SAMPLE_NB
    n_cs = CONV_W - 1
    assert dec_seq == SUBLANES, "one decode sequence per 8-row group"
    rows = nb * dec_seq

    y_s, conv_s, lru_s, vrows_s = pl.pallas_call(
        _sample_kernel,
        out_shape=(jax.ShapeDtypeStruct((dec_batch * dec_seq, D_MODEL), F32),
                   jax.ShapeDtypeStruct((n_cs, dec_batch, D_RG), F32),
                   jax.ShapeDtypeStruct((dec_batch, D_RG), F32),
                   jax.ShapeDtypeStruct((dec_batch * dec_seq, D_SG), F32)),
        grid=(dec_batch // nb,),
        in_specs=[pl.BlockSpec((rows, D_MODEL), lambda i: (i, 0)),
                  pl.BlockSpec((nb, 6 * D_MODEL), lambda i: (i, 0)),
                  pl.BlockSpec((n_cs, nb, D_RG), lambda i: (0, i, 0)),
                  pl.BlockSpec((nb, D_RG), lambda i: (i, 0)),
                  _resident(v1024.shape), _resident(v512.shape)] + weight_specs +
                 [_resident(w8.shape), _resident(b8.shape)] + [pl.BlockSpec(memory_space=pl.ANY)] * 3,
        out_specs=(pl.BlockSpec((rows, D_MODEL), lambda i: (i, 0)),
                   pl.BlockSpec((n_cs, nb, D_RG), lambda i: (0, i, 0)),
                   pl.BlockSpec((nb, D_RG), lambda i: (i, 0)),
                   pl.BlockSpec((rows, D_SG), lambda i: (i, 0))),
        scratch_shapes=[pltpu.VMEM((D_MODEL // LANES, rows, LANES), F32),
                        pltpu.VMEM((D_MODEL // LANES, rows, LANES), F32),
                        pltpu.VMEM((D_SG // LANES, rows, LANES), F32)] +
                       [pltpu.VMEM(w.shape, BF16) for w in tail_weights] + [pltpu.SemaphoreType.DMA((3,))],
        compiler_params=cparams(dimension_semantics=("arbitrary",)),
        name="sample_trunk",
    )(x_sample.reshape(dec_batch * dec_seq, D_MODEL), ada, state_conv[0].transpose(1, 0, 2), state_rglru[0],
      v1024, v512, *weights, w8, b8, *tail_weights)

    return (y_p, y_s.reshape(dec_batch, dec_seq, D_MODEL), conv_p.transpose(1, 0, 2)[None], lru_p[None],
            vrows_p[None], conv_s.transpose(1, 0, 2)[None], lru_s[None],
            vrows_s.reshape(1, dec_batch, dec_seq, D_SG))
```

```python
import functools

import jax
import jax.numpy as jnp
from jax import lax
from jax.experimental import pallas as pl
from jax.experimental.pallas import tpu as pltpu

D_MODEL = 1024
D_RG = 512
D_SG = 512
D_IN = 2 * D_RG + 2 * D_SG
D_FF = 4 * D_MODEL
RG_HEADS = 8
RG_HEAD_DIM = D_RG // RG_HEADS
SG_HEADS = 8
SG_HEAD_DIM = D_SG // SG_HEADS
CHUNK = 128
CONV_W = 4
LRU_C = 8.0
EPS = 1e-6

SUBLANES = 8
LANES = 128
GATE_GROUPS = 2
GATE_W = D_RG // GATE_GROUPS
PROMPT_TILE = 512
SAMPLE_NB = 64
FF_CHUNK = 1024
N_FF_CHUNKS = D_FF // FF_CHUNK
PREP_STEPS = 4
CAST_ROWS = 64
CAST_BUFS = 16
CAST_SEM0 = 3
VMEM_LIMIT_BYTES = 60000 * 1024

F32 = jnp.float32
BF16 = jnp.bfloat16

_V_CONV_W, _V_CONV_B, _V_B_RA, _V_B_RI, _V_LRU_L, _V_LN_G, _V_LN_B, _V_G_RG, _V_G_SG = 0, 4, 5, 6, 7, 8, 9, 10, 11


def _dot(a, b):
    return jnp.dot(a, b, preferred_element_type=F32)


def _rms(x, g):
    ms = jnp.mean(x * x, axis=-1, keepdims=True)
    return (x * lax.rsqrt(ms + EPS)) * g


def _gelu_tanh(x):
    k = 0.7978845608028654
    u = x * (k + (k * 0.044715) * (x * x))
    return (0.5 * x) * (1.0 + jnp.tanh(u))


def _layer_norm(x, g, b):
    mu = jnp.mean(x, axis=-1, keepdims=True)
    xc = x - mu
    var = jnp.mean(xc * xc, axis=-1, keepdims=True)
    return (xc * lax.rsqrt(var + EPS)) * g + b


def _cat(parts, axis=0):
    return parts[0] if len(parts) == 1 else jnp.concatenate(parts, axis=axis)


def _split_mods(ada):
    return [ada[:, k * D_MODEL:(k + 1) * D_MODEL] for k in range(6)]


def _modulated_norm(x_blocks, g, sc, sh):
    gain = g * (1.0 + sc)
    return _cat([(_rms(xb, gain) + sh).astype(BF16) for xb in x_blocks])


def _gate_preacts(xc, wg_ref):
    xcb = xc.astype(BF16)
    return _cat([_dot(xcb[:, g * GATE_W:(g + 1) * GATE_W], wg_ref[g]) for g in range(GATE_GROUPS)], 1)


def _gate_tanh(pre_half, v512):
    r_parts, i_parts = [], []
    for g in range(GATE_GROUPS):
        lo, hi = g * GATE_W, (g + 1) * GATE_W
        ra = pre_half[:, 2 * lo:2 * lo + GATE_W]
        ri = pre_half[:, 2 * lo + GATE_W:2 * hi]
        r_parts.append(jnp.tanh(ra + 0.5 * v512[_V_B_RA:_V_B_RA + 1, lo:hi]))
        i_parts.append(jnp.tanh(ri + 0.5 * v512[_V_B_RI:_V_B_RI + 1, lo:hi]))
    return _cat(r_parts, 1), _cat(i_parts, 1)


def _lru_coeffs(t_r, v512):
    lam = v512[_V_LRU_L:_V_LRU_L + 1, :]
    log_sig = jnp.minimum(lam, 0.0) - jnp.log1p(jnp.exp(-jnp.abs(lam)))
    k = (0.5 * LRU_C) * log_sig
    log_a = k * t_r + k
    a = jnp.exp(log_a)
    z = 1.0 - a * a
    mult = jnp.where(z > 0.0, z * lax.rsqrt(z), 0.0)
    return a, mult


def _tail_thunks(x_blocks, merged_fn, mods, v1024, w_out_ref, w_ff1_ref, w_ff2_ref, store_y):
    _, _, gt_m, sh_f, sc_f, gt_f = mods
    rows = x_blocks[0].shape[0]
    st = {}

    def w_out():
        mm = _dot(merged_fn(), w_out_ref[...])
        st['x1'] = [xb + gt_m * mm[i * rows:(i + 1) * rows] for i, xb in enumerate(x_blocks)]

    def hf():
        st['hf'] = _modulated_norm(st['x1'], v1024[1:2, :], sc_f, sh_f)

    def ff1(c):
        pre = _dot(st['hf'], w_ff1_ref[:, c * FF_CHUNK:(c + 1) * FF_CHUNK])
        h1 = jnp.maximum(pre.astype(BF16), 0.0)
        st['h1_%d' % c] = h1 * h1

    def ff2():
        h1 = _cat([st.pop('h1_%d' % c) for c in range(N_FF_CHUNKS)], 1)
        st['acc'] = _dot(h1, w_ff2_ref[...])

    def final():
        for i, xb in enumerate(st['x1']):
            x2 = xb + gt_f * st['acc'][i * rows:(i + 1) * rows]
            store_y(i, _rms(x2, v1024[2:3, :]))

    thunks = {'w_out': w_out, 'hf': hf, 'ff2': ff2, 'final': final}
    for c in range(N_FF_CHUNKS):
        thunks['ff1_%d' % c] = functools.partial(ff1, c)
    return thunks


TAIL_ORDER = ['w_out', 'hf'] + ['ff1_%d' % c for c in range(N_FF_CHUNKS)] + ['ff2', 'final']


def _run(thunks, order):
    assert sorted(order) == sorted(thunks), (sorted(order), sorted(thunks))
    for name in order:
        thunks[name]()


def _store_if(valid, ref, idx, val):
    keep = jnp.zeros(val.shape, jnp.int32) + valid.astype(jnp.int32)
    ref[idx] = jnp.where(keep > 0, val, ref[idx])


def _prompt_mix_thunks(x_ref, ada, v1024, v512, w_in_ref, wg_ref, wsp_ref, bias_s_ref, conv_ref, lru_ref,
                       vrows_ref, xperm_ref, hperm_ref, tail_ref, h_ref, merged_ref, j, slot, valid):
    tt = x_ref.shape[0]
    n_sub = SUBLANES
    sub_len = tt // n_sub
    n_slab = D_RG // LANES
    st = {}

    def norm():
        mods = _split_mods(ada)
        st['hm'] = _modulated_norm([x_ref[...]], v1024[0:1, :], mods[1], mods[0])

    def project():
        st['proj'] = proj = _dot(st['hm'], w_in_ref[...])
        xr = proj[:, 0:D_RG]
        st['xr_tail'] = xr[tt - SUBLANES:, :]
        for p in range(n_sub):
            for s in range(n_slab):
                xperm_ref[s, pl.ds(p, sub_len, stride=n_sub), :] = (
                    xr[p * sub_len:(p + 1) * sub_len, s * LANES:(s + 1) * LANES])

    def conv():
        xr_p = _cat([xperm_ref[s] for s in range(n_slab)], 1)
        sub = lax.broadcasted_iota(jnp.int32, (SUBLANES, D_RG), 0)
        prev_tail = tail_ref[...]
        head = []
        for k in range(CONV_W - 1, 0, -1):
            grp = pltpu.roll(xr_p[(sub_len - k) * SUBLANES:(sub_len - k + 1) * SUBLANES, :], 1, 0)
            head.append(jnp.where(sub == 0, prev_tail[SUBLANES - k:SUBLANES - k + 1, :], grp))
        ext = _cat(head + [xr_p])
        xc = v512[_V_CONV_B:_V_CONV_B + 1, :]
        for k in range(CONV_W):
            xc = xc + ext[k * SUBLANES:k * SUBLANES + tt, :] * v512[_V_CONV_W + k:_V_CONV_W + k + 1, :]
        st['xc'] = xc
        tail_ref[...] = st['xr_tail']
        for k in range(CONV_W - 1):
            row = SUBLANES - (CONV_W - 1) + k
            _store_if(valid, conv_ref, (k, pl.ds(slot, 1), slice(None)), st['xr_tail'][row:row + 1, :])

    def gate_mm():
        st['gate_pre'] = _gate_preacts(st['xc'], wg_ref)

    def coefficients():
        t_r, t_i = _gate_tanh(st.pop('gate_pre'), v512)
        a, mult = _lru_coeffs(t_r, v512)
        row = lax.broadcasted_iota(jnp.int32, (tt, D_RG), 0)
        mult = jnp.where(row + j * tt == 0, 1.0, mult)
        st['a'] = a
        st['b'] = mult * (0.5 * t_i + 0.5) * st.pop('xc')

    def recurrence():
        a, b = st['a'], st['b']
        sub = lax.broadcasted_iota(jnp.int32, (SUBLANES, D_RG), 0)
        hs, ps = [], []
        for q in range(sub_len):
            aq = a[q * SUBLANES:(q + 1) * SUBLANES, :]
            bq = b[q * SUBLANES:(q + 1) * SUBLANES, :]
            hs.append(bq if q == 0 else aq * hs[-1] + bq)
            ps.append(aq if q == 0 else aq * ps[-1])
        init = jnp.where(sub == 0, h_ref[SUBLANES - 1:SUBLANES, :], 0.0)
        for p in range(1, n_sub):
            end = hs[-1] + ps[-1] * init
            init = jnp.where(sub == p, pltpu.roll(end, 1, 0), init)
        end = hs[-1] + ps[-1] * init
        h_ref[...] = end
        _store_if(valid, lru_ref, (pl.ds(slot, 1), slice(None)), end[SUBLANES - 1:SUBLANES, :])
        h_perm = _cat([hq + pq * init for hq, pq in zip(hs, ps)])
        for s in range(n_slab):
            hperm_ref[s] = h_perm[:, s * LANES:(s + 1) * LANES]

    def recurrent_out(c):
        r0 = c * CHUNK
        yg = st['proj'][r0:r0 + CHUNK, D_RG:2 * D_RG]
        subs = range(r0 // sub_len, (r0 + CHUNK) // sub_len)
        h = _cat([_cat([hperm_ref[s, pl.ds(p, sub_len, stride=n_sub), :] for p in subs]) for s in range(n_slab)], 1)
        rg_out = h * _gelu_tanh(yg)
        merged_ref[r0:r0 + CHUNK, 0:D_RG] = _rms(rg_out, v512[_V_G_RG:_V_G_RG + 1, :]).astype(BF16)

    def spatial_pre(c):
        r0 = c * CHUNK
        lane = lax.broadcasted_iota(jnp.int32, (CHUNK, 2 * SG_HEAD_DIM), 1)
        v = st['proj'][r0:r0 + CHUNK, 2 * D_RG + D_SG:]
        vn = _layer_norm(v, v512[_V_LN_G:_V_LN_G + 1, :], v512[_V_LN_B:_V_LN_B + 1, :])
        rhs = []
        for p in range(SG_HEADS // 2):
            vp = vn[:, p * 2 * SG_HEAD_DIM:(p + 1) * 2 * SG_HEAD_DIM]
            rhs.append(jnp.concatenate([jnp.where(lane < SG_HEAD_DIM, vp, 0.0),
                                        jnp.where(lane >= SG_HEAD_DIM, vp, 0.0)], axis=0).astype(BF16))
        st['rhs_%d' % c] = rhs
        if r0 + CHUNK == tt:
            _store_if(valid, vrows_ref, slot, v)

    def spatial_mm(c):
        t_idx = lax.broadcasted_iota(jnp.int32, (CHUNK, 2 * CHUNK), 0)
        s_idx = lax.broadcasted_iota(jnp.int32, (CHUNK, 2 * CHUNK), 1) & (CHUNK - 1)
        outs = []
        for p, rhs in enumerate(st.pop('rhs_%d' % c)):
            ws_pair = jnp.where(s_idx <= t_idx, wsp_ref[p], jnp.zeros((), BF16))
            outs.append(_dot(ws_pair, rhs))
        st['mixed_%d' % c] = _cat(outs, 1)

    def spatial_post(c):
        r0 = c * CHUNK
        u = st['proj'][r0:r0 + CHUNK, 2 * D_RG:2 * D_RG + D_SG]
        sg_out = u * (st.pop('mixed_%d' % c) + bias_s_ref[...])
        merged_ref[r0:r0 + CHUNK, D_RG:] = _rms(sg_out, v512[_V_G_SG:_V_G_SG + 1, :]).astype(BF16)

    thunks = {'norm': norm, 'w_in': project, 'conv': conv, 'gate_mm': gate_mm, 'coef': coefficients,
              'rec': recurrence}
    for c in range(tt // CHUNK):
        thunks['sp_pre_%d' % c] = functools.partial(spatial_pre, c)
        thunks['sp_mm_%d' % c] = functools.partial(spatial_mm, c)
        thunks['sp_post_%d' % c] = functools.partial(spatial_post, c)
        thunks['rg_out_%d' % c] = functools.partial(recurrent_out, c)
    return thunks


N_PROMPT_CHUNKS = PROMPT_TILE // CHUNK
MIX_ORDER = (['norm', 'w_in', 'conv', 'gate_mm', 'coef', 'rec'] +
             [f'{name}_{c}' for c in range(N_PROMPT_CHUNKS) for name in ('sp_pre', 'sp_mm', 'sp_post', 'rg_out')])

assert N_PROMPT_CHUNKS == 4 and N_FF_CHUNKS == 4
INTERLEAVED_ORDER = [
    ('a', 'norm'), ('b', 'w_out'), ('a', 'w_in'), ('b', 'hf'), ('b', 'ff1_0'), ('a', 'conv'),
    ('a', 'sp_pre_0'), ('a', 'sp_pre_1'), ('b', 'ff1_1'), ('a', 'gate_mm'), ('a', 'sp_pre_2'), ('a', 'sp_pre_3'),
    ('b', 'ff1_2'), ('a', 'coef'), ('a', 'sp_mm_0'), ('a', 'sp_mm_1'), ('b', 'ff1_3'), ('a', 'rec'),
    ('a', 'sp_mm_2'), ('a', 'sp_mm_3'), ('a', 'sp_post_0'), ('a', 'sp_post_1'),
    ('a', 'rg_out_0'), ('a', 'rg_out_1'), ('a', 'sp_post_2'), ('a', 'sp_post_3'),
    ('a', 'rg_out_2'), ('a', 'rg_out_3'), ('b', 'ff2'), ('b', 'final'),
]


def _tail_weight_copies(src_refs, dst_refs, sem):
    return [pltpu.make_async_copy(src, dst, sem.at[k]) for k, (src, dst) in enumerate(zip(src_refs, dst_refs))]


def _cast_weight(src_hbm, dst_ref, stage_ref, sem):
    rows, cols = dst_ref.shape
    n_col = cols // D_MODEL
    n = (rows // CAST_ROWS) * n_col
    assert rows % CAST_ROWS == 0 and cols % D_MODEL == 0 and n >= CAST_BUFS

    def window(k):
        r = pl.multiple_of((k // n_col) * CAST_ROWS, CAST_ROWS)
        c = pl.multiple_of((k % n_col) * D_MODEL, D_MODEL)
        return pl.ds(r, CAST_ROWS), pl.ds(c, D_MODEL)

    def copy(k):
        rs, cs = window(k)
        slot = k % CAST_BUFS
        return pltpu.make_async_copy(src_hbm.at[rs, cs], stage_ref.at[slot], sem.at[CAST_SEM0 + slot])

    for k in range(CAST_BUFS):
        copy(k).start()

    def body(k, carry):
        copy(k).wait()
        rs, cs = window(k)
        dst_ref[rs, cs] = stage_ref[k % CAST_BUFS].astype(BF16)

        @pl.when(k + CAST_BUFS < n)
        def _():
            copy(k + CAST_BUFS).start()

        return carry

    lax.fori_loop(0, n, body, 0)


def _prompt_kernel(x_ref, xprev_ref, ada_ref, v1024_ref, v512_ref, w_in_ref, wg_ref, wsp_ref,
                   bias_s_ref, w_out_hbm, w_ff1_hbm, w_ff2_hbm,
                   y_ref, conv_ref, lru_ref, vrows_ref, w_out_o, w_ff1_o, w_ff2_o,
                   xperm_ref, hperm_ref, tail_ref, h_ref, merged_ref, w_out_ref, w_ff1_ref, w_ff2_ref, stage_ref, w_sem,
                   *, tiles_per_seq):
    s = pl.program_id(0)
    n_tiles = pl.num_programs(0) - 1
    j = lax.rem(s, tiles_per_seq)
    seq_cur = jnp.minimum(s, n_tiles - 1) // tiles_per_seq
    seq_prev = jnp.maximum(s - 1, 0) // tiles_per_seq

    tail_refs = (w_out_ref, w_ff1_ref, w_ff2_ref)
    out_copies = _tail_weight_copies(tail_refs, (w_out_o, w_ff1_o, w_ff2_o), w_sem)

    @pl.when(s == 0)
    def _():
        merged_ref[...] = jnp.zeros(merged_ref.shape, BF16)
        conv_ref[...] = jnp.zeros(conv_ref.shape, F32)
        lru_ref[...] = jnp.zeros(lru_ref.shape, F32)
        vrows_ref[...] = jnp.zeros(vrows_ref.shape, F32)
        for src, dst in zip((w_out_hbm, w_ff1_hbm, w_ff2_hbm), tail_refs):
            _cast_weight(src, dst, stage_ref, w_sem)
        for cp in out_copies:
            cp.start()

    @pl.when(j == 0)
    def _():
        tail_ref[...] = jnp.zeros((SUBLANES, D_RG), F32)
        h_ref[...] = jnp.zeros((SUBLANES, D_RG), F32)

    v1024 = v1024_ref[...]
    v512 = v512_ref[...]

    def store_y(_, val):
        y_ref[...] = val

    def stage_a():
        return _prompt_mix_thunks(x_ref, ada_ref[pl.ds(seq_cur, 1), :], v1024, v512, w_in_ref, wg_ref, wsp_ref,
                                  bias_s_ref, conv_ref, lru_ref, vrows_ref, xperm_ref, hperm_ref, tail_ref, h_ref,
                                  merged_ref, j, seq_cur, s < n_tiles)

    def stage_b():
        return _tail_thunks([xprev_ref[...]], lambda: merged_ref[...], _split_mods(ada_ref[pl.ds(seq_prev, 1), :]),
                            v1024, w_out_ref, w_ff1_ref, w_ff2_ref, store_y)

    stages = {'a': stage_a(), 'b': stage_b()}
    assert sorted(n for k, n in INTERLEAVED_ORDER if k == 'a') == sorted(stages['a'])
    assert sorted(n for k, n in INTERLEAVED_ORDER if k == 'b') == sorted(stages['b'])
    for stage, name in INTERLEAVED_ORDER:
        stages[stage][name]()

    @pl.when(s == n_tiles)
    def _():
        for cp in out_copies:
            cp.wait()


def _sample_kernel(x_ref, ada_ref, cs_ref, h0_ref, v1024_ref, v512_ref, w_in_ref, wg_ref, w8_ref, b8_ref,
                   w_out_hbm, w_ff1_hbm, w_ff2_hbm,
                   y_ref, conv_ref, lru_ref, vrows_ref,
                   xslab_ref, yslab_ref, vslab_ref, w_out_ref, w_ff1_ref, w_ff2_ref, w_sem):
    n_t = SUBLANES
    nb = x_ref.shape[0] // n_t
    copies = _tail_weight_copies((w_out_hbm, w_ff1_hbm, w_ff2_hbm), (w_out_ref, w_ff1_ref, w_ff2_ref), w_sem)

    @pl.when(pl.program_id(0) == 0)
    def _():
        for cp in copies:
            cp.start()

    mods = _split_mods(ada_ref[...])
    sh_m, sc_m = mods[0], mods[1]
    v1024 = v1024_ref[...]
    v512 = v512_ref[...]

    def blk(arr, t):
        return arr[t * nb:(t + 1) * nb, :]

    for s in range(D_MODEL // LANES):
        xslab_ref[s] = x_ref[:, s * LANES:(s + 1) * LANES]
    x_blocks = [_cat([xslab_ref[s, pl.ds(t, nb, stride=n_t), :] for s in range(D_MODEL // LANES)], 1)
                for t in range(n_t)]
    hm = _modulated_norm(x_blocks, v1024[0:1, :], sc_m, sh_m)
    proj = _dot(hm, w_in_ref[...])
    xr = proj[:, 0:D_RG]
    yg = proj[:, D_RG:2 * D_RG]
    u = proj[:, 2 * D_RG:2 * D_RG + D_SG]
    v = proj[:, 2 * D_RG + D_SG:]

    xp = [cs_ref[k] for k in range(CONV_W - 1)] + [blk(xr, t) for t in range(n_t)]
    xc_blocks = []
    for t in range(n_t):
        acc = v512[_V_CONV_B:_V_CONV_B + 1, :]
        for k in range(CONV_W):
            acc = acc + xp[t + k] * v512[_V_CONV_W + k:_V_CONV_W + k + 1, :]
        xc_blocks.append(acc)
    for k in range(CONV_W - 1):
        conv_ref[k] = xp[n_t + k]
    xc = _cat(xc_blocks)

    t_r, t_i = _gate_tanh(_gate_preacts(xc, wg_ref), v512)
    a, mult = _lru_coeffs(t_r, v512)
    bterm = mult * (0.5 * t_i + 0.5) * xc
    h = h0_ref[...]
    h_blocks = []
    for t in range(n_t):
        h = blk(a, t) * h + blk(bterm, t)
        h_blocks.append(h)
    lru_ref[...] = h
    rg_out = _cat(h_blocks) * _gelu_tanh(yg)

    vn = _layer_norm(v, v512[_V_LN_G:_V_LN_G + 1, :], v512[_V_LN_B:_V_LN_B + 1, :])
    sg_blocks = []
    for t in range(n_t):
        for s in range(D_SG // LANES):
            vslab_ref[s, pl.ds(t, nb, stride=n_t), :] = blk(v, t)[:, s * LANES:(s + 1) * LANES]
        mixed = b8_ref[t:t + 1, :]
        for s in range(t + 1):
            mixed = mixed + w8_ref[t, s:s + 1, :] * blk(vn, s)
        sg_blocks.append(blk(u, t) * mixed)
    sg_out = _cat(sg_blocks)
    for s in range(D_SG // LANES):
        vrows_ref[:, s * LANES:(s + 1) * LANES] = vslab_ref[s]

    merged = jnp.concatenate(
        [_rms(rg_out, v512[_V_G_RG:_V_G_RG + 1, :]).astype(BF16),
         _rms(sg_out, v512[_V_G_SG:_V_G_SG + 1, :]).astype(BF16)], axis=1)

    def store_y(t, val):
        for s in range(D_MODEL // LANES):
            yslab_ref[s, pl.ds(t, nb, stride=n_t), :] = val[:, s * LANES:(s + 1) * LANES]

    @pl.when(pl.program_id(0) == 0)
    def _():
        for cp in copies:
            cp.wait()

    _run(_tail_thunks(x_blocks, lambda: merged, mods, v1024, w_out_ref, w_ff1_ref, w_ff2_ref, store_y), TAIL_ORDER)
    for s in range(D_MODEL // LANES):
        y_ref[:, s * LANES:(s + 1) * LANES] = yslab_ref[s]


def _pack_params(g_mix_ref, g_ffn_ref, g_final_ref, conv_w_ref, conv_b_ref, b_ra_ref, b_ri_ref, lru_l_ref,
                 ln_v_g_ref, ln_v_b_ref, g_rg_ref, g_sg_ref, w_ra_ref, w_ri_ref, w_s_ref, b_s_ref,
                 v1024_o, v512_o, wg_o, wsp_o, bias_s_o, w8_o, b8_o, wg_f):
    v1024_o[0:1, :] = g_mix_ref[...]
    v1024_o[1:2, :] = g_ffn_ref[...]
    v1024_o[2:3, :] = g_final_ref[...]
    v512_o[_V_CONV_W:_V_CONV_W + CONV_W, :] = conv_w_ref[...]
    for row, ref in ((_V_CONV_B, conv_b_ref), (_V_LRU_L, lru_l_ref), (_V_LN_G, ln_v_g_ref), (_V_LN_B, ln_v_b_ref),
                     (_V_G_RG, g_rg_ref), (_V_G_SG, g_sg_ref)):
        v512_o[row:row + 1, :] = ref[...]
    wg_f[...] = jnp.zeros(wg_f.shape, F32)
    per = RG_HEADS // GATE_GROUPS
    for h in range(RG_HEADS):
        g, lo = h // per, (h % per) * RG_HEAD_DIM
        v512_o[_V_B_RA:_V_B_RA + 1, h * RG_HEAD_DIM:(h + 1) * RG_HEAD_DIM] = b_ra_ref[h:h + 1, :]
        v512_o[_V_B_RI:_V_B_RI + 1, h * RG_HEAD_DIM:(h + 1) * RG_HEAD_DIM] = b_ri_ref[h:h + 1, :]
        wg_f[g, lo:lo + RG_HEAD_DIM, lo:lo + RG_HEAD_DIM] = w_ra_ref[h]
        wg_f[g, lo:lo + RG_HEAD_DIM, GATE_W + lo:GATE_W + lo + RG_HEAD_DIM] = w_ri_ref[h]
    wg_o[...] = (0.5 * wg_f[...]).astype(BF16)
    for h in range(SG_HEADS):
        wsp_o[h // 2, :, (h % 2) * CHUNK:(h % 2 + 1) * CHUNK] = w_s_ref[h].astype(BF16)
    head = lax.broadcasted_iota(jnp.int32, (1, D_SG), 1) // SG_HEAD_DIM
    b_t = b_s_ref[...].T
    bias = jnp.zeros((CHUNK, D_SG), F32)
    w_t = []
    for h in range(SG_HEADS):
        bias = jnp.where(head == h, b_t[:, h:h + 1], bias)
        w_t.append(w_s_ref[h, 0:SUBLANES, :].T[0:SUBLANES, :])
    bias_s_o[...] = bias
    b8_o[...] = bias[0:SUBLANES, :]
    for t in range(SUBLANES):
        acc = jnp.zeros((SUBLANES, D_SG), F32)
        for h in range(SG_HEADS):
            acc = jnp.where(head == h, w_t[h][:, t:t + 1], acc)
        w8_o[t] = acc


def _prep_kernel(cs_ref, cp_ref, w_ada_ref, b_ada_ref, w_in_ref, *rest):
    small, (ada_ref, w_in_o), packed = rest[:16], rest[16:18], rest[18:]
    c = jnp.concatenate([cs_ref[...], cp_ref[...]], axis=0)
    s = (c * jax.nn.sigmoid(c)).astype(BF16)
    ada_ref[...] = _dot(s, w_ada_ref[...].astype(BF16)) + b_ada_ref[...]
    w_in_o[...] = w_in_ref[...].astype(BF16)

    @pl.when(pl.program_id(0) == 0)
    def _():
        _pack_params(*small, *packed)


def _resident(shape):
    nd = len(shape)
    return pl.BlockSpec(shape, lambda *_: (0,) * nd, pipeline_mode=pl.Buffered(1))


def kernel(x_prompt, x_sample, c_prompt, c_sample, state_conv, state_rglru, w_ada, b_ada, g_mix, g_ffn, w_in, conv_w, conv_b, w_ra, b_ra, w_ri, b_ri, lru_l, ln_v_g, ln_v_b, w_s, b_s, g_rg, g_sg, w_out, w_ff1, w_ff2, g_final):
    batch, seq, _ = x_prompt.shape
    dec_batch, dec_seq, _ = x_sample.shape
    assert w_ada.shape[0] == 1, "single-layer step"
    assert seq % PROMPT_TILE == 0 and PROMPT_TILE % CHUNK == 0 and dec_batch % SAMPLE_NB == 0
    assert batch * (seq // PROMPT_TILE) >= 2, "the prompt pipeline needs at least two tiles"
    assert dec_seq <= CHUNK

    n_c = dec_batch + batch
    assert dec_batch % SUBLANES == 0 and dec_batch % batch == 0
    steps = PREP_STEPS

    def row_block(w):
        return pl.BlockSpec((w.shape[0] // steps, w.shape[1]), lambda n: (n, 0))

    def whole(a, lead=0):
        nd = a.ndim
        return pl.BlockSpec((None,) * lead + a.shape[lead:], lambda n: (0,) * nd)

    mats = (w_in[0],)
    gains =(g_mix, g_ffn, g_final.reshape(1, D_MODEL))
    rows512 = (conv_b, lru_l, ln_v_g, ln_v_b, g_rg, g_sg)
    small = gains + (conv_w, rows512[0], b_ra, b_ri) + rows512[1:] + (w_ra, w_ri, w_s, b_s)
    small_specs = [whole(a) for a in gains] + [whole(conv_w, 1), whole(conv_b), whole(b_ra, 1), whole(b_ri, 1)] + \
                  [whole(a) for a in rows512[1:]] + [whole(w_ra, 1), whole(w_ri, 1), whole(w_s, 1), whole(b_s, 1)]
    packed_shapes = ((3, D_MODEL), (12, D_RG), (GATE_GROUPS, GATE_W, 2 * GATE_W), (SG_HEADS // 2, CHUNK, 2 * CHUNK),
                     (CHUNK, D_SG), (SUBLANES, SUBLANES, D_SG), (SUBLANES, D_SG))
    packed_dtypes = (F32, F32, BF16, BF16, F32, F32, F32)
    packed_out = tuple(jax.ShapeDtypeStruct(s, d) for s, d in zip(packed_shapes, packed_dtypes))
    ada, w_in_b, v1024, v512, wg, wsp, bias_s, w8, b8 = pl.pallas_call(
        _prep_kernel,
        out_shape=(jax.ShapeDtypeStruct((n_c, 6 * D_MODEL), F32),) +
                  tuple(jax.ShapeDtypeStruct(w.shape, BF16) for w in mats) + packed_out,
        grid=(steps,),
        in_specs=[pl.BlockSpec((dec_batch, D_MODEL), lambda n: (0, 0)),
                  pl.BlockSpec((batch, D_MODEL), lambda n: (0, 0)),
                  pl.BlockSpec((D_MODEL, 6 * D_MODEL // steps), lambda n: (0, n)),
                  pl.BlockSpec((1, 6 * D_MODEL // steps), lambda n: (0, n))] + [row_block(w) for w in mats] +
                 small_specs,
        out_specs=(pl.BlockSpec((n_c, 6 * D_MODEL // steps), lambda n: (0, n)),) +
                  tuple(row_block(w) for w in mats) + tuple(whole(o) for o in packed_out),
        scratch_shapes=[pltpu.VMEM(packed_shapes[2], F32)],
        compiler_params=pltpu.CompilerParams(dimension_semantics=("arbitrary",),
                                             vmem_limit_bytes=VMEM_LIMIT_BYTES),
        name="prep",
    )(c_sample, c_prompt, w_ada[0], b_ada, *mats, *small)

    weights = (w_in_b, wg)
    weight_specs = [_resident(w_in_b.shape), _resident(wg.shape)]
    tail_f32 = (w_out[0], w_ff1[0], w_ff2[0])
    tail_shapes = tuple(jax.ShapeDtypeStruct(w.shape, BF16) for w in tail_f32)
    cparams = functools.partial(pltpu.CompilerParams, vmem_limit_bytes=VMEM_LIMIT_BYTES)

    tt = PROMPT_TILE
    nt = seq // tt
    n_tiles = batch * nt

    def cur(s):
        return jnp.minimum(s, n_tiles - 1)

    def prev(s):
        return jnp.maximum(s - 1, 0)

    y_p, conv_p, lru_p, vrows_p, *tail_weights = pl.pallas_call(
        functools.partial(_prompt_kernel, tiles_per_seq=nt),
        out_shape=(jax.ShapeDtypeStruct((batch, seq, D_MODEL), F32),
                   jax.ShapeDtypeStruct((CONV_W - 1, batch, D_RG), F32),
                   jax.ShapeDtypeStruct((batch, D_RG), F32),
                   jax.ShapeDtypeStruct((batch, CHUNK, D_SG), F32)) + tail_shapes,
        grid=(n_tiles + 1,),
        in_specs=[pl.BlockSpec((None, tt, D_MODEL), lambda s: (cur(s) // nt, cur(s) % nt, 0)),
                  pl.BlockSpec((None, tt, D_MODEL), lambda s: (prev(s) // nt, prev(s) % nt, 0)),
                  pl.BlockSpec((batch, 6 * D_MODEL), lambda s: (dec_batch // batch, 0),
                               pipeline_mode=pl.Buffered(1)),
                  _resident(v1024.shape), _resident(v512.shape)] + weight_specs +
                 [_resident(wsp.shape), _resident(bias_s.shape)] + [pl.BlockSpec(memory_space=pl.ANY)] * 3,
        out_specs=(pl.BlockSpec((None, tt, D_MODEL), lambda s: (prev(s) // nt, prev(s) % nt, 0)),
                   pl.BlockSpec((CONV_W - 1, batch, D_RG), lambda s: (0, 0, 0)),
                   pl.BlockSpec((batch, D_RG), lambda s: (0, 0)),
                   pl.BlockSpec((batch, CHUNK, D_SG), lambda s: (0, 0, 0))) + (pl.BlockSpec(memory_space=pl.ANY),) * 3,
        scratch_shapes=[pltpu.VMEM((D_RG // LANES, tt, LANES), F32), pltpu.VMEM((D_RG // LANES, tt, LANES), F32),
                        pltpu.VMEM((SUBLANES, D_RG), F32), pltpu.VMEM((SUBLANES, D_RG), F32),
                        pltpu.VMEM((tt, D_MODEL), BF16)] +
                       [pltpu.VMEM(w.shape, BF16) for w in tail_f32] +
                       [pltpu.VMEM((CAST_BUFS, CAST_ROWS, D_MODEL), F32),
                        pltpu.SemaphoreType.DMA((CAST_SEM0 + CAST_BUFS,))],
        compiler_params=cparams(dimension_semantics=("arbitrary",)),
        name="prompt_trunk",
    )(x_prompt, x_prompt, ada, v1024, v512, *weights, wsp, bias_s, *tail_f32)

    nb = SAMPLE_NB
    n_cs = CONV_W - 1
    assert dec_seq == SUBLANES, "one decode sequence per 8-row group"
    rows = nb * dec_seq

    y_s, conv_s, lru_s, vrows_s = pl.pallas_call(
        _sample_kernel,
        out_shape=(jax.ShapeDtypeStruct((dec_batch * dec_seq, D_MODEL), F32),
                   jax.ShapeDtypeStruct((n_cs, dec_batch, D_RG), F32),
                   jax.ShapeDtypeStruct((dec_batch, D_RG), F32),
                   jax.ShapeDtypeStruct((dec_batch * dec_seq, D_SG), F32)),
        grid=(dec_batch // nb,),
        in_specs=[pl.BlockSpec((rows, D_MODEL), lambda i: (i, 0)),
                  pl.BlockSpec((nb, 6 * D_MODEL), lambda i: (i, 0)),
                  pl.BlockSpec((n_cs, nb, D_RG), lambda i: (0, i, 0)),
                  pl.BlockSpec((nb, D_RG), lambda i: (i, 0)),
                  _resident(v1024.shape), _resident(v512.shape)] + weight_specs +
                 [_resident(w8.shape), _resident(b8.shape)] + [pl.BlockSpec(memory_space=pl.ANY)] * 3,
        out_specs=(pl.BlockSpec((rows, D_MODEL), lambda i: (i, 0)),
                   pl.BlockSpec((n_cs, nb, D_RG), lambda i: (0, i, 0)),
                   pl.BlockSpec((nb, D_RG), lambda i: (i, 0)),
                   pl.BlockSpec((rows, D_SG), lambda i: (i, 0))),
        scratch_shapes=[pltpu.VMEM((D_MODEL // LANES, rows, LANES), F32),
                        pltpu.VMEM((D_MODEL // LANES, rows, LANES), F32),
                        pltpu.VMEM((D_SG // LANES, rows, LANES), F32)] +
                       [pltpu.VMEM(w.shape, BF16) for w in tail_weights] + [pltpu.SemaphoreType.DMA((3,))],
        compiler_params=cparams(dimension_semantics=("arbitrary",)),
        name="sample_trunk",
    )(x_sample.reshape(dec_batch * dec_seq, D_MODEL), ada, state_conv[0].transpose(1, 0, 2), state_rglru[0],
      v1024, v512, *weights, w8, b8, *tail_weights)

    return (y_p, y_s.reshape(dec_batch, dec_seq, D_MODEL), conv_p.transpose(1, 0, 2)[None], lru_p[None],
            vrows_p[None], conv_s.transpose(1, 0, 2)[None], lru_s[None],
            vrows_s.reshape(1, dec_batch, dec_seq, D_SG))
```

```python
import functools

import jax
import jax.numpy as jnp
from jax import lax
from jax.experimental import pallas as pl
from jax.experimental.pallas import tpu as pltpu

D_MODEL = 1024
D_RG = 512
D_SG = 512
D_IN = 2 * D_RG + 2 * D_SG
D_FF = 4 * D_MODEL
RG_HEADS = 8
RG_HEAD_DIM = D_RG // RG_HEADS
SG_HEADS = 8
SG_HEAD_DIM = D_SG // SG_HEADS
CHUNK = 128
CONV_W = 4
LRU_C = 8.0
EPS = 1e-6

SUBLANES = 8
LANES = 128
GATE_GROUPS = 2
GATE_W = D_RG // GATE_GROUPS
PROMPT_TILE = 512
SAMPLE_NB = 64
FF_CHUNK = 1024
N_FF_CHUNKS = D_FF // FF_CHUNK
PREP_STEPS = 4
CAST_ROWS = 256
CAST_BUFS = 4
CAST_SEM0 = 3
VMEM_LIMIT_BYTES = 60000 * 1024

F32 = jnp.float32
BF16 = jnp.bfloat16

_V_CONV_W, _V_CONV_B, _V_B_RA, _V_B_RI, _V_LRU_L, _V_LN_G, _V_LN_B, _V_G_RG, _V_G_SG = 0, 4, 5, 6, 7, 8, 9, 10, 11


def _dot(a, b):
    return jnp.dot(a, b, preferred_element_type=F32)


def _rms(x, g):
    ms = jnp.mean(x * x, axis=-1, keepdims=True)
    return (x * lax.rsqrt(ms + EPS)) * g


def _gelu_tanh(x):
    k = 0.7978845608028654
    u = x * (k + (k * 0.044715) * (x * x))
    return (0.5 * x) * (1.0 + jnp.tanh(u))


def _layer_norm(x, g, b):
    mu = jnp.mean(x, axis=-1, keepdims=True)
    xc = x - mu
    var = jnp.mean(xc * xc, axis=-1, keepdims=True)
    return (xc * lax.rsqrt(var + EPS)) * g + b


def _cat(parts, axis=0):
    return parts[0] if len(parts) == 1 else jnp.concatenate(parts, axis=axis)


def _split_mods(ada):
    return [ada[:, k * D_MODEL:(k + 1) * D_MODEL] for k in range(6)]


def _modulated_norm(x_blocks, g, sc, sh):
    gain = g * (1.0 + sc)
    return _cat([(_rms(xb, gain) + sh).astype(BF16) for xb in x_blocks])


def _gate_preacts(xc, wg_ref):
    xcb = xc.astype(BF16)
    return _cat([_dot(xcb[:, g * GATE_W:(g + 1) * GATE_W], wg_ref[g]) for g in range(GATE_GROUPS)], 1)


def _gate_tanh(pre_half, v512):
    r_parts, i_parts = [], []
    for g in range(GATE_GROUPS):
        lo, hi = g * GATE_W, (g + 1) * GATE_W
        ra = pre_half[:, 2 * lo:2 * lo + GATE_W]
        ri = pre_half[:, 2 * lo + GATE_W:2 * hi]
        r_parts.append(jnp.tanh(ra + 0.5 * v512[_V_B_RA:_V_B_RA + 1, lo:hi]))
        i_parts.append(jnp.tanh(ri + 0.5 * v512[_V_B_RI:_V_B_RI + 1, lo:hi]))
    return _cat(r_parts, 1), _cat(i_parts, 1)


def _lru_coeffs(t_r, v512):
    lam = v512[_V_LRU_L:_V_LRU_L + 1, :]
    log_sig = jnp.minimum(lam, 0.0) - jnp.log1p(jnp.exp(-jnp.abs(lam)))
    k = (0.5 * LRU_C) * log_sig
    log_a = k * t_r + k
    a = jnp.exp(log_a)
    z = 1.0 - a * a
    mult = jnp.where(z > 0.0, z * lax.rsqrt(z), 0.0)
    return a, mult


def _tail_thunks(x_blocks, merged_fn, mods, v1024, w_out_ref, w_ff1_ref, w_ff2_ref, store_y):
    _, _, gt_m, sh_f, sc_f, gt_f = mods
    rows = x_blocks[0].shape[0]
    st = {}

    def w_out():
        mm = _dot(merged_fn(), w_out_ref[...])
        st['x1'] = [xb + gt_m * mm[i * rows:(i + 1) * rows] for i, xb in enumerate(x_blocks)]

    def hf():
        st['hf'] = _modulated_norm(st['x1'], v1024[1:2, :], sc_f, sh_f)

    def ff1(c):
        pre = _dot(st['hf'], w_ff1_ref[:, c * FF_CHUNK:(c + 1) * FF_CHUNK])
        h1 = jnp.maximum(pre.astype(BF16), 0.0)
        st['h1_%d' % c] = h1 * h1

    def ff2():
        h1 = _cat([st.pop('h1_%d' % c) for c in range(N_FF_CHUNKS)], 1)
        st['acc'] = _dot(h1, w_ff2_ref[...])

    def final():
        for i, xb in enumerate(st['x1']):
            x2 = xb + gt_f * st['acc'][i * rows:(i + 1) * rows]
            store_y(i, _rms(x2, v1024[2:3, :]))

    thunks = {'w_out': w_out, 'hf': hf, 'ff2': ff2, 'final': final}
    for c in range(N_FF_CHUNKS):
        thunks['ff1_%d' % c] = functools.partial(ff1, c)
    return thunks


TAIL_ORDER = ['w_out', 'hf'] + ['ff1_%d' % c for c in range(N_FF_CHUNKS)] + ['ff2', 'final']


def _run(thunks, order):
    assert sorted(order) == sorted(thunks), (sorted(order), sorted(thunks))
    for name in order:
        thunks[name]()


def _store_if(valid, ref, idx, val):
    keep = jnp.zeros(val.shape, jnp.int32) + valid.astype(jnp.int32)
    ref[idx] = jnp.where(keep > 0, val, ref[idx])


def _prompt_mix_thunks(x_ref, ada, v1024, v512, w_in_ref, wg_ref, wsp_ref, bias_s_ref, conv_ref, lru_ref,
                       vrows_ref, xperm_ref, hperm_ref, tail_ref, h_ref, merged_ref, j, slot, valid):
    tt = x_ref.shape[0]
    n_sub = SUBLANES
    sub_len = tt // n_sub
    n_slab = D_RG // LANES
    st = {}

    def norm():
        mods = _split_mods(ada)
        st['hm'] = _modulated_norm([x_ref[...]], v1024[0:1, :], mods[1], mods[0])

    def project():
        st['proj'] = proj = _dot(st['hm'], w_in_ref[...])
        xr = proj[:, 0:D_RG]
        st['xr_tail'] = xr[tt - SUBLANES:, :]
        for p in range(n_sub):
            for s in range(n_slab):
                xperm_ref[s, pl.ds(p, sub_len, stride=n_sub), :] = (
                    xr[p * sub_len:(p + 1) * sub_len, s * LANES:(s + 1) * LANES])

    def conv():
        xr_p = _cat([xperm_ref[s] for s in range(n_slab)], 1)
        sub = lax.broadcasted_iota(jnp.int32, (SUBLANES, D_RG), 0)
        prev_tail = tail_ref[...]
        head = []
        for k in range(CONV_W - 1, 0, -1):
            grp = pltpu.roll(xr_p[(sub_len - k) * SUBLANES:(sub_len - k + 1) * SUBLANES, :], 1, 0)
            head.append(jnp.where(sub == 0, prev_tail[SUBLANES - k:SUBLANES - k + 1, :], grp))
        ext = _cat(head + [xr_p])
        xc = v512[_V_CONV_B:_V_CONV_B + 1, :]
        for k in range(CONV_W):
            xc = xc + ext[k * SUBLANES:k * SUBLANES + tt, :] * v512[_V_CONV_W + k:_V_CONV_W + k + 1, :]
        st['xc'] = xc
        tail_ref[...] = st['xr_tail']
        for k in range(CONV_W - 1):
            row = SUBLANES - (CONV_W - 1) + k
            _store_if(valid, conv_ref, (k, pl.ds(slot, 1), slice(None)), st['xr_tail'][row:row + 1, :])

    def gate_mm():
        st['gate_pre'] = _gate_preacts(st['xc'], wg_ref)

    def coefficients():
        t_r, t_i = _gate_tanh(st.pop('gate_pre'), v512)
        a, mult = _lru_coeffs(t_r, v512)
        row = lax.broadcasted_iota(jnp.int32, (tt, D_RG), 0)
        mult = jnp.where(row + j * tt == 0, 1.0, mult)
        st['a'] = a
        st['b'] = mult * (0.5 * t_i + 0.5) * st.pop('xc')

    def recurrence():
        a, b = st['a'], st['b']
        sub = lax.broadcasted_iota(jnp.int32, (SUBLANES, D_RG), 0)
        hs, ps = [], []
        for q in range(sub_len):
            aq = a[q * SUBLANES:(q + 1) * SUBLANES, :]
            bq = b[q * SUBLANES:(q + 1) * SUBLANES, :]
            hs.append(bq if q == 0 else aq * hs[-1] + bq)
            ps.append(aq if q == 0 else aq * ps[-1])
        init = jnp.where(sub == 0, h_ref[SUBLANES - 1:SUBLANES, :], 0.0)
        for p in range(1, n_sub):
            end = hs[-1] + ps[-1] * init
            init = jnp.where(sub == p, pltpu.roll(end, 1, 0), init)
        end = hs[-1] + ps[-1] * init
        h_ref[...] = end
        _store_if(valid, lru_ref, (pl.ds(slot, 1), slice(None)), end[SUBLANES - 1:SUBLANES, :])
        h_perm = _cat([hq + pq * init for hq, pq in zip(hs, ps)])
        for s in range(n_slab):
            hperm_ref[s] = h_perm[:, s * LANES:(s + 1) * LANES]

    def recurrent_out(c):
        r0 = c * CHUNK
        yg = st['proj'][r0:r0 + CHUNK, D_RG:2 * D_RG]
        subs = range(r0 // sub_len, (r0 + CHUNK) // sub_len)
        h = _cat([_cat([hperm_ref[s, pl.ds(p, sub_len, stride=n_sub), :] for p in subs]) for s in range(n_slab)], 1)
        rg_out = h * _gelu_tanh(yg)
        merged_ref[r0:r0 + CHUNK, 0:D_RG] = _rms(rg_out, v512[_V_G_RG:_V_G_RG + 1, :]).astype(BF16)

    def spatial_pre(c):
        r0 = c * CHUNK
        lane = lax.broadcasted_iota(jnp.int32, (CHUNK, 2 * SG_HEAD_DIM), 1)
        v = st['proj'][r0:r0 + CHUNK, 2 * D_RG + D_SG:]
        vn = _layer_norm(v, v512[_V_LN_G:_V_LN_G + 1, :], v512[_V_LN_B:_V_LN_B + 1, :])
        rhs = []
        for p in range(SG_HEADS // 2):
            vp = vn[:, p * 2 * SG_HEAD_DIM:(p + 1) * 2 * SG_HEAD_DIM]
            rhs.append(jnp.concatenate([jnp.where(lane < SG_HEAD_DIM, vp, 0.0),
                                        jnp.where(lane >= SG_HEAD_DIM, vp, 0.0)], axis=0).astype(BF16))
        st['rhs_%d' % c] = rhs
        if r0 + CHUNK == tt:
            _store_if(valid, vrows_ref, slot, v)

    def spatial_mm(c):
        t_idx = lax.broadcasted_iota(jnp.int32, (CHUNK, 2 * CHUNK), 0)
        s_idx = lax.broadcasted_iota(jnp.int32, (CHUNK, 2 * CHUNK), 1) & (CHUNK - 1)
        outs = []
        for p, rhs in enumerate(st.pop('rhs_%d' % c)):
            ws_pair = jnp.where(s_idx <= t_idx, wsp_ref[p], jnp.zeros((), BF16))
            outs.append(_dot(ws_pair, rhs))
        st['mixed_%d' % c] = _cat(outs, 1)

    def spatial_post(c):
        r0 = c * CHUNK
        u = st['proj'][r0:r0 + CHUNK, 2 * D_RG:2 * D_RG + D_SG]
        sg_out = u * (st.pop('mixed_%d' % c) + bias_s_ref[...])
        merged_ref[r0:r0 + CHUNK, D_RG:] = _rms(sg_out, v512[_V_G_SG:_V_G_SG + 1, :]).astype(BF16)

    thunks = {'norm': norm, 'w_in': project, 'conv': conv, 'gate_mm': gate_mm, 'coef': coefficients,
              'rec': recurrence}
    for c in range(tt // CHUNK):
        thunks['sp_pre_%d' % c] = functools.partial(spatial_pre, c)
        thunks['sp_mm_%d' % c] = functools.partial(spatial_mm, c)
        thunks['sp_post_%d' % c] = functools.partial(spatial_post, c)
        thunks['rg_out_%d' % c] = functools.partial(recurrent_out, c)
    return thunks


N_PROMPT_CHUNKS = PROMPT_TILE // CHUNK
MIX_ORDER = (['norm', 'w_in', 'conv', 'gate_mm', 'coef', 'rec'] +
             [f'{name}_{c}' for c in range(N_PROMPT_CHUNKS) for name in ('sp_pre', 'sp_mm', 'sp_post', 'rg_out')])

assert N_PROMPT_CHUNKS == 4 and N_FF_CHUNKS == 4
INTERLEAVED_ORDER = [
    ('a', 'norm'), ('b', 'w_out'), ('a', 'w_in'), ('b', 'hf'), ('b', 'ff1_0'), ('a', 'conv'),
    ('a', 'sp_pre_0'), ('a', 'sp_pre_1'), ('b', 'ff1_1'), ('a', 'gate_mm'), ('a', 'sp_pre_2'), ('a', 'sp_pre_3'),
    ('b', 'ff1_2'), ('a', 'coef'), ('a', 'sp_mm_0'), ('a', 'sp_mm_1'), ('b', 'ff1_3'), ('a', 'rec'),
    ('a', 'sp_mm_2'), ('a', 'sp_mm_3'), ('a', 'sp_post_0'), ('a', 'sp_post_1'),
    ('a', 'rg_out_0'), ('a', 'rg_out_1'), ('a', 'sp_post_2'), ('a', 'sp_post_3'),
    ('a', 'rg_out_2'), ('a', 'rg_out_3'), ('b', 'ff2'), ('b', 'final'),
]


def _tail_weight_copies(src_refs, dst_refs, sem):
    return [pltpu.make_async_copy(src, dst, sem.at[k]) for k, (src, dst) in enumerate(zip(src_refs, dst_refs))]


def _weight_cast_thunks(f32_hbm_refs, bf16_refs, stage_ref, sem):
    windows = []
    for src, dst in zip(f32_hbm_refs, bf16_refs):
        rows, cols = dst.shape
        for r in range(0, rows, CAST_ROWS):
            for c in range(0, cols, D_MODEL):
                windows.append((src, dst, pl.ds(r, CAST_ROWS), pl.ds(c, D_MODEL)))
    copies = [pltpu.make_async_copy(src.at[rs, cs], stage_ref.at[k % CAST_BUFS], sem.at[CAST_SEM0 + k % CAST_BUFS])
              for k, (src, _, rs, cs) in enumerate(windows)]

    def make(k):
        def run():
            if k == 0:
                for cp in copies[:CAST_BUFS]:
                    cp.start()
            copies[k].wait()
            _, dst, rs, cs = windows[k]
            dst[rs, cs] = stage_ref[k % CAST_BUFS].astype(BF16)
            if k + CAST_BUFS < len(copies):
                copies[k + CAST_BUFS].start()
        return run

    return [make(k) for k in range(len(copies))]


def _prompt_kernel(x_ref, xprev_ref, ada_ref, v1024_ref, v512_ref, w_in_ref, wg_ref, wsp_ref,
                   bias_s_ref, w_out_hbm, w_ff1_hbm, w_ff2_hbm,
                   y_ref, conv_ref, lru_ref, vrows_ref, w_out_o, w_ff1_o, w_ff2_o,
                   xperm_ref, hperm_ref, tail_ref, h_ref, merged_ref, w_out_ref, w_ff1_ref, w_ff2_ref, stage_ref, w_sem,
                   *, tiles_per_seq):
    s = pl.program_id(0)
    n_tiles = pl.num_programs(0) - 1
    j = lax.rem(s, tiles_per_seq)
    seq_cur = jnp.minimum(s, n_tiles - 1) // tiles_per_seq
    seq_prev = jnp.maximum(s - 1, 0) // tiles_per_seq

    tail_refs = (w_out_ref, w_ff1_ref, w_ff2_ref)
    out_copies = _tail_weight_copies(tail_refs, (w_out_o, w_ff1_o, w_ff2_o), w_sem)

    @pl.when(s == 0)
    def _():
        conv_ref[...] = jnp.zeros(conv_ref.shape, F32)
        lru_ref[...] = jnp.zeros(lru_ref.shape, F32)
        vrows_ref[...] = jnp.zeros(vrows_ref.shape, F32)

    @pl.when(j == 0)
    def _():
        tail_ref[...] = jnp.zeros((SUBLANES, D_RG), F32)
        h_ref[...] = jnp.zeros((SUBLANES, D_RG), F32)

    v1024 = v1024_ref[...]
    v512 = v512_ref[...]

    def store_y(_, val):
        y_ref[...] = val

    def stage_a():
        return _prompt_mix_thunks(x_ref, ada_ref[pl.ds(seq_cur, 1), :], v1024, v512, w_in_ref, wg_ref, wsp_ref,
                                  bias_s_ref, conv_ref, lru_ref, vrows_ref, xperm_ref, hperm_ref, tail_ref, h_ref,
                                  merged_ref, j, seq_cur, s < n_tiles)

    def stage_b():
        return _tail_thunks([xprev_ref[...]], lambda: merged_ref[...], _split_mods(ada_ref[pl.ds(seq_prev, 1), :]),
                            v1024, w_out_ref, w_ff1_ref, w_ff2_ref, store_y)

    @pl.when(s == 0)
    def _():
        casts = _weight_cast_thunks((w_out_hbm, w_ff1_hbm, w_ff2_hbm), tail_refs, stage_ref, w_sem)
        mix = stage_a()
        per = -(-len(casts) // len(MIX_ORDER))
        for n, name in enumerate(MIX_ORDER):
            for cast in casts[n * per:(n + 1) * per]:
                cast()
            mix[name]()
        for cp in out_copies:
            cp.start()

    @pl.when(s > 0)
    def _():
        stages = {'a': stage_a(), 'b': stage_b()}
        assert sorted(n for k, n in INTERLEAVED_ORDER if k == 'a') == sorted(stages['a'])
        assert sorted(n for k, n in INTERLEAVED_ORDER if k == 'b') == sorted(stages['b'])
        for stage, name in INTERLEAVED_ORDER:
            stages[stage][name]()

    @pl.when(s == n_tiles)
    def _():
        for cp in out_copies:
            cp.wait()


def _sample_kernel(x_ref, ada_ref, cs_ref, h0_ref, v1024_ref, v512_ref, w_in_ref, wg_ref, w8_ref, b8_ref,
                   w_out_hbm, w_ff1_hbm, w_ff2_hbm,
                   y_ref, conv_ref, lru_ref, vrows_ref,
                   xslab_ref, yslab_ref, vslab_ref, w_out_ref, w_ff1_ref, w_ff2_ref, w_sem):
    n_t = SUBLANES
    nb = x_ref.shape[0] // n_t
    copies = _tail_weight_copies((w_out_hbm, w_ff1_hbm, w_ff2_hbm), (w_out_ref, w_ff1_ref, w_ff2_ref), w_sem)

    @pl.when(pl.program_id(0) == 0)
    def _():
        for cp in copies:
            cp.start()

    mods = _split_mods(ada_ref[...])
    sh_m, sc_m = mods[0], mods[1]
    v1024 = v1024_ref[...]
    v512 = v512_ref[...]

    def blk(arr, t):
        return arr[t * nb:(t + 1) * nb, :]

    for s in range(D_MODEL // LANES):
        xslab_ref[s] = x_ref[:, s * LANES:(s + 1) * LANES]
    x_blocks = [_cat([xslab_ref[s, pl.ds(t, nb, stride=n_t), :] for s in range(D_MODEL // LANES)], 1)
                for t in range(n_t)]
    hm = _modulated_norm(x_blocks, v1024[0:1, :], sc_m, sh_m)
    proj = _dot(hm, w_in_ref[...])
    xr = proj[:, 0:D_RG]
    yg = proj[:, D_RG:2 * D_RG]
    u = proj[:, 2 * D_RG:2 * D_RG + D_SG]
    v = proj[:, 2 * D_RG + D_SG:]

    xp = [cs_ref[k] for k in range(CONV_W - 1)] + [blk(xr, t) for t in range(n_t)]
    xc_blocks = []
    for t in range(n_t):
        acc = v512[_V_CONV_B:_V_CONV_B + 1, :]
        for k in range(CONV_W):
            acc = acc + xp[t + k] * v512[_V_CONV_W + k:_V_CONV_W + k + 1, :]
        xc_blocks.append(acc)
    for k in range(CONV_W - 1):
        conv_ref[k] = xp[n_t + k]
    xc = _cat(xc_blocks)

    t_r, t_i = _gate_tanh(_gate_preacts(xc, wg_ref), v512)
    a, mult = _lru_coeffs(t_r, v512)
    bterm = mult * (0.5 * t_i + 0.5) * xc
    h = h0_ref[...]
    h_blocks = []
    for t in range(n_t):
        h = blk(a, t) * h + blk(bterm, t)
        h_blocks.append(h)
    lru_ref[...] = h
    rg_out = _cat(h_blocks) * _gelu_tanh(yg)

    vn = _layer_norm(v, v512[_V_LN_G:_V_LN_G + 1, :], v512[_V_LN_B:_V_LN_B + 1, :])
    sg_blocks = []
    for t in range(n_t):
        for s in range(D_SG // LANES):
            vslab_ref[s, pl.ds(t, nb, stride=n_t), :] = blk(v, t)[:, s * LANES:(s + 1) * LANES]
        mixed = b8_ref[t:t + 1, :]
        for s in range(t + 1):
            mixed = mixed + w8_ref[t, s:s + 1, :] * blk(vn, s)
        sg_blocks.append(blk(u, t) * mixed)
    sg_out = _cat(sg_blocks)
    for s in range(D_SG // LANES):
        vrows_ref[:, s * LANES:(s + 1) * LANES] = vslab_ref[s]

    merged = jnp.concatenate(
        [_rms(rg_out, v512[_V_G_RG:_V_G_RG + 1, :]).astype(BF16),
         _rms(sg_out, v512[_V_G_SG:_V_G_SG + 1, :]).astype(BF16)], axis=1)

    def store_y(t, val):
        for s in range(D_MODEL // LANES):
            yslab_ref[s, pl.ds(t, nb, stride=n_t), :] = val[:, s * LANES:(s + 1) * LANES]

    @pl.when(pl.program_id(0) == 0)
    def _():
        for cp in copies:
            cp.wait()

    _run(_tail_thunks(x_blocks, lambda: merged, mods, v1024, w_out_ref, w_ff1_ref, w_ff2_ref, store_y), TAIL_ORDER)
    for s in range(D_MODEL // LANES):
        y_ref[:, s * LANES:(s + 1) * LANES] = yslab_ref[s]


def _pack_params(g_mix_ref, g_ffn_ref, g_final_ref, conv_w_ref, conv_b_ref, b_ra_ref, b_ri_ref, lru_l_ref,
                 ln_v_g_ref, ln_v_b_ref, g_rg_ref, g_sg_ref, w_ra_ref, w_ri_ref, w_s_ref, b_s_ref,
                 v1024_o, v512_o, wg_o, wsp_o, bias_s_o, w8_o, b8_o, wg_f):
    v1024_o[0:1, :] = g_mix_ref[...]
    v1024_o[1:2, :] = g_ffn_ref[...]
    v1024_o[2:3, :] = g_final_ref[...]
    v512_o[_V_CONV_W:_V_CONV_W + CONV_W, :] = conv_w_ref[...]
    for row, ref in ((_V_CONV_B, conv_b_ref), (_V_LRU_L, lru_l_ref), (_V_LN_G, ln_v_g_ref), (_V_LN_B, ln_v_b_ref),
                     (_V_G_RG, g_rg_ref), (_V_G_SG, g_sg_ref)):
        v512_o[row:row + 1, :] = ref[...]
    wg_f[...] = jnp.zeros(wg_f.shape, F32)
    per = RG_HEADS // GATE_GROUPS
    for h in range(RG_HEADS):
        g, lo = h // per, (h % per) * RG_HEAD_DIM
        v512_o[_V_B_RA:_V_B_RA + 1, h * RG_HEAD_DIM:(h + 1) * RG_HEAD_DIM] = b_ra_ref[h:h + 1, :]
        v512_o[_V_B_RI:_V_B_RI + 1, h * RG_HEAD_DIM:(h + 1) * RG_HEAD_DIM] = b_ri_ref[h:h + 1, :]
        wg_f[g, lo:lo + RG_HEAD_DIM, lo:lo + RG_HEAD_DIM] = w_ra_ref[h]
        wg_f[g, lo:lo + RG_HEAD_DIM, GATE_W + lo:GATE_W + lo + RG_HEAD_DIM] = w_ri_ref[h]
    wg_o[...] = (0.5 * wg_f[...]).astype(BF16)
    for h in range(SG_HEADS):
        wsp_o[h // 2, :, (h % 2) * CHUNK:(h % 2 + 1) * CHUNK] = w_s_ref[h].astype(BF16)
    head = lax.broadcasted_iota(jnp.int32, (1, D_SG), 1) // SG_HEAD_DIM
    b_t = b_s_ref[...].T
    bias = jnp.zeros((CHUNK, D_SG), F32)
    w_t = []
    for h in range(SG_HEADS):
        bias = jnp.where(head == h, b_t[:, h:h + 1], bias)
        w_t.append(w_s_ref[h, 0:SUBLANES, :].T[0:SUBLANES, :])
    bias_s_o[...] = bias
    b8_o[...] = bias[0:SUBLANES, :]
    for t in range(SUBLANES):
        acc = jnp.zeros((SUBLANES, D_SG), F32)
        for h in range(SG_HEADS):
            acc = jnp.where(head == h, w_t[h][:, t:t + 1], acc)
        w8_o[t] = acc


def _prep_kernel(cs_ref, cp_ref, w_ada_ref, b_ada_ref, w_in_ref, *rest):
    small, (ada_ref, w_in_o), packed = rest[:16], rest[16:18], rest[18:]
    c = jnp.concatenate([cs_ref[...], cp_ref[...]], axis=0)
    s = (c * jax.nn.sigmoid(c)).astype(BF16)
    ada_ref[...] = _dot(s, w_ada_ref[...].astype(BF16)) + b_ada_ref[...]
    w_in_o[...] = w_in_ref[...].astype(BF16)

    @pl.when(pl.program_id(0) == 0)
    def _():
        _pack_params(*small, *packed)


def _resident(shape):
    nd = len(shape)
    return pl.BlockSpec(shape, lambda *_: (0,) * nd, pipeline_mode=pl.Buffered(1))


def kernel(x_prompt, x_sample, c_prompt, c_sample, state_conv, state_rglru, w_ada, b_ada, g_mix, g_ffn, w_in, conv_w, conv_b, w_ra, b_ra, w_ri, b_ri, lru_l, ln_v_g, ln_v_b, w_s, b_s, g_rg, g_sg, w_out, w_ff1, w_ff2, g_final):
    batch, seq, _ = x_prompt.shape
    dec_batch, dec_seq, _ = x_sample.shape
    assert w_ada.shape[0] == 1, "single-layer step"
    assert seq % PROMPT_TILE == 0 and PROMPT_TILE % CHUNK == 0 and dec_batch % SAMPLE_NB == 0
    assert batch * (seq // PROMPT_TILE) >= 2, "the prompt pipeline needs at least two tiles"
    assert dec_seq <= CHUNK

    n_c = dec_batch + batch
    assert dec_batch % SUBLANES == 0 and dec_batch % batch == 0
    steps = PREP_STEPS

    def row_block(w):
        return pl.BlockSpec((w.shape[0] // steps, w.shape[1]), lambda n: (n, 0))

    def whole(a, lead=0):
        nd = a.ndim
        return pl.BlockSpec((None,) * lead + a.shape[lead:], lambda n: (0,) * nd)

    mats = (w_in[0],)
    gains =(g_mix, g_ffn, g_final.reshape(1, D_MODEL))
    rows512 = (conv_b, lru_l, ln_v_g, ln_v_b, g_rg, g_sg)
    small = gains + (conv_w, rows512[0], b_ra, b_ri) + rows512[1:] + (w_ra, w_ri, w_s, b_s)
    small_specs = [whole(a) for a in gains] + [whole(conv_w, 1), whole(conv_b), whole(b_ra, 1), whole(b_ri, 1)] + \
                  [whole(a) for a in rows512[1:]] + [whole(w_ra, 1), whole(w_ri, 1), whole(w_s, 1), whole(b_s, 1)]
    packed_shapes = ((3, D_MODEL), (12, D_RG), (GATE_GROUPS, GATE_W, 2 * GATE_W), (SG_HEADS // 2, CHUNK, 2 * CHUNK),
                     (CHUNK, D_SG), (SUBLANES, SUBLANES, D_SG), (SUBLANES, D_SG))
    packed_dtypes = (F32, F32, BF16, BF16, F32, F32, F32)
    packed_out = tuple(jax.ShapeDtypeStruct(s, d) for s, d in zip(packed_shapes, packed_dtypes))
    ada, w_in_b, v1024, v512, wg, wsp, bias_s, w8, b8 = pl.pallas_call(
        _prep_kernel,
        out_shape=(jax.ShapeDtypeStruct((n_c, 6 * D_MODEL), F32),) +
                  tuple(jax.ShapeDtypeStruct(w.shape, BF16) for w in mats) + packed_out,
        grid=(steps,),
        in_specs=[pl.BlockSpec((dec_batch, D_MODEL), lambda n: (0, 0)),
                  pl.BlockSpec((batch, D_MODEL), lambda n: (0, 0)),
                  pl.BlockSpec((D_MODEL, 6 * D_MODEL // steps), lambda n: (0, n)),
                  pl.BlockSpec((1, 6 * D_MODEL // steps), lambda n: (0, n))] + [row_block(w) for w in mats] +
                 small_specs,
        out_specs=(pl.BlockSpec((n_c, 6 * D_MODEL // steps), lambda n: (0, n)),) +
                  tuple(row_block(w) for w in mats) + tuple(whole(o) for o in packed_out),
        scratch_shapes=[pltpu.VMEM(packed_shapes[2], F32)],
        compiler_params=pltpu.CompilerParams(dimension_semantics=("arbitrary",),
                                             vmem_limit_bytes=VMEM_LIMIT_BYTES),
        name="prep",
    )(c_sample, c_prompt, w_ada[0], b_ada, *mats, *small)

    weights = (w_in_b, wg)
    weight_specs = [_resident(w_in_b.shape), _resident(wg.shape)]
    tail_f32 = (w_out[0], w_ff1[0], w_ff2[0])
    tail_shapes = tuple(jax.ShapeDtypeStruct(w.shape, BF16) for w in tail_f32)
    cparams = functools.partial(pltpu.CompilerParams, vmem_limit_bytes=VMEM_LIMIT_BYTES)

    tt = PROMPT_TILE
    nt = seq // tt
    n_tiles = batch * nt

    def cur(s):
        return jnp.minimum(s, n_tiles - 1)

    def prev(s):
        return jnp.maximum(s - 1, 0)

    y_p, conv_p, lru_p, vrows_p, *tail_weights = pl.pallas_call(
        functools.partial(_prompt_kernel, tiles_per_seq=nt),
        out_shape=(jax.ShapeDtypeStruct((batch, seq, D_MODEL), F32),
                   jax.ShapeDtypeStruct((CONV_W - 1, batch, D_RG), F32),
                   jax.ShapeDtypeStruct((batch, D_RG), F32),
                   jax.ShapeDtypeStruct((batch, CHUNK, D_SG), F32)) + tail_shapes,
        grid=(n_tiles + 1,),
        in_specs=[pl.BlockSpec((None, tt, D_MODEL), lambda s: (cur(s) // nt, cur(s) % nt, 0)),
                  pl.BlockSpec((None, tt, D_MODEL), lambda s: (prev(s) // nt, prev(s) % nt, 0)),
                  pl.BlockSpec((batch, 6 * D_MODEL), lambda s: (dec_batch // batch, 0),
                               pipeline_mode=pl.Buffered(1)),
                  _resident(v1024.shape), _resident(v512.shape)] + weight_specs +
                 [_resident(wsp.shape), _resident(bias_s.shape)] + [pl.BlockSpec(memory_space=pl.ANY)] * 3,
        out_specs=(pl.BlockSpec((None, tt, D_MODEL), lambda s: (prev(s) // nt, prev(s) % nt, 0)),
                   pl.BlockSpec((CONV_W - 1, batch, D_RG), lambda s: (0, 0, 0)),
                   pl.BlockSpec((batch, D_RG), lambda s: (0, 0)),
                   pl.BlockSpec((batch, CHUNK, D_SG), lambda s: (0, 0, 0))) + (pl.BlockSpec(memory_space=pl.ANY),) * 3,
        scratch_shapes=[pltpu.VMEM((D_RG // LANES, tt, LANES), F32), pltpu.VMEM((D_RG // LANES, tt, LANES), F32),
                        pltpu.VMEM((SUBLANES, D_RG), F32), pltpu.VMEM((SUBLANES, D_RG), F32),
                        pltpu.VMEM((tt, D_MODEL), BF16)] +
                       [pltpu.VMEM(w.shape, BF16) for w in tail_f32] +
                       [pltpu.VMEM((CAST_BUFS, CAST_ROWS, D_MODEL), F32),
                        pltpu.SemaphoreType.DMA((CAST_SEM0 + CAST_BUFS,))],
        compiler_params=cparams(dimension_semantics=("arbitrary",)),
        name="prompt_trunk",
    )(x_prompt, x_prompt, ada, v1024, v512, *weights, wsp, bias_s, *tail_f32)

    nb = SAMPLE_NB
    n_cs = CONV_W - 1
    assert dec_seq == SUBLANES, "one decode sequence per 8-row group"
    rows = nb * dec_seq

    y_s, conv_s, lru_s, vrows_s = pl.pallas_call(
        _sample_kernel,
        out_shape=(jax.ShapeDtypeStruct((dec_batch * dec_seq, D_MODEL), F32),
                   jax.ShapeDtypeStruct((n_cs, dec_batch, D_RG), F32),
                   jax.ShapeDtypeStruct((dec_batch, D_RG), F32),
                   jax.ShapeDtypeStruct((dec_batch * dec_seq, D_SG), F32)),
        grid=(dec_batch // nb,),
        in_specs=[pl.BlockSpec((rows, D_MODEL), lambda i: (i, 0)),
                  pl.BlockSpec((nb, 6 * D_MODEL), lambda i: (i, 0)),
                  pl.BlockSpec((n_cs, nb, D_RG), lambda i: (0, i, 0)),
                  pl.BlockSpec((nb, D_RG), lambda i: (i, 0)),
                  _resident(v1024.shape), _resident(v512.shape)] + weight_specs +
                 [_resident(w8.shape), _resident(b8.shape)] + [pl.BlockSpec(memory_space=pl.ANY)] * 3,
        out_specs=(pl.BlockSpec((rows, D_MODEL), lambda i: (i, 0)),
                   pl.BlockSpec((n_cs, nb, D_RG), lambda i: (0, i, 0)),
                   pl.BlockSpec((nb, D_RG), lambda i: (i, 0)),
                   pl.BlockSpec((rows, D_SG), lambda i: (i, 0))),
        scratch_shapes=[pltpu.VMEM((D_MODEL // LANES, rows, LANES), F32),
                        pltpu.VMEM((D_MODEL // LANES, rows, LANES), F32),
                        pltpu.VMEM((D_SG // LANES, rows, LANES), F32)] +
                       [pltpu.VMEM(w.shape, BF16) for w in tail_weights] + [pltpu.SemaphoreType.DMA((3,))],
        compiler_params=cparams(dimension_semantics=("arbitrary",)),
        name="sample_trunk",
    )(x_sample.reshape(dec_batch * dec_seq, D_MODEL), ada, state_conv[0].transpose(1, 0, 2), state_rglru[0],
      v1024, v512, *weights, w8, b8, *tail_weights)

    return (y_p, y_s.reshape(dec_batch, dec_seq, D_MODEL), conv_p.transpose(1, 0, 2)[None], lru_p[None],
            vrows_p[None], conv_s.transpose(1, 0, 2)[None], lru_s[None],
            vrows_s.reshape(1, dec_batch, dec_seq, D_SG))
```

```python
import functools

import jax
import jax.numpy as jnp
from jax import lax
from jax.experimental import pallas as pl
from jax.experimental.pallas import tpu as pltpu

D_MODEL = 1024
D_RG = 512
D_SG = 512
D_IN = 2 * D_RG + 2 * D_SG
D_FF = 4 * D_MODEL
RG_HEADS = 8
RG_HEAD_DIM = D_RG // RG_HEADS
SG_HEADS = 8
SG_HEAD_DIM = D_SG // SG_HEADS
CHUNK = 128
CONV_W = 4
LRU_C = 8.0
EPS = 1e-6

SUBLANES = 8
LANES = 128
GATE_GROUPS = 2
GATE_W = D_RG // GATE_GROUPS
PROMPT_TILE = 512
SAMPLE_NB = 64
FF_CHUNK = 1024
N_FF_CHUNKS = D_FF // FF_CHUNK
PREP_STEPS = 4
CAST_ROWS = 256
CAST_BUFS = 4
CAST_SEM0 = 3
VMEM_LIMIT_BYTES = 60000 * 1024

F32 = jnp.float32
BF16 = jnp.bfloat16

_V_CONV_W, _V_CONV_B, _V_B_RA, _V_B_RI, _V_LRU_L, _V_LN_G, _V_LN_B, _V_G_RG, _V_G_SG = 0, 4, 5, 6, 7, 8, 9, 10, 11


def _dot(a, b):
    return jnp.dot(a, b, preferred_element_type=F32)


def _rms(x, g):
    ms = jnp.mean(x * x, axis=-1, keepdims=True)
    return (x * lax.rsqrt(ms + EPS)) * g


def _gelu_tanh(x):
    k = 0.7978845608028654
    u = x * (k + (k * 0.044715) * (x * x))
    return (0.5 * x) * (1.0 + jnp.tanh(u))


def _layer_norm(x, g, b):
    mu = jnp.mean(x, axis=-1, keepdims=True)
    xc = x - mu
    var = jnp.mean(xc * xc, axis=-1, keepdims=True)
    return (xc * lax.rsqrt(var + EPS)) * g + b


def _cat(parts, axis=0):
    return parts[0] if len(parts) == 1 else jnp.concatenate(parts, axis=axis)


def _split_mods(ada):
    return [ada[:, k * D_MODEL:(k + 1) * D_MODEL] for k in range(6)]


def _modulated_norm(x_blocks, g, sc, sh):
    gain = g * (1.0 + sc)
    return _cat([(_rms(xb, gain) + sh).astype(BF16) for xb in x_blocks])


def _gate_preacts(xc, wg_ref):
    xcb = xc.astype(BF16)
    return _cat([_dot(xcb[:, g * GATE_W:(g + 1) * GATE_W], wg_ref[g]) for g in range(GATE_GROUPS)], 1)


def _gate_tanh(pre_half, v512):
    r_parts, i_parts = [], []
    for g in range(GATE_GROUPS):
        lo, hi = g * GATE_W, (g + 1) * GATE_W
        ra = pre_half[:, 2 * lo:2 * lo + GATE_W]
        ri = pre_half[:, 2 * lo + GATE_W:2 * hi]
        r_parts.append(jnp.tanh(ra + 0.5 * v512[_V_B_RA:_V_B_RA + 1, lo:hi]))
        i_parts.append(jnp.tanh(ri + 0.5 * v512[_V_B_RI:_V_B_RI + 1, lo:hi]))
    return _cat(r_parts, 1), _cat(i_parts, 1)


def _lru_coeffs(t_r, v512):
    lam = v512[_V_LRU_L:_V_LRU_L + 1, :]
    log_sig = jnp.minimum(lam, 0.0) - jnp.log1p(jnp.exp(-jnp.abs(lam)))
    k = (0.5 * LRU_C) * log_sig
    log_a = k * t_r + k
    a = jnp.exp(log_a)
    z = 1.0 - a * a
    mult = jnp.where(z > 0.0, z * lax.rsqrt(z), 0.0)
    return a, mult


def _tail_thunks(x_blocks, merged_fn, mods, v1024, w_out_ref, w_ff1_ref, w_ff2_ref, store_y):
    _, _, gt_m, sh_f, sc_f, gt_f = mods
    rows = x_blocks[0].shape[0]
    st = {}

    def w_out():
        mm = _dot(merged_fn(), w_out_ref[...])
        st['x1'] = [xb + gt_m * mm[i * rows:(i + 1) * rows] for i, xb in enumerate(x_blocks)]

    def hf():
        st['hf'] = _modulated_norm(st['x1'], v1024[1:2, :], sc_f, sh_f)

    def ff1(c):
        pre = _dot(st['hf'], w_ff1_ref[:, c * FF_CHUNK:(c + 1) * FF_CHUNK])
        h1 = jnp.maximum(pre.astype(BF16), 0.0)
        st['h1_%d' % c] = h1 * h1

    def ff2():
        h1 = _cat([st.pop('h1_%d' % c) for c in range(N_FF_CHUNKS)], 1)
        st['acc'] = _dot(h1, w_ff2_ref[...])

    def final():
        for i, xb in enumerate(st['x1']):
            x2 = xb + gt_f * st['acc'][i * rows:(i + 1) * rows]
            store_y(i, _rms(x2, v1024[2:3, :]))

    thunks = {'w_out': w_out, 'hf': hf, 'ff2': ff2, 'final': final}
    for c in range(N_FF_CHUNKS):
        thunks['ff1_%d' % c] = functools.partial(ff1, c)
    return thunks


TAIL_ORDER = ['w_out', 'hf'] + ['ff1_%d' % c for c in range(N_FF_CHUNKS)] + ['ff2', 'final']


def _run(thunks, order):
    assert sorted(order) == sorted(thunks), (sorted(order), sorted(thunks))
    for name in order:
        thunks[name]()


def _store_if(valid, ref, idx, val):
    keep = jnp.zeros(val.shape, jnp.int32) + valid.astype(jnp.int32)
    ref[idx] = jnp.where(keep > 0, val, ref[idx])


def _prompt_mix_thunks(x_ref, ada, v1024, v512, w_in_ref, wg_ref, wsp_ref, bias_s_ref, conv_ref, lru_ref,
                       vrows_ref, xperm_ref, hperm_ref, tail_ref, h_ref, merged_ref, j, slot, valid):
    tt = x_ref.shape[0]
    n_sub = SUBLANES
    sub_len = tt // n_sub
    n_slab = D_RG // LANES
    st = {}

    def norm():
        mods = _split_mods(ada)
        st['hm'] = _modulated_norm([x_ref[...]], v1024[0:1, :], mods[1], mods[0])

    def project():
        st['proj'] = proj = _dot(st['hm'], w_in_ref[...])
        xr = proj[:, 0:D_RG]
        st['xr_tail'] = xr[tt - SUBLANES:, :]
        for p in range(n_sub):
            for s in range(n_slab):
                xperm_ref[s, pl.ds(p, sub_len, stride=n_sub), :] = (
                    xr[p * sub_len:(p + 1) * sub_len, s * LANES:(s + 1) * LANES])

    def conv():
        xr_p = _cat([xperm_ref[s] for s in range(n_slab)], 1)
        sub = lax.broadcasted_iota(jnp.int32, (SUBLANES, D_RG), 0)
        prev_tail = tail_ref[...]
        head = []
        for k in range(CONV_W - 1, 0, -1):
            grp = pltpu.roll(xr_p[(sub_len - k) * SUBLANES:(sub_len - k + 1) * SUBLANES, :], 1, 0)
            head.append(jnp.where(sub == 0, prev_tail[SUBLANES - k:SUBLANES - k + 1, :], grp))
        ext = _cat(head + [xr_p])
        xc = v512[_V_CONV_B:_V_CONV_B + 1, :]
        for k in range(CONV_W):
            xc = xc + ext[k * SUBLANES:k * SUBLANES + tt, :] * v512[_V_CONV_W + k:_V_CONV_W + k + 1, :]
        st['xc'] = xc
        tail_ref[...] = st['xr_tail']
        for k in range(CONV_W - 1):
            row = SUBLANES - (CONV_W - 1) + k
            _store_if(valid, conv_ref, (k, pl.ds(slot, 1), slice(None)), st['xr_tail'][row:row + 1, :])

    def gate_mm():
        st['gate_pre'] = _gate_preacts(st['xc'], wg_ref)

    def coefficients():
        t_r, t_i = _gate_tanh(st.pop('gate_pre'), v512)
        a, mult = _lru_coeffs(t_r, v512)
        row = lax.broadcasted_iota(jnp.int32, (tt, D_RG), 0)
        mult = jnp.where(row + j * tt == 0, 1.0, mult)
        st['a'] = a
        st['b'] = mult * (0.5 * t_i + 0.5) * st.pop('xc')

    def recurrence():
        a, b = st['a'], st['b']
        sub = lax.broadcasted_iota(jnp.int32, (SUBLANES, D_RG), 0)
        hs, ps = [], []
        for q in range(sub_len):
            aq = a[q * SUBLANES:(q + 1) * SUBLANES, :]
            bq = b[q * SUBLANES:(q + 1) * SUBLANES, :]
            hs.append(bq if q == 0 else aq * hs[-1] + bq)
            ps.append(aq if q == 0 else aq * ps[-1])
        init = jnp.where(sub == 0, h_ref[SUBLANES - 1:SUBLANES, :], 0.0)
        for p in range(1, n_sub):
            end = hs[-1] + ps[-1] * init
            init = jnp.where(sub == p, pltpu.roll(end, 1, 0), init)
        end = hs[-1] + ps[-1] * init
        h_ref[...] = end
        _store_if(valid, lru_ref, (pl.ds(slot, 1), slice(None)), end[SUBLANES - 1:SUBLANES, :])
        h_perm = _cat([hq + pq * init for hq, pq in zip(hs, ps)])
        for s in range(n_slab):
            hperm_ref[s] = h_perm[:, s * LANES:(s + 1) * LANES]

    def recurrent_out(c):
        r0 = c * CHUNK
        yg = st['proj'][r0:r0 + CHUNK, D_RG:2 * D_RG]
        subs = range(r0 // sub_len, (r0 + CHUNK) // sub_len)
        h = _cat([_cat([hperm_ref[s, pl.ds(p, sub_len, stride=n_sub), :] for p in subs]) for s in range(n_slab)], 1)
        rg_out = h * _gelu_tanh(yg)
        merged_ref[r0:r0 + CHUNK, 0:D_RG] = _rms(rg_out, v512[_V_G_RG:_V_G_RG + 1, :]).astype(BF16)

    def spatial_pre(c):
        r0 = c * CHUNK
        lane = lax.broadcasted_iota(jnp.int32, (CHUNK, 2 * SG_HEAD_DIM), 1)
        v = st['proj'][r0:r0 + CHUNK, 2 * D_RG + D_SG:]
        vn = _layer_norm(v, v512[_V_LN_G:_V_LN_G + 1, :], v512[_V_LN_B:_V_LN_B + 1, :])
        rhs = []
        for p in range(SG_HEADS // 2):
            vp = vn[:, p * 2 * SG_HEAD_DIM:(p + 1) * 2 * SG_HEAD_DIM]
            rhs.append(jnp.concatenate([jnp.where(lane < SG_HEAD_DIM, vp, 0.0),
                                        jnp.where(lane >= SG_HEAD_DIM, vp, 0.0)], axis=0).astype(BF16))
        st['rhs_%d' % c] = rhs
        if r0 + CHUNK == tt:
            _store_if(valid, vrows_ref, slot, v)

    def spatial_mm(c):
        t_idx = lax.broadcasted_iota(jnp.int32, (CHUNK, 2 * CHUNK), 0)
        s_idx = lax.broadcasted_iota(jnp.int32, (CHUNK, 2 * CHUNK), 1) & (CHUNK - 1)
        outs = []
        for p, rhs in enumerate(st.pop('rhs_%d' % c)):
            ws_pair = jnp.where(s_idx <= t_idx, wsp_ref[p], jnp.zeros((), BF16))
            outs.append(_dot(ws_pair, rhs))
        st['mixed_%d' % c] = _cat(outs, 1)

    def spatial_post(c):
        r0 = c * CHUNK
        u = st['proj'][r0:r0 + CHUNK, 2 * D_RG:2 * D_RG + D_SG]
        sg_out = u * (st.pop('mixed_%d' % c) + bias_s_ref[...])
        merged_ref[r0:r0 + CHUNK, D_RG:] = _rms(sg_out, v512[_V_G_SG:_V_G_SG + 1, :]).astype(BF16)

    thunks = {'norm': norm, 'w_in': project, 'conv': conv, 'gate_mm': gate_mm, 'coef': coefficients,
              'rec': recurrence}
    for c in range(tt // CHUNK):
        thunks['sp_pre_%d' % c] = functools.partial(spatial_pre, c)
        thunks['sp_mm_%d' % c] = functools.partial(spatial_mm, c)
        thunks['sp_post_%d' % c] = functools.partial(spatial_post, c)
        thunks['rg_out_%d' % c] = functools.partial(recurrent_out, c)
    return thunks


N_PROMPT_CHUNKS = PROMPT_TILE // CHUNK
MIX_ORDER = (['norm', 'w_in', 'conv', 'gate_mm', 'coef', 'rec'] +
             [f'{name}_{c}' for c in range(N_PROMPT_CHUNKS) for name in ('sp_pre', 'sp_mm', 'sp_post', 'rg_out')])

assert N_PROMPT_CHUNKS == 4 and N_FF_CHUNKS == 4
INTERLEAVED_ORDER = [
    ('a', 'norm'), ('b', 'w_out'), ('a', 'w_in'), ('b', 'hf'), ('b', 'ff1_0'), ('a', 'conv'),
    ('a', 'sp_pre_0'), ('a', 'sp_pre_1'), ('b', 'ff1_1'), ('a', 'gate_mm'), ('a', 'sp_pre_2'), ('a', 'sp_pre_3'),
    ('b', 'ff1_2'), ('a', 'coef'), ('a', 'sp_mm_0'), ('a', 'sp_mm_1'), ('b', 'ff1_3'), ('a', 'rec'),
    ('a', 'sp_mm_2'), ('a', 'sp_mm_3'), ('a', 'sp_post_0'), ('a', 'sp_post_1'),
    ('a', 'rg_out_0'), ('a', 'rg_out_1'), ('a', 'sp_post_2'), ('a', 'sp_post_3'),
    ('a', 'rg_out_2'), ('a', 'rg_out_3'), ('b', 'ff2'), ('b', 'final'),
]


def _tail_weight_copies(src_refs, dst_refs, sem):
    return [pltpu.make_async_copy(src, dst, sem.at[k]) for k, (src, dst) in enumerate(zip(src_refs, dst_refs))]


def _weight_cast_thunks(f32_hbm_refs, bf16_refs, stage_ref, sem):
    windows = []
    for src, dst in zip(f32_hbm_refs, bf16_refs):
        rows, cols = dst.shape
        for r in range(0, rows, CAST_ROWS):
            for c in range(0, cols, D_MODEL):
                windows.append((src, dst, pl.ds(r, CAST_ROWS), pl.ds(c, D_MODEL)))
    copies = [pltpu.make_async_copy(src.at[rs, cs], stage_ref.at[k % CAST_BUFS], sem.at[CAST_SEM0 + k % CAST_BUFS])
              for k, (src, _, rs, cs) in enumerate(windows)]

    def make(k):
        def run():
            if k == 0:
                for cp in copies[:CAST_BUFS]:
                    cp.start()
            copies[k].wait()
            _, dst, rs, cs = windows[k]
            dst[rs, cs] = stage_ref[k % CAST_BUFS].astype(BF16)
            if k + CAST_BUFS < len(copies):
                copies[k + CAST_BUFS].start()
        return run

    return [make(k) for k in range(len(copies))]


def _prompt_kernel(x_ref, xprev_ref, ada_ref, v1024_ref, v512_ref, w_in_ref, wg_ref, wsp_ref,
                   bias_s_ref, w_out_hbm, w_ff1_hbm, w_ff2_hbm,
                   y_ref, conv_ref, lru_ref, vrows_ref, w_out_o, w_ff1_o, w_ff2_o,
                   xperm_ref, hperm_ref, tail_ref, h_ref, merged_ref, w_out_ref, w_ff1_ref, w_ff2_ref, stage_ref, w_sem,
                   *, tiles_per_seq):
    s = pl.program_id(0)
    n_tiles = pl.num_programs(0) - 1
    j = lax.rem(s, tiles_per_seq)
    seq_cur = jnp.minimum(s, n_tiles - 1) // tiles_per_seq
    seq_prev = jnp.maximum(s - 1, 0) // tiles_per_seq

    tail_refs = (w_out_ref, w_ff1_ref, w_ff2_ref)
    out_copies = _tail_weight_copies(tail_refs, (w_out_o, w_ff1_o, w_ff2_o), w_sem)

    @pl.when(s == 0)
    def _():
        conv_ref[...] = jnp.zeros(conv_ref.shape, F32)
        lru_ref[...] = jnp.zeros(lru_ref.shape, F32)
        vrows_ref[...] = jnp.zeros(vrows_ref.shape, F32)

    @pl.when(j == 0)
    def _():
        tail_ref[...] = jnp.zeros((SUBLANES, D_RG), F32)
        h_ref[...] = jnp.zeros((SUBLANES, D_RG), F32)

    v1024 = v1024_ref[...]
    v512 = v512_ref[...]

    def store_y(_, val):
        y_ref[...] = val

    def stage_a():
        return _prompt_mix_thunks(x_ref, ada_ref[pl.ds(seq_cur, 1), :], v1024, v512, w_in_ref, wg_ref, wsp_ref,
                                  bias_s_ref, conv_ref, lru_ref, vrows_ref, xperm_ref, hperm_ref, tail_ref, h_ref,
                                  merged_ref, j, seq_cur, s < n_tiles)

    def stage_b():
        return _tail_thunks([xprev_ref[...]], lambda: merged_ref[...], _split_mods(ada_ref[pl.ds(seq_prev, 1), :]),
                            v1024, w_out_ref, w_ff1_ref, w_ff2_ref, store_y)

    @pl.when(s == 0)
    def _():
        casts = _weight_cast_thunks((w_out_hbm, w_ff1_hbm, w_ff2_hbm), tail_refs, stage_ref, w_sem)
        mix = stage_a()
        per = -(-len(casts) // len(MIX_ORDER))
        for n, name in enumerate(MIX_ORDER):
            for cast in casts[n * per:(n + 1) * per]:
                cast()
            mix[name]()
        for cp in out_copies:
            cp.start()

    @pl.when(s > 0)
    def _():
        stages = {'a': stage_a(), 'b': stage_b()}
        assert sorted(n for k, n in INTERLEAVED_ORDER if k == 'a') == sorted(stages['a'])
        assert sorted(n for k, n in INTERLEAVED_ORDER if k == 'b') == sorted(stages['b'])
        for stage, name in INTERLEAVED_ORDER:
            stages[stage][name]()

    @pl.when(s == n_tiles)
    def _():
        for cp in out_copies:
            cp.wait()


def _sample_kernel(x_ref, ada_ref, cs_ref, h0_ref, v1024_ref, v512_ref, w_in_ref, wg_ref, w8_ref, b8_ref,
                   w_out_hbm, w_ff1_hbm, w_ff2_hbm,
                   y_ref, conv_ref, lru_ref, vrows_ref,
                   xslab_ref, yslab_ref, vslab_ref, w_out_ref, w_ff1_ref, w_ff2_ref, w_sem):
    n_t = SUBLANES
    nb = x_ref.shape[0] // n_t
    copies = _tail_weight_copies((w_out_hbm, w_ff1_hbm, w_ff2_hbm), (w_out_ref, w_ff1_ref, w_ff2_ref), w_sem)

    def at_first_step(fn):
        pl.when(pl.program_id(0) == 0)(fn)

    @at_first_step
    def _():
        copies[0].start()
        copies[1].start()

    mods = _split_mods(ada_ref[...])
    sh_m, sc_m = mods[0], mods[1]
    v1024 = v1024_ref[...]
    v512 = v512_ref[...]

    def blk(arr, t):
        return arr[t * nb:(t + 1) * nb, :]

    for s in range(D_MODEL // LANES):
        xslab_ref[s] = x_ref[:, s * LANES:(s + 1) * LANES]
    x_blocks = [_cat([xslab_ref[s, pl.ds(t, nb, stride=n_t), :] for s in range(D_MODEL // LANES)], 1)
                for t in range(n_t)]
    hm = _modulated_norm(x_blocks, v1024[0:1, :], sc_m, sh_m)
    proj = _dot(hm, w_in_ref[...])
    xr = proj[:, 0:D_RG]
    yg = proj[:, D_RG:2 * D_RG]
    u = proj[:, 2 * D_RG:2 * D_RG + D_SG]
    v = proj[:, 2 * D_RG + D_SG:]

    xp = [cs_ref[k] for k in range(CONV_W - 1)] + [blk(xr, t) for t in range(n_t)]
    xc_blocks = []
    for t in range(n_t):
        acc = v512[_V_CONV_B:_V_CONV_B + 1, :]
        for k in range(CONV_W):
            acc = acc + xp[t + k] * v512[_V_CONV_W + k:_V_CONV_W + k + 1, :]
        xc_blocks.append(acc)
    for k in range(CONV_W - 1):
        conv_ref[k] = xp[n_t + k]
    xc = _cat(xc_blocks)

    t_r, t_i = _gate_tanh(_gate_preacts(xc, wg_ref), v512)
    a, mult = _lru_coeffs(t_r, v512)
    bterm = mult * (0.5 * t_i + 0.5) * xc
    h = h0_ref[...]
    h_blocks = []
    for t in range(n_t):
        h = blk(a, t) * h + blk(bterm, t)
        h_blocks.append(h)
    lru_ref[...] = h
    rg_out = _cat(h_blocks) * _gelu_tanh(yg)

    vn = _layer_norm(v, v512[_V_LN_G:_V_LN_G + 1, :], v512[_V_LN_B:_V_LN_B + 1, :])
    sg_blocks = []
    for t in range(n_t):
        for s in range(D_SG // LANES):
            vslab_ref[s, pl.ds(t, nb, stride=n_t), :] = blk(v, t)[:, s * LANES:(s + 1) * LANES]
        mixed = b8_ref[t:t + 1, :]
        for s in range(t + 1):
            mixed = mixed + w8_ref[t, s:s + 1, :] * blk(vn, s)
        sg_blocks.append(blk(u, t) * mixed)
    sg_out = _cat(sg_blocks)
    for s in range(D_SG // LANES):
        vrows_ref[:, s * LANES:(s + 1) * LANES] = vslab_ref[s]

    merged = jnp.concatenate(
        [_rms(rg_out, v512[_V_G_RG:_V_G_RG + 1, :]).astype(BF16),
         _rms(sg_out, v512[_V_G_SG:_V_G_SG + 1, :]).astype(BF16)], axis=1)

    def store_y(t, val):
        for s in range(D_MODEL // LANES):
            yslab_ref[s, pl.ds(t, nb, stride=n_t), :] = val[:, s * LANES:(s + 1) * LANES]

    tail = _tail_thunks(x_blocks, lambda: merged, mods, v1024, w_out_ref, w_ff1_ref, w_ff2_ref, store_y)
    assert TAIL_ORDER[0] == 'w_out' and TAIL_ORDER[2] == 'ff1_0' and TAIL_ORDER[-2] == 'ff2'

    @at_first_step
    def _():
        copies[0].wait()

    _run({n: tail[n] for n in TAIL_ORDER[:2]}, TAIL_ORDER[:2])

    @at_first_step
    def _():
        copies[1].wait()
        copies[2].start()

    _run({n: tail[n] for n in TAIL_ORDER[2:-2]}, TAIL_ORDER[2:-2])

    @at_first_step
    def _():
        copies[2].wait()

    _run({n: tail[n] for n in TAIL_ORDER[-2:]}, TAIL_ORDER[-2:])
    for s in range(D_MODEL // LANES):
        y_ref[:, s * LANES:(s + 1) * LANES] = yslab_ref[s]


def _pack_params(g_mix_ref, g_ffn_ref, g_final_ref, conv_w_ref, conv_b_ref, b_ra_ref, b_ri_ref, lru_l_ref,
                 ln_v_g_ref, ln_v_b_ref, g_rg_ref, g_sg_ref, w_ra_ref, w_ri_ref, w_s_ref, b_s_ref,
                 v1024_o, v512_o, wg_o, wsp_o, bias_s_o, w8_o, b8_o, wg_f):
    v1024_o[0:1, :] = g_mix_ref[...]
    v1024_o[1:2, :] = g_ffn_ref[...]
    v1024_o[2:3, :] = g_final_ref[...]
    v512_o[_V_CONV_W:_V_CONV_W + CONV_W, :] = conv_w_ref[...]
    for row, ref in ((_V_CONV_B, conv_b_ref), (_V_LRU_L, lru_l_ref), (_V_LN_G, ln_v_g_ref), (_V_LN_B, ln_v_b_ref),
                     (_V_G_RG, g_rg_ref), (_V_G_SG, g_sg_ref)):
        v512_o[row:row + 1, :] = ref[...]
    wg_f[...] = jnp.zeros(wg_f.shape, F32)
    per = RG_HEADS // GATE_GROUPS
    for h in range(RG_HEADS):
        g, lo = h // per, (h % per) * RG_HEAD_DIM
        v512_o[_V_B_RA:_V_B_RA + 1, h * RG_HEAD_DIM:(h + 1) * RG_HEAD_DIM] = b_ra_ref[h:h + 1, :]
        v512_o[_V_B_RI:_V_B_RI + 1, h * RG_HEAD_DIM:(h + 1) * RG_HEAD_DIM] = b_ri_ref[h:h + 1, :]
        wg_f[g, lo:lo + RG_HEAD_DIM, lo:lo + RG_HEAD_DIM] = w_ra_ref[h]
        wg_f[g, lo:lo + RG_HEAD_DIM, GATE_W + lo:GATE_W + lo + RG_HEAD_DIM] = w_ri_ref[h]
    wg_o[...] = (0.5 * wg_f[...]).astype(BF16)
    for h in range(SG_HEADS):
        wsp_o[h // 2, :, (h % 2) * CHUNK:(h % 2 + 1) * CHUNK] = w_s_ref[h].astype(BF16)
    head = lax.broadcasted_iota(jnp.int32, (1, D_SG), 1) // SG_HEAD_DIM
    b_t = b_s_ref[...].T
    bias = jnp.zeros((CHUNK, D_SG), F32)
    w_t = []
    for h in range(SG_HEADS):
        bias = jnp.where(head == h, b_t[:, h:h + 1], bias)
        w_t.append(w_s_ref[h, 0:SUBLANES, :].T[0:SUBLANES, :])
    bias_s_o[...] = bias
    b8_o[...] = bias[0:SUBLANES, :]
    for t in range(SUBLANES):
        acc = jnp.zeros((SUBLANES, D_SG), F32)
        for h in range(SG_HEADS):
            acc = jnp.where(head == h, w_t[h][:, t:t + 1], acc)
        w8_o[t] = acc


def _prep_kernel(cs_ref, cp_ref, w_ada_ref, b_ada_ref, w_in_ref, *rest):
    small, (ada_ref, w_in_o), packed = rest[:16], rest[16:18], rest[18:]
    c = jnp.concatenate([cs_ref[...], cp_ref[...]], axis=0)
    s = (c * jax.nn.sigmoid(c)).astype(BF16)
    ada_ref[...] = _dot(s, w_ada_ref[...].astype(BF16)) + b_ada_ref[...]
    w_in_o[...] = w_in_ref[...].astype(BF16)

    @pl.when(pl.program_id(0) == 0)
    def _():
        _pack_params(*small, *packed)


def _resident(shape):
    nd = len(shape)
    return pl.BlockSpec(shape, lambda *_: (0,) * nd, pipeline_mode=pl.Buffered(1))


def kernel(x_prompt, x_sample, c_prompt, c_sample, state_conv, state_rglru, w_ada, b_ada, g_mix, g_ffn, w_in, conv_w, conv_b, w_ra, b_ra, w_ri, b_ri, lru_l, ln_v_g, ln_v_b, w_s, b_s, g_rg, g_sg, w_out, w_ff1, w_ff2, g_final):
    batch, seq, _ = x_prompt.shape
    dec_batch, dec_seq, _ = x_sample.shape
    assert w_ada.shape[0] == 1, "single-layer step"
    assert seq % PROMPT_TILE == 0 and PROMPT_TILE % CHUNK == 0 and dec_batch % SAMPLE_NB == 0
    assert batch * (seq // PROMPT_TILE) >= 2, "the prompt pipeline needs at least two tiles"
    assert dec_seq <= CHUNK

    n_c = dec_batch + batch
    assert dec_batch % SUBLANES == 0 and dec_batch % batch == 0
    steps = PREP_STEPS

    def row_block(w):
        return pl.BlockSpec((w.shape[0] // steps, w.shape[1]), lambda n: (n, 0))

    def whole(a, lead=0):
        nd = a.ndim
        return pl.BlockSpec((None,) * lead + a.shape[lead:], lambda n: (0,) * nd)

    mats = (w_in[0],)
    gains =(g_mix, g_ffn, g_final.reshape(1, D_MODEL))
    rows512 = (conv_b, lru_l, ln_v_g, ln_v_b, g_rg, g_sg)
    small = gains + (conv_w, rows512[0], b_ra, b_ri) + rows512[1:] + (w_ra, w_ri, w_s, b_s)
    small_specs = [whole(a) for a in gains] + [whole(conv_w, 1), whole(conv_b), whole(b_ra, 1), whole(b_ri, 1)] + \
                  [whole(a) for a in rows512[1:]] + [whole(w_ra, 1), whole(w_ri, 1), whole(w_s, 1), whole(b_s, 1)]
    packed_shapes = ((3, D_MODEL), (12, D_RG), (GATE_GROUPS, GATE_W, 2 * GATE_W), (SG_HEADS // 2, CHUNK, 2 * CHUNK),
                     (CHUNK, D_SG), (SUBLANES, SUBLANES, D_SG), (SUBLANES, D_SG))
    packed_dtypes = (F32, F32, BF16, BF16, F32, F32, F32)
    packed_out = tuple(jax.ShapeDtypeStruct(s, d) for s, d in zip(packed_shapes, packed_dtypes))
    ada, w_in_b, v1024, v512, wg, wsp, bias_s, w8, b8 = pl.pallas_call(
        _prep_kernel,
        out_shape=(jax.ShapeDtypeStruct((n_c, 6 * D_MODEL), F32),) +
                  tuple(jax.ShapeDtypeStruct(w.shape, BF16) for w in mats) + packed_out,
        grid=(steps,),
        in_specs=[pl.BlockSpec((dec_batch, D_MODEL), lambda n: (0, 0)),
                  pl.BlockSpec((batch, D_MODEL), lambda n: (0, 0)),
                  pl.BlockSpec((D_MODEL, 6 * D_MODEL // steps), lambda n: (0, n)),
                  pl.BlockSpec((1, 6 * D_MODEL // steps), lambda n: (0, n))] + [row_block(w) for w in mats] +
                 small_specs,
        out_specs=(pl.BlockSpec((n_c, 6 * D_MODEL // steps), lambda n: (0, n)),) +
                  tuple(row_block(w) for w in mats) + tuple(whole(o) for o in packed_out),
        scratch_shapes=[pltpu.VMEM(packed_shapes[2], F32)],
        compiler_params=pltpu.CompilerParams(dimension_semantics=("arbitrary",),
                                             vmem_limit_bytes=VMEM_LIMIT_BYTES),
        name="prep",
    )(c_sample, c_prompt, w_ada[0], b_ada, *mats, *small)

    weights = (w_in_b, wg)
    weight_specs = [_resident(w_in_b.shape), _resident(wg.shape)]
    tail_f32 = (w_out[0], w_ff1[0], w_ff2[0])
    tail_shapes = tuple(jax.ShapeDtypeStruct(w.shape, BF16) for w in tail_f32)
    cparams = functools.partial(pltpu.CompilerParams, vmem_limit_bytes=VMEM_LIMIT_BYTES)

    tt = PROMPT_TILE
    nt = seq // tt
    n_tiles = batch * nt

    def cur(s):
        return jnp.minimum(s, n_tiles - 1)

    def prev(s):
        return jnp.maximum(s - 1, 0)

    y_p, conv_p, lru_p, vrows_p, *tail_weights = pl.pallas_call(
        functools.partial(_prompt_kernel, tiles_per_seq=nt),
        out_shape=(jax.ShapeDtypeStruct((batch, seq, D_MODEL), F32),
                   jax.ShapeDtypeStruct((CONV_W - 1, batch, D_RG), F32),
                   jax.ShapeDtypeStruct((batch, D_RG), F32),
                   jax.ShapeDtypeStruct((batch, CHUNK, D_SG), F32)) + tail_shapes,
        grid=(n_tiles + 1,),
        in_specs=[pl.BlockSpec((None, tt, D_MODEL), lambda s: (cur(s) // nt, cur(s) % nt, 0)),
                  pl.BlockSpec((None, tt, D_MODEL), lambda s: (prev(s) // nt, prev(s) % nt, 0)),
                  pl.BlockSpec((batch, 6 * D_MODEL), lambda s: (dec_batch // batch, 0),
                               pipeline_mode=pl.Buffered(1)),
                  _resident(v1024.shape), _resident(v512.shape)] + weight_specs +
                 [_resident(wsp.shape), _resident(bias_s.shape)] + [pl.BlockSpec(memory_space=pl.ANY)] * 3,
        out_specs=(pl.BlockSpec((None, tt, D_MODEL), lambda s: (prev(s) // nt, prev(s) % nt, 0)),
                   pl.BlockSpec((CONV_W - 1, batch, D_RG), lambda s: (0, 0, 0)),
                   pl.BlockSpec((batch, D_RG), lambda s: (0, 0)),
                   pl.BlockSpec((batch, CHUNK, D_SG), lambda s: (0, 0, 0))) + (pl.BlockSpec(memory_space=pl.ANY),) * 3,
        scratch_shapes=[pltpu.VMEM((D_RG // LANES, tt, LANES), F32), pltpu.VMEM((D_RG // LANES, tt, LANES), F32),
                        pltpu.VMEM((SUBLANES, D_RG), F32), pltpu.VMEM((SUBLANES, D_RG), F32),
                        pltpu.VMEM((tt, D_MODEL), BF16)] +
                       [pltpu.VMEM(w.shape, BF16) for w in tail_f32] +
                       [pltpu.VMEM((CAST_BUFS, CAST_ROWS, D_MODEL), F32),
                        pltpu.SemaphoreType.DMA((CAST_SEM0 + CAST_BUFS,))],
        compiler_params=cparams(dimension_semantics=("arbitrary",)),
        name="prompt_trunk",
    )(x_prompt, x_prompt, ada, v1024, v512, *weights, wsp, bias_s, *tail_f32)

    nb = SAMPLE_NB
    n_cs = CONV_W - 1
    assert dec_seq == SUBLANES, "one decode sequence per 8-row group"
    rows = nb * dec_seq

    y_s, conv_s, lru_s, vrows_s = pl.pallas_call(
        _sample_kernel,
        out_shape=(jax.ShapeDtypeStruct((dec_batch * dec_seq, D_MODEL), F32),
                   jax.ShapeDtypeStruct((n_cs, dec_batch, D_RG), F32),
                   jax.ShapeDtypeStruct((dec_batch, D_RG), F32),
                   jax.ShapeDtypeStruct((dec_batch * dec_seq, D_SG), F32)),
        grid=(dec_batch // nb,),
        in_specs=[pl.BlockSpec((rows, D_MODEL), lambda i: (i, 0)),
                  pl.BlockSpec((nb, 6 * D_MODEL), lambda i: (i, 0)),
                  pl.BlockSpec((n_cs, nb, D_RG), lambda i: (0, i, 0)),
                  pl.BlockSpec((nb, D_RG), lambda i: (i, 0)),
                  _resident(v1024.shape), _resident(v512.shape)] + weight_specs +
                 [_resident(w8.shape), _resident(b8.shape)] + [pl.BlockSpec(memory_space=pl.ANY)] * 3,
        out_specs=(pl.BlockSpec((rows, D_MODEL), lambda i: (i, 0)),
                   pl.BlockSpec((n_cs, nb, D_RG), lambda i: (0, i, 0)),
                   pl.BlockSpec((nb, D_RG), lambda i: (i, 0)),
                   pl.BlockSpec((rows, D_SG), lambda i: (i, 0))),
        scratch_shapes=[pltpu.VMEM((D_MODEL // LANES, rows, LANES), F32),
                        pltpu.VMEM((D_MODEL // LANES, rows, LANES), F32),
                        pltpu.VMEM((D_SG // LANES, rows, LANES), F32)] +
                       [pltpu.VMEM(w.shape, BF16) for w in tail_weights] + [pltpu.SemaphoreType.DMA((3,))],
        compiler_params=cparams(dimension_semantics=("arbitrary",)),
        name="sample_trunk",
    )(x_sample.reshape(dec_batch * dec_seq, D_MODEL), ada, state_conv[0].transpose(1, 0, 2), state_rglru[0],
      v1024, v512, *weights, w8, b8, *tail_weights)

    return (y_p, y_s.reshape(dec_batch, dec_seq, D_MODEL), conv_p.transpose(1, 0, 2)[None], lru_p[None],
            vrows_p[None], conv_s.transpose(1, 0, 2)[None], lru_s[None],
            vrows_s.reshape(1, dec_batch, dec_seq, D_SG))
```

```python
import functools

import jax
import jax.numpy as jnp
from jax import lax
from jax.experimental import pallas as pl
from jax.experimental.pallas import tpu as pltpu

D_MODEL = 1024
D_RG = 512
D_SG = 512
D_IN = 2 * D_RG + 2 * D_SG
D_FF = 4 * D_MODEL
RG_HEADS = 8
RG_HEAD_DIM = D_RG // RG_HEADS
SG_HEADS = 8
SG_HEAD_DIM = D_SG // SG_HEADS
CHUNK = 128
CONV_W = 4
LRU_C = 8.0
EPS = 1e-6

SUBLANES = 8
LANES = 128
GATE_GROUPS = 2
GATE_W = D_RG // GATE_GROUPS
PROMPT_TILE = 512
SAMPLE_NB = 64
FF_CHUNK = 1024
N_FF_CHUNKS = D_FF // FF_CHUNK
PREP_STEPS = 2
CAST_ROWS = 256
CAST_BUFS = 4
CAST_SEM0 = 3
VMEM_LIMIT_BYTES = 60000 * 1024

F32 = jnp.float32
BF16 = jnp.bfloat16

_V_CONV_W, _V_CONV_B, _V_B_RA, _V_B_RI, _V_LRU_L, _V_LN_G, _V_LN_B, _V_G_RG, _V_G_SG = 0, 4, 5, 6, 7, 8, 9, 10, 11


def _dot(a, b):
    return jnp.dot(a, b, preferred_element_type=F32)


def _rms(x, g):
    ms = jnp.mean(x * x, axis=-1, keepdims=True)
    return (x * lax.rsqrt(ms + EPS)) * g


def _gelu_tanh(x):
    k = 0.7978845608028654
    u = x * (k + (k * 0.044715) * (x * x))
    return (0.5 * x) * (1.0 + jnp.tanh(u))


def _layer_norm(x, g, b):
    mu = jnp.mean(x, axis=-1, keepdims=True)
    xc = x - mu
    var = jnp.mean(xc * xc, axis=-1, keepdims=True)
    return (xc * lax.rsqrt(var + EPS)) * g + b


def _cat(parts, axis=0):
    return parts[0] if len(parts) == 1 else jnp.concatenate(parts, axis=axis)


def _split_mods(ada):
    return [ada[:, k * D_MODEL:(k + 1) * D_MODEL] for k in range(6)]


def _modulated_norm(x_blocks, g, sc, sh):
    gain = g * (1.0 + sc)
    return _cat([(_rms(xb, gain) + sh).astype(BF16) for xb in x_blocks])


def _gate_preacts(xc, wg_ref):
    xcb = xc.astype(BF16)
    return _cat([_dot(xcb[:, g * GATE_W:(g + 1) * GATE_W], wg_ref[g]) for g in range(GATE_GROUPS)], 1)


def _gate_tanh(pre_half, v512):
    r_parts, i_parts = [], []
    for g in range(GATE_GROUPS):
        lo, hi = g * GATE_W, (g + 1) * GATE_W
        ra = pre_half[:, 2 * lo:2 * lo + GATE_W]
        ri = pre_half[:, 2 * lo + GATE_W:2 * hi]
        r_parts.append(jnp.tanh(ra + 0.5 * v512[_V_B_RA:_V_B_RA + 1, lo:hi]))
        i_parts.append(jnp.tanh(ri + 0.5 * v512[_V_B_RI:_V_B_RI + 1, lo:hi]))
    return _cat(r_parts, 1), _cat(i_parts, 1)


def _lru_coeffs(t_r, v512):
    lam = v512[_V_LRU_L:_V_LRU_L + 1, :]
    log_sig = jnp.minimum(lam, 0.0) - jnp.log1p(jnp.exp(-jnp.abs(lam)))
    k = (0.5 * LRU_C) * log_sig
    log_a = k * t_r + k
    a = jnp.exp(log_a)
    z = 1.0 - a * a
    mult = jnp.where(z > 0.0, z * lax.rsqrt(z), 0.0)
    return a, mult


def _tail_thunks(x_blocks, merged_fn, mods, v1024, w_out_ref, w_ff1_ref, w_ff2_ref, store_y):
    _, _, gt_m, sh_f, sc_f, gt_f = mods
    rows = x_blocks[0].shape[0]
    st = {}

    def w_out():
        mm = _dot(merged_fn(), w_out_ref[...])
        st['x1'] = [xb + gt_m * mm[i * rows:(i + 1) * rows] for i, xb in enumerate(x_blocks)]

    def hf():
        st['hf'] = _modulated_norm(st['x1'], v1024[1:2, :], sc_f, sh_f)

    def ff1(c):
        pre = _dot(st['hf'], w_ff1_ref[:, c * FF_CHUNK:(c + 1) * FF_CHUNK])
        h1 = jnp.maximum(pre.astype(BF16), 0.0)
        st['h1_%d' % c] = h1 * h1

    def ff2():
        h1 = _cat([st.pop('h1_%d' % c) for c in range(N_FF_CHUNKS)], 1)
        st['acc'] = _dot(h1, w_ff2_ref[...])

    def final():
        for i, xb in enumerate(st['x1']):
            x2 = xb + gt_f * st['acc'][i * rows:(i + 1) * rows]
            store_y(i, _rms(x2, v1024[2:3, :]))

    thunks = {'w_out': w_out, 'hf': hf, 'ff2': ff2, 'final': final}
    for c in range(N_FF_CHUNKS):
        thunks['ff1_%d' % c] = functools.partial(ff1, c)
    return thunks


TAIL_ORDER = ['w_out', 'hf'] + ['ff1_%d' % c for c in range(N_FF_CHUNKS)] + ['ff2', 'final']


def _run(thunks, order):
    assert sorted(order) == sorted(thunks), (sorted(order), sorted(thunks))
    for name in order:
        thunks[name]()


def _store_if(valid, ref, idx, val):
    keep = jnp.zeros(val.shape, jnp.int32) + valid.astype(jnp.int32)
    ref[idx] = jnp.where(keep > 0, val, ref[idx])


def _prompt_mix_thunks(x_ref, ada, v1024, v512, w_in_ref, wg_ref, wsp_ref, bias_s_ref, conv_ref, lru_ref,
                       vrows_ref, xperm_ref, hperm_ref, tail_ref, h_ref, merged_ref, j, slot, valid):
    tt = x_ref.shape[0]
    n_sub = SUBLANES
    sub_len = tt // n_sub
    n_slab = D_RG // LANES
    st = {}

    def norm():
        mods = _split_mods(ada)
        st['hm'] = _modulated_norm([x_ref[...]], v1024[0:1, :], mods[1], mods[0])

    def project():
        st['proj'] = proj = _dot(st['hm'], w_in_ref[...])
        xr = proj[:, 0:D_RG]
        st['xr_tail'] = xr[tt - SUBLANES:, :]
        for p in range(n_sub):
            for s in range(n_slab):
                xperm_ref[s, pl.ds(p, sub_len, stride=n_sub), :] = (
                    xr[p * sub_len:(p + 1) * sub_len, s * LANES:(s + 1) * LANES])

    def conv():
        xr_p = _cat([xperm_ref[s] for s in range(n_slab)], 1)
        sub = lax.broadcasted_iota(jnp.int32, (SUBLANES, D_RG), 0)
        prev_tail = tail_ref[...]
        head = []
        for k in range(CONV_W - 1, 0, -1):
            grp = pltpu.roll(xr_p[(sub_len - k) * SUBLANES:(sub_len - k + 1) * SUBLANES, :], 1, 0)
            head.append(jnp.where(sub == 0, prev_tail[SUBLANES - k:SUBLANES - k + 1, :], grp))
        ext = _cat(head + [xr_p])
        xc = v512[_V_CONV_B:_V_CONV_B + 1, :]
        for k in range(CONV_W):
            xc = xc + ext[k * SUBLANES:k * SUBLANES + tt, :] * v512[_V_CONV_W + k:_V_CONV_W + k + 1, :]
        st['xc'] = xc
        tail_ref[...] = st['xr_tail']
        for k in range(CONV_W - 1):
            row = SUBLANES - (CONV_W - 1) + k
            _store_if(valid, conv_ref, (k, pl.ds(slot, 1), slice(None)), st['xr_tail'][row:row + 1, :])

    def gate_mm():
        st['gate_pre'] = _gate_preacts(st['xc'], wg_ref)

    def coefficients():
        t_r, t_i = _gate_tanh(st.pop('gate_pre'), v512)
        a, mult = _lru_coeffs(t_r, v512)
        row = lax.broadcasted_iota(jnp.int32, (tt, D_RG), 0)
        mult = jnp.where(row + j * tt == 0, 1.0, mult)
        st['a'] = a
        st['b'] = mult * (0.5 * t_i + 0.5) * st.pop('xc')

    def recurrence():
        a, b = st['a'], st['b']
        sub = lax.broadcasted_iota(jnp.int32, (SUBLANES, D_RG), 0)
        hs, ps = [], []
        for q in range(sub_len):
            aq = a[q * SUBLANES:(q + 1) * SUBLANES, :]
            bq = b[q * SUBLANES:(q + 1) * SUBLANES, :]
            hs.append(bq if q == 0 else aq * hs[-1] + bq)
            ps.append(aq if q == 0 else aq * ps[-1])
        init = jnp.where(sub == 0, h_ref[SUBLANES - 1:SUBLANES, :], 0.0)
        for p in range(1, n_sub):
            end = hs[-1] + ps[-1] * init
            init = jnp.where(sub == p, pltpu.roll(end, 1, 0), init)
        end = hs[-1] + ps[-1] * init
        h_ref[...] = end
        _store_if(valid, lru_ref, (pl.ds(slot, 1), slice(None)), end[SUBLANES - 1:SUBLANES, :])
        h_perm = _cat([hq + pq * init for hq, pq in zip(hs, ps)])
        for s in range(n_slab):
            hperm_ref[s] = h_perm[:, s * LANES:(s + 1) * LANES]

    def recurrent_out(c):
        r0 = c * CHUNK
        yg = st['proj'][r0:r0 + CHUNK, D_RG:2 * D_RG]
        subs = range(r0 // sub_len, (r0 + CHUNK) // sub_len)
        h = _cat([_cat([hperm_ref[s, pl.ds(p, sub_len, stride=n_sub), :] for p in subs]) for s in range(n_slab)], 1)
        rg_out = h * _gelu_tanh(yg)
        merged_ref[r0:r0 + CHUNK, 0:D_RG] = _rms(rg_out, v512[_V_G_RG:_V_G_RG + 1, :]).astype(BF16)

    def spatial_pre(c):
        r0 = c * CHUNK
        lane = lax.broadcasted_iota(jnp.int32, (CHUNK, 2 * SG_HEAD_DIM), 1)
        v = st['proj'][r0:r0 + CHUNK, 2 * D_RG + D_SG:]
        vn = _layer_norm(v, v512[_V_LN_G:_V_LN_G + 1, :], v512[_V_LN_B:_V_LN_B + 1, :])
        rhs = []
        for p in range(SG_HEADS // 2):
            vp = vn[:, p * 2 * SG_HEAD_DIM:(p + 1) * 2 * SG_HEAD_DIM]
            rhs.append(jnp.concatenate([jnp.where(lane < SG_HEAD_DIM, vp, 0.0),
                                        jnp.where(lane >= SG_HEAD_DIM, vp, 0.0)], axis=0).astype(BF16))
        st['rhs_%d' % c] = rhs
        if r0 + CHUNK == tt:
            _store_if(valid, vrows_ref, slot, v)

    def spatial_mm(c):
        t_idx = lax.broadcasted_iota(jnp.int32, (CHUNK, 2 * CHUNK), 0)
        s_idx = lax.broadcasted_iota(jnp.int32, (CHUNK, 2 * CHUNK), 1) & (CHUNK - 1)
        outs = []
        for p, rhs in enumerate(st.pop('rhs_%d' % c)):
            ws_pair = jnp.where(s_idx <= t_idx, wsp_ref[p], jnp.zeros((), BF16))
            outs.append(_dot(ws_pair, rhs))
        st['mixed_%d' % c] = _cat(outs, 1)

    def spatial_post(c):
        r0 = c * CHUNK
        u = st['proj'][r0:r0 + CHUNK, 2 * D_RG:2 * D_RG + D_SG]
        sg_out = u * (st.pop('mixed_%d' % c) + bias_s_ref[...])
        merged_ref[r0:r0 + CHUNK, D_RG:] = _rms(sg_out, v512[_V_G_SG:_V_G_SG + 1, :]).astype(BF16)

    thunks = {'norm': norm, 'w_in': project, 'conv': conv, 'gate_mm': gate_mm, 'coef': coefficients,
              'rec': recurrence}
    for c in range(tt // CHUNK):
        thunks['sp_pre_%d' % c] = functools.partial(spatial_pre, c)
        thunks['sp_mm_%d' % c] = functools.partial(spatial_mm, c)
        thunks['sp_post_%d' % c] = functools.partial(spatial_post, c)
        thunks['rg_out_%d' % c] = functools.partial(recurrent_out, c)
    return thunks


N_PROMPT_CHUNKS = PROMPT_TILE // CHUNK
MIX_ORDER = (['norm', 'w_in', 'conv', 'gate_mm', 'coef', 'rec'] +
             [f'{name}_{c}' for c in range(N_PROMPT_CHUNKS) for name in ('sp_pre', 'sp_mm', 'sp_post', 'rg_out')])

assert N_PROMPT_CHUNKS == 4 and N_FF_CHUNKS == 4
INTERLEAVED_ORDER = [
    ('a', 'norm'), ('b', 'w_out'), ('a', 'w_in'), ('b', 'hf'), ('b', 'ff1_0'), ('a', 'conv'),
    ('a', 'sp_pre_0'), ('a', 'sp_pre_1'), ('b', 'ff1_1'), ('a', 'gate_mm'), ('a', 'sp_pre_2'), ('a', 'sp_pre_3'),
    ('b', 'ff1_2'), ('a', 'coef'), ('a', 'sp_mm_0'), ('a', 'sp_mm_1'), ('b', 'ff1_3'), ('a', 'rec'),
    ('a', 'sp_mm_2'), ('a', 'sp_mm_3'), ('a', 'sp_post_0'), ('a', 'sp_post_1'),
    ('a', 'rg_out_0'), ('a', 'rg_out_1'), ('a', 'sp_post_2'), ('a', 'sp_post_3'),
    ('a', 'rg_out_2'), ('a', 'rg_out_3'), ('b', 'ff2'), ('b', 'final'),
]


def _tail_weight_copies(src_refs, dst_refs, sem):
    return [pltpu.make_async_copy(src, dst, sem.at[k]) for k, (src, dst) in enumerate(zip(src_refs, dst_refs))]


def _weight_cast_thunks(f32_hbm_refs, bf16_refs, stage_ref, sem):
    windows = []
    for src, dst in zip(f32_hbm_refs, bf16_refs):
        rows, cols = dst.shape
        for r in range(0, rows, CAST_ROWS):
            for c in range(0, cols, D_MODEL):
                windows.append((src, dst, pl.ds(r, CAST_ROWS), pl.ds(c, D_MODEL)))
    copies = [pltpu.make_async_copy(src.at[rs, cs], stage_ref.at[k % CAST_BUFS], sem.at[CAST_SEM0 + k % CAST_BUFS])
              for k, (src, _, rs, cs) in enumerate(windows)]

    def make(k):
        def run():
            if k == 0:
                for cp in copies[:CAST_BUFS]:
                    cp.start()
            copies[k].wait()
            _, dst, rs, cs = windows[k]
            dst[rs, cs] = stage_ref[k % CAST_BUFS].astype(BF16)
            if k + CAST_BUFS < len(copies):
                copies[k + CAST_BUFS].start()
        return run

    return [make(k) for k in range(len(copies))]


def _prompt_kernel(x_ref, xprev_ref, ada_ref, v1024_ref, v512_ref, w_in_ref, wg_ref, wsp_ref,
                   bias_s_ref, w_out_hbm, w_ff1_hbm, w_ff2_hbm,
                   y_ref, conv_ref, lru_ref, vrows_ref, w_out_o, w_ff1_o, w_ff2_o,
                   xperm_ref, hperm_ref, tail_ref, h_ref, merged_ref, w_out_ref, w_ff1_ref, w_ff2_ref, stage_ref, w_sem,
                   *, tiles_per_seq):
    s = pl.program_id(0)
    n_tiles = pl.num_programs(0) - 1
    j = lax.rem(s, tiles_per_seq)
    seq_cur = jnp.minimum(s, n_tiles - 1) // tiles_per_seq
    seq_prev = jnp.maximum(s - 1, 0) // tiles_per_seq

    tail_refs = (w_out_ref, w_ff1_ref, w_ff2_ref)
    out_copies = _tail_weight_copies(tail_refs, (w_out_o, w_ff1_o, w_ff2_o), w_sem)

    @pl.when(s == 0)
    def _():
        conv_ref[...] = jnp.zeros(conv_ref.shape, F32)
        lru_ref[...] = jnp.zeros(lru_ref.shape, F32)
        vrows_ref[...] = jnp.zeros(vrows_ref.shape, F32)

    @pl.when(j == 0)
    def _():
        tail_ref[...] = jnp.zeros((SUBLANES, D_RG), F32)
        h_ref[...] = jnp.zeros((SUBLANES, D_RG), F32)

    v1024 = v1024_ref[...]
    v512 = v512_ref[...]

    def store_y(_, val):
        y_ref[...] = val

    def stage_a():
        return _prompt_mix_thunks(x_ref, ada_ref[pl.ds(seq_cur, 1), :], v1024, v512, w_in_ref, wg_ref, wsp_ref,
                                  bias_s_ref, conv_ref, lru_ref, vrows_ref, xperm_ref, hperm_ref, tail_ref, h_ref,
                                  merged_ref, j, seq_cur, s < n_tiles)

    def stage_b():
        return _tail_thunks([xprev_ref[...]], lambda: merged_ref[...], _split_mods(ada_ref[pl.ds(seq_prev, 1), :]),
                            v1024, w_out_ref, w_ff1_ref, w_ff2_ref, store_y)

    @pl.when(s == 0)
    def _():
        casts = _weight_cast_thunks((w_out_hbm, w_ff1_hbm, w_ff2_hbm), tail_refs, stage_ref, w_sem)
        mix = stage_a()
        per = -(-len(casts) // len(MIX_ORDER))
        for n, name in enumerate(MIX_ORDER):
            for cast in casts[n * per:(n + 1) * per]:
                cast()
            mix[name]()
        for cp in out_copies:
            cp.start()

    @pl.when(s > 0)
    def _():
        stages = {'a': stage_a(), 'b': stage_b()}
        assert sorted(n for k, n in INTERLEAVED_ORDER if k == 'a') == sorted(stages['a'])
        assert sorted(n for k, n in INTERLEAVED_ORDER if k == 'b') == sorted(stages['b'])
        for stage, name in INTERLEAVED_ORDER:
            stages[stage][name]()

    @pl.when(s == n_tiles)
    def _():
        for cp in out_copies:
            cp.wait()


def _sample_kernel(x_ref, ada_ref, cs_ref, h0_ref, v1024_ref, v512_ref, w_in_ref, wg_ref, w8_ref, b8_ref,
                   w_out_hbm, w_ff1_hbm, w_ff2_hbm,
                   y_ref, conv_ref, lru_ref, vrows_ref,
                   xslab_ref, yslab_ref, vslab_ref, w_out_ref, w_ff1_ref, w_ff2_ref, w_sem):
    n_t = SUBLANES
    nb = x_ref.shape[0] // n_t
    copies = _tail_weight_copies((w_out_hbm, w_ff1_hbm, w_ff2_hbm), (w_out_ref, w_ff1_ref, w_ff2_ref), w_sem)

    def at_first_step(fn):
        pl.when(pl.program_id(0) == 0)(fn)

    @at_first_step
    def _():
        copies[0].start()
        copies[1].start()

    mods = _split_mods(ada_ref[...])
    sh_m, sc_m = mods[0], mods[1]
    v1024 = v1024_ref[...]
    v512 = v512_ref[...]

    def blk(arr, t):
        return arr[t * nb:(t + 1) * nb, :]

    for s in range(D_MODEL // LANES):
        xslab_ref[s] = x_ref[:, s * LANES:(s + 1) * LANES]
    x_blocks = [_cat([xslab_ref[s, pl.ds(t, nb, stride=n_t), :] for s in range(D_MODEL // LANES)], 1)
                for t in range(n_t)]
    hm = _modulated_norm(x_blocks, v1024[0:1, :], sc_m, sh_m)
    proj = _dot(hm, w_in_ref[...])
    xr = proj[:, 0:D_RG]
    yg = proj[:, D_RG:2 * D_RG]
    u = proj[:, 2 * D_RG:2 * D_RG + D_SG]
    v = proj[:, 2 * D_RG + D_SG:]

    xp = [cs_ref[k] for k in range(CONV_W - 1)] + [blk(xr, t) for t in range(n_t)]
    xc_blocks = []
    for t in range(n_t):
        acc = v512[_V_CONV_B:_V_CONV_B + 1, :]
        for k in range(CONV_W):
            acc = acc + xp[t + k] * v512[_V_CONV_W + k:_V_CONV_W + k + 1, :]
        xc_blocks.append(acc)
    for k in range(CONV_W - 1):
        conv_ref[k] = xp[n_t + k]
    xc = _cat(xc_blocks)

    t_r, t_i = _gate_tanh(_gate_preacts(xc, wg_ref), v512)
    a, mult = _lru_coeffs(t_r, v512)
    bterm = mult * (0.5 * t_i + 0.5) * xc
    h = h0_ref[...]
    h_blocks = []
    for t in range(n_t):
        h = blk(a, t) * h + blk(bterm, t)
        h_blocks.append(h)
    lru_ref[...] = h
    rg_out = _cat(h_blocks) * _gelu_tanh(yg)

    vn = _layer_norm(v, v512[_V_LN_G:_V_LN_G + 1, :], v512[_V_LN_B:_V_LN_B + 1, :])
    sg_blocks = []
    for t in range(n_t):
        for s in range(D_SG // LANES):
            vslab_ref[s, pl.ds(t, nb, stride=n_t), :] = blk(v, t)[:, s * LANES:(s + 1) * LANES]
        mixed = b8_ref[t:t + 1, :]
        for s in range(t + 1):
            mixed = mixed + w8_ref[t, s:s + 1, :] * blk(vn, s)
        sg_blocks.append(blk(u, t) * mixed)
    sg_out = _cat(sg_blocks)
    for s in range(D_SG // LANES):
        vrows_ref[:, s * LANES:(s + 1) * LANES] = vslab_ref[s]

    merged = jnp.concatenate(
        [_rms(rg_out, v512[_V_G_RG:_V_G_RG + 1, :]).astype(BF16),
         _rms(sg_out, v512[_V_G_SG:_V_G_SG + 1, :]).astype(BF16)], axis=1)

    def store_y(t, val):
        for s in range(D_MODEL // LANES):
            yslab_ref[s, pl.ds(t, nb, stride=n_t), :] = val[:, s * LANES:(s + 1) * LANES]

    tail = _tail_thunks(x_blocks, lambda: merged, mods, v1024, w_out_ref, w_ff1_ref, w_ff2_ref, store_y)
    assert TAIL_ORDER[0] == 'w_out' and TAIL_ORDER[2] == 'ff1_0' and TAIL_ORDER[-2] == 'ff2'

    @at_first_step
    def _():
        copies[0].wait()

    _run({n: tail[n] for n in TAIL_ORDER[:2]}, TAIL_ORDER[:2])

    @at_first_step
    def _():
        copies[1].wait()
        copies[2].start()

    _run({n: tail[n] for n in TAIL_ORDER[2:-2]}, TAIL_ORDER[2:-2])

    @at_first_step
    def _():
        copies[2].wait()

    _run({n: tail[n] for n in TAIL_ORDER[-2:]}, TAIL_ORDER[-2:])
    for s in range(D_MODEL // LANES):
        y_ref[:, s * LANES:(s + 1) * LANES] = yslab_ref[s]


def _pack_params(g_mix_ref, g_ffn_ref, g_final_ref, conv_w_ref, conv_b_ref, b_ra_ref, b_ri_ref, lru_l_ref,
                 ln_v_g_ref, ln_v_b_ref, g_rg_ref, g_sg_ref, w_ra_ref, w_ri_ref, w_s_ref, b_s_ref,
                 v1024_o, v512_o, wg_o, wsp_o, bias_s_o, w8_o, b8_o, wg_f):
    v1024_o[0:1, :] = g_mix_ref[...]
    v1024_o[1:2, :] = g_ffn_ref[...]
    v1024_o[2:3, :] = g_final_ref[...]
    v512_o[_V_CONV_W:_V_CONV_W + CONV_W, :] = conv_w_ref[...]
    for row, ref in ((_V_CONV_B, conv_b_ref), (_V_LRU_L, lru_l_ref), (_V_LN_G, ln_v_g_ref), (_V_LN_B, ln_v_b_ref),
                     (_V_G_RG, g_rg_ref), (_V_G_SG, g_sg_ref)):
        v512_o[row:row + 1, :] = ref[...]
    wg_f[...] = jnp.zeros(wg_f.shape, F32)
    per = RG_HEADS // GATE_GROUPS
    for h in range(RG_HEADS):
        g, lo = h // per, (h % per) * RG_HEAD_DIM
        v512_o[_V_B_RA:_V_B_RA + 1, h * RG_HEAD_DIM:(h + 1) * RG_HEAD_DIM] = b_ra_ref[h:h + 1, :]
        v512_o[_V_B_RI:_V_B_RI + 1, h * RG_HEAD_DIM:(h + 1) * RG_HEAD_DIM] = b_ri_ref[h:h + 1, :]
        wg_f[g, lo:lo + RG_HEAD_DIM, lo:lo + RG_HEAD_DIM] = w_ra_ref[h]
        wg_f[g, lo:lo + RG_HEAD_DIM, GATE_W + lo:GATE_W + lo + RG_HEAD_DIM] = w_ri_ref[h]
    wg_o[...] = (0.5 * wg_f[...]).astype(BF16)
    for h in range(SG_HEADS):
        wsp_o[h // 2, :, (h % 2) * CHUNK:(h % 2 + 1) * CHUNK] = w_s_ref[h].astype(BF16)
    head = lax.broadcasted_iota(jnp.int32, (1, D_SG), 1) // SG_HEAD_DIM
    b_t = b_s_ref[...].T
    bias = jnp.zeros((CHUNK, D_SG), F32)
    w_t = []
    for h in range(SG_HEADS):
        bias = jnp.where(head == h, b_t[:, h:h + 1], bias)
        w_t.append(w_s_ref[h, 0:SUBLANES, :].T[0:SUBLANES, :])
    bias_s_o[...] = bias
    b8_o[...] = bias[0:SUBLANES, :]
    for t in range(SUBLANES):
        acc = jnp.zeros((SUBLANES, D_SG), F32)
        for h in range(SG_HEADS):
            acc = jnp.where(head == h, w_t[h][:, t:t + 1], acc)
        w8_o[t] = acc


def _prep_kernel(cs_ref, cp_ref, w_ada_ref, b_ada_ref, w_in_ref, *rest):
    small, (ada_ref, w_in_o), packed = rest[:16], rest[16:18], rest[18:]
    c = jnp.concatenate([cs_ref[...], cp_ref[...]], axis=0)
    s = (c * jax.nn.sigmoid(c)).astype(BF16)
    ada_ref[...] = _dot(s, w_ada_ref[...].astype(BF16)) + b_ada_ref[...]
    w_in_o[...] = w_in_ref[...].astype(BF16)

    @pl.when(pl.program_id(0) == 0)
    def _():
        _pack_params(*small, *packed)


def _resident(shape):
    nd = len(shape)
    return pl.BlockSpec(shape, lambda *_: (0,) * nd, pipeline_mode=pl.Buffered(1))


def kernel(x_prompt, x_sample, c_prompt, c_sample, state_conv, state_rglru, w_ada, b_ada, g_mix, g_ffn, w_in, conv_w, conv_b, w_ra, b_ra, w_ri, b_ri, lru_l, ln_v_g, ln_v_b, w_s, b_s, g_rg, g_sg, w_out, w_ff1, w_ff2, g_final):
    batch, seq, _ = x_prompt.shape
    dec_batch, dec_seq, _ = x_sample.shape
    assert w_ada.shape[0] == 1, "single-layer step"
    assert seq % PROMPT_TILE == 0 and PROMPT_TILE % CHUNK == 0 and dec_batch % SAMPLE_NB == 0
    assert batch * (seq // PROMPT_TILE) >= 2, "the prompt pipeline needs at least two tiles"
    assert dec_seq <= CHUNK

    n_c = dec_batch + batch
    assert dec_batch % SUBLANES == 0 and dec_batch % batch == 0
    steps = PREP_STEPS

    def row_block(w):
        return pl.BlockSpec((w.shape[0] // steps, w.shape[1]), lambda n: (n, 0))

    def whole(a, lead=0):
        nd = a.ndim
        return pl.BlockSpec((None,) * lead + a.shape[lead:], lambda n: (0,) * nd)

    mats = (w_in[0],)
    gains =(g_mix, g_ffn, g_final.reshape(1, D_MODEL))
    rows512 = (conv_b, lru_l, ln_v_g, ln_v_b, g_rg, g_sg)
    small = gains + (conv_w, rows512[0], b_ra, b_ri) + rows512[1:] + (w_ra, w_ri, w_s, b_s)
    small_specs = [whole(a) for a in gains] + [whole(conv_w, 1), whole(conv_b), whole(b_ra, 1), whole(b_ri, 1)] + \
                  [whole(a) for a in rows512[1:]] + [whole(w_ra, 1), whole(w_ri, 1), whole(w_s, 1), whole(b_s, 1)]
    packed_shapes = ((3, D_MODEL), (12, D_RG), (GATE_GROUPS, GATE_W, 2 * GATE_W), (SG_HEADS // 2, CHUNK, 2 * CHUNK),
                     (CHUNK, D_SG), (SUBLANES, SUBLANES, D_SG), (SUBLANES, D_SG))
    packed_dtypes = (F32, F32, BF16, BF16, F32, F32, F32)
    packed_out = tuple(jax.ShapeDtypeStruct(s, d) for s, d in zip(packed_shapes, packed_dtypes))
    ada, w_in_b, v1024, v512, wg, wsp, bias_s, w8, b8 = pl.pallas_call(
        _prep_kernel,
        out_shape=(jax.ShapeDtypeStruct((n_c, 6 * D_MODEL), F32),) +
                  tuple(jax.ShapeDtypeStruct(w.shape, BF16) for w in mats) + packed_out,
        grid=(steps,),
        in_specs=[pl.BlockSpec((dec_batch, D_MODEL), lambda n: (0, 0)),
                  pl.BlockSpec((batch, D_MODEL), lambda n: (0, 0)),
                  pl.BlockSpec((D_MODEL, 6 * D_MODEL // steps), lambda n: (0, n)),
                  pl.BlockSpec((1, 6 * D_MODEL // steps), lambda n: (0, n))] + [row_block(w) for w in mats] +
                 small_specs,
        out_specs=(pl.BlockSpec((n_c, 6 * D_MODEL // steps), lambda n: (0, n)),) +
                  tuple(row_block(w) for w in mats) + tuple(whole(o) for o in packed_out),
        scratch_shapes=[pltpu.VMEM(packed_shapes[2], F32)],
        compiler_params=pltpu.CompilerParams(dimension_semantics=("arbitrary",),
                                             vmem_limit_bytes=VMEM_LIMIT_BYTES),
        name="prep",
    )(c_sample, c_prompt, w_ada[0], b_ada, *mats, *small)

    weights = (w_in_b, wg)
    weight_specs = [_resident(w_in_b.shape), _resident(wg.shape)]
    tail_f32 = (w_out[0], w_ff1[0], w_ff2[0])
    tail_shapes = tuple(jax.ShapeDtypeStruct(w.shape, BF16) for w in tail_f32)
    cparams = functools.partial(pltpu.CompilerParams, vmem_limit_bytes=VMEM_LIMIT_BYTES)

    tt = PROMPT_TILE
    nt = seq // tt
    n_tiles = batch * nt

    def cur(s):
        return jnp.minimum(s, n_tiles - 1)

    def prev(s):
        return jnp.maximum(s - 1, 0)

    y_p, conv_p, lru_p, vrows_p, *tail_weights = pl.pallas_call(
        functools.partial(_prompt_kernel, tiles_per_seq=nt),
        out_shape=(jax.ShapeDtypeStruct((batch, seq, D_MODEL), F32),
                   jax.ShapeDtypeStruct((CONV_W - 1, batch, D_RG), F32),
                   jax.ShapeDtypeStruct((batch, D_RG), F32),
                   jax.ShapeDtypeStruct((batch, CHUNK, D_SG), F32)) + tail_shapes,
        grid=(n_tiles + 1,),
        in_specs=[pl.BlockSpec((None, tt, D_MODEL), lambda s: (cur(s) // nt, cur(s) % nt, 0)),
                  pl.BlockSpec((None, tt, D_MODEL), lambda s: (prev(s) // nt, prev(s) % nt, 0)),
                  pl.BlockSpec((batch, 6 * D_MODEL), lambda s: (dec_batch // batch, 0),
                               pipeline_mode=pl.Buffered(1)),
                  _resident(v1024.shape), _resident(v512.shape)] + weight_specs +
                 [_resident(wsp.shape), _resident(bias_s.shape)] + [pl.BlockSpec(memory_space=pl.ANY)] * 3,
        out_specs=(pl.BlockSpec((None, tt, D_MODEL), lambda s: (prev(s) // nt, prev(s) % nt, 0)),
                   pl.BlockSpec((CONV_W - 1, batch, D_RG), lambda s: (0, 0, 0)),
                   pl.BlockSpec((batch, D_RG), lambda s: (0, 0)),
                   pl.BlockSpec((batch, CHUNK, D_SG), lambda s: (0, 0, 0))) + (pl.BlockSpec(memory_space=pl.ANY),) * 3,
        scratch_shapes=[pltpu.VMEM((D_RG // LANES, tt, LANES), F32), pltpu.VMEM((D_RG // LANES, tt, LANES), F32),
                        pltpu.VMEM((SUBLANES, D_RG), F32), pltpu.VMEM((SUBLANES, D_RG), F32),
                        pltpu.VMEM((tt, D_MODEL), BF16)] +
                       [pltpu.VMEM(w.shape, BF16) for w in tail_f32] +
                       [pltpu.VMEM((CAST_BUFS, CAST_ROWS, D_MODEL), F32),
                        pltpu.SemaphoreType.DMA((CAST_SEM0 + CAST_BUFS,))],
        compiler_params=cparams(dimension_semantics=("arbitrary",)),
        name="prompt_trunk",
    )(x_prompt, x_prompt, ada, v1024, v512, *weights, wsp, bias_s, *tail_f32)

    nb = SAMPLE_NB
    n_cs = CONV_W - 1
    assert dec_seq == SUBLANES, "one decode sequence per 8-row group"
    rows = nb * dec_seq

    y_s, conv_s, lru_s, vrows_s = pl.pallas_call(
        _sample_kernel,
        out_shape=(jax.ShapeDtypeStruct((dec_batch * dec_seq, D_MODEL), F32),
                   jax.ShapeDtypeStruct((n_cs, dec_batch, D_RG), F32),
                   jax.ShapeDtypeStruct((dec_batch, D_RG), F32),
                   jax.ShapeDtypeStruct((dec_batch * dec_seq, D_SG), F32)),
        grid=(dec_batch // nb,),
        in_specs=[pl.BlockSpec((rows, D_MODEL), lambda i: (i, 0)),
                  pl.BlockSpec((nb, 6 * D_MODEL), lambda i: (i, 0)),
                  pl.BlockSpec((n_cs, nb, D_RG), lambda i: (0, i, 0)),
                  pl.BlockSpec((nb, D_RG), lambda i: (i, 0)),
                  _resident(v1024.shape), _resident(v512.shape)] + weight_specs +
                 [_resident(w8.shape), _resident(b8.shape)] + [pl.BlockSpec(memory_space=pl.ANY)] * 3,
        out_specs=(pl.BlockSpec((rows, D_MODEL), lambda i: (i, 0)),
                   pl.BlockSpec((n_cs, nb, D_RG), lambda i: (0, i, 0)),
                   pl.BlockSpec((nb, D_RG), lambda i: (i, 0)),
                   pl.BlockSpec((rows, D_SG), lambda i: (i, 0))),
        scratch_shapes=[pltpu.VMEM((D_MODEL // LANES, rows, LANES), F32),
                        pltpu.VMEM((D_MODEL // LANES, rows, LANES), F32),
                        pltpu.VMEM((D_SG // LANES, rows, LANES), F32)] +
                       [pltpu.VMEM(w.shape, BF16) for w in tail_weights] + [pltpu.SemaphoreType.DMA((3,))],
        compiler_params=cparams(dimension_semantics=("arbitrary",)),
        name="sample_trunk",
    )(x_sample.reshape(dec_batch * dec_seq, D_MODEL), ada, state_conv[0].transpose(1, 0, 2), state_rglru[0],
      v1024, v512, *weights, w8, b8, *tail_weights)

    return (y_p, y_s.reshape(dec_batch, dec_seq, D_MODEL), conv_p.transpose(1, 0, 2)[None], lru_p[None],
            vrows_p[None], conv_s.transpose(1, 0, 2)[None], lru_s[None],
            vrows_s.reshape(1, dec_batch, dec_seq, D_SG))
```

```python
import functools

import jax
import jax.numpy as jnp
from jax import lax
from jax.experimental import pallas as pl
from jax.experimental.pallas import tpu as pltpu

D_MODEL = 1024
D_RG = 512
D_SG = 512
D_IN = 2 * D_RG + 2 * D_SG
D_FF = 4 * D_MODEL
RG_HEADS = 8
RG_HEAD_DIM = D_RG // RG_HEADS
SG_HEADS = 8
SG_HEAD_DIM = D_SG // SG_HEADS
CHUNK = 128
CONV_W = 4
LRU_C = 8.0
EPS = 1e-6

SUBLANES = 8
LANES = 128
GATE_GROUPS = 2
GATE_W = D_RG // GATE_GROUPS
PROMPT_TILE = 512
SAMPLE_NB = 64
FF_CHUNK = 1024
N_FF_CHUNKS = D_FF // FF_CHUNK
PREP_STEPS = 2
CAST_ROWS = 256
CAST_BUFS = 4
CAST_SEM0 = 3
VMEM_LIMIT_BYTES = 60000 * 1024

F32 = jnp.float32
BF16 = jnp.bfloat16

_V_CONV_W, _V_CONV_B, _V_B_RA, _V_B_RI, _V_LRU_L, _V_LN_G, _V_LN_B, _V_G_RG, _V_G_SG = 0, 4, 5, 6, 7, 8, 9, 10, 11


def _dot(a, b):
    return jnp.dot(a, b, preferred_element_type=F32)


def _rms(x, g):
    ms = jnp.mean(x * x, axis=-1, keepdims=True)
    return (x * lax.rsqrt(ms + EPS)) * g


def _gelu_tanh(x):
    k = 0.7978845608028654
    u = x * (k + (k * 0.044715) * (x * x))
    return (0.5 * x) * (1.0 + jnp.tanh(u))


def _layer_norm(x, g, b):
    mu = jnp.mean(x, axis=-1, keepdims=True)
    xc = x - mu
    var = jnp.mean(xc * xc, axis=-1, keepdims=True)
    return (xc * lax.rsqrt(var + EPS)) * g + b


def _cat(parts, axis=0):
    return parts[0] if len(parts) == 1 else jnp.concatenate(parts, axis=axis)


def _split_mods(ada):
    return [ada[:, k * D_MODEL:(k + 1) * D_MODEL] for k in range(6)]


def _modulated_norm(x_blocks, g, sc, sh):
    gain = g * (1.0 + sc)
    return _cat([(_rms(xb, gain) + sh).astype(BF16) for xb in x_blocks])


def _gate_preacts(xc, wg_ref):
    xcb = xc.astype(BF16)
    return _cat([_dot(xcb[:, g * GATE_W:(g + 1) * GATE_W], wg_ref[g]) for g in range(GATE_GROUPS)], 1)


def _gate_tanh(pre_half, v512):
    r_parts, i_parts = [], []
    for g in range(GATE_GROUPS):
        lo, hi = g * GATE_W, (g + 1) * GATE_W
        ra = pre_half[:, 2 * lo:2 * lo + GATE_W]
        ri = pre_half[:, 2 * lo + GATE_W:2 * hi]
        r_parts.append(jnp.tanh(ra + 0.5 * v512[_V_B_RA:_V_B_RA + 1, lo:hi]))
        i_parts.append(jnp.tanh(ri + 0.5 * v512[_V_B_RI:_V_B_RI + 1, lo:hi]))
    return _cat(r_parts, 1), _cat(i_parts, 1)


def _lru_coeffs(t_r, v512):
    lam = v512[_V_LRU_L:_V_LRU_L + 1, :]
    log_sig = jnp.minimum(lam, 0.0) - jnp.log1p(jnp.exp(-jnp.abs(lam)))
    k = (0.5 * LRU_C) * log_sig
    log_a = k * t_r + k
    a = jnp.exp(log_a)
    z = 1.0 - a * a
    mult = jnp.where(z > 0.0, z * lax.rsqrt(z), 0.0)
    return a, mult


def _tail_thunks(x_blocks, merged_fn, mods, v1024, w_out_ref, w_ff1_ref, w_ff2_ref, store_y):
    _, _, gt_m, sh_f, sc_f, gt_f = mods
    rows = x_blocks[0].shape[0]
    st = {}

    def w_out():
        mm = _dot(merged_fn(), w_out_ref[...])
        st['x1'] = [xb + gt_m * mm[i * rows:(i + 1) * rows] for i, xb in enumerate(x_blocks)]

    def hf():
        st['hf'] = _modulated_norm(st['x1'], v1024[1:2, :], sc_f, sh_f)

    def ff1(c):
        pre = _dot(st['hf'], w_ff1_ref[:, c * FF_CHUNK:(c + 1) * FF_CHUNK])
        h1 = jnp.maximum(pre.astype(BF16), 0.0)
        st['h1_%d' % c] = h1 * h1

    def ff2():
        h1 = _cat([st.pop('h1_%d' % c) for c in range(N_FF_CHUNKS)], 1)
        st['acc'] = _dot(h1, w_ff2_ref[...])

    def final():
        for i, xb in enumerate(st['x1']):
            x2 = xb + gt_f * st['acc'][i * rows:(i + 1) * rows]
            store_y(i, _rms(x2, v1024[2:3, :]))

    thunks = {'w_out': w_out, 'hf': hf, 'ff2': ff2, 'final': final}
    for c in range(N_FF_CHUNKS):
        thunks['ff1_%d' % c] = functools.partial(ff1, c)
    return thunks


TAIL_ORDER = ['w_out', 'hf'] + ['ff1_%d' % c for c in range(N_FF_CHUNKS)] + ['ff2', 'final']


def _run(thunks, order):
    assert sorted(order) == sorted(thunks), (sorted(order), sorted(thunks))
    for name in order:
        thunks[name]()


def _store_if(valid, ref, idx, val):
    keep = jnp.zeros(val.shape, jnp.int32) + valid.astype(jnp.int32)
    ref[idx] = jnp.where(keep > 0, val, ref[idx])


def _prompt_mix_thunks(x_ref, ada, v1024, v512, w_in_ref, wg_ref, wsp_ref, bias_s_ref, conv_ref, lru_ref,
                       vrows_ref, xperm_ref, hperm_ref, tail_ref, h_ref, merged_ref, j, slot, valid):
    tt = x_ref.shape[0]
    n_sub = SUBLANES
    sub_len = tt // n_sub
    n_slab = D_RG // LANES
    st = {}

    def norm():
        mods = _split_mods(ada)
        st['hm'] = _modulated_norm([x_ref[...]], v1024[0:1, :], mods[1], mods[0])

    def project():
        st['proj'] = proj = _dot(st['hm'], w_in_ref[...])
        xr = proj[:, 0:D_RG]
        st['xr_tail'] = xr[tt - SUBLANES:, :]
        for p in range(n_sub):
            for s in range(n_slab):
                xperm_ref[s, pl.ds(p, sub_len, stride=n_sub), :] = (
                    xr[p * sub_len:(p + 1) * sub_len, s * LANES:(s + 1) * LANES])

    def conv():
        xr_p = _cat([xperm_ref[s] for s in range(n_slab)], 1)
        sub = lax.broadcasted_iota(jnp.int32, (SUBLANES, D_RG), 0)
        prev_tail = tail_ref[...]
        head = []
        for k in range(CONV_W - 1, 0, -1):
            grp = pltpu.roll(xr_p[(sub_len - k) * SUBLANES:(sub_len - k + 1) * SUBLANES, :], 1, 0)
            head.append(jnp.where(sub == 0, prev_tail[SUBLANES - k:SUBLANES - k + 1, :], grp))
        ext = _cat(head + [xr_p])
        xc = v512[_V_CONV_B:_V_CONV_B + 1, :]
        for k in range(CONV_W):
            xc = xc + ext[k * SUBLANES:k * SUBLANES + tt, :] * v512[_V_CONV_W + k:_V_CONV_W + k + 1, :]
        st['xc'] = xc
        tail_ref[...] = st['xr_tail']
        for k in range(CONV_W - 1):
            row = SUBLANES - (CONV_W - 1) + k
            _store_if(valid, conv_ref, (k, pl.ds(slot, 1), slice(None)), st['xr_tail'][row:row + 1, :])

    def gate_mm():
        st['gate_pre'] = _gate_preacts(st['xc'], wg_ref)

    def coefficients():
        t_r, t_i = _gate_tanh(st.pop('gate_pre'), v512)
        a, mult = _lru_coeffs(t_r, v512)
        row = lax.broadcasted_iota(jnp.int32, (tt, D_RG), 0)
        mult = jnp.where(row + j * tt == 0, 1.0, mult)
        st['a'] = a
        st['b'] = mult * (0.5 * t_i + 0.5) * st.pop('xc')

    def recurrence():
        a, b = st['a'], st['b']
        sub = lax.broadcasted_iota(jnp.int32, (SUBLANES, D_RG), 0)
        hs, ps = [], []
        for q in range(sub_len):
            aq = a[q * SUBLANES:(q + 1) * SUBLANES, :]
            bq = b[q * SUBLANES:(q + 1) * SUBLANES, :]
            hs.append(bq if q == 0 else aq * hs[-1] + bq)
            ps.append(aq if q == 0 else aq * ps[-1])
        init = jnp.where(sub == 0, h_ref[SUBLANES - 1:SUBLANES, :], 0.0)
        for p in range(1, n_sub):
            end = hs[-1] + ps[-1] * init
            init = jnp.where(sub == p, pltpu.roll(end, 1, 0), init)
        end = hs[-1] + ps[-1] * init
        h_ref[...] = end
        _store_if(valid, lru_ref, (pl.ds(slot, 1), slice(None)), end[SUBLANES - 1:SUBLANES, :])
        h_perm = _cat([hq + pq * init for hq, pq in zip(hs, ps)])
        for s in range(n_slab):
            hperm_ref[s] = h_perm[:, s * LANES:(s + 1) * LANES]

    def recurrent_out(c):
        r0 = c * CHUNK
        yg = st['proj'][r0:r0 + CHUNK, D_RG:2 * D_RG]
        subs = range(r0 // sub_len, (r0 + CHUNK) // sub_len)
        h = _cat([_cat([hperm_ref[s, pl.ds(p, sub_len, stride=n_sub), :] for p in subs]) for s in range(n_slab)], 1)
        rg_out = h * _gelu_tanh(yg)
        merged_ref[r0:r0 + CHUNK, 0:D_RG] = _rms(rg_out, v512[_V_G_RG:_V_G_RG + 1, :]).astype(BF16)

    def spatial_pre(c):
        r0 = c * CHUNK
        lane = lax.broadcasted_iota(jnp.int32, (CHUNK, 2 * SG_HEAD_DIM), 1)
        v = st['proj'][r0:r0 + CHUNK, 2 * D_RG + D_SG:]
        vn = _layer_norm(v, v512[_V_LN_G:_V_LN_G + 1, :], v512[_V_LN_B:_V_LN_B + 1, :])
        rhs = []
        for p in range(SG_HEADS // 2):
            vp = vn[:, p * 2 * SG_HEAD_DIM:(p + 1) * 2 * SG_HEAD_DIM]
            rhs.append(jnp.concatenate([jnp.where(lane < SG_HEAD_DIM, vp, 0.0),
                                        jnp.where(lane >= SG_HEAD_DIM, vp, 0.0)], axis=0).astype(BF16))
        st['rhs_%d' % c] = rhs
        if r0 + CHUNK == tt:
            _store_if(valid, vrows_ref, slot, v)

    def spatial_mm(c):
        t_idx = lax.broadcasted_iota(jnp.int32, (CHUNK, 2 * CHUNK), 0)
        s_idx = lax.broadcasted_iota(jnp.int32, (CHUNK, 2 * CHUNK), 1) & (CHUNK - 1)
        outs = []
        for p, rhs in enumerate(st.pop('rhs_%d' % c)):
            ws_pair = jnp.where(s_idx <= t_idx, wsp_ref[p], jnp.zeros((), BF16))
            outs.append(_dot(ws_pair, rhs))
        st['mixed_%d' % c] = _cat(outs, 1)

    def spatial_post(c):
        r0 = c * CHUNK
        u = st['proj'][r0:r0 + CHUNK, 2 * D_RG:2 * D_RG + D_SG]
        sg_out = u * (st.pop('mixed_%d' % c) + bias_s_ref[...])
        merged_ref[r0:r0 + CHUNK, D_RG:] = _rms(sg_out, v512[_V_G_SG:_V_G_SG + 1, :]).astype(BF16)

    thunks = {'norm': norm, 'w_in': project, 'conv': conv, 'gate_mm': gate_mm, 'coef': coefficients,
              'rec': recurrence}
    for c in range(tt // CHUNK):
        thunks['sp_pre_%d' % c] = functools.partial(spatial_pre, c)
        thunks['sp_mm_%d' % c] = functools.partial(spatial_mm, c)
        thunks['sp_post_%d' % c] = functools.partial(spatial_post, c)
        thunks['rg_out_%d' % c] = functools.partial(recurrent_out, c)
    return thunks


N_PROMPT_CHUNKS = PROMPT_TILE // CHUNK
MIX_ORDER = (['norm', 'w_in', 'conv', 'gate_mm', 'coef', 'rec'] +
             [f'{name}_{c}' for c in range(N_PROMPT_CHUNKS) for name in ('sp_pre', 'sp_mm', 'sp_post', 'rg_out')])

assert N_PROMPT_CHUNKS == 4 and N_FF_CHUNKS == 4
INTERLEAVED_ORDER = [
    ('a', 'norm'), ('b', 'w_out'), ('a', 'w_in'), ('b', 'hf'), ('b', 'ff1_0'), ('a', 'conv'),
    ('a', 'sp_pre_0'), ('a', 'sp_pre_1'), ('b', 'ff1_1'), ('a', 'gate_mm'), ('a', 'sp_pre_2'), ('a', 'sp_pre_3'),
    ('b', 'ff1_2'), ('a', 'coef'), ('a', 'sp_mm_0'), ('a', 'sp_mm_1'), ('b', 'ff1_3'), ('a', 'rec'),
    ('a', 'sp_mm_2'), ('a', 'sp_mm_3'), ('a', 'sp_post_0'), ('a', 'sp_post_1'),
    ('a', 'rg_out_0'), ('a', 'rg_out_1'), ('a', 'sp_post_2'), ('a', 'sp_post_3'),
    ('a', 'rg_out_2'), ('a', 'rg_out_3'), ('b', 'ff2'), ('b', 'final'),
]


def _tail_weight_copies(src_refs, dst_refs, sem):
    return [pltpu.make_async_copy(src, dst, sem.at[k]) for k, (src, dst) in enumerate(zip(src_refs, dst_refs))]


def _weight_cast_thunks(f32_hbm_refs, bf16_refs, stage_ref, sem):
    windows = []
    for src, dst in zip(f32_hbm_refs, bf16_refs):
        rows, cols = dst.shape
        for r in range(0, rows, CAST_ROWS):
            for c in range(0, cols, D_MODEL):
                windows.append((src, dst, pl.ds(r, CAST_ROWS), pl.ds(c, D_MODEL)))
    copies = [pltpu.make_async_copy(src.at[rs, cs], stage_ref.at[k % CAST_BUFS], sem.at[CAST_SEM0 + k % CAST_BUFS])
              for k, (src, _, rs, cs) in enumerate(windows)]

    def make(k):
        def run():
            if k == 0:
                for cp in copies[:CAST_BUFS]:
                    cp.start()
            copies[k].wait()
            _, dst, rs, cs = windows[k]
            dst[rs, cs] = stage_ref[k % CAST_BUFS].astype(BF16)
            if k + CAST_BUFS < len(copies):
                copies[k + CAST_BUFS].start()
        return run

    return [make(k) for k in range(len(copies))]


def _prompt_kernel(x_ref, xprev_ref, ada_ref, v1024_ref, v512_ref, w_in_ref, wg_ref, wsp_ref,
                   bias_s_ref, w_out_hbm, w_ff1_hbm, w_ff2_hbm,
                   y_ref, conv_ref, lru_ref, vrows_ref, w_out_o, w_ff1_o, w_ff2_o,
                   xperm_ref, hperm_ref, tail_ref, h_ref, merged_ref, w_out_ref, w_ff1_ref, w_ff2_ref, stage_ref, w_sem,
                   *, tiles_per_seq):
    s = pl.program_id(0)
    n_tiles = pl.num_programs(0)
    a = s + 1
    j = lax.rem(a, tiles_per_seq)
    seq_cur = jnp.minimum(a, n_tiles - 1) // tiles_per_seq
    seq_prev = s // tiles_per_seq

    tail_refs = (w_out_ref, w_ff1_ref, w_ff2_ref)
    out_copies = _tail_weight_copies(tail_refs, (w_out_o, w_ff1_o, w_ff2_o), w_sem)

    @pl.when(s == 0)
    def _():
        conv_ref[...] = jnp.zeros(conv_ref.shape, F32)
        lru_ref[...] = jnp.zeros(lru_ref.shape, F32)
        vrows_ref[...] = jnp.zeros(vrows_ref.shape, F32)

    def reset_state():
        tail_ref[...] = jnp.zeros((SUBLANES, D_RG), F32)
        h_ref[...] = jnp.zeros((SUBLANES, D_RG), F32)

    pl.when(jnp.logical_and(s > 0, j == 0))(reset_state)

    v1024 = v1024_ref[...]
    v512 = v512_ref[...]

    def store_y(_, val):
        y_ref[...] = val

    def stage_a(src_ref, seq, j_tile, valid):
        return _prompt_mix_thunks(src_ref, ada_ref[pl.ds(seq, 1), :], v1024, v512, w_in_ref, wg_ref, wsp_ref,
                                  bias_s_ref, conv_ref, lru_ref, vrows_ref, xperm_ref, hperm_ref, tail_ref, h_ref,
                                  merged_ref, j_tile, seq, valid)

    def stage_b():
        return _tail_thunks([xprev_ref[...]], lambda: merged_ref[...], _split_mods(ada_ref[pl.ds(seq_prev, 1), :]),
                            v1024, w_out_ref, w_ff1_ref, w_ff2_ref, store_y)

    def spread(casts, thunks):
        per = -(-len(casts) // len(thunks))
        for n, thunk in enumerate(thunks):
            for cast in casts[n * per:(n + 1) * per]:
                cast()
            thunk()

    @pl.when(s == 0)
    def _():
        casts = _weight_cast_thunks((w_out_hbm, w_ff1_hbm, w_ff2_hbm), tail_refs, stage_ref, w_sem)
        n_early = len(casts) - w_ff2_ref.shape[0] // CAST_ROWS
        reset_state()
        mix = stage_a(xprev_ref, 0, 0, s >= 0)
        spread(casts[:n_early], [mix[name] for name in MIX_ORDER])
        tail = stage_b()
        cut = TAIL_ORDER.index('ff2')
        spread(casts[n_early:], [tail[name] for name in TAIL_ORDER[:cut]])
        for name in TAIL_ORDER[cut:]:
            tail[name]()
        if tiles_per_seq == 1:
            reset_state()
        _run(stage_a(x_ref, 1 // tiles_per_seq, 1 % tiles_per_seq, s >= 0), MIX_ORDER)
        for cp in out_copies:
            cp.start()

    @pl.when(s > 0)
    def _():
        stages = {'a': stage_a(x_ref, seq_cur, j, a < n_tiles), 'b': stage_b()}
        assert sorted(n for k, n in INTERLEAVED_ORDER if k == 'a') == sorted(stages['a'])
        assert sorted(n for k, n in INTERLEAVED_ORDER if k == 'b') == sorted(stages['b'])
        for stage, name in INTERLEAVED_ORDER:
            stages[stage][name]()

    @pl.when(s == n_tiles - 1)
    def _():
        for cp in out_copies:
            cp.wait()


def _sample_kernel(x_ref, ada_ref, cs_ref, h0_ref, v1024_ref, v512_ref, w_in_ref, wg_ref, w8_ref, b8_ref,
                   w_out_hbm, w_ff1_hbm, w_ff2_hbm,
                   y_ref, conv_ref, lru_ref, vrows_ref,
                   xslab_ref, yslab_ref, vslab_ref, w_out_ref, w_ff1_ref, w_ff2_ref, w_sem):
    n_t = SUBLANES
    nb = x_ref.shape[0] // n_t
    copies = _tail_weight_copies((w_out_hbm, w_ff1_hbm, w_ff2_hbm), (w_out_ref, w_ff1_ref, w_ff2_ref), w_sem)

    def at_first_step(fn):
        pl.when(pl.program_id(0) == 0)(fn)

    @at_first_step
    def _():
        copies[0].start()
        copies[1].start()

    mods = _split_mods(ada_ref[...])
    sh_m, sc_m = mods[0], mods[1]
    v1024 = v1024_ref[...]
    v512 = v512_ref[...]

    def blk(arr, t):
        return arr[t * nb:(t + 1) * nb, :]

    for s in range(D_MODEL // LANES):
        xslab_ref[s] = x_ref[:, s * LANES:(s + 1) * LANES]
    x_blocks = [_cat([xslab_ref[s, pl.ds(t, nb, stride=n_t), :] for s in range(D_MODEL // LANES)], 1)
                for t in range(n_t)]
    hm = _modulated_norm(x_blocks, v1024[0:1, :], sc_m, sh_m)
    proj = _dot(hm, w_in_ref[...])
    xr = proj[:, 0:D_RG]
    yg = proj[:, D_RG:2 * D_RG]
    u = proj[:, 2 * D_RG:2 * D_RG + D_SG]
    v = proj[:, 2 * D_RG + D_SG:]

    xp = [cs_ref[k] for k in range(CONV_W - 1)] + [blk(xr, t) for t in range(n_t)]
    xc_blocks = []
    for t in range(n_t):
        acc = v512[_V_CONV_B:_V_CONV_B + 1, :]
        for k in range(CONV_W):
            acc = acc + xp[t + k] * v512[_V_CONV_W + k:_V_CONV_W + k + 1, :]
        xc_blocks.append(acc)
    for k in range(CONV_W - 1):
        conv_ref[k] = xp[n_t + k]
    xc = _cat(xc_blocks)

    t_r, t_i = _gate_tanh(_gate_preacts(xc, wg_ref), v512)
    a, mult = _lru_coeffs(t_r, v512)
    bterm = mult * (0.5 * t_i + 0.5) * xc
    h = h0_ref[...]
    h_blocks = []
    for t in range(n_t):
        h = blk(a, t) * h + blk(bterm, t)
        h_blocks.append(h)
    lru_ref[...] = h
    rg_out = _cat(h_blocks) * _gelu_tanh(yg)

    vn = _layer_norm(v, v512[_V_LN_G:_V_LN_G + 1, :], v512[_V_LN_B:_V_LN_B + 1, :])
    sg_blocks = []
    for t in range(n_t):
        for s in range(D_SG // LANES):
            vslab_ref[s, pl.ds(t, nb, stride=n_t), :] = blk(v, t)[:, s * LANES:(s + 1) * LANES]
        mixed = b8_ref[t:t + 1, :]
        for s in range(t + 1):
            mixed = mixed + w8_ref[t, s:s + 1, :] * blk(vn, s)
        sg_blocks.append(blk(u, t) * mixed)
    sg_out = _cat(sg_blocks)
    for s in range(D_SG // LANES):
        vrows_ref[:, s * LANES:(s + 1) * LANES] = vslab_ref[s]

    merged = jnp.concatenate(
        [_rms(rg_out, v512[_V_G_RG:_V_G_RG + 1, :]).astype(BF16),
         _rms(sg_out, v512[_V_G_SG:_V_G_SG + 1, :]).astype(BF16)], axis=1)

    def store_y(t, val):
        for s in range(D_MODEL // LANES):
            yslab_ref[s, pl.ds(t, nb, stride=n_t), :] = val[:, s * LANES:(s + 1) * LANES]

    tail = _tail_thunks(x_blocks, lambda: merged, mods, v1024, w_out_ref, w_ff1_ref, w_ff2_ref, store_y)
    assert TAIL_ORDER[0] == 'w_out' and TAIL_ORDER[2] == 'ff1_0' and TAIL_ORDER[-2] == 'ff2'

    @at_first_step
    def _():
        copies[0].wait()

    _run({n: tail[n] for n in TAIL_ORDER[:2]}, TAIL_ORDER[:2])

    @at_first_step
    def _():
        copies[1].wait()
        copies[2].start()

    _run({n: tail[n] for n in TAIL_ORDER[2:-2]}, TAIL_ORDER[2:-2])

    @at_first_step
    def _():
        copies[2].wait()

    _run({n: tail[n] for n in TAIL_ORDER[-2:]}, TAIL_ORDER[-2:])
    for s in range(D_MODEL // LANES):
        y_ref[:, s * LANES:(s + 1) * LANES] = yslab_ref[s]


def _pack_params(g_mix_ref, g_ffn_ref, g_final_ref, conv_w_ref, conv_b_ref, b_ra_ref, b_ri_ref, lru_l_ref,
                 ln_v_g_ref, ln_v_b_ref, g_rg_ref, g_sg_ref, w_ra_ref, w_ri_ref, w_s_ref, b_s_ref,
                 v1024_o, v512_o, wg_o, wsp_o, bias_s_o, w8_o, b8_o, wg_f):
    v1024_o[0:1, :] = g_mix_ref[...]
    v1024_o[1:2, :] = g_ffn_ref[...]
    v1024_o[2:3, :] = g_final_ref[...]
    v512_o[_V_CONV_W:_V_CONV_W + CONV_W, :] = conv_w_ref[...]
    for row, ref in ((_V_CONV_B, conv_b_ref), (_V_LRU_L, lru_l_ref), (_V_LN_G, ln_v_g_ref), (_V_LN_B, ln_v_b_ref),
                     (_V_G_RG, g_rg_ref), (_V_G_SG, g_sg_ref)):
        v512_o[row:row + 1, :] = ref[...]
    wg_f[...] = jnp.zeros(wg_f.shape, F32)
    per = RG_HEADS // GATE_GROUPS
    for h in range(RG_HEADS):
        g, lo = h // per, (h % per) * RG_HEAD_DIM
        v512_o[_V_B_RA:_V_B_RA + 1, h * RG_HEAD_DIM:(h + 1) * RG_HEAD_DIM] = b_ra_ref[h:h + 1, :]
        v512_o[_V_B_RI:_V_B_RI + 1, h * RG_HEAD_DIM:(h + 1) * RG_HEAD_DIM] = b_ri_ref[h:h + 1, :]
        wg_f[g, lo:lo + RG_HEAD_DIM, lo:lo + RG_HEAD_DIM] = w_ra_ref[h]
        wg_f[g, lo:lo + RG_HEAD_DIM, GATE_W + lo:GATE_W + lo + RG_HEAD_DIM] = w_ri_ref[h]
    wg_o[...] = (0.5 * wg_f[...]).astype(BF16)
    for h in range(SG_HEADS):
        wsp_o[h // 2, :, (h % 2) * CHUNK:(h % 2 + 1) * CHUNK] = w_s_ref[h].astype(BF16)
    head = lax.broadcasted_iota(jnp.int32, (1, D_SG), 1) // SG_HEAD_DIM
    b_t = b_s_ref[...].T
    bias = jnp.zeros((CHUNK, D_SG), F32)
    w_t = []
    for h in range(SG_HEADS):
        bias = jnp.where(head == h, b_t[:, h:h + 1], bias)
        w_t.append(w_s_ref[h, 0:SUBLANES, :].T[0:SUBLANES, :])
    bias_s_o[...] = bias
    b8_o[...] = bias[0:SUBLANES, :]
    for t in range(SUBLANES):
        acc = jnp.zeros((SUBLANES, D_SG), F32)
        for h in range(SG_HEADS):
            acc = jnp.where(head == h, w_t[h][:, t:t + 1], acc)
        w8_o[t] = acc


def _prep_kernel(cs_ref, cp_ref, w_ada_ref, b_ada_ref, w_in_ref, *rest):
    small, (ada_ref, w_in_o), packed = rest[:16], rest[16:18], rest[18:]
    c = jnp.concatenate([cs_ref[...], cp_ref[...]], axis=0)
    s = (c * jax.nn.sigmoid(c)).astype(BF16)
    ada_ref[...] = _dot(s, w_ada_ref[...].astype(BF16)) + b_ada_ref[...]
    w_in_o[...] = w_in_ref[...].astype(BF16)

    @pl.when(pl.program_id(0) == 0)
    def _():
        _pack_params(*small, *packed)


def _resident(shape):
    nd = len(shape)
    return pl.BlockSpec(shape, lambda *_: (0,) * nd, pipeline_mode=pl.Buffered(1))


def kernel(x_prompt, x_sample, c_prompt, c_sample, state_conv, state_rglru, w_ada, b_ada, g_mix, g_ffn, w_in, conv_w, conv_b, w_ra, b_ra, w_ri, b_ri, lru_l, ln_v_g, ln_v_b, w_s, b_s, g_rg, g_sg, w_out, w_ff1, w_ff2, g_final):
    batch, seq, _ = x_prompt.shape
    dec_batch, dec_seq, _ = x_sample.shape
    assert w_ada.shape[0] == 1, "single-layer step"
    assert seq % PROMPT_TILE == 0 and PROMPT_TILE % CHUNK == 0 and dec_batch % SAMPLE_NB == 0
    assert batch * (seq // PROMPT_TILE) >= 2, "the prompt pipeline needs at least two tiles"
    assert dec_seq <= CHUNK

    n_c = dec_batch + batch
    assert dec_batch % SUBLANES == 0 and dec_batch % batch == 0
    steps = PREP_STEPS

    def row_block(w):
        return pl.BlockSpec((w.shape[0] // steps, w.shape[1]), lambda n: (n, 0))

    def whole(a, lead=0):
        nd = a.ndim
        return pl.BlockSpec((None,) * lead + a.shape[lead:], lambda n: (0,) * nd)

    mats = (w_in[0],)
    gains =(g_mix, g_ffn, g_final.reshape(1, D_MODEL))
    rows512 = (conv_b, lru_l, ln_v_g, ln_v_b, g_rg, g_sg)
    small = gains + (conv_w, rows512[0], b_ra, b_ri) + rows512[1:] + (w_ra, w_ri, w_s, b_s)
    small_specs = [whole(a) for a in gains] + [whole(conv_w, 1), whole(conv_b), whole(b_ra, 1), whole(b_ri, 1)] + \
                  [whole(a) for a in rows512[1:]] + [whole(w_ra, 1), whole(w_ri, 1), whole(w_s, 1), whole(b_s, 1)]
    packed_shapes = ((3, D_MODEL), (12, D_RG), (GATE_GROUPS, GATE_W, 2 * GATE_W), (SG_HEADS // 2, CHUNK, 2 * CHUNK),
                     (CHUNK, D_SG), (SUBLANES, SUBLANES, D_SG), (SUBLANES, D_SG))
    packed_dtypes = (F32, F32, BF16, BF16, F32, F32, F32)
    packed_out = tuple(jax.ShapeDtypeStruct(s, d) for s, d in zip(packed_shapes, packed_dtypes))
    ada, w_in_b, v1024, v512, wg, wsp, bias_s, w8, b8 = pl.pallas_call(
        _prep_kernel,
        out_shape=(jax.ShapeDtypeStruct((n_c, 6 * D_MODEL), F32),) +
                  tuple(jax.ShapeDtypeStruct(w.shape, BF16) for w in mats) + packed_out,
        grid=(steps,),
        in_specs=[pl.BlockSpec((dec_batch, D_MODEL), lambda n: (0, 0)),
                  pl.BlockSpec((batch, D_MODEL), lambda n: (0, 0)),
                  pl.BlockSpec((D_MODEL, 6 * D_MODEL // steps), lambda n: (0, n)),
                  pl.BlockSpec((1, 6 * D_MODEL // steps), lambda n: (0, n))] + [row_block(w) for w in mats] +
                 small_specs,
        out_specs=(pl.BlockSpec((n_c, 6 * D_MODEL // steps), lambda n: (0, n)),) +
                  tuple(row_block(w) for w in mats) + tuple(whole(o) for o in packed_out),
        scratch_shapes=[pltpu.VMEM(packed_shapes[2], F32)],
        compiler_params=pltpu.CompilerParams(dimension_semantics=("arbitrary",),
                                             vmem_limit_bytes=VMEM_LIMIT_BYTES),
        name="prep",
    )(c_sample, c_prompt, w_ada[0], b_ada, *mats, *small)

    weights = (w_in_b, wg)
    weight_specs = [_resident(w_in_b.shape), _resident(wg.shape)]
    tail_f32 = (w_out[0], w_ff1[0], w_ff2[0])
    tail_shapes = tuple(jax.ShapeDtypeStruct(w.shape, BF16) for w in tail_f32)
    cparams = functools.partial(pltpu.CompilerParams, vmem_limit_bytes=VMEM_LIMIT_BYTES)

    tt = PROMPT_TILE
    nt = seq // tt
    n_tiles = batch * nt

    def cur(s):
        return jnp.minimum(s + 1, n_tiles - 1)

    def prev(s):
        return s

    y_p, conv_p, lru_p, vrows_p, *tail_weights = pl.pallas_call(
        functools.partial(_prompt_kernel, tiles_per_seq=nt),
        out_shape=(jax.ShapeDtypeStruct((batch, seq, D_MODEL), F32),
                   jax.ShapeDtypeStruct((CONV_W - 1, batch, D_RG), F32),
                   jax.ShapeDtypeStruct((batch, D_RG), F32),
                   jax.ShapeDtypeStruct((batch, CHUNK, D_SG), F32)) + tail_shapes,
        grid=(n_tiles,),
        in_specs=[pl.BlockSpec((None, tt, D_MODEL), lambda s: (cur(s) // nt, cur(s) % nt, 0)),
                  pl.BlockSpec((None, tt, D_MODEL), lambda s: (prev(s) // nt, prev(s) % nt, 0)),
                  pl.BlockSpec((batch, 6 * D_MODEL), lambda s: (dec_batch // batch, 0),
                               pipeline_mode=pl.Buffered(1)),
                  _resident(v1024.shape), _resident(v512.shape)] + weight_specs +
                 [_resident(wsp.shape), _resident(bias_s.shape)] + [pl.BlockSpec(memory_space=pl.ANY)] * 3,
        out_specs=(pl.BlockSpec((None, tt, D_MODEL), lambda s: (prev(s) // nt, prev(s) % nt, 0)),
                   pl.BlockSpec((CONV_W - 1, batch, D_RG), lambda s: (0, 0, 0)),
                   pl.BlockSpec((batch, D_RG), lambda s: (0, 0)),
                   pl.BlockSpec((batch, CHUNK, D_SG), lambda s: (0, 0, 0))) + (pl.BlockSpec(memory_space=pl.ANY),) * 3,
        scratch_shapes=[pltpu.VMEM((D_RG // LANES, tt, LANES), F32), pltpu.VMEM((D_RG // LANES, tt, LANES), F32),
                        pltpu.VMEM((SUBLANES, D_RG), F32), pltpu.VMEM((SUBLANES, D_RG), F32),
                        pltpu.VMEM((tt, D_MODEL), BF16)] +
                       [pltpu.VMEM(w.shape, BF16) for w in tail_f32] +
                       [pltpu.VMEM((CAST_BUFS, CAST_ROWS, D_MODEL), F32),
                        pltpu.SemaphoreType.DMA((CAST_SEM0 + CAST_BUFS,))],
        compiler_params=cparams(dimension_semantics=("arbitrary",)),
        name="prompt_trunk",
    )(x_prompt, x_prompt, ada, v1024, v512, *weights, wsp, bias_s, *tail_f32)

    nb = SAMPLE_NB
    n_cs = CONV_W - 1
    assert dec_seq == SUBLANES, "one decode sequence per 8-row group"
    rows = nb * dec_seq

    y_s, conv_s, lru_s, vrows_s = pl.pallas_call(
        _sample_kernel,
        out_shape=(jax.ShapeDtypeStruct((dec_batch * dec_seq, D_MODEL), F32),
                   jax.ShapeDtypeStruct((n_cs, dec_batch, D_RG), F32),
                   jax.ShapeDtypeStruct((dec_batch, D_RG), F32),
                   jax.ShapeDtypeStruct((dec_batch * dec_seq, D_SG), F32)),
        grid=(dec_batch // nb,),
        in_specs=[pl.BlockSpec((rows, D_MODEL), lambda i: (i, 0)),
                  pl.BlockSpec((nb, 6 * D_MODEL), lambda i: (i, 0)),
                  pl.BlockSpec((n_cs, nb, D_RG), lambda i: (0, i, 0)),
                  pl.BlockSpec((nb, D_RG), lambda i: (i, 0)),
                  _resident(v1024.shape), _resident(v512.shape)] + weight_specs +
                 [_resident(w8.shape), _resident(b8.shape)] + [pl.BlockSpec(memory_space=pl.ANY)] * 3,
        out_specs=(pl.BlockSpec((rows, D_MODEL), lambda i: (i, 0)),
                   pl.BlockSpec((n_cs, nb, D_RG), lambda i: (0, i, 0)),
                   pl.BlockSpec((nb, D_RG), lambda i: (i, 0)),
                   pl.BlockSpec((rows, D_SG), lambda i: (i, 0))),
        scratch_shapes=[pltpu.VMEM((D_MODEL // LANES, rows, LANES), F32),
                        pltpu.VMEM((D_MODEL // LANES, rows, LANES), F32),
                        pltpu.VMEM((D_SG // LANES, rows, LANES), F32)] +
                       [pltpu.VMEM(w.shape, BF16) for w in tail_weights] + [pltpu.SemaphoreType.DMA((3,))],
        compiler_params=cparams(dimension_semantics=("arbitrary",)),
        name="sample_trunk",
    )(x_sample.reshape(dec_batch * dec_seq, D_MODEL), ada, state_conv[0].transpose(1, 0, 2), state_rglru[0],
      v1024, v512, *weights, w8, b8, *tail_weights)

    return (y_p, y_s.reshape(dec_batch, dec_seq, D_MODEL), conv_p.transpose(1, 0, 2)[None], lru_p[None],
            vrows_p[None], conv_s.transpose(1, 0, 2)[None], lru_s[None],
            vrows_s.reshape(1, dec_batch, dec_seq, D_SG))
```

```python
import functools

import jax
import jax.numpy as jnp
from jax import lax
from jax.experimental import pallas as pl
from jax.experimental.pallas import tpu as pltpu

D_MODEL = 1024
D_RG = 512
D_SG = 512
D_IN = 2 * D_RG + 2 * D_SG
D_FF = 4 * D_MODEL
RG_HEADS = 8
RG_HEAD_DIM = D_RG // RG_HEADS
SG_HEADS = 8
SG_HEAD_DIM = D_SG // SG_HEADS
CHUNK = 128
CONV_W = 4
LRU_C = 8.0
EPS = 1e-6

SUBLANES = 8
LANES = 128
GATE_GROUPS = 2
GATE_W = D_RG // GATE_GROUPS
PROMPT_TILE = 512
SAMPLE_NB = 64
FF_CHUNK = 1024
N_FF_CHUNKS = D_FF // FF_CHUNK
PREP_STEPS = 2
CAST_ROWS = 256
CAST_BUFS = 4
CAST_SEM0 = 3
VMEM_LIMIT_BYTES = 60000 * 1024

F32 = jnp.float32
BF16 = jnp.bfloat16

_V_CONV_W, _V_CONV_B, _V_B_RA, _V_B_RI, _V_LRU_L, _V_LN_G, _V_LN_B, _V_G_RG, _V_G_SG = 0, 4, 5, 6, 7, 8, 9, 10, 11


def _dot(a, b):
    return jnp.dot(a, b, preferred_element_type=F32)


def _rms(x, g):
    ms = jnp.mean(x * x, axis=-1, keepdims=True)
    return (x * lax.rsqrt(ms + EPS)) * g


def _gelu_tanh(x):
    k = 0.7978845608028654
    u = x * (k + (k * 0.044715) * (x * x))
    return (0.5 * x) * (1.0 + jnp.tanh(u))


def _layer_norm(x, g, b):
    mu = jnp.mean(x, axis=-1, keepdims=True)
    xc = x - mu
    var = jnp.mean(xc * xc, axis=-1, keepdims=True)
    return (xc * lax.rsqrt(var + EPS)) * g + b


def _cat(parts, axis=0):
    return parts[0] if len(parts) == 1 else jnp.concatenate(parts, axis=axis)


def _split_mods(ada):
    return [ada[:, k * D_MODEL:(k + 1) * D_MODEL] for k in range(6)]


def _modulated_norm(x_blocks, g, sc, sh):
    gain = g * (1.0 + sc)
    return _cat([(_rms(xb, gain) + sh).astype(BF16) for xb in x_blocks])


def _gate_preacts(xc, wg_ref):
    xcb = xc.astype(BF16)
    return _cat([_dot(xcb[:, g * GATE_W:(g + 1) * GATE_W], wg_ref[g]) for g in range(GATE_GROUPS)], 1)


def _gate_tanh(pre_half, v512):
    r_parts, i_parts = [], []
    for g in range(GATE_GROUPS):
        lo, hi = g * GATE_W, (g + 1) * GATE_W
        ra = pre_half[:, 2 * lo:2 * lo + GATE_W]
        ri = pre_half[:, 2 * lo + GATE_W:2 * hi]
        r_parts.append(jnp.tanh(ra + 0.5 * v512[_V_B_RA:_V_B_RA + 1, lo:hi]))
        i_parts.append(jnp.tanh(ri + 0.5 * v512[_V_B_RI:_V_B_RI + 1, lo:hi]))
    return _cat(r_parts, 1), _cat(i_parts, 1)


def _lru_coeffs(t_r, v512):
    lam = v512[_V_LRU_L:_V_LRU_L + 1, :]
    log_sig = jnp.minimum(lam, 0.0) - jnp.log1p(jnp.exp(-jnp.abs(lam)))
    k = (0.5 * LRU_C) * log_sig
    log_a = k * t_r + k
    a = jnp.exp(log_a)
    z = 1.0 - a * a
    mult = jnp.where(z > 0.0, z * lax.rsqrt(z), 0.0)
    return a, mult


def _tail_thunks(x_blocks, merged_fn, mods, v1024, w_out_ref, w_ff1_ref, w_ff2_ref, store_y):
    _, _, gt_m, sh_f, sc_f, gt_f = mods
    rows = x_blocks[0].shape[0]
    st = {}

    def w_out():
        mm = _dot(merged_fn(), w_out_ref[...])
        st['x1'] = [xb + gt_m * mm[i * rows:(i + 1) * rows] for i, xb in enumerate(x_blocks)]

    def hf():
        st['hf'] = _modulated_norm(st['x1'], v1024[1:2, :], sc_f, sh_f)

    def ff1(c):
        pre = _dot(st['hf'], w_ff1_ref[:, c * FF_CHUNK:(c + 1) * FF_CHUNK])
        h1 = jnp.maximum(pre.astype(BF16), 0.0)
        st['h1_%d' % c] = h1 * h1

    def ff2():
        h1 = _cat([st.pop('h1_%d' % c) for c in range(N_FF_CHUNKS)], 1)
        st['acc'] = _dot(h1, w_ff2_ref[...])

    def final():
        for i, xb in enumerate(st['x1']):
            x2 = xb + gt_f * st['acc'][i * rows:(i + 1) * rows]
            store_y(i, _rms(x2, v1024[2:3, :]))

    thunks = {'w_out': w_out, 'hf': hf, 'ff2': ff2, 'final': final}
    for c in range(N_FF_CHUNKS):
        thunks['ff1_%d' % c] = functools.partial(ff1, c)
    return thunks


TAIL_ORDER = ['w_out', 'hf'] + ['ff1_%d' % c for c in range(N_FF_CHUNKS)] + ['ff2', 'final']


def _run(thunks, order):
    assert sorted(order) == sorted(thunks), (sorted(order), sorted(thunks))
    for name in order:
        thunks[name]()


def _store_if(valid, ref, idx, val):
    keep = jnp.zeros(val.shape, jnp.int32) + valid.astype(jnp.int32)
    ref[idx] = jnp.where(keep > 0, val, ref[idx])


def _prompt_mix_thunks(x_ref, ada, v1024, v512, w_in_ref, wg_ref, wsp_ref, bias_s_ref, conv_ref, lru_ref,
                       vrows_ref, xperm_ref, hperm_ref, tail_ref, h_ref, merged_ref, j, slot, valid):
    tt = x_ref.shape[0]
    n_sub = SUBLANES
    sub_len = tt // n_sub
    n_slab = D_RG // LANES
    st = {}

    def norm():
        mods = _split_mods(ada)
        st['hm'] = _modulated_norm([x_ref[...]], v1024[0:1, :], mods[1], mods[0])

    def project():
        st['proj'] = proj = _dot(st['hm'], w_in_ref[...])
        xr = proj[:, 0:D_RG]
        st['xr_tail'] = xr[tt - SUBLANES:, :]
        for p in range(n_sub):
            for s in range(n_slab):
                xperm_ref[s, pl.ds(p, sub_len, stride=n_sub), :] = (
                    xr[p * sub_len:(p + 1) * sub_len, s * LANES:(s + 1) * LANES])

    def conv():
        xr_p = _cat([xperm_ref[s] for s in range(n_slab)], 1)
        sub = lax.broadcasted_iota(jnp.int32, (SUBLANES, D_RG), 0)
        prev_tail = tail_ref[...]
        head = []
        for k in range(CONV_W - 1, 0, -1):
            grp = pltpu.roll(xr_p[(sub_len - k) * SUBLANES:(sub_len - k + 1) * SUBLANES, :], 1, 0)
            head.append(jnp.where(sub == 0, prev_tail[SUBLANES - k:SUBLANES - k + 1, :], grp))
        ext = _cat(head + [xr_p])
        xc = v512[_V_CONV_B:_V_CONV_B + 1, :]
        for k in range(CONV_W):
            xc = xc + ext[k * SUBLANES:k * SUBLANES + tt, :] * v512[_V_CONV_W + k:_V_CONV_W + k + 1, :]
        st['xc'] = xc
        tail_ref[...] = st['xr_tail']
        for k in range(CONV_W - 1):
            row = SUBLANES - (CONV_W - 1) + k
            _store_if(valid, conv_ref, (k, pl.ds(slot, 1), slice(None)), st['xr_tail'][row:row + 1, :])

    def gate_mm():
        st['gate_pre'] = _gate_preacts(st['xc'], wg_ref)

    def coefficients():
        t_r, t_i = _gate_tanh(st.pop('gate_pre'), v512)
        a, mult = _lru_coeffs(t_r, v512)
        row = lax.broadcasted_iota(jnp.int32, (tt, D_RG), 0)
        mult = jnp.where(row + j * tt == 0, 1.0, mult)
        st['a'] = a
        st['b'] = mult * (0.5 * t_i + 0.5) * st.pop('xc')

    def recurrence():
        a, b = st['a'], st['b']
        sub = lax.broadcasted_iota(jnp.int32, (SUBLANES, D_RG), 0)
        hs, ps = [], []
        for q in range(sub_len):
            aq = a[q * SUBLANES:(q + 1) * SUBLANES, :]
            bq = b[q * SUBLANES:(q + 1) * SUBLANES, :]
            hs.append(bq if q == 0 else aq * hs[-1] + bq)
            ps.append(aq if q == 0 else aq * ps[-1])
        init = jnp.where(sub == 0, h_ref[SUBLANES - 1:SUBLANES, :], 0.0)
        for p in range(1, n_sub):
            end = hs[-1] + ps[-1] * init
            init = jnp.where(sub == p, pltpu.roll(end, 1, 0), init)
        end = hs[-1] + ps[-1] * init
        h_ref[...] = end
        _store_if(valid, lru_ref, (pl.ds(slot, 1), slice(None)), end[SUBLANES - 1:SUBLANES, :])
        h_perm = _cat([hq + pq * init for hq, pq in zip(hs, ps)])
        for s in range(n_slab):
            hperm_ref[s] = h_perm[:, s * LANES:(s + 1) * LANES]

    def recurrent_out(c):
        r0 = c * CHUNK
        yg = st['proj'][r0:r0 + CHUNK, D_RG:2 * D_RG]
        subs = range(r0 // sub_len, (r0 + CHUNK) // sub_len)
        h = _cat([_cat([hperm_ref[s, pl.ds(p, sub_len, stride=n_sub), :] for p in subs]) for s in range(n_slab)], 1)
        rg_out = h * _gelu_tanh(yg)
        merged_ref[r0:r0 + CHUNK, 0:D_RG] = _rms(rg_out, v512[_V_G_RG:_V_G_RG + 1, :]).astype(BF16)

    def spatial_pre(c):
        r0 = c * CHUNK
        lane = lax.broadcasted_iota(jnp.int32, (CHUNK, 2 * SG_HEAD_DIM), 1)
        v = st['proj'][r0:r0 + CHUNK, 2 * D_RG + D_SG:]
        vn = _layer_norm(v, v512[_V_LN_G:_V_LN_G + 1, :], v512[_V_LN_B:_V_LN_B + 1, :])
        rhs = []
        for p in range(SG_HEADS // 2):
            vp = vn[:, p * 2 * SG_HEAD_DIM:(p + 1) * 2 * SG_HEAD_DIM]
            rhs.append(jnp.concatenate([jnp.where(lane < SG_HEAD_DIM, vp, 0.0),
                                        jnp.where(lane >= SG_HEAD_DIM, vp, 0.0)], axis=0).astype(BF16))
        st['rhs_%d' % c] = rhs
        if r0 + CHUNK == tt:
            _store_if(valid, vrows_ref, slot, v)

    def spatial_mm(c):
        t_idx = lax.broadcasted_iota(jnp.int32, (CHUNK, 2 * CHUNK), 0)
        s_idx = lax.broadcasted_iota(jnp.int32, (CHUNK, 2 * CHUNK), 1) & (CHUNK - 1)
        outs = []
        for p, rhs in enumerate(st.pop('rhs_%d' % c)):
            ws_pair = jnp.where(s_idx <= t_idx, wsp_ref[p], jnp.zeros((), BF16))
            outs.append(_dot(ws_pair, rhs))
        st['mixed_%d' % c] = _cat(outs, 1)

    def spatial_post(c):
        r0 = c * CHUNK
        u = st['proj'][r0:r0 + CHUNK, 2 * D_RG:2 * D_RG + D_SG]
        sg_out = u * (st.pop('mixed_%d' % c) + bias_s_ref[...])
        merged_ref[r0:r0 + CHUNK, D_RG:] = _rms(sg_out, v512[_V_G_SG:_V_G_SG + 1, :]).astype(BF16)

    thunks = {'norm': norm, 'w_in': project, 'conv': conv, 'gate_mm': gate_mm, 'coef': coefficients,
              'rec': recurrence}
    for c in range(tt // CHUNK):
        thunks['sp_pre_%d' % c] = functools.partial(spatial_pre, c)
        thunks['sp_mm_%d' % c] = functools.partial(spatial_mm, c)
        thunks['sp_post_%d' % c] = functools.partial(spatial_post, c)
        thunks['rg_out_%d' % c] = functools.partial(recurrent_out, c)
    return thunks


N_PROMPT_CHUNKS = PROMPT_TILE // CHUNK
MIX_ORDER = (['norm', 'w_in', 'conv', 'gate_mm', 'coef', 'rec'] +
             [f'{name}_{c}' for c in range(N_PROMPT_CHUNKS) for name in ('sp_pre', 'sp_mm', 'sp_post', 'rg_out')])

assert N_PROMPT_CHUNKS == 4 and N_FF_CHUNKS == 4
INTERLEAVED_ORDER = [
    ('a', 'norm'), ('b', 'w_out'), ('a', 'w_in'), ('b', 'hf'), ('b', 'ff1_0'), ('a', 'conv'),
    ('a', 'sp_pre_0'), ('a', 'sp_pre_1'), ('b', 'ff1_1'), ('a', 'gate_mm'), ('a', 'sp_pre_2'), ('a', 'sp_pre_3'),
    ('b', 'ff1_2'), ('a', 'coef'), ('a', 'sp_mm_0'), ('a', 'sp_mm_1'), ('b', 'ff1_3'), ('a', 'rec'),
    ('a', 'sp_mm_2'), ('a', 'sp_mm_3'), ('a', 'sp_post_0'), ('a', 'sp_post_1'),
    ('a', 'rg_out_0'), ('a', 'rg_out_1'), ('a', 'sp_post_2'), ('a', 'sp_post_3'),
    ('a', 'rg_out_2'), ('a', 'rg_out_3'), ('b', 'ff2'), ('b', 'final'),
]


def _tail_weight_copies(src_refs, dst_refs, sem):
    return [pltpu.make_async_copy(src, dst, sem.at[k]) for k, (src, dst) in enumerate(zip(src_refs, dst_refs))]


def _weight_cast_thunks(f32_hbm_refs, bf16_refs, stage_ref, sem):
    windows = []
    for src, dst in zip(f32_hbm_refs, bf16_refs):
        rows, cols = dst.shape
        for r in range(0, rows, CAST_ROWS):
            for c in range(0, cols, D_MODEL):
                windows.append((src, dst, pl.ds(r, CAST_ROWS), pl.ds(c, D_MODEL)))
    copies = [pltpu.make_async_copy(src.at[rs, cs], stage_ref.at[k % CAST_BUFS], sem.at[CAST_SEM0 + k % CAST_BUFS])
              for k, (src, _, rs, cs) in enumerate(windows)]

    def start(k):
        copies[k].start(priority=k % 2)

    def make(k):
        def run():
            if k == 0:
                for q in range(CAST_BUFS):
                    start(q)
            copies[k].wait()
            _, dst, rs, cs = windows[k]
            dst[rs, cs] = stage_ref[k % CAST_BUFS].astype(BF16)
            if k + CAST_BUFS < len(copies):
                start(k + CAST_BUFS)
        return run

    return [make(k) for k in range(len(copies))]


def _prompt_kernel(x_ref, xprev_ref, ada_ref, v1024_ref, v512_ref, w_in_ref, wg_ref, wsp_ref,
                   bias_s_ref, w_out_hbm, w_ff1_hbm, w_ff2_hbm,
                   y_ref, conv_ref, lru_ref, vrows_ref, w_out_o, w_ff1_o, w_ff2_o,
                   xperm_ref, hperm_ref, tail_ref, h_ref, merged_ref, w_out_ref, w_ff1_ref, w_ff2_ref, stage_ref, w_sem,
                   *, tiles_per_seq):
    s = pl.program_id(0)
    n_tiles = pl.num_programs(0) - 1
    j = lax.rem(s, tiles_per_seq)
    seq_cur = jnp.minimum(s, n_tiles - 1) // tiles_per_seq
    seq_prev = jnp.maximum(s - 1, 0) // tiles_per_seq

    tail_refs = (w_out_ref, w_ff1_ref, w_ff2_ref)
    out_copies = _tail_weight_copies(tail_refs, (w_out_o, w_ff1_o, w_ff2_o), w_sem)

    @pl.when(s == 0)
    def _():
        conv_ref[...] = jnp.zeros(conv_ref.shape, F32)
        lru_ref[...] = jnp.zeros(lru_ref.shape, F32)
        vrows_ref[...] = jnp.zeros(vrows_ref.shape, F32)

    @pl.when(j == 0)
    def _():
        tail_ref[...] = jnp.zeros((SUBLANES, D_RG), F32)
        h_ref[...] = jnp.zeros((SUBLANES, D_RG), F32)

    v1024 = v1024_ref[...]
    v512 = v512_ref[...]

    def store_y(_, val):
        y_ref[...] = val

    def stage_a():
        return _prompt_mix_thunks(x_ref, ada_ref[pl.ds(seq_cur, 1), :], v1024, v512, w_in_ref, wg_ref, wsp_ref,
                                  bias_s_ref, conv_ref, lru_ref, vrows_ref, xperm_ref, hperm_ref, tail_ref, h_ref,
                                  merged_ref, j, seq_cur, s < n_tiles)

    def stage_b():
        return _tail_thunks([xprev_ref[...]], lambda: merged_ref[...], _split_mods(ada_ref[pl.ds(seq_prev, 1), :]),
                            v1024, w_out_ref, w_ff1_ref, w_ff2_ref, store_y)

    @pl.when(s == 0)
    def _():
        casts = _weight_cast_thunks((w_out_hbm, w_ff1_hbm, w_ff2_hbm), tail_refs, stage_ref, w_sem)
        mix = stage_a()
        per = -(-len(casts) // len(MIX_ORDER))
        for n, name in enumerate(MIX_ORDER):
            for cast in casts[n * per:(n + 1) * per]:
                cast()
            mix[name]()
        for cp in out_copies:
            cp.start()

    @pl.when(s > 0)
    def _():
        stages = {'a': stage_a(), 'b': stage_b()}
        assert sorted(n for k, n in INTERLEAVED_ORDER if k == 'a') == sorted(stages['a'])
        assert sorted(n for k, n in INTERLEAVED_ORDER if k == 'b') == sorted(stages['b'])
        for stage, name in INTERLEAVED_ORDER:
            stages[stage][name]()

    @pl.when(s == n_tiles)
    def _():
        for cp in out_copies:
            cp.wait()


def _sample_kernel(x_ref, ada_ref, cs_ref, h0_ref, v1024_ref, v512_ref, w_in_ref, wg_ref, w8_ref, b8_ref,
                   w_out_hbm, w_ff1_hbm, w_ff2_hbm,
                   y_ref, conv_ref, lru_ref, vrows_ref,
                   xslab_ref, yslab_ref, vslab_ref, w_out_ref, w_ff1_ref, w_ff2_ref, w_sem):
    n_t = SUBLANES
    nb = x_ref.shape[0] // n_t
    copies = _tail_weight_copies((w_out_hbm, w_ff1_hbm, w_ff2_hbm), (w_out_ref, w_ff1_ref, w_ff2_ref), w_sem)

    def at_first_step(fn):
        pl.when(pl.program_id(0) == 0)(fn)

    @at_first_step
    def _():
        copies[0].start()
        copies[1].start(priority=1)

    mods = _split_mods(ada_ref[...])
    sh_m, sc_m = mods[0], mods[1]
    v1024 = v1024_ref[...]
    v512 = v512_ref[...]

    def blk(arr, t):
        return arr[t * nb:(t + 1) * nb, :]

    for s in range(D_MODEL // LANES):
        xslab_ref[s] = x_ref[:, s * LANES:(s + 1) * LANES]
    x_blocks = [_cat([xslab_ref[s, pl.ds(t, nb, stride=n_t), :] for s in range(D_MODEL // LANES)], 1)
                for t in range(n_t)]
    hm = _modulated_norm(x_blocks, v1024[0:1, :], sc_m, sh_m)
    proj = _dot(hm, w_in_ref[...])
    xr = proj[:, 0:D_RG]
    yg = proj[:, D_RG:2 * D_RG]
    u = proj[:, 2 * D_RG:2 * D_RG + D_SG]
    v = proj[:, 2 * D_RG + D_SG:]

    xp = [cs_ref[k] for k in range(CONV_W - 1)] + [blk(xr, t) for t in range(n_t)]
    xc_blocks = []
    for t in range(n_t):
        acc = v512[_V_CONV_B:_V_CONV_B + 1, :]
        for k in range(CONV_W):
            acc = acc + xp[t + k] * v512[_V_CONV_W + k:_V_CONV_W + k + 1, :]
        xc_blocks.append(acc)
    for k in range(CONV_W - 1):
        conv_ref[k] = xp[n_t + k]
    xc = _cat(xc_blocks)

    t_r, t_i = _gate_tanh(_gate_preacts(xc, wg_ref), v512)
    a, mult = _lru_coeffs(t_r, v512)
    bterm = mult * (0.5 * t_i + 0.5) * xc
    h = h0_ref[...]
    h_blocks = []
    for t in range(n_t):
        h = blk(a, t) * h + blk(bterm, t)
        h_blocks.append(h)
    lru_ref[...] = h
    rg_out = _cat(h_blocks) * _gelu_tanh(yg)

    vn = _layer_norm(v, v512[_V_LN_G:_V_LN_G + 1, :], v512[_V_LN_B:_V_LN_B + 1, :])
    sg_blocks = []
    for t in range(n_t):
        for s in range(D_SG // LANES):
            vslab_ref[s, pl.ds(t, nb, stride=n_t), :] = blk(v, t)[:, s * LANES:(s + 1) * LANES]
        mixed = b8_ref[t:t + 1, :]
        for s in range(t + 1):
            mixed = mixed + w8_ref[t, s:s + 1, :] * blk(vn, s)
        sg_blocks.append(blk(u, t) * mixed)
    sg_out = _cat(sg_blocks)
    for s in range(D_SG // LANES):
        vrows_ref[:, s * LANES:(s + 1) * LANES] = vslab_ref[s]

    merged = jnp.concatenate(
        [_rms(rg_out, v512[_V_G_RG:_V_G_RG + 1, :]).astype(BF16),
         _rms(sg_out, v512[_V_G_SG:_V_G_SG + 1, :]).astype(BF16)], axis=1)

    def store_y(t, val):
        for s in range(D_MODEL // LANES):
            yslab_ref[s, pl.ds(t, nb, stride=n_t), :] = val[:, s * LANES:(s + 1) * LANES]

    tail = _tail_thunks(x_blocks, lambda: merged, mods, v1024, w_out_ref, w_ff1_ref, w_ff2_ref, store_y)
    assert TAIL_ORDER[0] == 'w_out' and TAIL_ORDER[2] == 'ff1_0' and TAIL_ORDER[-2] == 'ff2'

    @at_first_step
    def _():
        copies[0].wait()

    _run({n: tail[n] for n in TAIL_ORDER[:2]}, TAIL_ORDER[:2])

    @at_first_step
    def _():
        copies[1].wait()
        copies[2].start()

    _run({n: tail[n] for n in TAIL_ORDER[2:-2]}, TAIL_ORDER[2:-2])

    @at_first_step
    def _():
        copies[2].wait()

    _run({n: tail[n] for n in TAIL_ORDER[-2:]}, TAIL_ORDER[-2:])
    for s in range(D_MODEL // LANES):
        y_ref[:, s * LANES:(s + 1) * LANES] = yslab_ref[s]


def _pack_params(g_mix_ref, g_ffn_ref, g_final_ref, conv_w_ref, conv_b_ref, b_ra_ref, b_ri_ref, lru_l_ref,
                 ln_v_g_ref, ln_v_b_ref, g_rg_ref, g_sg_ref, w_ra_ref, w_ri_ref, w_s_ref, b_s_ref,
                 v1024_o, v512_o, wg_o, wsp_o, bias_s_o, w8_o, b8_o, wg_f):
    v1024_o[0:1, :] = g_mix_ref[...]
    v1024_o[1:2, :] = g_ffn_ref[...]
    v1024_o[2:3, :] = g_final_ref[...]
    v512_o[_V_CONV_W:_V_CONV_W + CONV_W, :] = conv_w_ref[...]
    for row, ref in ((_V_CONV_B, conv_b_ref), (_V_LRU_L, lru_l_ref), (_V_LN_G, ln_v_g_ref), (_V_LN_B, ln_v_b_ref),
                     (_V_G_RG, g_rg_ref), (_V_G_SG, g_sg_ref)):
        v512_o[row:row + 1, :] = ref[...]
    wg_f[...] = jnp.zeros(wg_f.shape, F32)
    per = RG_HEADS // GATE_GROUPS
    for h in range(RG_HEADS):
        g, lo = h // per, (h % per) * RG_HEAD_DIM
        v512_o[_V_B_RA:_V_B_RA + 1, h * RG_HEAD_DIM:(h + 1) * RG_HEAD_DIM] = b_ra_ref[h:h + 1, :]
        v512_o[_V_B_RI:_V_B_RI + 1, h * RG_HEAD_DIM:(h + 1) * RG_HEAD_DIM] = b_ri_ref[h:h + 1, :]
        wg_f[g, lo:lo + RG_HEAD_DIM, lo:lo + RG_HEAD_DIM] = w_ra_ref[h]
        wg_f[g, lo:lo + RG_HEAD_DIM, GATE_W + lo:GATE_W + lo + RG_HEAD_DIM] = w_ri_ref[h]
    wg_o[...] = (0.5 * wg_f[...]).astype(BF16)
    for h in range(SG_HEADS):
        wsp_o[h // 2, :, (h % 2) * CHUNK:(h % 2 + 1) * CHUNK] = w_s_ref[h].astype(BF16)
    head = lax.broadcasted_iota(jnp.int32, (1, D_SG), 1) // SG_HEAD_DIM
    b_t = b_s_ref[...].T
    bias = jnp.zeros((CHUNK, D_SG), F32)
    w_t = []
    for h in range(SG_HEADS):
        bias = jnp.where(head == h, b_t[:, h:h + 1], bias)
        w_t.append(w_s_ref[h, 0:SUBLANES, :].T[0:SUBLANES, :])
    bias_s_o[...] = bias
    b8_o[...] = bias[0:SUBLANES, :]
    for t in range(SUBLANES):
        acc = jnp.zeros((SUBLANES, D_SG), F32)
        for h in range(SG_HEADS):
            acc = jnp.where(head == h, w_t[h][:, t:t + 1], acc)
        w8_o[t] = acc


def _prep_kernel(cs_ref, cp_ref, w_ada_ref, b_ada_ref, w_in_ref, *rest):
    small, (ada_ref, w_in_o), packed = rest[:16], rest[16:18], rest[18:]
    c = jnp.concatenate([cs_ref[...], cp_ref[...]], axis=0)
    s = (c * jax.nn.sigmoid(c)).astype(BF16)
    ada_ref[...] = _dot(s, w_ada_ref[...].astype(BF16)) + b_ada_ref[...]
    w_in_o[...] = w_in_ref[...].astype(BF16)

    @pl.when(pl.program_id(0) == 0)
    def _():
        _pack_params(*small, *packed)


def _resident(shape):
    nd = len(shape)
    return pl.BlockSpec(shape, lambda *_: (0,) * nd, pipeline_mode=pl.Buffered(1))


def kernel(x_prompt, x_sample, c_prompt, c_sample, state_conv, state_rglru, w_ada, b_ada, g_mix, g_ffn, w_in, conv_w, conv_b, w_ra, b_ra, w_ri, b_ri, lru_l, ln_v_g, ln_v_b, w_s, b_s, g_rg, g_sg, w_out, w_ff1, w_ff2, g_final):
    batch, seq, _ = x_prompt.shape
    dec_batch, dec_seq, _ = x_sample.shape
    assert w_ada.shape[0] == 1, "single-layer step"
    assert seq % PROMPT_TILE == 0 and PROMPT_TILE % CHUNK == 0 and dec_batch % SAMPLE_NB == 0
    assert batch * (seq // PROMPT_TILE) >= 2, "the prompt pipeline needs at least two tiles"
    assert dec_seq <= CHUNK

    n_c = dec_batch + batch
    assert dec_batch % SUBLANES == 0 and dec_batch % batch == 0
    steps = PREP_STEPS

    def row_block(w):
        return pl.BlockSpec((w.shape[0] // steps, w.shape[1]), lambda n: (n, 0))

    def whole(a, lead=0):
        nd = a.ndim
        return pl.BlockSpec((None,) * lead + a.shape[lead:], lambda n: (0,) * nd)

    mats = (w_in[0],)
    gains =(g_mix, g_ffn, g_final.reshape(1, D_MODEL))
    rows512 = (conv_b, lru_l, ln_v_g, ln_v_b, g_rg, g_sg)
    small = gains + (conv_w, rows512[0], b_ra, b_ri) + rows512[1:] + (w_ra, w_ri, w_s, b_s)
    small_specs = [whole(a) for a in gains] + [whole(conv_w, 1), whole(conv_b), whole(b_ra, 1), whole(b_ri, 1)] + \
                  [whole(a) for a in rows512[1:]] + [whole(w_ra, 1), whole(w_ri, 1), whole(w_s, 1), whole(b_s, 1)]
    packed_shapes = ((3, D_MODEL), (12, D_RG), (GATE_GROUPS, GATE_W, 2 * GATE_W), (SG_HEADS // 2, CHUNK, 2 * CHUNK),
                     (CHUNK, D_SG), (SUBLANES, SUBLANES, D_SG), (SUBLANES, D_SG))
    packed_dtypes = (F32, F32, BF16, BF16, F32, F32, F32)
    packed_out = tuple(jax.ShapeDtypeStruct(s, d) for s, d in zip(packed_shapes, packed_dtypes))
    ada, w_in_b, v1024, v512, wg, wsp, bias_s, w8, b8 = pl.pallas_call(
        _prep_kernel,
        out_shape=(jax.ShapeDtypeStruct((n_c, 6 * D_MODEL), F32),) +
                  tuple(jax.ShapeDtypeStruct(w.shape, BF16) for w in mats) + packed_out,
        grid=(steps,),
        in_specs=[pl.BlockSpec((dec_batch, D_MODEL), lambda n: (0, 0)),
                  pl.BlockSpec((batch, D_MODEL), lambda n: (0, 0)),
                  pl.BlockSpec((D_MODEL, 6 * D_MODEL // steps), lambda n: (0, n)),
                  pl.BlockSpec((1, 6 * D_MODEL // steps), lambda n: (0, n))] + [row_block(w) for w in mats] +
                 small_specs,
        out_specs=(pl.BlockSpec((n_c, 6 * D_MODEL // steps), lambda n: (0, n)),) +
                  tuple(row_block(w) for w in mats) + tuple(whole(o) for o in packed_out),
        scratch_shapes=[pltpu.VMEM(packed_shapes[2], F32)],
        compiler_params=pltpu.CompilerParams(dimension_semantics=("arbitrary",),
                                             vmem_limit_bytes=VMEM_LIMIT_BYTES),
        name="prep",
    )(c_sample, c_prompt, w_ada[0], b_ada, *mats, *small)

    weights = (w_in_b, wg)
    weight_specs = [_resident(w_in_b.shape), _resident(wg.shape)]
    tail_f32 = (w_out[0], w_ff1[0], w_ff2[0])
    tail_shapes = tuple(jax.ShapeDtypeStruct(w.shape, BF16) for w in tail_f32)
    cparams = functools.partial(pltpu.CompilerParams, vmem_limit_bytes=VMEM_LIMIT_BYTES)

    tt = PROMPT_TILE
    nt = seq // tt
    n_tiles = batch * nt

    def cur(s):
        return jnp.minimum(s, n_tiles - 1)

    def prev(s):
        return jnp.maximum(s - 1, 0)

    y_p, conv_p, lru_p, vrows_p, *tail_weights = pl.pallas_call(
        functools.partial(_prompt_kernel, tiles_per_seq=nt),
        out_shape=(jax.ShapeDtypeStruct((batch, seq, D_MODEL), F32),
                   jax.ShapeDtypeStruct((CONV_W - 1, batch, D_RG), F32),
                   jax.ShapeDtypeStruct((batch, D_RG), F32),
                   jax.ShapeDtypeStruct((batch, CHUNK, D_SG), F32)) + tail_shapes,
        grid=(n_tiles + 1,),
        in_specs=[pl.BlockSpec((None, tt, D_MODEL), lambda s: (cur(s) // nt, cur(s) % nt, 0)),
                  pl.BlockSpec((None, tt, D_MODEL), lambda s: (prev(s) // nt, prev(s) % nt, 0)),
                  pl.BlockSpec((batch, 6 * D_MODEL), lambda s: (dec_batch // batch, 0),
                               pipeline_mode=pl.Buffered(1)),
                  _resident(v1024.shape), _resident(v512.shape)] + weight_specs +
                 [_resident(wsp.shape), _resident(bias_s.shape)] + [pl.BlockSpec(memory_space=pl.ANY)] * 3,
        out_specs=(pl.BlockSpec((None, tt, D_MODEL), lambda s: (prev(s) // nt, prev(s) % nt, 0)),
                   pl.BlockSpec((CONV_W - 1, batch, D_RG), lambda s: (0, 0, 0)),
                   pl.BlockSpec((batch, D_RG), lambda s: (0, 0)),
                   pl.BlockSpec((batch, CHUNK, D_SG), lambda s: (0, 0, 0))) + (pl.BlockSpec(memory_space=pl.ANY),) * 3,
        scratch_shapes=[pltpu.VMEM((D_RG // LANES, tt, LANES), F32), pltpu.VMEM((D_RG // LANES, tt, LANES), F32),
                        pltpu.VMEM((SUBLANES, D_RG), F32), pltpu.VMEM((SUBLANES, D_RG), F32),
                        pltpu.VMEM((tt, D_MODEL), BF16)] +
                       [pltpu.VMEM(w.shape, BF16) for w in tail_f32] +
                       [pltpu.VMEM((CAST_BUFS, CAST_ROWS, D_MODEL), F32),
                        pltpu.SemaphoreType.DMA((CAST_SEM0 + CAST_BUFS,))],
        compiler_params=cparams(dimension_semantics=("arbitrary",)),
        name="prompt_trunk",
    )(x_prompt, x_prompt, ada, v1024, v512, *weights, wsp, bias_s, *tail_f32)

    nb = SAMPLE_NB
    n_cs = CONV_W - 1
    assert dec_seq == SUBLANES, "one decode sequence per 8-row group"
    rows = nb * dec_seq

    y_s, conv_s, lru_s, vrows_s = pl.pallas_call(
        _sample_kernel,
        out_shape=(jax.ShapeDtypeStruct((dec_batch * dec_seq, D_MODEL), F32),
                   jax.ShapeDtypeStruct((n_cs, dec_batch, D_RG), F32),
                   jax.ShapeDtypeStruct((dec_batch, D_RG), F32),
                   jax.ShapeDtypeStruct((dec_batch * dec_seq, D_SG), F32)),
        grid=(dec_batch // nb,),
        in_specs=[pl.BlockSpec((rows, D_MODEL), lambda i: (i, 0)),
                  pl.BlockSpec((nb, 6 * D_MODEL), lambda i: (i, 0)),
                  pl.BlockSpec((n_cs, nb, D_RG), lambda i: (0, i, 0)),
                  pl.BlockSpec((nb, D_RG), lambda i: (i, 0)),
                  _resident(v1024.shape), _resident(v512.shape)] + weight_specs +
                 [_resident(w8.shape), _resident(b8.shape)] + [pl.BlockSpec(memory_space=pl.ANY)] * 3,
        out_specs=(pl.BlockSpec((rows, D_MODEL), lambda i: (i, 0)),
                   pl.BlockSpec((n_cs, nb, D_RG), lambda i: (0, i, 0)),
                   pl.BlockSpec((nb, D_RG), lambda i: (i, 0)),
                   pl.BlockSpec((rows, D_SG), lambda i: (i, 0))),
        scratch_shapes=[pltpu.VMEM((D_MODEL // LANES, rows, LANES), F32),
                        pltpu.VMEM((D_MODEL // LANES, rows, LANES), F32),
                        pltpu.VMEM((D_SG // LANES, rows, LANES), F32)] +
                       [pltpu.VMEM(w.shape, BF16) for w in tail_weights] + [pltpu.SemaphoreType.DMA((3,))],
        compiler_params=cparams(dimension_semantics=("arbitrary",)),
        name="sample_trunk",
    )(x_sample.reshape(dec_batch * dec_seq, D_MODEL), ada, state_conv[0].transpose(1, 0, 2), state_rglru[0],
      v1024, v512, *weights, w8, b8, *tail_weights)

    return (y_p, y_s.reshape(dec_batch, dec_seq, D_MODEL), conv_p.transpose(1, 0, 2)[None], lru_p[None],
            vrows_p[None], conv_s.transpose(1, 0, 2)[None], lru_s[None],
            vrows_s.reshape(1, dec_batch, dec_seq, D_SG))
```

```python
import functools

import jax
import jax.numpy as jnp
from jax import lax
from jax.experimental import pallas as pl
from jax.experimental.pallas import tpu as pltpu

D_MODEL = 1024
D_RG = 512
D_SG = 512
D_IN = 2 * D_RG + 2 * D_SG
D_FF = 4 * D_MODEL
RG_HEADS = 8
RG_HEAD_DIM = D_RG // RG_HEADS
SG_HEADS = 8
SG_HEAD_DIM = D_SG // SG_HEADS
CHUNK = 128
CONV_W = 4
LRU_C = 8.0
EPS = 1e-6

SUBLANES = 8
LANES = 128
GATE_GROUPS = 2
GATE_W = D_RG // GATE_GROUPS
PROMPT_TILE = 512
SAMPLE_NB = 64
FF_CHUNK = 1024
N_FF_CHUNKS = D_FF // FF_CHUNK
PREP_STEPS = 2
CAST_ROWS = 256
CAST_BUFS = 4
CAST_SEM0 = 3
VMEM_LIMIT_BYTES = 60000 * 1024

F32 = jnp.float32
BF16 = jnp.bfloat16

_V_CONV_W, _V_CONV_B, _V_B_RA, _V_B_RI, _V_LRU_L, _V_LN_G, _V_LN_B, _V_G_RG, _V_G_SG = 0, 4, 5, 6, 7, 8, 9, 10, 11


def _dot(a, b):
    return jnp.dot(a, b, preferred_element_type=F32)


def _rms(x, g):
    ms = jnp.mean(x * x, axis=-1, keepdims=True)
    return (x * lax.rsqrt(ms + EPS)) * g


def _gelu_tanh(x):
    k = 0.7978845608028654
    u = x * (k + (k * 0.044715) * (x * x))
    return (0.5 * x) * (1.0 + jnp.tanh(u))


def _layer_norm(x, g, b):
    mu = jnp.mean(x, axis=-1, keepdims=True)
    xc = x - mu
    var = jnp.mean(xc * xc, axis=-1, keepdims=True)
    return (xc * lax.rsqrt(var + EPS)) * g + b


def _cat(parts, axis=0):
    return parts[0] if len(parts) == 1 else jnp.concatenate(parts, axis=axis)


def _split_mods(ada):
    return [ada[:, k * D_MODEL:(k + 1) * D_MODEL] for k in range(6)]


def _modulated_norm(x_blocks, g, sc, sh):
    gain = g * (1.0 + sc)
    return _cat([(_rms(xb, gain) + sh).astype(BF16) for xb in x_blocks])


def _gate_preacts(xc, wg_ref):
    xcb = xc.astype(BF16)
    return _cat([_dot(xcb[:, g * GATE_W:(g + 1) * GATE_W], wg_ref[g]) for g in range(GATE_GROUPS)], 1)


def _gate_tanh(pre_half, v512):
    r_parts, i_parts = [], []
    for g in range(GATE_GROUPS):
        lo, hi = g * GATE_W, (g + 1) * GATE_W
        ra = pre_half[:, 2 * lo:2 * lo + GATE_W]
        ri = pre_half[:, 2 * lo + GATE_W:2 * hi]
        r_parts.append(jnp.tanh(ra + 0.5 * v512[_V_B_RA:_V_B_RA + 1, lo:hi]))
        i_parts.append(jnp.tanh(ri + 0.5 * v512[_V_B_RI:_V_B_RI + 1, lo:hi]))
    return _cat(r_parts, 1), _cat(i_parts, 1)


def _lru_coeffs(t_r, v512):
    lam = v512[_V_LRU_L:_V_LRU_L + 1, :]
    log_sig = jnp.minimum(lam, 0.0) - jnp.log1p(jnp.exp(-jnp.abs(lam)))
    k = (0.5 * LRU_C) * log_sig
    log_a = k * t_r + k
    a = jnp.exp(log_a)
    z = 1.0 - a * a
    mult = jnp.where(z > 0.0, z * lax.rsqrt(z), 0.0)
    return a, mult


def _tail_thunks(x_blocks, merged_fn, mods, v1024, w_out_ref, w_ff1_ref, w_ff2_ref, store_y):
    _, _, gt_m, sh_f, sc_f, gt_f = mods
    rows = x_blocks[0].shape[0]
    st = {}

    def w_out():
        mm = _dot(merged_fn(), w_out_ref[...])
        st['x1'] = [xb[...] + gt_m * mm[i * rows:(i + 1) * rows] for i, xb in enumerate(x_blocks)]

    def hf():
        st['hf'] = _modulated_norm(st['x1'], v1024[1:2, :], sc_f, sh_f)

    def ff1(c):
        pre = _dot(st['hf'], w_ff1_ref[:, c * FF_CHUNK:(c + 1) * FF_CHUNK])
        h1 = jnp.maximum(pre.astype(BF16), 0.0)
        st['h1_%d' % c] = h1 * h1

    def ff2():
        h1 = _cat([st.pop('h1_%d' % c) for c in range(N_FF_CHUNKS)], 1)
        st['acc'] = _dot(h1, w_ff2_ref[...])

    def final():
        for i, xb in enumerate(st['x1']):
            x2 = xb + gt_f * st['acc'][i * rows:(i + 1) * rows]
            store_y(i, _rms(x2, v1024[2:3, :]))

    thunks = {'w_out': w_out, 'hf': hf, 'ff2': ff2, 'final': final}
    for c in range(N_FF_CHUNKS):
        thunks['ff1_%d' % c] = functools.partial(ff1, c)
    return thunks


TAIL_ORDER = ['w_out', 'hf'] + ['ff1_%d' % c for c in range(N_FF_CHUNKS)] + ['ff2', 'final']


def _run(thunks, order):
    assert sorted(order) == sorted(thunks), (sorted(order), sorted(thunks))
    for name in order:
        thunks[name]()


def _store_if(valid, ref, idx, val):
    keep = jnp.zeros(val.shape, jnp.int32) + valid.astype(jnp.int32)
    ref[idx] = jnp.where(keep > 0, val, ref[idx])


def _prompt_mix_thunks(x_ref, ada, v1024, v512, w_in_ref, wg_ref, wsp_ref, bias_s_ref, conv_ref, lru_ref,
                       vrows_ref, xperm_ref, hperm_ref, tail_ref, h_ref, merged_ref, j, slot, valid):
    tt = x_ref.shape[0]
    n_sub = SUBLANES
    sub_len = tt // n_sub
    n_slab = D_RG // LANES
    st = {}

    def norm():
        mods = _split_mods(ada)
        st['hm'] = _modulated_norm([x_ref[...]], v1024[0:1, :], mods[1], mods[0])

    def project():
        st['proj'] = proj = _dot(st['hm'], w_in_ref[...])
        xr = proj[:, 0:D_RG]
        st['xr_tail'] = xr[tt - SUBLANES:, :]
        for p in range(n_sub):
            for s in range(n_slab):
                xperm_ref[s, pl.ds(p, sub_len, stride=n_sub), :] = (
                    xr[p * sub_len:(p + 1) * sub_len, s * LANES:(s + 1) * LANES])

    def conv():
        xr_p = _cat([xperm_ref[s] for s in range(n_slab)], 1)
        sub = lax.broadcasted_iota(jnp.int32, (SUBLANES, D_RG), 0)
        prev_tail = tail_ref[...]
        head = []
        for k in range(CONV_W - 1, 0, -1):
            grp = pltpu.roll(xr_p[(sub_len - k) * SUBLANES:(sub_len - k + 1) * SUBLANES, :], 1, 0)
            head.append(jnp.where(sub == 0, prev_tail[SUBLANES - k:SUBLANES - k + 1, :], grp))
        ext = _cat(head + [xr_p])
        xc = v512[_V_CONV_B:_V_CONV_B + 1, :]
        for k in range(CONV_W):
            xc = xc + ext[k * SUBLANES:k * SUBLANES + tt, :] * v512[_V_CONV_W + k:_V_CONV_W + k + 1, :]
        st['xc'] = xc
        tail_ref[...] = st['xr_tail']
        for k in range(CONV_W - 1):
            row = SUBLANES - (CONV_W - 1) + k
            _store_if(valid, conv_ref, (k, pl.ds(slot, 1), slice(None)), st['xr_tail'][row:row + 1, :])

    def gate_mm():
        st['gate_pre'] = _gate_preacts(st['xc'], wg_ref)

    def coefficients():
        t_r, t_i = _gate_tanh(st.pop('gate_pre'), v512)
        a, mult = _lru_coeffs(t_r, v512)
        row = lax.broadcasted_iota(jnp.int32, (tt, D_RG), 0)
        mult = jnp.where(row + j * tt == 0, 1.0, mult)
        st['a'] = a
        st['b'] = mult * (0.5 * t_i + 0.5) * st.pop('xc')

    def recurrence():
        a, b = st['a'], st['b']
        sub = lax.broadcasted_iota(jnp.int32, (SUBLANES, D_RG), 0)
        hs, ps = [], []
        for q in range(sub_len):
            aq = a[q * SUBLANES:(q + 1) * SUBLANES, :]
            bq = b[q * SUBLANES:(q + 1) * SUBLANES, :]
            hs.append(bq if q == 0 else aq * hs[-1] + bq)
            ps.append(aq if q == 0 else aq * ps[-1])
        init = jnp.where(sub == 0, h_ref[SUBLANES - 1:SUBLANES, :], 0.0)
        for p in range(1, n_sub):
            end = hs[-1] + ps[-1] * init
            init = jnp.where(sub == p, pltpu.roll(end, 1, 0), init)
        end = hs[-1] + ps[-1] * init
        h_ref[...] = end
        _store_if(valid, lru_ref, (pl.ds(slot, 1), slice(None)), end[SUBLANES - 1:SUBLANES, :])
        h_perm = _cat([hq + pq * init for hq, pq in zip(hs, ps)])
        for s in range(n_slab):
            hperm_ref[s] = h_perm[:, s * LANES:(s + 1) * LANES]

    def recurrent_out(c):
        r0 = c * CHUNK
        yg = st['proj'][r0:r0 + CHUNK, D_RG:2 * D_RG]
        subs = range(r0 // sub_len, (r0 + CHUNK) // sub_len)
        h = _cat([_cat([hperm_ref[s, pl.ds(p, sub_len, stride=n_sub), :] for p in subs]) for s in range(n_slab)], 1)
        rg_out = h * _gelu_tanh(yg)
        merged_ref[r0:r0 + CHUNK, 0:D_RG] = _rms(rg_out, v512[_V_G_RG:_V_G_RG + 1, :]).astype(BF16)

    def spatial_pre(c):
        r0 = c * CHUNK
        lane = lax.broadcasted_iota(jnp.int32, (CHUNK, 2 * SG_HEAD_DIM), 1)
        v = st['proj'][r0:r0 + CHUNK, 2 * D_RG + D_SG:]
        vn = _layer_norm(v, v512[_V_LN_G:_V_LN_G + 1, :], v512[_V_LN_B:_V_LN_B + 1, :])
        rhs = []
        for p in range(SG_HEADS // 2):
            vp = vn[:, p * 2 * SG_HEAD_DIM:(p + 1) * 2 * SG_HEAD_DIM]
            rhs.append(jnp.concatenate([jnp.where(lane < SG_HEAD_DIM, vp, 0.0),
                                        jnp.where(lane >= SG_HEAD_DIM, vp, 0.0)], axis=0).astype(BF16))
        st['rhs_%d' % c] = rhs
        if r0 + CHUNK == tt:
            _store_if(valid, vrows_ref, slot, v)

    def spatial_mm(c):
        t_idx = lax.broadcasted_iota(jnp.int32, (CHUNK, 2 * CHUNK), 0)
        s_idx = lax.broadcasted_iota(jnp.int32, (CHUNK, 2 * CHUNK), 1) & (CHUNK - 1)
        outs = []
        for p, rhs in enumerate(st.pop('rhs_%d' % c)):
            ws_pair = jnp.where(s_idx <= t_idx, wsp_ref[p], jnp.zeros((), BF16))
            outs.append(_dot(ws_pair, rhs))
        st['mixed_%d' % c] = _cat(outs, 1)

    def spatial_post(c):
        r0 = c * CHUNK
        u = st['proj'][r0:r0 + CHUNK, 2 * D_RG:2 * D_RG + D_SG]
        sg_out = u * (st.pop('mixed_%d' % c) + bias_s_ref[...])
        merged_ref[r0:r0 + CHUNK, D_RG:] = _rms(sg_out, v512[_V_G_SG:_V_G_SG + 1, :]).astype(BF16)

    thunks = {'norm': norm, 'w_in': project, 'conv': conv, 'gate_mm': gate_mm, 'coef': coefficients,
              'rec': recurrence}
    for c in range(tt // CHUNK):
        thunks['sp_pre_%d' % c] = functools.partial(spatial_pre, c)
        thunks['sp_mm_%d' % c] = functools.partial(spatial_mm, c)
        thunks['sp_post_%d' % c] = functools.partial(spatial_post, c)
        thunks['rg_out_%d' % c] = functools.partial(recurrent_out, c)
    return thunks


N_PROMPT_CHUNKS = PROMPT_TILE // CHUNK
MIX_ORDER = (['norm', 'w_in', 'conv', 'gate_mm', 'coef', 'rec'] +
             [f'{name}_{c}' for c in range(N_PROMPT_CHUNKS) for name in ('sp_pre', 'sp_mm', 'sp_post', 'rg_out')])

assert N_PROMPT_CHUNKS == 4 and N_FF_CHUNKS == 4
INTERLEAVED_ORDER = [
    ('a', 'norm'), ('b', 'w_out'), ('a', 'w_in'), ('b', 'hf'), ('b', 'ff1_0'), ('a', 'conv'),
    ('a', 'sp_pre_0'), ('a', 'sp_pre_1'), ('b', 'ff1_1'), ('a', 'gate_mm'), ('a', 'sp_pre_2'), ('a', 'sp_pre_3'),
    ('b', 'ff1_2'), ('a', 'coef'), ('a', 'sp_mm_0'), ('a', 'sp_mm_1'), ('b', 'ff1_3'), ('a', 'rec'),
    ('a', 'sp_mm_2'), ('a', 'sp_mm_3'), ('a', 'sp_post_0'), ('a', 'sp_post_1'),
    ('a', 'rg_out_0'), ('a', 'rg_out_1'), ('a', 'sp_post_2'), ('a', 'sp_post_3'),
    ('a', 'rg_out_2'), ('a', 'rg_out_3'), ('b', 'ff2'), ('b', 'final'),
]


def _tail_weight_copies(src_refs, dst_refs, sem):
    return [pltpu.make_async_copy(src, dst, sem.at[k]) for k, (src, dst) in enumerate(zip(src_refs, dst_refs))]


def _weight_cast_thunks(f32_hbm_refs, bf16_refs, stage_ref, sem):
    windows = []
    for src, dst in zip(f32_hbm_refs, bf16_refs):
        rows, cols = dst.shape
        for r in range(0, rows, CAST_ROWS):
            for c in range(0, cols, D_MODEL):
                windows.append((src, dst, pl.ds(r, CAST_ROWS), pl.ds(c, D_MODEL)))
    copies = [pltpu.make_async_copy(src.at[rs, cs], stage_ref.at[k % CAST_BUFS], sem.at[CAST_SEM0 + k % CAST_BUFS])
              for k, (src, _, rs, cs) in enumerate(windows)]

    def make(k):
        def run():
            if k == 0:
                for cp in copies[:CAST_BUFS]:
                    cp.start()
            copies[k].wait()
            _, dst, rs, cs = windows[k]
            dst[rs, cs] = stage_ref[k % CAST_BUFS].astype(BF16)
            if k + CAST_BUFS < len(copies):
                copies[k + CAST_BUFS].start()
        return run

    return [make(k) for k in range(len(copies))]


def _prompt_kernel(x_ref, xprev_ref, ada_ref, v1024_ref, v512_ref, w_in_ref, wg_ref, wsp_ref,
                   bias_s_ref, w_out_hbm, w_ff1_hbm, w_ff2_hbm,
                   y_ref, conv_ref, lru_ref, vrows_ref, w_out_o, w_ff1_o, w_ff2_o,
                   xperm_ref, hperm_ref, tail_ref, h_ref, merged_ref, w_out_ref, w_ff1_ref, w_ff2_ref, stage_ref, w_sem,
                   *, tiles_per_seq):
    s = pl.program_id(0)
    n_tiles = pl.num_programs(0) - 1
    j = lax.rem(s, tiles_per_seq)
    seq_cur = jnp.minimum(s, n_tiles - 1) // tiles_per_seq
    seq_prev = jnp.maximum(s - 1, 0) // tiles_per_seq

    tail_refs = (w_out_ref, w_ff1_ref, w_ff2_ref)
    out_copies = _tail_weight_copies(tail_refs, (w_out_o, w_ff1_o, w_ff2_o), w_sem)

    @pl.when(s == 0)
    def _():
        conv_ref[...] = jnp.zeros(conv_ref.shape, F32)
        lru_ref[...] = jnp.zeros(lru_ref.shape, F32)
        vrows_ref[...] = jnp.zeros(vrows_ref.shape, F32)

    @pl.when(j == 0)
    def _():
        tail_ref[...] = jnp.zeros((SUBLANES, D_RG), F32)
        h_ref[...] = jnp.zeros((SUBLANES, D_RG), F32)

    v1024 = v1024_ref[...]
    v512 = v512_ref[...]

    def store_y(_, val):
        y_ref[...] = val

    def stage_a():
        return _prompt_mix_thunks(x_ref, ada_ref[pl.ds(seq_cur, 1), :], v1024, v512, w_in_ref, wg_ref, wsp_ref,
                                  bias_s_ref, conv_ref, lru_ref, vrows_ref, xperm_ref, hperm_ref, tail_ref, h_ref,
                                  merged_ref, j, seq_cur, s < n_tiles)

    def stage_b():
        return _tail_thunks([xprev_ref], lambda: merged_ref[...], _split_mods(ada_ref[pl.ds(seq_prev, 1), :]),
                            v1024, w_out_ref, w_ff1_ref, w_ff2_ref, store_y)

    @pl.when(s == 0)
    def _():
        casts = _weight_cast_thunks((w_out_hbm, w_ff1_hbm, w_ff2_hbm), tail_refs, stage_ref, w_sem)
        mix = stage_a()
        per = -(-len(casts) // len(MIX_ORDER))
        for n, name in enumerate(MIX_ORDER):
            for cast in casts[n * per:(n + 1) * per]:
                cast()
            mix[name]()
        for cp in out_copies:
            cp.start()

    @pl.when(s > 0)
    def _():
        stages = {'a': stage_a(), 'b': stage_b()}
        assert sorted(n for k, n in INTERLEAVED_ORDER if k == 'a') == sorted(stages['a'])
        assert sorted(n for k, n in INTERLEAVED_ORDER if k == 'b') == sorted(stages['b'])
        for stage, name in INTERLEAVED_ORDER:
            stages[stage][name]()

    @pl.when(s == n_tiles)
    def _():
        for cp in out_copies:
            cp.wait()


def _sample_kernel(x_ref, ada_ref, cs_ref, h0_ref, v1024_ref, v512_ref, w_in_ref, wg_ref, w8_ref, b8_ref,
                   w_out_hbm, w_ff1_hbm, w_ff2_hbm,
                   y_ref, conv_ref, lru_ref, vrows_ref,
                   xslab_ref, yslab_ref, vslab_ref, w_out_ref, w_ff1_ref, w_ff2_ref, w_sem):
    n_t = SUBLANES
    nb = x_ref.shape[0] // n_t
    copies = _tail_weight_copies((w_out_hbm, w_ff1_hbm, w_ff2_hbm), (w_out_ref, w_ff1_ref, w_ff2_ref), w_sem)

    def at_first_step(fn):
        pl.when(pl.program_id(0) == 0)(fn)

    @at_first_step
    def _():
        copies[0].start()
        copies[1].start()

    mods = _split_mods(ada_ref[...])
    sh_m, sc_m = mods[0], mods[1]
    v1024 = v1024_ref[...]
    v512 = v512_ref[...]

    def blk(arr, t):
        return arr[t * nb:(t + 1) * nb, :]

    for s in range(D_MODEL // LANES):
        xslab_ref[s] = x_ref[:, s * LANES:(s + 1) * LANES]
    x_blocks = [_cat([xslab_ref[s, pl.ds(t, nb, stride=n_t), :] for s in range(D_MODEL // LANES)], 1)
                for t in range(n_t)]
    hm = _modulated_norm(x_blocks, v1024[0:1, :], sc_m, sh_m)
    proj = _dot(hm, w_in_ref[...])
    xr = proj[:, 0:D_RG]
    yg = proj[:, D_RG:2 * D_RG]
    u = proj[:, 2 * D_RG:2 * D_RG + D_SG]
    v = proj[:, 2 * D_RG + D_SG:]

    xp = [cs_ref[k] for k in range(CONV_W - 1)] + [blk(xr, t) for t in range(n_t)]
    xc_blocks = []
    for t in range(n_t):
        acc = v512[_V_CONV_B:_V_CONV_B + 1, :]
        for k in range(CONV_W):
            acc = acc + xp[t + k] * v512[_V_CONV_W + k:_V_CONV_W + k + 1, :]
        xc_blocks.append(acc)
    for k in range(CONV_W - 1):
        conv_ref[k] = xp[n_t + k]
    xc = _cat(xc_blocks)

    t_r, t_i = _gate_tanh(_gate_preacts(xc, wg_ref), v512)
    a, mult = _lru_coeffs(t_r, v512)
    bterm = mult * (0.5 * t_i + 0.5) * xc
    h = h0_ref[...]
    h_blocks = []
    for t in range(n_t):
        h = blk(a, t) * h + blk(bterm, t)
        h_blocks.append(h)
    lru_ref[...] = h
    rg_out = _cat(h_blocks) * _gelu_tanh(yg)

    vn = _layer_norm(v, v512[_V_LN_G:_V_LN_G + 1, :], v512[_V_LN_B:_V_LN_B + 1, :])
    sg_blocks = []
    for t in range(n_t):
        for s in range(D_SG // LANES):
            vslab_ref[s, pl.ds(t, nb, stride=n_t), :] = blk(v, t)[:, s * LANES:(s + 1) * LANES]
        mixed = b8_ref[t:t + 1, :]
        for s in range(t + 1):
            mixed = mixed + w8_ref[t, s:s + 1, :] * blk(vn, s)
        sg_blocks.append(blk(u, t) * mixed)
    sg_out = _cat(sg_blocks)
    for s in range(D_SG // LANES):
        vrows_ref[:, s * LANES:(s + 1) * LANES] = vslab_ref[s]

    merged = jnp.concatenate(
        [_rms(rg_out, v512[_V_G_RG:_V_G_RG + 1, :]).astype(BF16),
         _rms(sg_out, v512[_V_G_SG:_V_G_SG + 1, :]).astype(BF16)], axis=1)

    def store_y(t, val):
        for s in range(D_MODEL // LANES):
            yslab_ref[s, pl.ds(t, nb, stride=n_t), :] = val[:, s * LANES:(s + 1) * LANES]

    tail = _tail_thunks(x_blocks, lambda: merged, mods, v1024, w_out_ref, w_ff1_ref, w_ff2_ref, store_y)
    assert TAIL_ORDER[0] == 'w_out' and TAIL_ORDER[2] == 'ff1_0' and TAIL_ORDER[-2] == 'ff2'

    @at_first_step
    def _():
        copies[0].wait()

    _run({n: tail[n] for n in TAIL_ORDER[:2]}, TAIL_ORDER[:2])

    @at_first_step
    def _():
        copies[1].wait()
        copies[2].start()

    _run({n: tail[n] for n in TAIL_ORDER[2:-2]}, TAIL_ORDER[2:-2])

    @at_first_step
    def _():
        copies[2].wait()

    _run({n: tail[n] for n in TAIL_ORDER[-2:]}, TAIL_ORDER[-2:])
    for s in range(D_MODEL // LANES):
        y_ref[:, s * LANES:(s + 1) * LANES] = yslab_ref[s]


def _pack_params(g_mix_ref, g_ffn_ref, g_final_ref, conv_w_ref, conv_b_ref, b_ra_ref, b_ri_ref, lru_l_ref,
                 ln_v_g_ref, ln_v_b_ref, g_rg_ref, g_sg_ref, w_ra_ref, w_ri_ref, w_s_ref, b_s_ref,
                 v1024_o, v512_o, wg_o, wsp_o, bias_s_o, w8_o, b8_o, wg_f):
    v1024_o[0:1, :] = g_mix_ref[...]
    v1024_o[1:2, :] = g_ffn_ref[...]
    v1024_o[2:3, :] = g_final_ref[...]
    v512_o[_V_CONV_W:_V_CONV_W + CONV_W, :] = conv_w_ref[...]
    for row, ref in ((_V_CONV_B, conv_b_ref), (_V_LRU_L, lru_l_ref), (_V_LN_G, ln_v_g_ref), (_V_LN_B, ln_v_b_ref),
                     (_V_G_RG, g_rg_ref), (_V_G_SG, g_sg_ref)):
        v512_o[row:row + 1, :] = ref[...]
    wg_f[...] = jnp.zeros(wg_f.shape, F32)
    per = RG_HEADS // GATE_GROUPS
    for h in range(RG_HEADS):
        g, lo = h // per, (h % per) * RG_HEAD_DIM
        v512_o[_V_B_RA:_V_B_RA + 1, h * RG_HEAD_DIM:(h + 1) * RG_HEAD_DIM] = b_ra_ref[h:h + 1, :]
        v512_o[_V_B_RI:_V_B_RI + 1, h * RG_HEAD_DIM:(h + 1) * RG_HEAD_DIM] = b_ri_ref[h:h + 1, :]
        wg_f[g, lo:lo + RG_HEAD_DIM, lo:lo + RG_HEAD_DIM] = w_ra_ref[h]
        wg_f[g, lo:lo + RG_HEAD_DIM, GATE_W + lo:GATE_W + lo + RG_HEAD_DIM] = w_ri_ref[h]
    wg_o[...] = (0.5 * wg_f[...]).astype(BF16)
    for h in range(SG_HEADS):
        wsp_o[h // 2, :, (h % 2) * CHUNK:(h % 2 + 1) * CHUNK] = w_s_ref[h].astype(BF16)
    head = lax.broadcasted_iota(jnp.int32, (1, D_SG), 1) // SG_HEAD_DIM
    b_t = b_s_ref[...].T
    bias = jnp.zeros((CHUNK, D_SG), F32)
    w_t = []
    for h in range(SG_HEADS):
        bias = jnp.where(head == h, b_t[:, h:h + 1], bias)
        w_t.append(w_s_ref[h, 0:SUBLANES, :].T[0:SUBLANES, :])
    bias_s_o[...] = bias
    b8_o[...] = bias[0:SUBLANES, :]
    for t in range(SUBLANES):
        acc = jnp.zeros((SUBLANES, D_SG), F32)
        for h in range(SG_HEADS):
            acc = jnp.where(head == h, w_t[h][:, t:t + 1], acc)
        w8_o[t] = acc


def _prep_kernel(cs_ref, cp_ref, w_ada_ref, b_ada_ref, w_in_ref, *rest):
    small, (ada_ref, w_in_o), packed = rest[:16], rest[16:18], rest[18:]
    c = jnp.concatenate([cs_ref[...], cp_ref[...]], axis=0)
    s = (c * jax.nn.sigmoid(c)).astype(BF16)
    ada_ref[...] = _dot(s, w_ada_ref[...].astype(BF16)) + b_ada_ref[...]
    w_in_o[...] = w_in_ref[...].astype(BF16)

    @pl.when(pl.program_id(0) == 0)
    def _():
        _pack_params(*small, *packed)


def _resident(shape):
    nd = len(shape)
    return pl.BlockSpec(shape, lambda *_: (0,) * nd, pipeline_mode=pl.Buffered(1))


def kernel(x_prompt, x_sample, c_prompt, c_sample, state_conv, state_rglru, w_ada, b_ada, g_mix, g_ffn, w_in, conv_w, conv_b, w_ra, b_ra, w_ri, b_ri, lru_l, ln_v_g, ln_v_b, w_s, b_s, g_rg, g_sg, w_out, w_ff1, w_ff2, g_final):
    batch, seq, _ = x_prompt.shape
    dec_batch, dec_seq, _ = x_sample.shape
    assert w_ada.shape[0] == 1, "single-layer step"
    assert seq % PROMPT_TILE == 0 and PROMPT_TILE % CHUNK == 0 and dec_batch % SAMPLE_NB == 0
    assert batch * (seq // PROMPT_TILE) >= 2, "the prompt pipeline needs at least two tiles"
    assert dec_seq <= CHUNK

    n_c = dec_batch + batch
    assert dec_batch % SUBLANES == 0 and dec_batch % batch == 0
    steps = PREP_STEPS

    def row_block(w):
        return pl.BlockSpec((w.shape[0] // steps, w.shape[1]), lambda n: (n, 0))

    def whole(a, lead=0):
        nd = a.ndim
        return pl.BlockSpec((None,) * lead + a.shape[lead:], lambda n: (0,) * nd)

    mats = (w_in[0],)
    gains =(g_mix, g_ffn, g_final.reshape(1, D_MODEL))
    rows512 = (conv_b, lru_l, ln_v_g, ln_v_b, g_rg, g_sg)
    small = gains + (conv_w, rows512[0], b_ra, b_ri) + rows512[1:] + (w_ra, w_ri, w_s, b_s)
    small_specs = [whole(a) for a in gains] + [whole(conv_w, 1), whole(conv_b), whole(b_ra, 1), whole(b_ri, 1)] + \
                  [whole(a) for a in rows512[1:]] + [whole(w_ra, 1), whole(w_ri, 1), whole(w_s, 1), whole(b_s, 1)]
    packed_shapes = ((3, D_MODEL), (12, D_RG), (GATE_GROUPS, GATE_W, 2 * GATE_W), (SG_HEADS // 2, CHUNK, 2 * CHUNK),
                     (CHUNK, D_SG), (SUBLANES, SUBLANES, D_SG), (SUBLANES, D_SG))
    packed_dtypes = (F32, F32, BF16, BF16, F32, F32, F32)
    packed_out = tuple(jax.ShapeDtypeStruct(s, d) for s, d in zip(packed_shapes, packed_dtypes))
    ada, w_in_b, v1024, v512, wg, wsp, bias_s, w8, b8 = pl.pallas_call(
        _prep_kernel,
        out_shape=(jax.ShapeDtypeStruct((n_c, 6 * D_MODEL), F32),) +
                  tuple(jax.ShapeDtypeStruct(w.shape, BF16) for w in mats) + packed_out,
        grid=(steps,),
        in_specs=[pl.BlockSpec((dec_batch, D_MODEL), lambda n: (0, 0)),
                  pl.BlockSpec((batch, D_MODEL), lambda n: (0, 0)),
                  pl.BlockSpec((D_MODEL, 6 * D_MODEL // steps), lambda n: (0, n)),
                  pl.BlockSpec((1, 6 * D_MODEL // steps), lambda n: (0, n))] + [row_block(w) for w in mats] +
                 small_specs,
        out_specs=(pl.BlockSpec((n_c, 6 * D_MODEL // steps), lambda n: (0, n)),) +
                  tuple(row_block(w) for w in mats) + tuple(whole(o) for o in packed_out),
        scratch_shapes=[pltpu.VMEM(packed_shapes[2], F32)],
        compiler_params=pltpu.CompilerParams(dimension_semantics=("arbitrary",),
                                             vmem_limit_bytes=VMEM_LIMIT_BYTES),
        name="prep",
    )(c_sample, c_prompt, w_ada[0], b_ada, *mats, *small)

    weights = (w_in_b, wg)
    weight_specs = [_resident(w_in_b.shape), _resident(wg.shape)]
    tail_f32 = (w_out[0], w_ff1[0], w_ff2[0])
    tail_shapes = tuple(jax.ShapeDtypeStruct(w.shape, BF16) for w in tail_f32)
    cparams = functools.partial(pltpu.CompilerParams, vmem_limit_bytes=VMEM_LIMIT_BYTES)

    tt = PROMPT_TILE
    nt = seq // tt
    n_tiles = batch * nt

    def cur(s):
        return jnp.minimum(s, n_tiles - 1)

    def prev(s):
        return jnp.maximum(s - 1, 0)

    y_p, conv_p, lru_p, vrows_p, *tail_weights = pl.pallas_call(
        functools.partial(_prompt_kernel, tiles_per_seq=nt),
        out_shape=(jax.ShapeDtypeStruct((batch, seq, D_MODEL), F32),
                   jax.ShapeDtypeStruct((CONV_W - 1, batch, D_RG), F32),
                   jax.ShapeDtypeStruct((batch, D_RG), F32),
                   jax.ShapeDtypeStruct((batch, CHUNK, D_SG), F32)) + tail_shapes,
        grid=(n_tiles + 1,),
        in_specs=[pl.BlockSpec((None, tt, D_MODEL), lambda s: (cur(s) // nt, cur(s) % nt, 0)),
                  pl.BlockSpec((None, tt, D_MODEL), lambda s: (prev(s) // nt, prev(s) % nt, 0)),
                  pl.BlockSpec((batch, 6 * D_MODEL), lambda s: (dec_batch // batch, 0),
                               pipeline_mode=pl.Buffered(1)),
                  _resident(v1024.shape), _resident(v512.shape)] + weight_specs +
                 [_resident(wsp.shape), _resident(bias_s.shape)] + [pl.BlockSpec(memory_space=pl.ANY)] * 3,
        out_specs=(pl.BlockSpec((None, tt, D_MODEL), lambda s: (prev(s) // nt, prev(s) % nt, 0)),
                   pl.BlockSpec((CONV_W - 1, batch, D_RG), lambda s: (0, 0, 0)),
                   pl.BlockSpec((batch, D_RG), lambda s: (0, 0)),
                   pl.BlockSpec((batch, CHUNK, D_SG), lambda s: (0, 0, 0))) + (pl.BlockSpec(memory_space=pl.ANY),) * 3,
        scratch_shapes=[pltpu.VMEM((D_RG // LANES, tt, LANES), F32), pltpu.VMEM((D_RG // LANES, tt, LANES), F32),
                        pltpu.VMEM((SUBLANES, D_RG), F32), pltpu.VMEM((SUBLANES, D_RG), F32),
                        pltpu.VMEM((tt, D_MODEL), BF16)] +
                       [pltpu.VMEM(w.shape, BF16) for w in tail_f32] +
                       [pltpu.VMEM((CAST_BUFS, CAST_ROWS, D_MODEL), F32),
                        pltpu.SemaphoreType.DMA((CAST_SEM0 + CAST_BUFS,))],
        compiler_params=cparams(dimension_semantics=("arbitrary",)),
        name="prompt_trunk",
    )(x_prompt, x_prompt, ada, v1024, v512, *weights, wsp, bias_s, *tail_f32)

    nb = SAMPLE_NB
    n_cs = CONV_W - 1
    assert dec_seq == SUBLANES, "one decode sequence per 8-row group"
    rows = nb * dec_seq

    y_s, conv_s, lru_s, vrows_s = pl.pallas_call(
        _sample_kernel,
        out_shape=(jax.ShapeDtypeStruct((dec_batch * dec_seq, D_MODEL), F32),
                   jax.ShapeDtypeStruct((n_cs, dec_batch, D_RG), F32),
                   jax.ShapeDtypeStruct((dec_batch, D_RG), F32),
                   jax.ShapeDtypeStruct((dec_batch * dec_seq, D_SG), F32)),
        grid=(dec_batch // nb,),
        in_specs=[pl.BlockSpec((rows, D_MODEL), lambda i: (i, 0)),
                  pl.BlockSpec((nb, 6 * D_MODEL), lambda i: (i, 0)),
                  pl.BlockSpec((n_cs, nb, D_RG), lambda i: (0, i, 0)),
                  pl.BlockSpec((nb, D_RG), lambda i: (i, 0)),
                  _resident(v1024.shape), _resident(v512.shape)] + weight_specs +
                 [_resident(w8.shape), _resident(b8.shape)] + [pl.BlockSpec(memory_space=pl.ANY)] * 3,
        out_specs=(pl.BlockSpec((rows, D_MODEL), lambda i: (i, 0)),
                   pl.BlockSpec((n_cs, nb, D_RG), lambda i: (0, i, 0)),
                   pl.BlockSpec((nb, D_RG), lambda i: (i, 0)),
                   pl.BlockSpec((rows, D_SG), lambda i: (i, 0))),
        scratch_shapes=[pltpu.VMEM((D_MODEL // LANES, rows, LANES), F32),
                        pltpu.VMEM((D_MODEL // LANES, rows, LANES), F32),
                        pltpu.VMEM((D_SG // LANES, rows, LANES), F32)] +
                       [pltpu.VMEM(w.shape, BF16) for w in tail_weights] + [pltpu.SemaphoreType.DMA((3,))],
        compiler_params=cparams(dimension_semantics=("arbitrary",)),
        name="sample_trunk",
    )(x_sample.reshape(dec_batch * dec_seq, D_MODEL), ada, state_conv[0].transpose(1, 0, 2), state_rglru[0],
      v1024, v512, *weights, w8, b8, *tail_weights)

    return (y_p, y_s.reshape(dec_batch, dec_seq, D_MODEL), conv_p.transpose(1, 0, 2)[None], lru_p[None],
            vrows_p[None], conv_s.transpose(1, 0, 2)[None], lru_s[None],
            vrows_s.reshape(1, dec_batch, dec_seq, D_SG))
```
